```python
import math
import jax
import jax.numpy as jnp
from jax import lax
import numpy as np

D_MODEL = 1024
BATCH = 32
SEQ = 256
DEPTH = 4
DEC_BATCH = 2
DEC_SEQ = 4096
PAST_LEN = 256

GRID_W = 64
N_EVEN = (DEPTH + 1) // 2
N_ODD = DEPTH // 2
RMS_EPS = 1e-6
N_MOD = 6
D_FF = 4 * D_MODEL
CONV_K = 3
Q_BLOCK = 128

NA_HEADS = 8
NA_HEAD_DIM = 64
NA_WIDTH = NA_HEADS * NA_HEAD_DIM
NA_WIN_ROWS = 8
NA_WIN_COLS = 16
NA_RPB_ROWS = 2 * NA_WIN_ROWS - 1
NA_RPB_COLS = 2 * NA_WIN_COLS - 1

SSD_INNER = D_MODEL
SSD_HEAD_DIM = 64
SSD_HEADS = SSD_INNER // SSD_HEAD_DIM
SSD_GROUPS = 2
SSD_RPG = SSD_HEADS // SSD_GROUPS
SSD_STATE = 128
SSD_GN = SSD_GROUPS * SSD_STATE
SSD_CONV_DIM = SSD_INNER + 2 * SSD_GN
SSD_CHUNK = 128

ML_HEADS = 8
ML_QK_DIM = D_MODEL // 16
ML_V_DIM = D_MODEL // 8
ML_QK_WIDTH = ML_HEADS * ML_QK_DIM
ML_V_WIDTH = ML_HEADS * ML_V_DIM
ML_CHUNK = 64

EVEN_SPLITS = (NA_WIDTH, 2 * NA_WIDTH, 3 * NA_WIDTH, 3 * NA_WIDTH + SSD_INNER, 3 * NA_WIDTH + 2 * SSD_INNER + 2 * SSD_GN)
EVEN_IN = 3 * NA_WIDTH + 2 * SSD_INNER + 2 * SSD_GN + 2 * SSD_HEADS
EVEN_MIX = NA_WIDTH + SSD_INNER
ODD_SPLITS = (2 * ML_QK_WIDTH, 2 * ML_QK_WIDTH + ML_V_WIDTH, 2 * ML_QK_WIDTH + 2 * ML_V_WIDTH)
ODD_IN = 2 * ML_QK_WIDTH + 2 * ML_V_WIDTH + 4 * ML_HEADS

kernel_name = "hybrid_na_ssd_mlstm_prefix_dit_step"


def rmsnorm(x, g):
    xf = x.astype(jnp.float32)
    y = xf * lax.rsqrt(jnp.mean(xf * xf, axis=-1, keepdims=True) + RMS_EPS)
    return (y * g.astype(jnp.float32)).astype(x.dtype)


def modulate(x, g, shift, scale):
    return rmsnorm(x, g) * (1 + scale[:, None, :]) + shift[:, None, :]


def adaln_params(cond, w, b):
    return jnp.split(jax.nn.silu(cond) @ w + b, N_MOD, axis=-1)


def dwconv_same(x, w, b):
    L = x.shape[1]
    pad = CONV_K // 2
    xp = jnp.pad(x, ((0, 0), (pad, pad), (0, 0)))
    y = xp[:, 0:L] * w[0]
    for j in range(1, CONV_K):
        y = y + xp[:, j:j + L] * w[j]
    return y + b


def sqrelu_mlp(h, w1, w2):
    return jnp.square(jax.nn.relu(h @ w1)) @ w2


def even_inputs(h, w_in):
    b, L, _ = h.shape
    q, k, v, z, xbc, dt = jnp.split(h @ w_in, EVEN_SPLITS, axis=-1)
    heads = lambda t: t.reshape(b, L, NA_HEADS, NA_HEAD_DIM)
    return heads(q) * NA_HEAD_DIM ** -0.5, heads(k), heads(v), z, xbc, dt


def ctx_attention(q, k, v):
    b, S, H, dh = q.shape
    qb = jnp.moveaxis(q.reshape(b, S // Q_BLOCK, Q_BLOCK, H, dh), 1, 0)

    def block(qi):
        s = jnp.einsum('bqhd,bkhd->bhqk', qi, k).astype(jnp.float32)
        p = jax.nn.softmax(s, axis=-1).astype(v.dtype)
        return jnp.einsum('bhqk,bkhd->bqhd', p, v)

    o = lax.map(block, qb)
    return jnp.moveaxis(o, 0, 1).reshape(b, S, H * dh)


def na_latent(q, k, v, k_ctx, v_ctx, rpb):
    b, L, H, dh = q.shape
    rows = L // GRID_W
    wr = min(NA_WIN_ROWS, rows)
    r = jnp.arange(rows)
    key_rows = jnp.clip(r - NA_WIN_ROWS // 2, 0, rows - wr)[:, None] + jnp.arange(wr)[None]
    col = jnp.arange(GRID_W)
    col_start = jnp.clip(col - NA_WIN_COLS // 2, 0, GRID_W - NA_WIN_COLS)
    col_in = (col[None, :] >= col_start[:, None]) & (col[None, :] < col_start[:, None] + NA_WIN_COLS)
    rel_r = key_rows - r[:, None] + NA_WIN_ROWS - 1
    rel_c = jnp.clip(col[None, :] - col[:, None] + NA_WIN_COLS - 1, 0, NA_RPB_COLS - 1)
    bias = rpb[:, rel_r[:, None, :, None], rel_c[None, :, None, :]].astype(jnp.float32)
    bias = jnp.where(col_in[None, None, :, None, :], bias, -jnp.inf).reshape(H, rows, GRID_W, wr * GRID_W)
    qg = q.reshape(b, rows, GRID_W, H, dh)
    kg = k.reshape(b, rows, GRID_W, H, dh)[:, key_rows].reshape(b, rows, wr * GRID_W, H, dh)
    vg = v.reshape(b, rows, GRID_W, H, dh)[:, key_rows].reshape(b, rows, wr * GRID_W, H, dh)
    s_loc = jnp.einsum('brqhd,brkhd->bhrqk', qg, kg).astype(jnp.float32) + bias[None]
    s_ctx = jnp.einsum('brqhd,bhcd->bhrqc', qg, k_ctx).astype(jnp.float32)
    p = jax.nn.softmax(jnp.concatenate([s_loc, s_ctx], axis=-1), axis=-1).astype(v.dtype)
    n_loc = wr * GRID_W
    o = jnp.einsum('bhrqk,brkhd->brqhd', p[..., :n_loc], vg) + jnp.einsum('bhrqc,bhcd->brqhd', p[..., n_loc:], v_ctx)
    return o.reshape(b, L, H * dh)


def ssd_scan(x, dt, A, Bm, Cm, h0):
    b, L, G, R, P = x.shape
    nc = L // SSD_CHUNK
    chunks = lambda t: jnp.moveaxis(t.reshape((b, nc, SSD_CHUNK) + t.shape[2:]), 1, 0)
    causal = jnp.tril(jnp.ones((SSD_CHUNK, SSD_CHUNK), bool))[None, :, :, None, None]

    def step(h, inp):
        xc, dtc, Bc, Cc = inp
        cum = jnp.cumsum(dtc * A, axis=1)
        seg = jnp.where(causal, cum[:, :, None] - cum[:, None], -jnp.inf)
        wts = jnp.einsum('bign,bjgn->bijg', Cc, Bc)[..., None] * jnp.exp(seg) * dtc[:, None]
        y = jnp.einsum('bijgr,bjgrp->bigrp', wts, xc)
        y = y + jnp.einsum('bign,bgrpn->bigrp', Cc, h) * jnp.exp(cum)[..., None]
        w_end = jnp.exp(cum[:, -1:] - cum) * dtc
        h = h * jnp.exp(cum[:, -1])[..., None, None] + jnp.einsum('bjgn,bjgr,bjgrp->bgrpn', Bc, w_end, xc)
        return h, y

    h, ys = lax.scan(step, h0, (chunks(x), chunks(dt), chunks(Bm), chunks(Cm)))
    return jnp.moveaxis(ys, 0, 1).reshape(b, L, G, R, P), h


def ssd_branch(z, xbc, dt_raw, conv_w, conv_b, dt_bias, a_log, d_skip, norm_g, h0_f, h0_b):
    b, L, _ = z.shape
    f32 = jnp.float32
    xbc = jax.nn.silu(dwconv_same(xbc, conv_w, conv_b)).astype(f32)
    xs, Bm, Cm = jnp.split(xbc, (SSD_INNER, SSD_INNER + SSD_GN), axis=-1)
    xs = xs.reshape(b, L, SSD_GROUPS, SSD_RPG, SSD_HEAD_DIM)
    Bm = Bm.reshape(b, L, SSD_GROUPS, SSD_STATE)
    Cm = Cm.reshape(b, L, SSD_GROUPS, SSD_STATE)
    dt = jax.nn.softplus(dt_raw.astype(f32).reshape(b, L, 2, SSD_GROUPS, SSD_RPG)
                         + dt_bias.astype(f32).reshape(2, SSD_GROUPS, SSD_RPG))
    A = -jnp.exp(a_log.astype(f32)).reshape(2, SSD_GROUPS, SSD_RPG)
    st = lambda h: h.astype(f32).reshape(b, SSD_GROUPS, SSD_RPG, SSD_HEAD_DIM, SSD_STATE)
    flip = lambda t: jnp.flip(t, axis=1)
    y_f, hf = ssd_scan(xs, dt[:, :, 0], A[0], Bm, Cm, st(h0_f))
    y_b, hb = ssd_scan(flip(xs), flip(dt[:, :, 1]), A[1], flip(Bm), flip(Cm), st(h0_b))
    y = y_f + flip(y_b) + d_skip.astype(f32).reshape(SSD_GROUPS, SSD_RPG, 1) * xs
    y = y.reshape(b, L, SSD_INNER).astype(z.dtype)
    y = rmsnorm(y * jax.nn.silu(z), norm_g)
    unst = lambda h: h.reshape(b, SSD_HEADS, SSD_HEAD_DIM, SSD_STATE).astype(z.dtype)
    return y, unst(hf), unst(hb)


def mlstm_scan(q, k, v, log_i, log_f, c0, n0, m0):
    b, L, H, _ = q.shape
    nc = L // ML_CHUNK
    chunks = lambda t: jnp.moveaxis(t.reshape((b, nc, ML_CHUNK) + t.shape[2:]), 1, 0)
    causal = jnp.tril(jnp.ones((ML_CHUNK, ML_CHUNK), bool))[None, :, :, None]

    def step(carry, inp):
        c, n, m = carry
        qc, kc, vc, li, lf = inp
        cum = jnp.cumsum(lf, axis=1)
        dmat = jnp.where(causal, cum[:, :, None] - cum[:, None] + li[:, None], -jnp.inf)
        m_inter = cum + m[:, None]
        m_t = jnp.maximum(m_inter, jnp.max(dmat, axis=2))
        w_intra = jnp.exp(dmat - m_t[:, :, None])
        w_inter = jnp.exp(m_inter - m_t)
        s = jnp.einsum('bihd,bjhd->bijh', qc, kc) * w_intra
        num = jnp.einsum('bijh,bjhe->bihe', s, vc) + w_inter[..., None] * jnp.einsum('bhed,bihd->bihe', c, qc)
        den = jnp.sum(s, axis=2) + w_inter * jnp.einsum('bhd,bihd->bih', n, qc)
        h = num / jnp.maximum(jnp.abs(den), jnp.exp(-m_t))[..., None]
        g = cum[:, -1:] - cum + li
        m_new = jnp.maximum(cum[:, -1] + m, jnp.max(g, axis=1))
        wk = jnp.exp(g - m_new[:, None])
        wc = jnp.exp(cum[:, -1] + m - m_new)
        c_new = wc[..., None, None] * c + jnp.einsum('bjh,bjhe,bjhd->bhed', wk, vc, kc)
        n_new = wc[..., None] * n + jnp.einsum('bjh,bjhd->bhd', wk, kc)
        return (c_new, n_new, m_new), h

    xs = (chunks(q), chunks(k), chunks(v), chunks(log_i), chunks(log_f))
    (c, n, m), hs = lax.scan(step, (c0, n0, m0), xs)
    return jnp.moveaxis(hs, 0, 1).reshape(b, L, H, v.shape[-1]), c, n, m


def mlstm_branch(h, w_in, conv_w, conv_b, gate_b, st_f, st_b):
    b, L, _ = h.shape
    f32 = jnp.float32
    qk, v, o, gates = jnp.split(h @ w_in, ODD_SPLITS, axis=-1)
    qk = jax.nn.silu(dwconv_same(qk, conv_w, conv_b)).astype(f32)
    q, k = jnp.split(qk, 2, axis=-1)
    q = q.reshape(b, L, ML_HEADS, ML_QK_DIM) * ML_QK_DIM ** -0.5
    k = k.reshape(b, L, ML_HEADS, ML_QK_DIM)
    v = v.astype(f32).reshape(b, L, ML_HEADS, ML_V_DIM)
    gates = gates.astype(f32).reshape(b, L, 2, 2, ML_HEADS) + gate_b.astype(f32)
    log_i = gates[:, :, 0]
    log_f = jax.nn.log_sigmoid(gates[:, :, 1])
    flip = lambda t: jnp.flip(t, axis=1)
    hf, cf, nf, mf = mlstm_scan(q, k, v, log_i[:, :, 0], log_f[:, :, 0], st_f[0], st_f[1], st_f[2])
    hb, cb, nb, mb = mlstm_scan(flip(q), flip(k), flip(v), flip(log_i[:, :, 1]), flip(log_f[:, :, 1]),
                                st_b[0], st_b[1], st_b[2])
    y = (hf + flip(hb)).reshape(b, L, ML_V_WIDTH).astype(h.dtype) * jax.nn.sigmoid(o)
    return y, (cf, nf, mf), (cb, nb, mb)


def setup_inputs(seed: int = 0) -> dict:
    key = jax.random.key(seed)
    ks = jax.random.split(key, 40)
    f32 = jnp.float32
    nrm = lambda i, shape, s: s * jax.random.normal(ks[i], shape, f32)
    gain = lambda i, shape: 1.0 + 0.05 * jax.random.normal(ks[i], shape, f32)
    dt0 = jnp.exp(jax.random.uniform(ks[20], (N_EVEN, 2, SSD_HEADS), f32, math.log(1e-3), math.log(1e-1)))
    i_bias = nrm(30, (N_ODD, 1, 2, ML_HEADS), 0.1)
    f_bias = 3.0 + 3.0 * jax.random.uniform(ks[31], (N_ODD, 1, 2, ML_HEADS), f32)
    return {
        'x_prompt': nrm(0, (BATCH, SEQ, D_MODEL), 1.0),
        'x_sample': nrm(1, (DEC_BATCH, DEC_SEQ, D_MODEL), 1.0),
        'c': nrm(2, (DEC_BATCH, D_MODEL), 1.0),
        'cache_na_k': nrm(3, (DEC_BATCH, N_EVEN, NA_HEADS, PAST_LEN, NA_HEAD_DIM), 1.0),
        'cache_na_v': nrm(4, (DEC_BATCH, N_EVEN, NA_HEADS, PAST_LEN, NA_HEAD_DIM), 1.0),
        'state_ssd': nrm(5, (DEC_BATCH, N_EVEN, 2, SSD_HEADS, SSD_HEAD_DIM, SSD_STATE), 0.5),
        'state_mlstm_c': nrm(6, (DEC_BATCH, N_ODD, 2, ML_HEADS, ML_V_DIM, ML_QK_DIM), 0.5),
        'state_mlstm_n': nrm(7, (DEC_BATCH, N_ODD, 2, ML_HEADS, ML_QK_DIM), 0.5),
        'state_mlstm_m': nrm(8, (DEC_BATCH, N_ODD, 2, ML_HEADS), 1.0),
        'c_ctx': nrm(9, (D_MODEL,), 1.0),
        'w_mod': nrm(10, (DEPTH, D_MODEL, N_MOD * D_MODEL), 0.5 * D_MODEL ** -0.5),
        'b_mod': nrm(11, (DEPTH, N_MOD * D_MODEL), 0.02),
        'norm_mix': gain(12, (DEPTH, D_MODEL)),
        'norm_ffn': gain(13, (DEPTH, D_MODEL)),
        'w_in_even': nrm(14, (N_EVEN, D_MODEL, EVEN_IN), D_MODEL ** -0.5),
        'w_out_even': nrm(15, (N_EVEN, EVEN_MIX, D_MODEL), EVEN_MIX ** -0.5),
        'na_rpb': nrm(16, (N_EVEN, NA_HEADS, NA_RPB_ROWS, NA_RPB_COLS), 0.5),
        'ssd_conv_w': nrm(17, (N_EVEN, CONV_K, SSD_CONV_DIM), CONV_K ** -0.5),
        'ssd_conv_b': nrm(18, (N_EVEN, SSD_CONV_DIM), 0.02),
        'ssd_dt_bias': dt0 + jnp.log(-jnp.expm1(-dt0)),
        'ssd_a_log': jnp.log(jax.random.uniform(ks[21], (N_EVEN, 2, SSD_HEADS), f32, 1.0, 16.0)),
        'ssd_d': 1.0 + nrm(22, (N_EVEN, SSD_HEADS), 0.1),
        'ssd_norm': gain(23, (N_EVEN, SSD_INNER)),
        'w_in_odd': nrm(24, (N_ODD, D_MODEL, ODD_IN), D_MODEL ** -0.5),
        'w_out_odd': nrm(25, (N_ODD, ML_V_WIDTH, D_MODEL), ML_V_WIDTH ** -0.5),
        'ml_conv_w': nrm(26, (N_ODD, CONV_K, 2 * ML_QK_WIDTH), CONV_K ** -0.5),
        'ml_conv_b': nrm(27, (N_ODD, 2 * ML_QK_WIDTH), 0.02),
        'ml_gate_b': jnp.concatenate([i_bias, f_bias], axis=1),
        'w_ff1': nrm(32, (DEPTH, D_MODEL, D_FF), D_MODEL ** -0.5),
        'w_ff2': nrm(33, (DEPTH, D_FF, D_MODEL), D_FF ** -0.5),
        'norm_f': gain(34, (D_MODEL,)),
    }


def reference(x_prompt, x_sample, c, cache_na_k, cache_na_v, state_ssd, state_mlstm_c, state_mlstm_n, state_mlstm_m,
              c_ctx, w_mod, b_mod, norm_mix, norm_ffn, w_in_even, w_out_even, na_rpb, ssd_conv_w, ssd_conv_b,
              ssd_dt_bias, ssd_a_log, ssd_d, ssd_norm, w_in_odd, w_out_odd, ml_conv_w, ml_conv_b, ml_gate_b,
              w_ff1, w_ff2, norm_f):
    f32 = jnp.float32
    bp = x_prompt.shape[0]
    xp, xs = x_prompt, x_sample
    cond_ctx = c_ctx[None, :]
    out_k, out_v, out_ssd, out_c, out_n, out_m = [], [], [], [], [], []
    for l in range(DEPTH):
        sh_p, sc_p, g_p, sh2_p, sc2_p, g2_p = adaln_params(cond_ctx, w_mod[l], b_mod[l])
        sh_s, sc_s, g_s, sh2_s, sc2_s, g2_s = adaln_params(c, w_mod[l], b_mod[l])
        hp = modulate(xp, norm_mix[l], sh_p, sc_p)
        hs = modulate(xs, norm_mix[l], sh_s, sc_s)
        if l % 2 == 0:
            e = l // 2
            ssd_w = (ssd_conv_w[e], ssd_conv_b[e], ssd_dt_bias[e], ssd_a_log[e], ssd_d[e], ssd_norm[e])
            q, k, v, z, xbc, dt = even_inputs(hp, w_in_even[e])
            zero = jnp.zeros((bp, SSD_HEADS, SSD_HEAD_DIM, SSD_STATE), hp.dtype)
            y_ssd, s_f, s_b = ssd_branch(z, xbc, dt, *ssd_w, zero, zero)
            mp = jnp.concatenate([ctx_attention(q, k, v), y_ssd], axis=-1) @ w_out_even[e]
            out_k.append(jnp.swapaxes(k, 1, 2))
            out_v.append(jnp.swapaxes(v, 1, 2))
            out_ssd.append(jnp.stack([s_f, s_b], axis=1))
            q, k, v, z, xbc, dt = even_inputs(hs, w_in_even[e])
            y_ssd, _, _ = ssd_branch(z, xbc, dt, *ssd_w, state_ssd[:, e, 0], state_ssd[:, e, 1])
            y_na = na_latent(q, k, v, cache_na_k[:, e], cache_na_v[:, e], na_rpb[e])
            ms = jnp.concatenate([y_na, y_ssd], axis=-1) @ w_out_even[e]
        else:
            o = l // 2
            ml_w = (w_in_odd[o], ml_conv_w[o], ml_conv_b[o], ml_gate_b[o])
            zc = jnp.zeros((bp, ML_HEADS, ML_V_DIM, ML_QK_DIM), f32)
            zn = jnp.zeros((bp, ML_HEADS, ML_QK_DIM), f32)
            zm = jnp.zeros((bp, ML_HEADS), f32)
            y, (cf, nf, mf), (cb, nb, mb) = mlstm_branch(hp, *ml_w, (zc, zn, zm), (zc, zn, zm))
            mp = y @ w_out_odd[o]
            out_c.append(jnp.stack([cf, cb], axis=1).astype(xp.dtype))
            out_n.append(jnp.stack([nf, nb], axis=1).astype(xp.dtype))
            out_m.append(jnp.stack([mf, mb], axis=1).astype(xp.dtype))
            st_f = (state_mlstm_c[:, o, 0].astype(f32), state_mlstm_n[:, o, 0].astype(f32), state_mlstm_m[:, o, 0].astype(f32))
            st_b = (state_mlstm_c[:, o, 1].astype(f32), state_mlstm_n[:, o, 1].astype(f32), state_mlstm_m[:, o, 1].astype(f32))
            y, _, _ = mlstm_branch(hs, *ml_w, st_f, st_b)
            ms = y @ w_out_odd[o]
        xp = xp + g_p[:, None, :] * mp
        xs = xs + g_s[:, None, :] * ms
        xp = xp + g2_p[:, None, :] * sqrelu_mlp(modulate(xp, norm_ffn[l], sh2_p, sc2_p), w_ff1[l], w_ff2[l])
        xs = xs + g2_s[:, None, :] * sqrelu_mlp(modulate(xs, norm_ffn[l], sh2_s, sc2_s), w_ff1[l], w_ff2[l])
    y_prompt = rmsnorm(xp, norm_f)
    y_sample = rmsnorm(xs, norm_f)
    new_na_k = jnp.stack(out_k, axis=1)
    new_na_v = jnp.stack(out_v, axis=1)
    new_ssd = jnp.stack(out_ssd, axis=1)
    new_mlstm_c = jnp.stack(out_c, axis=1)
    new_mlstm_n = jnp.stack(out_n, axis=1)
    new_mlstm_m = jnp.stack(out_m, axis=1)
    return (y_prompt, y_sample, new_na_k, new_na_v, new_ssd, new_mlstm_c, new_mlstm_n, new_mlstm_m)
```

```python
import functools
import math

import jax
import jax.numpy as jnp
from jax import lax
from jax.experimental import pallas as pl
from jax.experimental.pallas import tpu as pltpu

D_MODEL = 1024
BATCH = 32
SEQ = 256
DEPTH = 4
DEC_BATCH = 2
DEC_SEQ = 4096
PAST_LEN = 256
GRID_W = 64
N_EVEN = (DEPTH + 1) // 2
N_ODD = DEPTH // 2
RMS_EPS = 1e-6
N_MOD = 6
D_FF = 4 * D_MODEL
CONV_K = 3
Q_BLOCK = 128
NA_HEADS = 8
NA_HEAD_DIM = 64
NA_WIDTH = NA_HEADS * NA_HEAD_DIM
NA_WIN_ROWS = 8
NA_WIN_COLS = 16
NA_RPB_ROWS = 2 * NA_WIN_ROWS - 1
NA_RPB_COLS = 2 * NA_WIN_COLS - 1
SSD_INNER = D_MODEL
SSD_HEAD_DIM = 64
SSD_HEADS = SSD_INNER // SSD_HEAD_DIM
SSD_GROUPS = 2
SSD_RPG = SSD_HEADS // SSD_GROUPS
SSD_STATE = 128
SSD_GN = SSD_GROUPS * SSD_STATE
SSD_CONV_DIM = SSD_INNER + 2 * SSD_GN
SSD_CHUNK = 128
ML_HEADS = 8
ML_QK_DIM = D_MODEL // 16
ML_V_DIM = D_MODEL // 8
ML_QK_WIDTH = ML_HEADS * ML_QK_DIM
ML_V_WIDTH = ML_HEADS * ML_V_DIM
ML_CHUNK = 64
EVEN_MIX = NA_WIDTH + SSD_INNER

N_PROMPT = BATCH * SEQ
N_TOK = N_PROMPT + DEC_BATCH * DEC_SEQ
N_COND = 1 + DEC_BATCH
LANES = 128
SUBLANES = 8
VMEM_LIMIT = 56 * 1024 * 1024
TM = 512

f32 = jnp.float32
bf16 = jnp.bfloat16


def _cond_row(i, tm):
    start = i * tm
    return jnp.where(start < N_PROMPT, 0, (start - N_PROMPT) // DEC_SEQ + 1)


def _const_spec(shape):
    nd = len(shape)
    return pl.BlockSpec(shape, lambda i: (0,) * nd, pipeline_mode=pl.Buffered(1))


def _rms(x):
    return x * lax.rsqrt(jnp.mean(x * x, axis=-1, keepdims=True) + RMS_EPS)


def _modulated(x, g, mod, k):
    shift = mod[:, k * D_MODEL:(k + 1) * D_MODEL]
    scale = mod[:, (k + 1) * D_MODEL:(k + 2) * D_MODEL]
    return (_rms(x) * g) * (1.0 + scale) + shift


def _mod_kernel(c_ref, w_ref, b_ref, o_ref):
    c = c_ref[...]
    a = (c * jax.nn.sigmoid(c)).astype(bf16)
    o_ref[...] = jnp.dot(a, w_ref[...].astype(bf16), preferred_element_type=f32) + b_ref[...]


def adaln_all(cond, w_mod, b_mod):
    tn = 1536
    nj = N_MOD * D_MODEL // tn
    return pl.pallas_call(
        _mod_kernel,
        grid=(DEPTH, nj),
        in_specs=[
            pl.BlockSpec((SUBLANES, D_MODEL), lambda l, j: (0, 0)),
            pl.BlockSpec((None, D_MODEL, tn), lambda l, j: (l, 0, j)),
            pl.BlockSpec((None, 1, tn), lambda l, j: (l, 0, j)),
        ],
        out_specs=pl.BlockSpec((None, SUBLANES, tn), lambda l, j: (l, 0, j)),
        out_shape=jax.ShapeDtypeStruct((DEPTH, SUBLANES, N_MOD * D_MODEL), f32),
        compiler_params=pltpu.CompilerParams(
            dimension_semantics=("arbitrary", "arbitrary"), vmem_limit_bytes=VMEM_LIMIT),
        name="adaln",
    )(cond, w_mod, b_mod.reshape(DEPTH, 1, N_MOD * D_MODEL))


def _in_proj_kernel(segs, x_ref, mod_ref, g_ref, w_ref, *o_refs):
    h = _modulated(x_ref[...], g_ref[...], mod_ref[...], 0).astype(bf16)
    for (a, b, width, scale), o_ref in zip(segs, o_refs):
        y = jnp.dot(h, w_ref[:, a:b], preferred_element_type=f32)
        if scale != 1.0:
            y = y * scale
        o_ref[...] = y[:, :width].astype(o_ref.dtype) if width != b - a else y.astype(o_ref.dtype)


def in_proj(x, mod_l, g, w, segs, dtypes):
    grid = (N_TOK // TM,)
    outs = [jax.ShapeDtypeStruct((N_TOK, s[2]), dt) for s, dt in zip(segs, dtypes)]
    return pl.pallas_call(
        functools.partial(_in_proj_kernel, segs),
        grid=grid,
        in_specs=[
            pl.BlockSpec((TM, D_MODEL), lambda i: (i, 0)),
            pl.BlockSpec((None, 1, N_MOD * D_MODEL), lambda i: (_cond_row(i, TM), 0, 0)),
            _const_spec((1, D_MODEL)),
            _const_spec(w.shape),
        ],
        out_specs=[pl.BlockSpec((TM, s[2]), lambda i: (i, 0)) for s in segs],
        out_shape=outs,
        compiler_params=pltpu.CompilerParams(
            dimension_semantics=("arbitrary",), vmem_limit_bytes=VMEM_LIMIT),
        name="in_proj",
    )(x, mod_l, g.reshape(1, D_MODEL), w)


def _out_mlp_kernel(n_mix, final, x_ref, mod_ref, g_ref, *refs):
    mix_refs = refs[:n_mix]
    wo_ref, w1_ref, w2_ref = refs[n_mix:n_mix + 3]
    rest = refs[n_mix + 3:]
    if final:
        gf_ref, o_ref = rest
    else:
        (o_ref,) = rest
    mod = mod_ref[...]
    m = None
    k0 = 0
    for r in mix_refs:
        kw = r.shape[-1]
        part = jnp.dot(r[...].astype(bf16), wo_ref[k0:k0 + kw, :], preferred_element_type=f32)
        m = part if m is None else m + part
        k0 += kw
    x1 = x_ref[...] + mod[:, 2 * D_MODEL:3 * D_MODEL] * m
    h2 = _modulated(x1, g_ref[...], mod, 3).astype(bf16)
    u = jnp.dot(h2, w1_ref[...], preferred_element_type=f32)
    a = jnp.square(jnp.maximum(u, 0.0)).astype(bf16)
    x2 = x1 + mod[:, 5 * D_MODEL:6 * D_MODEL] * jnp.dot(a, w2_ref[...], preferred_element_type=f32)
    if final:
        x2 = _rms(x2) * gf_ref[...]
    o_ref[...] = x2


def out_mlp(x, mod_l, g_ffn, mixes, w_out, w1, w2, norm_f=None):
    final = norm_f is not None
    tm = 256
    grid = (N_TOK // tm,)
    in_specs = [
        pl.BlockSpec((tm, D_MODEL), lambda i: (i, 0)),
        pl.BlockSpec((None, 1, N_MOD * D_MODEL), lambda i: (_cond_row(i, tm), 0, 0)),
        _const_spec((1, D_MODEL)),
    ]
    in_specs += [pl.BlockSpec((tm, m.shape[-1]), lambda i: (i, 0)) for m in mixes]
    in_specs += [_const_spec(w_out.shape), _const_spec(w1.shape), _const_spec(w2.shape)]
    args = [x, mod_l, g_ffn.reshape(1, D_MODEL), *mixes, w_out, w1, w2]
    if final:
        in_specs.append(_const_spec((1, D_MODEL)))
        args.append(norm_f.reshape(1, D_MODEL))
    return pl.pallas_call(
        functools.partial(_out_mlp_kernel, len(mixes), final),
        grid=grid,
        in_specs=in_specs,
        out_specs=pl.BlockSpec((tm, D_MODEL), lambda i: (i, 0)),
        out_shape=jax.ShapeDtypeStruct((N_TOK, D_MODEL), f32),
        compiler_params=pltpu.CompilerParams(
            dimension_semantics=("arbitrary",), vmem_limit_bytes=VMEM_LIMIT),
        name="out_mlp",
    )(*args)


def rmsnorm(x, g):
    xf = x.astype(jnp.float32)
    y = xf * lax.rsqrt(jnp.mean(xf * xf, axis=-1, keepdims=True) + RMS_EPS)
    return (y * g.astype(jnp.float32)).astype(x.dtype)


def dwconv_same(x, w, b):
    L = x.shape[1]
    pad = CONV_K // 2
    xp = jnp.pad(x, ((0, 0), (pad, pad), (0, 0)))
    y = xp[:, 0:L] * w[0]
    for j in range(1, CONV_K):
        y = y + xp[:, j:j + L] * w[j]
    return y + b


def ctx_attention(q, k, v):
    b, S, H, dh = q.shape
    qb = jnp.moveaxis(q.reshape(b, S // Q_BLOCK, Q_BLOCK, H, dh), 1, 0)

    def block(qi):
        s = jnp.einsum('bqhd,bkhd->bhqk', qi, k).astype(jnp.float32)
        p = jax.nn.softmax(s, axis=-1).astype(v.dtype)
        return jnp.einsum('bhqk,bkhd->bqhd', p, v)

    o = lax.map(block, qb)
    return jnp.moveaxis(o, 0, 1).reshape(b, S, H * dh)


def na_latent(q, k, v, k_ctx, v_ctx, rpb):
    b, L, H, dh = q.shape
    rows = L // GRID_W
    wr = min(NA_WIN_ROWS, rows)
    r = jnp.arange(rows)
    key_rows = jnp.clip(r - NA_WIN_ROWS // 2, 0, rows - wr)[:, None] + jnp.arange(wr)[None]
    col = jnp.arange(GRID_W)
    col_start = jnp.clip(col - NA_WIN_COLS // 2, 0, GRID_W - NA_WIN_COLS)
    col_in = (col[None, :] >= col_start[:, None]) & (col[None, :] < col_start[:, None] + NA_WIN_COLS)
    rel_r = key_rows - r[:, None] + NA_WIN_ROWS - 1
    rel_c = jnp.clip(col[None, :] - col[:, None] + NA_WIN_COLS - 1, 0, NA_RPB_COLS - 1)
    bias = rpb[:, rel_r[:, None, :, None], rel_c[None, :, None, :]].astype(jnp.float32)
    bias = jnp.where(col_in[None, None, :, None, :], bias, -jnp.inf).reshape(H, rows, GRID_W, wr * GRID_W)
    qg = q.reshape(b, rows, GRID_W, H, dh)
    kg = k.reshape(b, rows, GRID_W, H, dh)[:, key_rows].reshape(b, rows, wr * GRID_W, H, dh)
    vg = v.reshape(b, rows, GRID_W, H, dh)[:, key_rows].reshape(b, rows, wr * GRID_W, H, dh)
    s_loc = jnp.einsum('brqhd,brkhd->bhrqk', qg, kg).astype(jnp.float32) + bias[None]
    s_ctx = jnp.einsum('brqhd,bhcd->bhrqc', qg, k_ctx).astype(jnp.float32)
    p = jax.nn.softmax(jnp.concatenate([s_loc, s_ctx], axis=-1), axis=-1).astype(v.dtype)
    n_loc = wr * GRID_W
    o = jnp.einsum('bhrqk,brkhd->brqhd', p[..., :n_loc], vg) + jnp.einsum('bhrqc,bhcd->brqhd', p[..., n_loc:], v_ctx)
    return o.reshape(b, L, H * dh)


def ssd_scan(x, dt, A, Bm, Cm, h0):
    b, L, G, R, P = x.shape
    nc = L // SSD_CHUNK
    chunks = lambda t: jnp.moveaxis(t.reshape((b, nc, SSD_CHUNK) + t.shape[2:]), 1, 0)
    causal = jnp.tril(jnp.ones((SSD_CHUNK, SSD_CHUNK), bool))[None, :, :, None, None]

    def step(h, inp):
        xc, dtc, Bc, Cc = inp
        cum = jnp.cumsum(dtc * A, axis=1)
        seg = jnp.where(causal, cum[:, :, None] - cum[:, None], -jnp.inf)
        wts = jnp.einsum('bign,bjgn->bijg', Cc, Bc)[..., None] * jnp.exp(seg) * dtc[:, None]
        y = jnp.einsum('bijgr,bjgrp->bigrp', wts, xc)
        y = y + jnp.einsum('bign,bgrpn->bigrp', Cc, h) * jnp.exp(cum)[..., None]
        w_end = jnp.exp(cum[:, -1:] - cum) * dtc
        h = h * jnp.exp(cum[:, -1])[..., None, None] + jnp.einsum('bjgn,bjgr,bjgrp->bgrpn', Bc, w_end, xc)
        return h, y

    h, ys = lax.scan(step, h0, (chunks(x), chunks(dt), chunks(Bm), chunks(Cm)))
    return jnp.moveaxis(ys, 0, 1).reshape(b, L, G, R, P), h


def ssd_branch(z, xbc, dt_raw, conv_w, conv_b, dt_bias, a_log, d_skip, norm_g, h0_f, h0_b):
    b, L, _ = z.shape
    xbc = jax.nn.silu(dwconv_same(xbc, conv_w, conv_b)).astype(f32)
    xs, Bm, Cm = jnp.split(xbc, (SSD_INNER, SSD_INNER + SSD_GN), axis=-1)
    xs = xs.reshape(b, L, SSD_GROUPS, SSD_RPG, SSD_HEAD_DIM)
    Bm = Bm.reshape(b, L, SSD_GROUPS, SSD_STATE)
    Cm = Cm.reshape(b, L, SSD_GROUPS, SSD_STATE)
    dt = jax.nn.softplus(dt_raw.astype(f32).reshape(b, L, 2, SSD_GROUPS, SSD_RPG)
                         + dt_bias.astype(f32).reshape(2, SSD_GROUPS, SSD_RPG))
    A = -jnp.exp(a_log.astype(f32)).reshape(2, SSD_GROUPS, SSD_RPG)
    st = lambda h: h.astype(f32).reshape(b, SSD_GROUPS, SSD_RPG, SSD_HEAD_DIM, SSD_STATE)
    flip = lambda t: jnp.flip(t, axis=1)
    y_f, hf = ssd_scan(xs, dt[:, :, 0], A[0], Bm, Cm, st(h0_f))
    y_b, hb = ssd_scan(flip(xs), flip(dt[:, :, 1]), A[1], flip(Bm), flip(Cm), st(h0_b))
    y = y_f + flip(y_b) + d_skip.astype(f32).reshape(SSD_GROUPS, SSD_RPG, 1) * xs
    y = y.reshape(b, L, SSD_INNER).astype(z.dtype)
    y = rmsnorm(y * jax.nn.silu(z), norm_g)
    unst = lambda h: h.reshape(b, SSD_HEADS, SSD_HEAD_DIM, SSD_STATE).astype(z.dtype)
    return y, unst(hf), unst(hb)


def mlstm_scan(q, k, v, log_i, log_f, c0, n0, m0):
    b, L, H, _ = q.shape
    nc = L // ML_CHUNK
    chunks = lambda t: jnp.moveaxis(t.reshape((b, nc, ML_CHUNK) + t.shape[2:]), 1, 0)
    causal = jnp.tril(jnp.ones((ML_CHUNK, ML_CHUNK), bool))[None, :, :, None]

    def step(carry, inp):
        c, n, m = carry
        qc, kc, vc, li, lf = inp
        cum = jnp.cumsum(lf, axis=1)
        dmat = jnp.where(causal, cum[:, :, None] - cum[:, None] + li[:, None], -jnp.inf)
        m_inter = cum + m[:, None]
        m_t = jnp.maximum(m_inter, jnp.max(dmat, axis=2))
        w_intra = jnp.exp(dmat - m_t[:, :, None])
        w_inter = jnp.exp(m_inter - m_t)
        s = jnp.einsum('bihd,bjhd->bijh', qc, kc) * w_intra
        num = jnp.einsum('bijh,bjhe->bihe', s, vc) + w_inter[..., None] * jnp.einsum('bhed,bihd->bihe', c, qc)
        den = jnp.sum(s, axis=2) + w_inter * jnp.einsum('bhd,bihd->bih', n, qc)
        h = num / jnp.maximum(jnp.abs(den), jnp.exp(-m_t))[..., None]
        g = cum[:, -1:] - cum + li
        m_new = jnp.maximum(cum[:, -1] + m, jnp.max(g, axis=1))
        wk = jnp.exp(g - m_new[:, None])
        wc = jnp.exp(cum[:, -1] + m - m_new)
        c_new = wc[..., None, None] * c + jnp.einsum('bjh,bjhe,bjhd->bhed', wk, vc, kc)
        n_new = wc[..., None] * n + jnp.einsum('bjh,bjhd->bhd', wk, kc)
        return (c_new, n_new, m_new), h

    xs = (chunks(q), chunks(k), chunks(v), chunks(log_i), chunks(log_f))
    (c, n, m), hs = lax.scan(step, (c0, n0, m0), xs)
    return jnp.moveaxis(hs, 0, 1).reshape(b, L, H, v.shape[-1]), c, n, m


def mlstm_mix(qk, v, o, gates, conv_w, conv_b, gate_b, st_f, st_b):
    b, L, _ = qk.shape
    qk = jax.nn.silu(dwconv_same(qk, conv_w, conv_b)).astype(f32)
    q, k = jnp.split(qk, 2, axis=-1)
    q = q.reshape(b, L, ML_HEADS, ML_QK_DIM) * ML_QK_DIM ** -0.5
    k = k.reshape(b, L, ML_HEADS, ML_QK_DIM)
    v = v.astype(f32).reshape(b, L, ML_HEADS, ML_V_DIM)
    gates = gates.astype(f32).reshape(b, L, 2, 2, ML_HEADS) + gate_b.astype(f32)
    log_i = gates[:, :, 0]
    log_f = jax.nn.log_sigmoid(gates[:, :, 1])
    flip = lambda t: jnp.flip(t, axis=1)
    hf, cf, nf, mf = mlstm_scan(q, k, v, log_i[:, :, 0], log_f[:, :, 0], st_f[0], st_f[1], st_f[2])
    hb, cb, nb, mb = mlstm_scan(flip(q), flip(k), flip(v), flip(log_i[:, :, 1]), flip(log_f[:, :, 1]),
                                st_b[0], st_b[1], st_b[2])
    y = (hf + flip(hb)).reshape(b, L, ML_V_WIDTH) * jax.nn.sigmoid(o)
    return y, (cf, nf, mf), (cb, nb, mb)


def _split_tokens(t):
    w = t.shape[-1]
    return t[:N_PROMPT].reshape(BATCH, SEQ, w), t[N_PROMPT:].reshape(DEC_BATCH, DEC_SEQ, w)


def _join_tokens(p, s):
    w = p.shape[-1]
    return jnp.concatenate([p.reshape(N_PROMPT, w), s.reshape(DEC_BATCH * DEC_SEQ, w)], axis=0)


def kernel(x_prompt, x_sample, c, cache_na_k, cache_na_v, state_ssd, state_mlstm_c, state_mlstm_n, state_mlstm_m,
           c_ctx, w_mod, b_mod, norm_mix, norm_ffn, w_in_even, w_out_even, na_rpb, ssd_conv_w, ssd_conv_b,
           ssd_dt_bias, ssd_a_log, ssd_d, ssd_norm, w_in_odd, w_out_odd, ml_conv_w, ml_conv_b, ml_gate_b,
           w_ff1, w_ff2, norm_f):
    x = _join_tokens(x_prompt, x_sample)
    cond = jnp.concatenate([c_ctx[None, :], c, jnp.zeros((SUBLANES - N_COND, D_MODEL), f32)], axis=0)
    mod = adaln_all(cond, w_mod, b_mod)[:, :N_COND].reshape(DEPTH, N_COND, 1, N_MOD * D_MODEL)

    out_k, out_v, out_ssd, out_c, out_n, out_m = [], [], [], [], [], []
    for l in range(DEPTH):
        last = l == DEPTH - 1
        w1 = w_ff1[l].astype(bf16)
        w2 = w_ff2[l].astype(bf16)
        if l % 2 == 0:
            e = l // 2
            w = w_in_even[e]
            pad = jnp.zeros((D_MODEL, LANES - 2 * SSD_HEADS), f32)
            w = jnp.concatenate([w, pad], axis=1).astype(bf16)
            o0 = 3 * NA_WIDTH
            segs = ((0, NA_WIDTH, NA_WIDTH, NA_HEAD_DIM ** -0.5),
                    (NA_WIDTH, 2 * NA_WIDTH, NA_WIDTH, 1.0),
                    (2 * NA_WIDTH, 3 * NA_WIDTH, NA_WIDTH, 1.0),
                    (o0, o0 + SSD_INNER, SSD_INNER, 1.0),
                    (o0 + SSD_INNER, o0 + SSD_INNER + SSD_CONV_DIM, SSD_CONV_DIM, 1.0),
                    (o0 + SSD_INNER + SSD_CONV_DIM, o0 + SSD_INNER + SSD_CONV_DIM + LANES, 2 * SSD_HEADS, 1.0))
            q, k, v, z, xbc, dt = in_proj(x, mod[l], norm_mix[l], w, segs, (f32,) * 6)
            qp, qs = _split_tokens(q)
            kp, ks = _split_tokens(k)
            vp, vs = _split_tokens(v)
            zp, zs = _split_tokens(z)
            xbcp, xbcs = _split_tokens(xbc)
            dtp, dts = _split_tokens(dt)
            ssd_w = (ssd_conv_w[e], ssd_conv_b[e], ssd_dt_bias[e], ssd_a_log[e], ssd_d[e], ssd_norm[e])
            heads = lambda t: t.reshape(t.shape[0], t.shape[1], NA_HEADS, NA_HEAD_DIM)
            zero = jnp.zeros((BATCH, SSD_HEADS, SSD_HEAD_DIM, SSD_STATE), f32)
            y_ssd_p, s_f, s_b = ssd_branch(zp, xbcp, dtp, *ssd_w, zero, zero)
            y_na_p = ctx_attention(heads(qp), heads(kp), heads(vp))
            out_k.append(jnp.swapaxes(heads(kp), 1, 2))
            out_v.append(jnp.swapaxes(heads(vp), 1, 2))
            out_ssd.append(jnp.stack([s_f, s_b], axis=1))
            y_ssd_s, _, _ = ssd_branch(zs, xbcs, dts, *ssd_w, state_ssd[:, e, 0], state_ssd[:, e, 1])
            y_na_s = na_latent(heads(qs), heads(ks), heads(vs), cache_na_k[:, e], cache_na_v[:, e], na_rpb[e])
            mixes = [_join_tokens(y_na_p, y_na_s), _join_tokens(y_ssd_p, y_ssd_s)]
            w_out = w_out_even[e].astype(bf16)
        else:
            o = l // 2
            w = w_in_odd[o]
            pad = jnp.zeros((D_MODEL, LANES - 4 * ML_HEADS), f32)
            w = jnp.concatenate([w, pad], axis=1).astype(bf16)
            a0 = 2 * ML_QK_WIDTH
            segs = ((0, a0, a0, 1.0),
                    (a0, a0 + ML_V_WIDTH, ML_V_WIDTH, 1.0),
                    (a0 + ML_V_WIDTH, a0 + 2 * ML_V_WIDTH, ML_V_WIDTH, 1.0),
                    (a0 + 2 * ML_V_WIDTH, a0 + 2 * ML_V_WIDTH + LANES, 4 * ML_HEADS, 1.0))
            qk, v, og, gates = in_proj(x, mod[l], norm_mix[l], w, segs, (f32,) * 4)
            qkp, qks = _split_tokens(qk)
            vp, vs = _split_tokens(v)
            ogp, ogs = _split_tokens(og)
            gp, gs = _split_tokens(gates)
            ml_w = (ml_conv_w[o], ml_conv_b[o], ml_gate_b[o])
            zc = jnp.zeros((BATCH, ML_HEADS, ML_V_DIM, ML_QK_DIM), f32)
            zn = jnp.zeros((BATCH, ML_HEADS, ML_QK_DIM), f32)
            zm = jnp.zeros((BATCH, ML_HEADS), f32)
            y_p, (cf, nf, mf), (cb, nb, mb) = mlstm_mix(qkp, vp, ogp, gp, *ml_w, (zc, zn, zm), (zc, zn, zm))
            out_c.append(jnp.stack([cf, cb], axis=1))
            out_n.append(jnp.stack([nf, nb], axis=1))
            out_m.append(jnp.stack([mf, mb], axis=1))
            st_f = (state_mlstm_c[:, o, 0], state_mlstm_n[:, o, 0], state_mlstm_m[:, o, 0])
            st_b = (state_mlstm_c[:, o, 1], state_mlstm_n[:, o, 1], state_mlstm_m[:, o, 1])
            y_s, _, _ = mlstm_mix(qks, vs, ogs, gs, *ml_w, st_f, st_b)
            mixes = [_join_tokens(y_p, y_s)]
            w_out = w_out_odd[o].astype(bf16)
        x = out_mlp(x, mod[l], norm_ffn[l], mixes, w_out, w1, w2, norm_f if last else None)

    y_prompt = x[:N_PROMPT].reshape(BATCH, SEQ, D_MODEL)
    y_sample = x[N_PROMPT:].reshape(DEC_BATCH, DEC_SEQ, D_MODEL)
    return (y_prompt, y_sample, jnp.stack(out_k, axis=1), jnp.stack(out_v, axis=1), jnp.stack(out_ssd, axis=1),
            jnp.stack(out_c, axis=1), jnp.stack(out_n, axis=1), jnp.stack(out_m, axis=1))
```

```python
import functools
import math

import jax
import jax.numpy as jnp
from jax import lax
from jax.experimental import pallas as pl
from jax.experimental.pallas import tpu as pltpu

D_MODEL = 1024
BATCH = 32
SEQ = 256
DEPTH = 4
DEC_BATCH = 2
DEC_SEQ = 4096
PAST_LEN = 256
GRID_W = 64
N_EVEN = (DEPTH + 1) // 2
N_ODD = DEPTH // 2
RMS_EPS = 1e-6
N_MOD = 6
D_FF = 4 * D_MODEL
CONV_K = 3
Q_BLOCK = 128
NA_HEADS = 8
NA_HEAD_DIM = 64
NA_WIDTH = NA_HEADS * NA_HEAD_DIM
NA_WIN_ROWS = 8
NA_WIN_COLS = 16
NA_RPB_ROWS = 2 * NA_WIN_ROWS - 1
NA_RPB_COLS = 2 * NA_WIN_COLS - 1
SSD_INNER = D_MODEL
SSD_HEAD_DIM = 64
SSD_HEADS = SSD_INNER // SSD_HEAD_DIM
SSD_GROUPS = 2
SSD_RPG = SSD_HEADS // SSD_GROUPS
SSD_STATE = 128
SSD_GN = SSD_GROUPS * SSD_STATE
SSD_CONV_DIM = SSD_INNER + 2 * SSD_GN
SSD_CHUNK = 128
ML_HEADS = 8
ML_QK_DIM = D_MODEL // 16
ML_V_DIM = D_MODEL // 8
ML_QK_WIDTH = ML_HEADS * ML_QK_DIM
ML_V_WIDTH = ML_HEADS * ML_V_DIM
ML_CHUNK = 64
EVEN_MIX = NA_WIDTH + SSD_INNER

N_PROMPT = BATCH * SEQ
N_TOK = N_PROMPT + DEC_BATCH * DEC_SEQ
N_COND = 1 + DEC_BATCH
LANES = 128
SUBLANES = 8
VMEM_LIMIT = 56 * 1024 * 1024
TM = 512

f32 = jnp.float32
bf16 = jnp.bfloat16


def _cond_row(i, tm):
    start = i * tm
    return jnp.where(start < N_PROMPT, 0, (start - N_PROMPT) // DEC_SEQ + 1)


def _const_spec(shape):
    nd = len(shape)
    return pl.BlockSpec(shape, lambda i: (0,) * nd, pipeline_mode=pl.Buffered(1))


def _rms(x):
    return x * lax.rsqrt(jnp.mean(x * x, axis=-1, keepdims=True) + RMS_EPS)


def _modulated(x, g, mod, k):
    shift = mod[:, k * D_MODEL:(k + 1) * D_MODEL]
    scale = mod[:, (k + 1) * D_MODEL:(k + 2) * D_MODEL]
    return (_rms(x) * g) * (1.0 + scale) + shift


def _mod_kernel(c_ref, w_ref, b_ref, o_ref):
    c = c_ref[...]
    a = (c * jax.nn.sigmoid(c)).astype(bf16)
    o_ref[...] = jnp.dot(a, w_ref[...].astype(bf16), preferred_element_type=f32) + b_ref[...]


def adaln_all(cond, w_mod, b_mod):
    tn = 1536
    nj = N_MOD * D_MODEL // tn
    return pl.pallas_call(
        _mod_kernel,
        grid=(DEPTH, nj),
        in_specs=[
            pl.BlockSpec((SUBLANES, D_MODEL), lambda l, j: (0, 0)),
            pl.BlockSpec((None, D_MODEL, tn), lambda l, j: (l, 0, j)),
            pl.BlockSpec((None, 1, tn), lambda l, j: (l, 0, j)),
        ],
        out_specs=pl.BlockSpec((None, SUBLANES, tn), lambda l, j: (l, 0, j)),
        out_shape=jax.ShapeDtypeStruct((DEPTH, SUBLANES, N_MOD * D_MODEL), f32),
        compiler_params=pltpu.CompilerParams(
            dimension_semantics=("arbitrary", "arbitrary"), vmem_limit_bytes=VMEM_LIMIT),
        name="adaln",
    )(cond, w_mod, b_mod.reshape(DEPTH, 1, N_MOD * D_MODEL))


def _in_proj_kernel(segs, outs, x_ref, mod_ref, g_ref, w_ref, *o_refs):
    h = _modulated(x_ref[...], g_ref[...], mod_ref[...], 0).astype(bf16)
    ys = []
    for a, b, width, scale in segs:
        y = jnp.dot(h, w_ref[:, a:b], preferred_element_type=f32)
        if scale != 1.0:
            y = y * scale
        ys.append(y[:, :width] if width != b - a else y)
    for (si, _, prompt_only), o_ref in zip(outs, o_refs):
        if prompt_only:
            @pl.when(pl.program_id(0) < N_PROMPT // TM)
            def _(o_ref=o_ref, si=si):
                o_ref[...] = ys[si].astype(o_ref.dtype)
        else:
            o_ref[...] = ys[si].astype(o_ref.dtype)


def in_proj(x, mod_l, g, w, segs, outs):
    grid = (N_TOK // TM,)
    last_prompt = N_PROMPT // TM - 1
    out_shape, out_specs = [], []
    for si, dt, prompt_only in outs:
        width = segs[si][2]
        if prompt_only:
            out_shape.append(jax.ShapeDtypeStruct((N_PROMPT, width), dt))
            out_specs.append(pl.BlockSpec((TM, width), lambda i: (jnp.minimum(i, last_prompt), 0)))
        else:
            out_shape.append(jax.ShapeDtypeStruct((N_TOK, width), dt))
            out_specs.append(pl.BlockSpec((TM, width), lambda i: (i, 0)))
    return pl.pallas_call(
        functools.partial(_in_proj_kernel, segs, outs),
        grid=grid,
        in_specs=[
            pl.BlockSpec((TM, D_MODEL), lambda i: (i, 0)),
            pl.BlockSpec((None, 1, N_MOD * D_MODEL), lambda i: (_cond_row(i, TM), 0, 0)),
            _const_spec((1, D_MODEL)),
            _const_spec(w.shape),
        ],
        out_specs=out_specs,
        out_shape=out_shape,
        compiler_params=pltpu.CompilerParams(
            dimension_semantics=("arbitrary",), vmem_limit_bytes=VMEM_LIMIT),
        name="in_proj",
    )(x, mod_l, g.reshape(1, D_MODEL), w)


def _out_mlp_kernel(n_mix, final, x_ref, mod_ref, g_ref, *refs):
    mix_refs = refs[:n_mix]
    wo_ref, w1_ref, w2_ref = refs[n_mix:n_mix + 3]
    rest = refs[n_mix + 3:]
    if final:
        gf_ref, o_ref = rest
    else:
        (o_ref,) = rest
    mod = mod_ref[...]
    m = None
    k0 = 0
    for r in mix_refs:
        kw = r.shape[-1]
        part = jnp.dot(r[...].astype(bf16), wo_ref[k0:k0 + kw, :], preferred_element_type=f32)
        m = part if m is None else m + part
        k0 += kw
    x1 = x_ref[...] + mod[:, 2 * D_MODEL:3 * D_MODEL] * m
    h2 = _modulated(x1, g_ref[...], mod, 3).astype(bf16)
    u = jnp.dot(h2, w1_ref[...], preferred_element_type=f32)
    a = jnp.square(jnp.maximum(u, 0.0)).astype(bf16)
    x2 = x1 + mod[:, 5 * D_MODEL:6 * D_MODEL] * jnp.dot(a, w2_ref[...], preferred_element_type=f32)
    if final:
        x2 = _rms(x2) * gf_ref[...]
    o_ref[...] = x2


def out_mlp(x, mod_l, g_ffn, mixes, w_out, w1, w2, norm_f=None):
    final = norm_f is not None
    tm = 256
    grid = (N_TOK // tm,)
    in_specs = [
        pl.BlockSpec((tm, D_MODEL), lambda i: (i, 0)),
        pl.BlockSpec((None, 1, N_MOD * D_MODEL), lambda i: (_cond_row(i, tm), 0, 0)),
        _const_spec((1, D_MODEL)),
    ]
    in_specs += [pl.BlockSpec((tm, m.shape[-1]), lambda i: (i, 0)) for m in mixes]
    in_specs += [_const_spec(w_out.shape), _const_spec(w1.shape), _const_spec(w2.shape)]
    args = [x, mod_l, g_ffn.reshape(1, D_MODEL), *mixes, w_out, w1, w2]
    if final:
        in_specs.append(_const_spec((1, D_MODEL)))
        args.append(norm_f.reshape(1, D_MODEL))
    return pl.pallas_call(
        functools.partial(_out_mlp_kernel, len(mixes), final),
        grid=grid,
        in_specs=in_specs,
        out_specs=pl.BlockSpec((tm, D_MODEL), lambda i: (i, 0)),
        out_shape=jax.ShapeDtypeStruct((N_TOK, D_MODEL), f32),
        compiler_params=pltpu.CompilerParams(
            dimension_semantics=("arbitrary",), vmem_limit_bytes=VMEM_LIMIT),
        name="out_mlp",
    )(*args)


NA_PAIRS = NA_HEADS // 2
NA_ROWS = DEC_SEQ // GRID_W
NA_WIN = NA_WIN_ROWS * GRID_W


def _dot_nt(a, b):
    return lax.dot_general(a, b, (((1,), (1,)), ((), ())), preferred_element_type=f32)


def _pair_stack(x):
    lane = lax.broadcasted_iota(jnp.int32, x.shape, 1)
    zero = jnp.zeros_like(x)
    return jnp.concatenate([jnp.where(lane < NA_HEAD_DIM, x, zero), jnp.where(lane >= NA_HEAD_DIM, x, zero)], axis=0)


def _pair_unstack(o):
    n = o.shape[0] // 2
    lane = lax.broadcasted_iota(jnp.int32, (n, LANES), 1)
    return jnp.where(lane < NA_HEAD_DIM, o[:n], o[n:])


def _softmax_pv(scores, values):
    m = None
    for s in scores:
        mi = jnp.max(s, axis=1, keepdims=True)
        m = mi if m is None else jnp.maximum(m, mi)
    ps = [jnp.exp(s - m) for s in scores]
    l = None
    for p in ps:
        li = jnp.sum(p, axis=1, keepdims=True)
        l = li if l is None else l + li
    inv = 1.0 / l
    o = None
    for p, v in zip(ps, values):
        oi = jnp.dot((p * inv).astype(bf16), v, preferred_element_type=f32)
        o = oi if o is None else o + oi
    return o


def _ctx_attn_kernel(q_ref, k_ref, v_ref, o_ref):
    for p in range(NA_PAIRS):
        lanes = slice(p * LANES, (p + 1) * LANES)
        qq = _pair_stack(q_ref[:, lanes])
        s = _dot_nt(qq, k_ref[:, lanes])
        o = _softmax_pv([s], [v_ref[:, lanes]])
        o_ref[:, lanes] = _pair_unstack(o).astype(o_ref.dtype)


def ctx_attention(q, k, v):
    spec = pl.BlockSpec((SEQ, NA_WIDTH), lambda b: (b, 0))
    return pl.pallas_call(
        _ctx_attn_kernel,
        grid=(BATCH,),
        in_specs=[spec, spec, spec],
        out_specs=spec,
        out_shape=jax.ShapeDtypeStruct((N_PROMPT, NA_WIDTH), bf16),
        compiler_params=pltpu.CompilerParams(dimension_semantics=("arbitrary",), vmem_limit_bytes=VMEM_LIMIT),
        name="ctx_attn",
    )(q, k, v)


def _na_bias_kernel(rpb_ref, o_ref):
    pair = pl.program_id(0)
    shape = (GRID_W, LANES)
    qc = lax.broadcasted_iota(jnp.int32, shape, 0)
    lane = lax.broadcasted_iota(jnp.int32, shape, 1)
    kc = lane & (GRID_W - 1)
    low = lane < GRID_W
    col_start = jnp.clip(qc - NA_WIN_COLS // 2, 0, GRID_W - NA_WIN_COLS)
    valid = (kc >= col_start) & (kc < col_start + NA_WIN_COLS)
    rel_c = jnp.clip(kc - qc + NA_WIN_COLS - 1, 0, NA_RPB_COLS - 1)
    for e in range(2):
        base = (2 * pair + e) * NA_RPB_ROWS
        pieces = []
        for rr in range(NA_RPB_ROWS - 1):
            val = jnp.zeros(shape, f32)
            for t in range(NA_RPB_COLS):
                s_lo = rpb_ref[(base + rr) * NA_RPB_COLS + t]
                s_hi = rpb_ref[(base + rr + 1) * NA_RPB_COLS + t]
                val = jnp.where(rel_c == t, jnp.where(low, s_lo, s_hi), val)
            pieces.append(jnp.where(valid, val, -jnp.inf))
        for d in range(NA_WIN_ROWS):
            for i in range(0, NA_WIN_ROWS, 2):
                o_ref[d, e * GRID_W:(e + 1) * GRID_W, i * GRID_W:(i + 2) * GRID_W] = pieces[d + i]


def na_bias_table(rpb):
    return pl.pallas_call(
        _na_bias_kernel,
        grid=(NA_PAIRS,),
        in_specs=[pl.BlockSpec(memory_space=pltpu.SMEM)],
        out_specs=pl.BlockSpec((None, NA_WIN_ROWS, 2 * GRID_W, NA_WIN), lambda p: (p, 0, 0, 0)),
        out_shape=jax.ShapeDtypeStruct((NA_PAIRS, NA_WIN_ROWS, 2 * GRID_W, NA_WIN), f32),
        compiler_params=pltpu.CompilerParams(dimension_semantics=("arbitrary",), vmem_limit_bytes=VMEM_LIMIT),
        name="na_bias",
    )(rpb.reshape(-1))


def _na_first_key_row(r):
    return jnp.clip(r - NA_WIN_ROWS // 2, 0, NA_ROWS - NA_WIN_ROWS)


def _na_kernel(q_ref, k_ref, v_ref, kc_ref, vc_ref, bias_ref, o_ref):
    r = pl.program_id(1)
    start = pl.multiple_of(_na_first_key_row(r) * GRID_W, GRID_W)
    for p in range(NA_PAIRS):
        lanes = slice(p * LANES, (p + 1) * LANES)
        qq = _pair_stack(q_ref[:, lanes])
        s_loc = _dot_nt(qq, k_ref[pl.ds(start, NA_WIN), lanes]) + bias_ref[p]
        s_ctx = _dot_nt(qq, kc_ref[:, lanes])
        o = _softmax_pv([s_loc, s_ctx], [v_ref[pl.ds(start, NA_WIN), lanes], vc_ref[:, lanes]])
        o_ref[:, lanes] = _pair_unstack(o).astype(o_ref.dtype)


def na_latent(q, k, v, k_ctx, v_ctx, bias):
    row0 = N_PROMPT // GRID_W
    seq0 = N_PROMPT // DEC_SEQ
    kv_spec = pl.BlockSpec((DEC_SEQ, NA_WIDTH), lambda b, r: (seq0 + b, 0))
    ctx_spec = pl.BlockSpec((None, PAST_LEN, NA_WIDTH), lambda b, r: (b, 0, 0))
    return pl.pallas_call(
        _na_kernel,
        grid=(DEC_BATCH, NA_ROWS),
        in_specs=[
            pl.BlockSpec((GRID_W, NA_WIDTH), lambda b, r: (row0 + b * NA_ROWS + r, 0)),
            kv_spec, kv_spec, ctx_spec, ctx_spec,
            pl.BlockSpec((NA_PAIRS, None, 2 * GRID_W, NA_WIN),
                         lambda b, r: (0, _na_first_key_row(r) - r + NA_WIN_ROWS - 1, 0, 0)),
        ],
        out_specs=pl.BlockSpec((GRID_W, NA_WIDTH), lambda b, r: (b * NA_ROWS + r, 0)),
        out_shape=jax.ShapeDtypeStruct((DEC_BATCH * DEC_SEQ, NA_WIDTH), bf16),
        compiler_params=pltpu.CompilerParams(
            dimension_semantics=("arbitrary", "arbitrary"), vmem_limit_bytes=VMEM_LIMIT),
        name="na_latent",
    )(q, k, v, k_ctx, v_ctx, bias)


def rmsnorm(x, g):
    xf = x.astype(jnp.float32)
    y = xf * lax.rsqrt(jnp.mean(xf * xf, axis=-1, keepdims=True) + RMS_EPS)
    return (y * g.astype(jnp.float32)).astype(x.dtype)


def dwconv_same(x, w, b):
    L = x.shape[1]
    pad = CONV_K // 2
    xp = jnp.pad(x, ((0, 0), (pad, pad), (0, 0)))
    y = xp[:, 0:L] * w[0]
    for j in range(1, CONV_K):
        y = y + xp[:, j:j + L] * w[j]
    return y + b


def ssd_scan(x, dt, A, Bm, Cm, h0):
    b, L, G, R, P = x.shape
    nc = L // SSD_CHUNK
    chunks = lambda t: jnp.moveaxis(t.reshape((b, nc, SSD_CHUNK) + t.shape[2:]), 1, 0)
    causal = jnp.tril(jnp.ones((SSD_CHUNK, SSD_CHUNK), bool))[None, :, :, None, None]

    def step(h, inp):
        xc, dtc, Bc, Cc = inp
        cum = jnp.cumsum(dtc * A, axis=1)
        seg = jnp.where(causal, cum[:, :, None] - cum[:, None], -jnp.inf)
        wts = jnp.einsum('bign,bjgn->bijg', Cc, Bc)[..., None] * jnp.exp(seg) * dtc[:, None]
        y = jnp.einsum('bijgr,bjgrp->bigrp', wts, xc)
        y = y + jnp.einsum('bign,bgrpn->bigrp', Cc, h) * jnp.exp(cum)[..., None]
        w_end = jnp.exp(cum[:, -1:] - cum) * dtc
        h = h * jnp.exp(cum[:, -1])[..., None, None] + jnp.einsum('bjgn,bjgr,bjgrp->bgrpn', Bc, w_end, xc)
        return h, y

    h, ys = lax.scan(step, h0, (chunks(x), chunks(dt), chunks(Bm), chunks(Cm)))
    return jnp.moveaxis(ys, 0, 1).reshape(b, L, G, R, P), h


def ssd_branch(z, xbc, dt_raw, conv_w, conv_b, dt_bias, a_log, d_skip, norm_g, h0_f, h0_b):
    b, L, _ = z.shape
    xbc = jax.nn.silu(dwconv_same(xbc, conv_w, conv_b)).astype(f32)
    xs, Bm, Cm = jnp.split(xbc, (SSD_INNER, SSD_INNER + SSD_GN), axis=-1)
    xs = xs.reshape(b, L, SSD_GROUPS, SSD_RPG, SSD_HEAD_DIM)
    Bm = Bm.reshape(b, L, SSD_GROUPS, SSD_STATE)
    Cm = Cm.reshape(b, L, SSD_GROUPS, SSD_STATE)
    dt = jax.nn.softplus(dt_raw.astype(f32).reshape(b, L, 2, SSD_GROUPS, SSD_RPG)
                         + dt_bias.astype(f32).reshape(2, SSD_GROUPS, SSD_RPG))
    A = -jnp.exp(a_log.astype(f32)).reshape(2, SSD_GROUPS, SSD_RPG)
    st = lambda h: h.astype(f32).reshape(b, SSD_GROUPS, SSD_RPG, SSD_HEAD_DIM, SSD_STATE)
    flip = lambda t: jnp.flip(t, axis=1)
    y_f, hf = ssd_scan(xs, dt[:, :, 0], A[0], Bm, Cm, st(h0_f))
    y_b, hb = ssd_scan(flip(xs), flip(dt[:, :, 1]), A[1], flip(Bm), flip(Cm), st(h0_b))
    y = y_f + flip(y_b) + d_skip.astype(f32).reshape(SSD_GROUPS, SSD_RPG, 1) * xs
    y = y.reshape(b, L, SSD_INNER).astype(z.dtype)
    y = rmsnorm(y * jax.nn.silu(z), norm_g)
    unst = lambda h: h.reshape(b, SSD_HEADS, SSD_HEAD_DIM, SSD_STATE).astype(z.dtype)
    return y, unst(hf), unst(hb)


def mlstm_scan(q, k, v, log_i, log_f, c0, n0, m0):
    b, L, H, _ = q.shape
    nc = L // ML_CHUNK
    chunks = lambda t: jnp.moveaxis(t.reshape((b, nc, ML_CHUNK) + t.shape[2:]), 1, 0)
    causal = jnp.tril(jnp.ones((ML_CHUNK, ML_CHUNK), bool))[None, :, :, None]

    def step(carry, inp):
        c, n, m = carry
        qc, kc, vc, li, lf = inp
        cum = jnp.cumsum(lf, axis=1)
        dmat = jnp.where(causal, cum[:, :, None] - cum[:, None] + li[:, None], -jnp.inf)
        m_inter = cum + m[:, None]
        m_t = jnp.maximum(m_inter, jnp.max(dmat, axis=2))
        w_intra = jnp.exp(dmat - m_t[:, :, None])
        w_inter = jnp.exp(m_inter - m_t)
        s = jnp.einsum('bihd,bjhd->bijh', qc, kc) * w_intra
        num = jnp.einsum('bijh,bjhe->bihe', s, vc) + w_inter[..., None] * jnp.einsum('bhed,bihd->bihe', c, qc)
        den = jnp.sum(s, axis=2) + w_inter * jnp.einsum('bhd,bihd->bih', n, qc)
        h = num / jnp.maximum(jnp.abs(den), jnp.exp(-m_t))[..., None]
        g = cum[:, -1:] - cum + li
        m_new = jnp.maximum(cum[:, -1] + m, jnp.max(g, axis=1))
        wk = jnp.exp(g - m_new[:, None])
        wc = jnp.exp(cum[:, -1] + m - m_new)
        c_new = wc[..., None, None] * c + jnp.einsum('bjh,bjhe,bjhd->bhed', wk, vc, kc)
        n_new = wc[..., None] * n + jnp.einsum('bjh,bjhd->bhd', wk, kc)
        return (c_new, n_new, m_new), h

    xs = (chunks(q), chunks(k), chunks(v), chunks(log_i), chunks(log_f))
    (c, n, m), hs = lax.scan(step, (c0, n0, m0), xs)
    return jnp.moveaxis(hs, 0, 1).reshape(b, L, H, v.shape[-1]), c, n, m


def mlstm_mix(qk, v, o, gates, conv_w, conv_b, gate_b, st_f, st_b):
    b, L, _ = qk.shape
    qk = jax.nn.silu(dwconv_same(qk, conv_w, conv_b)).astype(f32)
    q, k = jnp.split(qk, 2, axis=-1)
    q = q.reshape(b, L, ML_HEADS, ML_QK_DIM) * ML_QK_DIM ** -0.5
    k = k.reshape(b, L, ML_HEADS, ML_QK_DIM)
    v = v.astype(f32).reshape(b, L, ML_HEADS, ML_V_DIM)
    gates = gates.astype(f32).reshape(b, L, 2, 2, ML_HEADS) + gate_b.astype(f32)
    log_i = gates[:, :, 0]
    log_f = jax.nn.log_sigmoid(gates[:, :, 1])
    flip = lambda t: jnp.flip(t, axis=1)
    hf, cf, nf, mf = mlstm_scan(q, k, v, log_i[:, :, 0], log_f[:, :, 0], st_f[0], st_f[1], st_f[2])
    hb, cb, nb, mb = mlstm_scan(flip(q), flip(k), flip(v), flip(log_i[:, :, 1]), flip(log_f[:, :, 1]),
                                st_b[0], st_b[1], st_b[2])
    y = (hf + flip(hb)).reshape(b, L, ML_V_WIDTH) * jax.nn.sigmoid(o)
    return y, (cf, nf, mf), (cb, nb, mb)


def _split_tokens(t):
    w = t.shape[-1]
    return t[:N_PROMPT].reshape(BATCH, SEQ, w), t[N_PROMPT:].reshape(DEC_BATCH, DEC_SEQ, w)


def _join_tokens(p, s):
    w = p.shape[-1]
    return jnp.concatenate([p.reshape(N_PROMPT, w), s.reshape(DEC_BATCH * DEC_SEQ, w)], axis=0)


def kernel(x_prompt, x_sample, c, cache_na_k, cache_na_v, state_ssd, state_mlstm_c, state_mlstm_n, state_mlstm_m,
           c_ctx, w_mod, b_mod, norm_mix, norm_ffn, w_in_even, w_out_even, na_rpb, ssd_conv_w, ssd_conv_b,
           ssd_dt_bias, ssd_a_log, ssd_d, ssd_norm, w_in_odd, w_out_odd, ml_conv_w, ml_conv_b, ml_gate_b,
           w_ff1, w_ff2, norm_f):
    x = _join_tokens(x_prompt, x_sample)
    cond = jnp.concatenate([c_ctx[None, :], c, jnp.zeros((SUBLANES - N_COND, D_MODEL), f32)], axis=0)
    mod = adaln_all(cond, w_mod, b_mod)[:, :N_COND].reshape(DEPTH, N_COND, 1, N_MOD * D_MODEL)

    out_k, out_v, out_ssd, out_c, out_n, out_m = [], [], [], [], [], []
    for l in range(DEPTH):
        last = l == DEPTH - 1
        w1 = w_ff1[l].astype(bf16)
        w2 = w_ff2[l].astype(bf16)
        if l % 2 == 0:
            e = l // 2
            w = w_in_even[e]
            pad = jnp.zeros((D_MODEL, LANES - 2 * SSD_HEADS), f32)
            w = jnp.concatenate([w, pad], axis=1).astype(bf16)
            o0 = 3 * NA_WIDTH
            segs = ((0, NA_WIDTH, NA_WIDTH, NA_HEAD_DIM ** -0.5),
                    (NA_WIDTH, 2 * NA_WIDTH, NA_WIDTH, 1.0),
                    (2 * NA_WIDTH, 3 * NA_WIDTH, NA_WIDTH, 1.0),
                    (o0, o0 + SSD_INNER, SSD_INNER, 1.0),
                    (o0 + SSD_INNER, o0 + SSD_INNER + SSD_CONV_DIM, SSD_CONV_DIM, 1.0),
                    (o0 + SSD_INNER + SSD_CONV_DIM, o0 + SSD_INNER + SSD_CONV_DIM + LANES, 2 * SSD_HEADS, 1.0))
            outs = ((0, bf16, False), (1, bf16, False), (2, bf16, False), (1, f32, True), (2, f32, True),
                    (3, f32, False), (4, f32, False), (5, f32, False))
            q, k, v, kp, vp, z, xbc, dt = in_proj(x, mod[l], norm_mix[l], w, segs, outs)
            zp, zs = _split_tokens(z)
            xbcp, xbcs = _split_tokens(xbc)
            dtp, dts = _split_tokens(dt)
            ssd_w = (ssd_conv_w[e], ssd_conv_b[e], ssd_dt_bias[e], ssd_a_log[e], ssd_d[e], ssd_norm[e])
            heads = lambda t: t.reshape(BATCH, SEQ, NA_HEADS, NA_HEAD_DIM)
            zero = jnp.zeros((BATCH, SSD_HEADS, SSD_HEAD_DIM, SSD_STATE), f32)
            y_ssd_p, s_f, s_b = ssd_branch(zp, xbcp, dtp, *ssd_w, zero, zero)
            out_k.append(jnp.swapaxes(heads(kp), 1, 2))
            out_v.append(jnp.swapaxes(heads(vp), 1, 2))
            out_ssd.append(jnp.stack([s_f, s_b], axis=1))
            y_ssd_s, _, _ = ssd_branch(zs, xbcs, dts, *ssd_w, state_ssd[:, e, 0], state_ssd[:, e, 1])
            tokens = lambda t: jnp.swapaxes(t, 1, 2).reshape(DEC_BATCH, PAST_LEN, NA_WIDTH).astype(bf16)
            y_na_p = ctx_attention(q, k, v)
            y_na_s = na_latent(q, k, v, tokens(cache_na_k[:, e]), tokens(cache_na_v[:, e]), na_bias_table(na_rpb[e]))
            mixes = [jnp.concatenate([y_na_p, y_na_s], axis=0), _join_tokens(y_ssd_p, y_ssd_s)]
            w_out = w_out_even[e].astype(bf16)
        else:
            o = l // 2
            w = w_in_odd[o]
            pad = jnp.zeros((D_MODEL, LANES - 4 * ML_HEADS), f32)
            w = jnp.concatenate([w, pad], axis=1).astype(bf16)
            a0 = 2 * ML_QK_WIDTH
            segs = ((0, a0, a0, 1.0),
                    (a0, a0 + ML_V_WIDTH, ML_V_WIDTH, 1.0),
                    (a0 + ML_V_WIDTH, a0 + 2 * ML_V_WIDTH, ML_V_WIDTH, 1.0),
                    (a0 + 2 * ML_V_WIDTH, a0 + 2 * ML_V_WIDTH + LANES, 4 * ML_HEADS, 1.0))
            outs = tuple((i, f32, False) for i in range(4))
            qk, v, og, gates = in_proj(x, mod[l], norm_mix[l], w, segs, outs)
            qkp, qks = _split_tokens(qk)
            vp, vs = _split_tokens(v)
            ogp, ogs = _split_tokens(og)
            gp, gs = _split_tokens(gates)
            ml_w = (ml_conv_w[o], ml_conv_b[o], ml_gate_b[o])
            zc = jnp.zeros((BATCH, ML_HEADS, ML_V_DIM, ML_QK_DIM), f32)
            zn = jnp.zeros((BATCH, ML_HEADS, ML_QK_DIM), f32)
            zm = jnp.zeros((BATCH, ML_HEADS), f32)
            y_p, (cf, nf, mf), (cb, nb, mb) = mlstm_mix(qkp, vp, ogp, gp, *ml_w, (zc, zn, zm), (zc, zn, zm))
            out_c.append(jnp.stack([cf, cb], axis=1))
            out_n.append(jnp.stack([nf, nb], axis=1))
            out_m.append(jnp.stack([mf, mb], axis=1))
            st_f = (state_mlstm_c[:, o, 0], state_mlstm_n[:, o, 0], state_mlstm_m[:, o, 0])
            st_b = (state_mlstm_c[:, o, 1], state_mlstm_n[:, o, 1], state_mlstm_m[:, o, 1])
            y_s, _, _ = mlstm_mix(qks, vs, ogs, gs, *ml_w, st_f, st_b)
            mixes = [_join_tokens(y_p, y_s)]
            w_out = w_out_odd[o].astype(bf16)
        x = out_mlp(x, mod[l], norm_ffn[l], mixes, w_out, w1, w2, norm_f if last else None)

    y_prompt = x[:N_PROMPT].reshape(BATCH, SEQ, D_MODEL)
    y_sample = x[N_PROMPT:].reshape(DEC_BATCH, DEC_SEQ, D_MODEL)
    return (y_prompt, y_sample, jnp.stack(out_k, axis=1), jnp.stack(out_v, axis=1), jnp.stack(out_ssd, axis=1),
            jnp.stack(out_c, axis=1), jnp.stack(out_n, axis=1), jnp.stack(out_m, axis=1))
```

```python
import functools
import math

import jax
import jax.numpy as jnp
from jax import lax
from jax.experimental import pallas as pl
from jax.experimental.pallas import tpu as pltpu

D_MODEL = 1024
BATCH = 32
SEQ = 256
DEPTH = 4
DEC_BATCH = 2
DEC_SEQ = 4096
PAST_LEN = 256
GRID_W = 64
N_EVEN = (DEPTH + 1) // 2
N_ODD = DEPTH // 2
RMS_EPS = 1e-6
N_MOD = 6
D_FF = 4 * D_MODEL
CONV_K = 3
Q_BLOCK = 128
NA_HEADS = 8
NA_HEAD_DIM = 64
NA_WIDTH = NA_HEADS * NA_HEAD_DIM
NA_WIN_ROWS = 8
NA_WIN_COLS = 16
NA_RPB_ROWS = 2 * NA_WIN_ROWS - 1
NA_RPB_COLS = 2 * NA_WIN_COLS - 1
SSD_INNER = D_MODEL
SSD_HEAD_DIM = 64
SSD_HEADS = SSD_INNER // SSD_HEAD_DIM
SSD_GROUPS = 2
SSD_RPG = SSD_HEADS // SSD_GROUPS
SSD_STATE = 128
SSD_GN = SSD_GROUPS * SSD_STATE
SSD_CONV_DIM = SSD_INNER + 2 * SSD_GN
SSD_CHUNK = 128
ML_HEADS = 8
ML_QK_DIM = D_MODEL // 16
ML_V_DIM = D_MODEL // 8
ML_QK_WIDTH = ML_HEADS * ML_QK_DIM
ML_V_WIDTH = ML_HEADS * ML_V_DIM
ML_CHUNK = 64
EVEN_MIX = NA_WIDTH + SSD_INNER

N_PROMPT = BATCH * SEQ
N_TOK = N_PROMPT + DEC_BATCH * DEC_SEQ
N_COND = 1 + DEC_BATCH
LANES = 128
SUBLANES = 8
VMEM_LIMIT = 56 * 1024 * 1024
TM = 512

f32 = jnp.float32
bf16 = jnp.bfloat16


def _cond_row(i, tm):
    start = i * tm
    return jnp.where(start < N_PROMPT, 0, (start - N_PROMPT) // DEC_SEQ + 1)


def _const_spec(shape):
    nd = len(shape)
    return pl.BlockSpec(shape, lambda i: (0,) * nd, pipeline_mode=pl.Buffered(1))


def _rms(x):
    return x * lax.rsqrt(jnp.mean(x * x, axis=-1, keepdims=True) + RMS_EPS)


def _modulated(x, g, mod, k):
    shift = mod[:, k * D_MODEL:(k + 1) * D_MODEL]
    scale = mod[:, (k + 1) * D_MODEL:(k + 2) * D_MODEL]
    return (_rms(x) * g) * (1.0 + scale) + shift


def _mod_kernel(c_ref, w_ref, b_ref, o_ref):
    c = c_ref[...]
    a = (c * jax.nn.sigmoid(c)).astype(bf16)
    o_ref[...] = jnp.dot(a, w_ref[...].astype(bf16), preferred_element_type=f32) + b_ref[...]


def adaln_all(cond, w_mod, b_mod):
    tn = 1536
    nj = N_MOD * D_MODEL // tn
    return pl.pallas_call(
        _mod_kernel,
        grid=(DEPTH, nj),
        in_specs=[
            pl.BlockSpec((SUBLANES, D_MODEL), lambda l, j: (0, 0)),
            pl.BlockSpec((None, D_MODEL, tn), lambda l, j: (l, 0, j)),
            pl.BlockSpec((None, 1, tn), lambda l, j: (l, 0, j)),
        ],
        out_specs=pl.BlockSpec((None, SUBLANES, tn), lambda l, j: (l, 0, j)),
        out_shape=jax.ShapeDtypeStruct((DEPTH, SUBLANES, N_MOD * D_MODEL), f32),
        compiler_params=pltpu.CompilerParams(
            dimension_semantics=("arbitrary", "arbitrary"), vmem_limit_bytes=VMEM_LIMIT),
        name="adaln",
    )(cond, w_mod, b_mod.reshape(DEPTH, 1, N_MOD * D_MODEL))


def _in_proj_kernel(segs, outs, x_ref, mod_ref, g_ref, w_ref, *o_refs):
    h = _modulated(x_ref[...], g_ref[...], mod_ref[...], 0).astype(bf16)
    ys = []
    for a, b, width, scale in segs:
        y = jnp.dot(h, w_ref[:, a:b], preferred_element_type=f32)
        if scale != 1.0:
            y = y * scale
        ys.append(y[:, :width] if width != b - a else y)
    for (si, _, prompt_only), o_ref in zip(outs, o_refs):
        if prompt_only:
            @pl.when(pl.program_id(0) < N_PROMPT // TM)
            def _(o_ref=o_ref, si=si):
                o_ref[...] = ys[si].astype(o_ref.dtype)
        else:
            o_ref[...] = ys[si].astype(o_ref.dtype)


def in_proj(x, mod_l, g, w, segs, outs):
    grid = (N_TOK // TM,)
    last_prompt = N_PROMPT // TM - 1
    out_shape, out_specs = [], []
    for si, dt, prompt_only in outs:
        width = segs[si][2]
        if prompt_only:
            out_shape.append(jax.ShapeDtypeStruct((N_PROMPT, width), dt))
            out_specs.append(pl.BlockSpec((TM, width), lambda i: (jnp.minimum(i, last_prompt), 0)))
        else:
            out_shape.append(jax.ShapeDtypeStruct((N_TOK, width), dt))
            out_specs.append(pl.BlockSpec((TM, width), lambda i: (i, 0)))
    return pl.pallas_call(
        functools.partial(_in_proj_kernel, segs, outs),
        grid=grid,
        in_specs=[
            pl.BlockSpec((TM, D_MODEL), lambda i: (i, 0)),
            pl.BlockSpec((None, 1, N_MOD * D_MODEL), lambda i: (_cond_row(i, TM), 0, 0)),
            _const_spec((1, D_MODEL)),
            _const_spec(w.shape),
        ],
        out_specs=out_specs,
        out_shape=out_shape,
        compiler_params=pltpu.CompilerParams(
            dimension_semantics=("arbitrary",), vmem_limit_bytes=VMEM_LIMIT),
        name="in_proj",
    )(x, mod_l, g.reshape(1, D_MODEL), w)


def _out_mlp_kernel(n_mix, final, x_ref, mod_ref, g_ref, *refs):
    mix_refs = refs[:n_mix]
    wo_ref, w1_ref, w2_ref = refs[n_mix:n_mix + 3]
    rest = refs[n_mix + 3:]
    if final:
        gf_ref, o_ref = rest
    else:
        (o_ref,) = rest
    mod = mod_ref[...]
    m = None
    k0 = 0
    for r in mix_refs:
        kw = r.shape[-1]
        part = jnp.dot(r[...].astype(bf16), wo_ref[k0:k0 + kw, :], preferred_element_type=f32)
        m = part if m is None else m + part
        k0 += kw
    x1 = x_ref[...] + mod[:, 2 * D_MODEL:3 * D_MODEL] * m
    h2 = _modulated(x1, g_ref[...], mod, 3).astype(bf16)
    u = jnp.dot(h2, w1_ref[...], preferred_element_type=f32)
    a = jnp.square(jnp.maximum(u, 0.0)).astype(bf16)
    x2 = x1 + mod[:, 5 * D_MODEL:6 * D_MODEL] * jnp.dot(a, w2_ref[...], preferred_element_type=f32)
    if final:
        x2 = _rms(x2) * gf_ref[...]
    o_ref[...] = x2


def out_mlp(x, mod_l, g_ffn, mixes, w_out, w1, w2, norm_f=None):
    final = norm_f is not None
    tm = 256
    grid = (N_TOK // tm,)
    in_specs = [
        pl.BlockSpec((tm, D_MODEL), lambda i: (i, 0)),
        pl.BlockSpec((None, 1, N_MOD * D_MODEL), lambda i: (_cond_row(i, tm), 0, 0)),
        _const_spec((1, D_MODEL)),
    ]
    in_specs += [pl.BlockSpec((tm, m.shape[-1]), lambda i: (i, 0)) for m in mixes]
    in_specs += [_const_spec(w_out.shape), _const_spec(w1.shape), _const_spec(w2.shape)]
    args = [x, mod_l, g_ffn.reshape(1, D_MODEL), *mixes, w_out, w1, w2]
    if final:
        in_specs.append(_const_spec((1, D_MODEL)))
        args.append(norm_f.reshape(1, D_MODEL))
    return pl.pallas_call(
        functools.partial(_out_mlp_kernel, len(mixes), final),
        grid=grid,
        in_specs=in_specs,
        out_specs=pl.BlockSpec((tm, D_MODEL), lambda i: (i, 0)),
        out_shape=jax.ShapeDtypeStruct((N_TOK, D_MODEL), f32),
        compiler_params=pltpu.CompilerParams(
            dimension_semantics=("arbitrary",), vmem_limit_bytes=VMEM_LIMIT),
        name="out_mlp",
    )(*args)


NA_PAIRS = NA_HEADS // 2
NA_ROWS = DEC_SEQ // GRID_W
NA_WIN = NA_WIN_ROWS * GRID_W


def _dot_nt(a, b):
    return lax.dot_general(a, b, (((1,), (1,)), ((), ())), preferred_element_type=f32)


def _pair_stack(x):
    lane = lax.broadcasted_iota(jnp.int32, x.shape, 1)
    zero = jnp.zeros_like(x)
    return jnp.concatenate([jnp.where(lane < NA_HEAD_DIM, x, zero), jnp.where(lane >= NA_HEAD_DIM, x, zero)], axis=0)


def _pair_unstack(o):
    n = o.shape[0] // 2
    lane = lax.broadcasted_iota(jnp.int32, (n, LANES), 1)
    return jnp.where(lane < NA_HEAD_DIM, o[:n], o[n:])


def _softmax_pv(scores, values):
    m = None
    for s in scores:
        mi = jnp.max(s, axis=1, keepdims=True)
        m = mi if m is None else jnp.maximum(m, mi)
    ps = [jnp.exp(s - m) for s in scores]
    l = None
    for p in ps:
        li = jnp.sum(p, axis=1, keepdims=True)
        l = li if l is None else l + li
    inv = 1.0 / l
    o = None
    for p, v in zip(ps, values):
        oi = jnp.dot((p * inv).astype(bf16), v, preferred_element_type=f32)
        o = oi if o is None else o + oi
    return o


def _ctx_attn_kernel(q_ref, k_ref, v_ref, o_ref):
    for p in range(NA_PAIRS):
        lanes = slice(p * LANES, (p + 1) * LANES)
        qq = _pair_stack(q_ref[:, lanes])
        s = _dot_nt(qq, k_ref[:, lanes])
        o = _softmax_pv([s], [v_ref[:, lanes]])
        o_ref[:, lanes] = _pair_unstack(o).astype(o_ref.dtype)


def ctx_attention(q, k, v):
    spec = pl.BlockSpec((SEQ, NA_WIDTH), lambda b: (b, 0))
    return pl.pallas_call(
        _ctx_attn_kernel,
        grid=(BATCH,),
        in_specs=[spec, spec, spec],
        out_specs=spec,
        out_shape=jax.ShapeDtypeStruct((N_PROMPT, NA_WIDTH), bf16),
        compiler_params=pltpu.CompilerParams(dimension_semantics=("arbitrary",), vmem_limit_bytes=VMEM_LIMIT),
        name="ctx_attn",
    )(q, k, v)


def _na_bias_kernel(rpb_ref, o_ref):
    pair = pl.program_id(0)
    shape = (GRID_W, LANES)
    qc = lax.broadcasted_iota(jnp.int32, shape, 0)
    lane = lax.broadcasted_iota(jnp.int32, shape, 1)
    kc = lane & (GRID_W - 1)
    low = lane < GRID_W
    col_start = jnp.clip(qc - NA_WIN_COLS // 2, 0, GRID_W - NA_WIN_COLS)
    valid = (kc >= col_start) & (kc < col_start + NA_WIN_COLS)
    rel_c = jnp.clip(kc - qc + NA_WIN_COLS - 1, 0, NA_RPB_COLS - 1)
    for e in range(2):
        base = (2 * pair + e) * NA_RPB_ROWS
        pieces = []
        for rr in range(NA_RPB_ROWS - 1):
            val = jnp.zeros(shape, f32)
            for t in range(NA_RPB_COLS):
                s_lo = rpb_ref[(base + rr) * NA_RPB_COLS + t]
                s_hi = rpb_ref[(base + rr + 1) * NA_RPB_COLS + t]
                val = jnp.where(rel_c == t, jnp.where(low, s_lo, s_hi), val)
            pieces.append(jnp.where(valid, val, -jnp.inf))
        for d in range(NA_WIN_ROWS):
            for i in range(0, NA_WIN_ROWS, 2):
                o_ref[d, e * GRID_W:(e + 1) * GRID_W, i * GRID_W:(i + 2) * GRID_W] = pieces[d + i]


def na_bias_table(rpb):
    return pl.pallas_call(
        _na_bias_kernel,
        grid=(NA_PAIRS,),
        in_specs=[pl.BlockSpec(memory_space=pltpu.SMEM)],
        out_specs=pl.BlockSpec((None, NA_WIN_ROWS, 2 * GRID_W, NA_WIN), lambda p: (p, 0, 0, 0)),
        out_shape=jax.ShapeDtypeStruct((NA_PAIRS, NA_WIN_ROWS, 2 * GRID_W, NA_WIN), f32),
        compiler_params=pltpu.CompilerParams(dimension_semantics=("arbitrary",), vmem_limit_bytes=VMEM_LIMIT),
        name="na_bias",
    )(rpb.reshape(-1))


def _na_first_key_row(r):
    return jnp.clip(r - NA_WIN_ROWS // 2, 0, NA_ROWS - NA_WIN_ROWS)


def _na_kernel(q_ref, k_ref, v_ref, kc_ref, vc_ref, bias_ref, o_ref):
    r = pl.program_id(1)
    start = pl.multiple_of(_na_first_key_row(r) * GRID_W, GRID_W)
    for p in range(NA_PAIRS):
        lanes = slice(p * LANES, (p + 1) * LANES)
        qq = _pair_stack(q_ref[:, lanes])
        s_loc = _dot_nt(qq, k_ref[pl.ds(start, NA_WIN), lanes]) + bias_ref[p]
        s_ctx = _dot_nt(qq, kc_ref[:, lanes])
        o = _softmax_pv([s_loc, s_ctx], [v_ref[pl.ds(start, NA_WIN), lanes], vc_ref[:, lanes]])
        o_ref[:, lanes] = _pair_unstack(o).astype(o_ref.dtype)


def na_latent(q, k, v, k_ctx, v_ctx, bias):
    row0 = N_PROMPT // GRID_W
    seq0 = N_PROMPT // DEC_SEQ
    kv_spec = pl.BlockSpec((DEC_SEQ, NA_WIDTH), lambda b, r: (seq0 + b, 0))
    ctx_spec = pl.BlockSpec((None, PAST_LEN, NA_WIDTH), lambda b, r: (b, 0, 0))
    return pl.pallas_call(
        _na_kernel,
        grid=(DEC_BATCH, NA_ROWS),
        in_specs=[
            pl.BlockSpec((GRID_W, NA_WIDTH), lambda b, r: (row0 + b * NA_ROWS + r, 0)),
            kv_spec, kv_spec, ctx_spec, ctx_spec,
            pl.BlockSpec((NA_PAIRS, None, 2 * GRID_W, NA_WIN),
                         lambda b, r: (0, _na_first_key_row(r) - r + NA_WIN_ROWS - 1, 0, 0)),
        ],
        out_specs=pl.BlockSpec((GRID_W, NA_WIDTH), lambda b, r: (b * NA_ROWS + r, 0)),
        out_shape=jax.ShapeDtypeStruct((DEC_BATCH * DEC_SEQ, NA_WIDTH), bf16),
        compiler_params=pltpu.CompilerParams(
            dimension_semantics=("arbitrary", "arbitrary"), vmem_limit_bytes=VMEM_LIMIT),
        name="na_latent",
    )(q, k, v, k_ctx, v_ctx, bias)


CONV_ROWS = SEQ


def _conv_silu_kernel(x_ref, prev_ref, next_ref, w_ref, b_ref, o_ref):
    i = pl.program_id(0)
    start = i * CONV_ROWS
    in_sample = start >= N_PROMPT
    off = start - N_PROMPT
    first = jnp.logical_or(jnp.logical_not(in_sample), off % DEC_SEQ == 0)
    last = jnp.logical_or(jnp.logical_not(in_sample), (off + CONV_ROWS) % DEC_SEQ == 0)
    x = x_ref[...]
    row = lax.broadcasted_iota(jnp.int32, x.shape, 0)
    prev_row = jnp.where(first, 0.0, prev_ref[SUBLANES - 1:SUBLANES, :])
    next_row = jnp.where(last, 0.0, next_ref[0:1, :])
    up = jnp.where(row == 0, prev_row, pltpu.roll(x, 1, axis=0))
    dn = jnp.where(row == CONV_ROWS - 1, next_row, pltpu.roll(x, CONV_ROWS - 1, axis=0))
    y = up * w_ref[0:1, :] + x * w_ref[1:2, :] + dn * w_ref[2:3, :] + b_ref[...]
    o_ref[...] = (y * jax.nn.sigmoid(y)).astype(o_ref.dtype)


def conv_silu(x, w, b):
    n, ch = x.shape
    per = CONV_ROWS // SUBLANES
    nblk8 = n // SUBLANES
    return pl.pallas_call(
        _conv_silu_kernel,
        grid=(n // CONV_ROWS,),
        in_specs=[
            pl.BlockSpec((CONV_ROWS, ch), lambda i: (i, 0)),
            pl.BlockSpec((SUBLANES, ch), lambda i: (jnp.maximum(i * per - 1, 0), 0)),
            pl.BlockSpec((SUBLANES, ch), lambda i: (jnp.minimum((i + 1) * per, nblk8 - 1), 0)),
            _const_spec((CONV_K, ch)),
            _const_spec((1, ch)),
        ],
        out_specs=pl.BlockSpec((CONV_ROWS, ch), lambda i: (i, 0)),
        out_shape=jax.ShapeDtypeStruct((n, ch), f32),
        compiler_params=pltpu.CompilerParams(dimension_semantics=("arbitrary",), vmem_limit_bytes=VMEM_LIMIT),
        name="conv_silu",
    )(x, x, x, w, b.reshape(1, ch))


CHUNK = 128


def _cumsum_rows(a, reverse=False):
    row = lax.broadcasted_iota(jnp.int32, a.shape, 0)
    s = 1
    while s < CHUNK:
        if reverse:
            a = a + jnp.where(row < CHUNK - s, pltpu.roll(a, CHUNK - s, axis=0), 0.0)
        else:
            a = a + jnp.where(row >= s, pltpu.roll(a, s, axis=0), 0.0)
        s *= 2
    return a


def _expand_heads(v, off, n_heads, width):
    shape = (v.shape[0], LANES)
    col = lambda h: jnp.broadcast_to(v[:, off + h:off + h + 1], shape)
    if width == LANES:
        return jnp.concatenate([col(h) for h in range(n_heads)], axis=1)
    assert 2 * width == LANES
    lane = lax.broadcasted_iota(jnp.int32, shape, 1)
    return jnp.concatenate(
        [jnp.where(lane < width, col(h), col(h + 1)) for h in range(0, n_heads, 2)], axis=1)


def _ssd_kernel(nc, has_h0, emit_state, *refs):
    xbc_ref, dt_ref, z_ref = refs[:3]
    refs = refs[3:]
    if has_h0:
        h0_ref, refs = refs[0], refs[1:]
    dtb_ref, alog_ref, dskip_ref, g_ref = refs[:4]
    refs = refs[4:]
    y_ref, refs = refs[0], refs[1:]
    if emit_state:
        hfin_ref, refs = refs[0], refs[1:]
    hb_store, carry = refs

    phase = pl.program_id(1)
    c = pl.program_id(2)
    gw = SSD_RPG * SSD_HEAD_DIM

    xbc = xbc_ref[...]
    x = xbc[:, :SSD_INNER]
    x_bf = x.astype(bf16)
    b_mat = xbc[:, SSD_INNER:SSD_INNER + SSD_GN]
    dt = jax.nn.softplus(dt_ref[...] + dtb_ref[...])
    a = dt * (-jnp.exp(alog_ref[...]))

    def load_h0(d):
        if has_h0:
            return h0_ref[d].reshape(SSD_INNER, SSD_STATE).T
        return jnp.zeros((SSD_STATE, SSD_INNER), f32)

    def state_update(cum, edge, off):
        cum_e = _expand_heads(cum, off, SSD_HEADS, SSD_HEAD_DIM)
        edge_row = cum_e[edge:edge + 1, :]
        w_end = jnp.exp(edge_row - cum_e) * _expand_heads(dt, off, SSD_HEADS, SSD_HEAD_DIM)
        xw = (x * w_end).astype(bf16)
        parts = []
        for g in range(SSD_GROUPS):
            bt = b_mat[:, g * SSD_STATE:(g + 1) * SSD_STATE].T.astype(bf16)
            parts.append(jnp.dot(bt, xw[:, g * gw:(g + 1) * gw], preferred_element_type=f32))
        carry[...] = carry[...] * jnp.exp(edge_row) + jnp.concatenate(parts, axis=1)
        return cum_e

    @pl.when(phase == 0)
    def _backward_states():
        @pl.when(c == 0)
        def _():
            carry[...] = load_h0(1)

        hb_store[nc - 1 - c] = carry[...].astype(bf16)
        state_update(_cumsum_rows(a, reverse=True), 0, SSD_HEADS)

        if emit_state:
            @pl.when(c == nc - 1)
            def _():
                hfin_ref[1] = carry[...].T.reshape(SSD_HEADS, SSD_HEAD_DIM, SSD_STATE)

    @pl.when(phase == 1)
    def _forward_and_outputs():
        @pl.when(c == 0)
        def _():
            carry[...] = load_h0(0)

        c_mat = xbc[:, SSD_INNER + SSD_GN:].astype(bf16)
        b_bf = b_mat.astype(bf16)
        cum = _cumsum_rows(a)
        rcum = _cumsum_rows(a, reverse=True)
        cum_t, rcum_t, dt_t = cum.T, rcum.T, dt.T
        row = lax.broadcasted_iota(jnp.int32, (CHUNK, CHUNK), 0)
        col = lax.broadcasted_iota(jnp.int32, (CHUNK, CHUNK), 1)
        causal = col <= row
        anti = col >= row
        lane = lax.broadcasted_iota(jnp.int32, (CHUNK, LANES), 1)
        hf = carry[...].astype(bf16)
        hb = hb_store[c]
        cb = [_dot_nt(c_mat[:, g * SSD_STATE:(g + 1) * SSD_STATE], b_bf[:, g * SSD_STATE:(g + 1) * SSD_STATE])
              for g in range(SSD_GROUPS)]

        y_parts = []
        for p in range(SSD_HEADS // 2):
            g = (2 * p) // SSD_RPG
            ws = []
            for h in (2 * p, 2 * p + 1):
                hb_i = SSD_HEADS + h
                seg_f = jnp.where(causal, cum[:, h:h + 1] - cum_t[h:h + 1, :], -jnp.inf)
                seg_b = jnp.where(anti, rcum[:, hb_i:hb_i + 1] - rcum_t[hb_i:hb_i + 1, :], -jnp.inf)
                w = jnp.exp(seg_f) * dt_t[h:h + 1, :] + jnp.exp(seg_b) * dt_t[hb_i:hb_i + 1, :]
                ws.append((cb[g] * w).astype(bf16))
            xp = x_bf[:, p * LANES:(p + 1) * LANES]
            zero = jnp.zeros_like(xp)
            rhs = jnp.concatenate([jnp.where(lane < SSD_HEAD_DIM, xp, zero),
                                   jnp.where(lane >= SSD_HEAD_DIM, xp, zero)], axis=0)
            y_parts.append(jnp.dot(jnp.concatenate(ws, axis=1), rhs, preferred_element_type=f32))
        y = jnp.concatenate(y_parts, axis=1)

        inter_f = jnp.concatenate(
            [jnp.dot(c_mat[:, g * SSD_STATE:(g + 1) * SSD_STATE], hf[:, g * gw:(g + 1) * gw],
                     preferred_element_type=f32) for g in range(SSD_GROUPS)], axis=1)
        inter_b = jnp.concatenate(
            [jnp.dot(c_mat[:, g * SSD_STATE:(g + 1) * SSD_STATE], hb[:, g * gw:(g + 1) * gw],
                     preferred_element_type=f32) for g in range(SSD_GROUPS)], axis=1)
        cum_e = state_update(cum, CHUNK - 1, 0)
        rcum_e = _expand_heads(rcum, SSD_HEADS, SSD_HEADS, SSD_HEAD_DIM)
        y = y + inter_f * jnp.exp(cum_e) + inter_b * jnp.exp(rcum_e) + dskip_ref[...] * x

        zv = z_ref[...]
        yz = y * (zv * jax.nn.sigmoid(zv))
        y_ref[...] = (_rms(yz) * g_ref[...]).astype(y_ref.dtype)

        if emit_state:
            @pl.when(c == nc - 1)
            def _():
                hfin_ref[0] = carry[...].T.reshape(SSD_HEADS, SSD_HEAD_DIM, SSD_STATE)


def ssd_mix(xbc, dt, z, h0, row0, n_seq, seq_len, dt_bias, a_log, d_skip, norm_g, emit_state):
    nc = seq_len // CHUNK
    blk0 = row0 // CHUNK
    has_h0 = h0 is not None

    def chunk_map(s, p, c):
        return (blk0 + s * nc + jnp.where(p == 0, nc - 1 - c, c), 0)

    state_spec = pl.BlockSpec((None, 2, SSD_HEADS, SSD_HEAD_DIM, SSD_STATE), lambda s, p, c: (s, 0, 0, 0, 0))
    vec = lambda n: pl.BlockSpec((1, n), lambda s, p, c: (0, 0))
    in_specs = [
        pl.BlockSpec((CHUNK, SSD_CONV_DIM), chunk_map),
        pl.BlockSpec((CHUNK, LANES), chunk_map),
        pl.BlockSpec((CHUNK, SSD_INNER), lambda s, p, c: (blk0 + s * nc + p * c, 0)),
    ]
    args = [xbc, dt, z]
    if has_h0:
        in_specs.append(state_spec)
        args.append(h0)
    in_specs += [vec(LANES), vec(LANES), vec(SSD_INNER), vec(SSD_INNER)]
    pad = lambda t: jnp.concatenate([t.reshape(1, -1), jnp.zeros((1, LANES - t.size), f32)], axis=1)
    args += [pad(dt_bias), pad(a_log), jnp.repeat(d_skip, SSD_HEAD_DIM).reshape(1, SSD_INNER),
             norm_g.reshape(1, SSD_INNER)]
    out_shape = [jax.ShapeDtypeStruct((n_seq * seq_len, SSD_INNER), bf16)]
    out_specs = [pl.BlockSpec((CHUNK, SSD_INNER), lambda s, p, c: (s * nc + p * c, 0))]
    if emit_state:
        out_shape.append(jax.ShapeDtypeStruct((n_seq, 2, SSD_HEADS, SSD_HEAD_DIM, SSD_STATE), f32))
        out_specs.append(state_spec)
    res = pl.pallas_call(
        functools.partial(_ssd_kernel, nc, has_h0, emit_state),
        grid=(n_seq, 2, nc),
        in_specs=in_specs,
        out_specs=out_specs,
        out_shape=out_shape,
        scratch_shapes=[pltpu.VMEM((nc, SSD_STATE, SSD_INNER), bf16), pltpu.VMEM((SSD_STATE, SSD_INNER), f32)],
        compiler_params=pltpu.CompilerParams(
            dimension_semantics=("arbitrary", "arbitrary", "arbitrary"), vmem_limit_bytes=VMEM_LIMIT),
        name="ssd_scan",
    )(*args)
    return res if emit_state else res[0]


ML_SW = 2 * ML_V_DIM
ML_DIRS = 2 * ML_HEADS


def _cummax_rows(a, reverse=False):
    row = lax.broadcasted_iota(jnp.int32, a.shape, 0)
    s = 1
    while s < CHUNK:
        if reverse:
            a = jnp.maximum(a, jnp.where(row < CHUNK - s, pltpu.roll(a, CHUNK - s, axis=0), -jnp.inf))
        else:
            a = jnp.maximum(a, jnp.where(row >= s, pltpu.roll(a, s, axis=0), -jnp.inf))
        s *= 2
    return a


def _mlstm_kernel(nc, has_state, emit_state, *refs):
    qk_ref, v_ref, og_ref, gates_ref, gb_ref = refs[:5]
    refs = refs[5:]
    if has_state:
        s0_ref, m0_ref = refs[:2]
        refs = refs[2:]
    y_ref, refs = refs[0], refs[1:]
    if emit_state:
        sfin_ref, mfin_ref = refs[:2]
        refs = refs[2:]
    s_store, m_store, s_carry, m_carry = refs

    phase = pl.program_id(1)
    c = pl.program_id(2)

    lane = lax.broadcasted_iota(jnp.int32, (CHUNK, LANES), 1)
    fwd_lane = lane < ML_HEADS
    fwd_row = lax.broadcasted_iota(jnp.int32, (1, LANES), 1) < ML_HEADS
    g = gates_ref[...] + gb_ref[...]
    li = g
    lf = pltpu.roll(jax.nn.log_sigmoid(g), LANES - ML_DIRS, axis=1)
    cum = jnp.where(fwd_lane, _cumsum_rows(lf), _cumsum_rows(lf, reverse=True))
    r = li - cum
    pm = jnp.where(fwd_lane, _cummax_rows(r), _cummax_rows(r, reverse=True))
    r_t = r.T

    qk = qk_ref[...]
    k_t = qk[:, ML_QK_WIDTH:].T
    ones = jnp.ones((CHUNK, ML_V_DIM), bf16)

    def state_update(m_row, big_m, d):
        edge = CHUNK - 1 if d == 0 else 0
        m_edge = big_m[edge:edge + 1, :]
        wc_row = jnp.exp(m_row - m_edge)
        wk_t = jnp.exp(r_t - big_m.T[:, edge:edge + 1])
        for h in range(ML_HEADS):
            cl = d * ML_HEADS + h
            kw = (k_t[h * ML_QK_DIM:(h + 1) * ML_QK_DIM, :] * wk_t[cl:cl + 1, :]).astype(bf16)
            rhs = jnp.concatenate([v_ref[:, h * ML_V_DIM:(h + 1) * ML_V_DIM], ones], axis=1)
            upd = jnp.dot(kw, rhs, preferred_element_type=f32)
            wc = jnp.broadcast_to(wc_row[:, cl:cl + 1], (ML_QK_DIM, ML_SW))
            s_carry[d, :, h * ML_SW:(h + 1) * ML_SW] = wc * s_carry[d, :, h * ML_SW:(h + 1) * ML_SW] + upd
        return cum[edge:edge + 1, :] + m_edge

    def init_state(d):
        if has_state:
            s_carry[d] = s0_ref[d]
        else:
            s_carry[d] = jnp.zeros((ML_QK_DIM, ML_HEADS * ML_SW), f32)

    @pl.when(jnp.logical_and(phase == 0, c == 0))
    def _():
        init_state(1)
        m_carry[...] = jnp.broadcast_to(m0_ref[...], m_carry.shape) if has_state else jnp.zeros(m_carry.shape, f32)

    @pl.when(phase == 0)
    def _backward_states():
        j = nc - 1 - c
        m_row = m_carry[0:1, :]
        s_store[j] = s_carry[1].astype(bf16)
        m_store[j] = m_carry[...]
        big_m = jnp.maximum(m_row, pm)
        m_new = state_update(m_row, big_m, 1)
        m_carry[...] = jnp.broadcast_to(jnp.where(fwd_row, m_row, m_new), m_carry.shape)

    @pl.when(phase == 1)
    def _forward_and_outputs():
        @pl.when(c == 0)
        def _():
            init_state(0)

        if emit_state:
            @pl.when(c == 0)
            def _():
                sfin_ref[1] = s_carry[1]

        m_both = m_carry[0:1, :]
        m_row = jnp.where(fwd_row, m_both, m_store[c][0:1, :])
        big_m = jnp.maximum(m_row, pm)
        w_inter = jnp.exp(m_row - big_m)
        floor = jnp.exp(-(cum + big_m))
        row = lax.broadcasted_iota(jnp.int32, (CHUNK, CHUNK), 0)
        col = lax.broadcasted_iota(jnp.int32, (CHUNK, CHUNK), 1)
        masks = (col <= row, col >= row)
        q = qk[:, :ML_QK_WIDTH] * (ML_QK_DIM ** -0.5)
        k_bf = qk[:, ML_QK_WIDTH:].astype(bf16)
        for h in range(ML_HEADS):
            p = h // 2
            qp = q[:, p * LANES:(p + 1) * LANES]
            keep = (lane < ML_QK_DIM) if h % 2 == 0 else (lane >= ML_QK_DIM)
            qm = jnp.where(keep, qp, 0.0).astype(bf16)
            s_raw = _dot_nt(qm, k_bf[:, p * LANES:(p + 1) * LANES])
            v_h = v_ref[:, h * ML_V_DIM:(h + 1) * ML_V_DIM]
            out = None
            for d in range(2):
                cl = d * ML_HEADS + h
                st = s_carry[0, :, h * ML_SW:(h + 1) * ML_SW].astype(bf16) if d == 0 \
                    else s_store[c, :, h * ML_SW:(h + 1) * ML_SW]
                inter = jnp.dot(qm, jnp.concatenate([st, st], axis=0), preferred_element_type=f32)
                wmat = jnp.exp(jnp.where(masks[d], r_t[cl:cl + 1, :] - big_m[:, cl:cl + 1], -jnp.inf))
                sw = s_raw * wmat
                wi = w_inter[:, cl:cl + 1]
                num = jnp.dot(sw.astype(bf16), v_h, preferred_element_type=f32) + wi * inter[:, :ML_V_DIM]
                den = jnp.sum(sw, axis=1, keepdims=True) + wi * inter[:, ML_V_DIM:]
                hd = num / jnp.maximum(jnp.abs(den), floor[:, cl:cl + 1])
                out = hd if out is None else out + hd
            og = og_ref[:, h * ML_V_DIM:(h + 1) * ML_V_DIM]
            y_ref[:, h * ML_V_DIM:(h + 1) * ML_V_DIM] = (out * jax.nn.sigmoid(og)).astype(y_ref.dtype)

        m_fin = jnp.where(fwd_row, state_update(m_row, big_m, 0), m_both)
        m_carry[...] = jnp.broadcast_to(m_fin, m_carry.shape)

        if emit_state:
            @pl.when(c == nc - 1)
            def _():
                sfin_ref[0] = s_carry[0]
                mfin_ref[...] = m_fin


def mlstm_mix(qk, v, og, gates, gate_b, state, row0, n_seq, seq_len, emit_state):
    nc = seq_len // CHUNK
    blk0 = row0 // CHUNK
    has_state = state is not None
    sw_all = ML_HEADS * ML_SW

    def chunk_map(s, p, c):
        return (blk0 + s * nc + jnp.where(p == 0, nc - 1 - c, c), 0)

    s_spec = pl.BlockSpec((None, 2, ML_QK_DIM, sw_all), lambda s, p, c: (s, 0, 0, 0))
    m_spec = pl.BlockSpec((None, 1, LANES), lambda s, p, c: (s, 0, 0))
    in_specs = [
        pl.BlockSpec((CHUNK, 2 * ML_QK_WIDTH), chunk_map),
        pl.BlockSpec((CHUNK, ML_V_WIDTH), chunk_map),
        pl.BlockSpec((CHUNK, ML_V_WIDTH), lambda s, p, c: (blk0 + s * nc + p * c, 0)),
        pl.BlockSpec((CHUNK, LANES), chunk_map),
        pl.BlockSpec((1, LANES), lambda s, p, c: (0, 0)),
    ]
    gb =jnp.concatenate([gate_b.reshape(1, 2 * ML_DIRS), jnp.zeros((1, LANES - 2 * ML_DIRS), f32)], axis=1)
    args = [qk, v, og, gates, gb]
    if has_state:
        c0, n0, m0 = state
        tile = jnp.concatenate([jnp.swapaxes(c0, -1, -2),
                                jnp.broadcast_to(n0[..., None], n0.shape + (ML_V_DIM,))], axis=-1)
        s0 = jnp.transpose(tile, (0, 1, 3, 2, 4)).reshape(n_seq, 2, ML_QK_DIM, sw_all)
        m0 = jnp.concatenate([m0.reshape(n_seq, 1, ML_DIRS), jnp.zeros((n_seq, 1, LANES - ML_DIRS), f32)], axis=-1)
        in_specs += [s_spec, m_spec]
        args += [s0, m0]
    out_shape = [jax.ShapeDtypeStruct((n_seq * seq_len, ML_V_WIDTH), bf16)]
    out_specs = [pl.BlockSpec((CHUNK, ML_V_WIDTH), lambda s, p, c: (s * nc + p * c, 0))]
    if emit_state:
        out_shape += [jax.ShapeDtypeStruct((n_seq, 2, ML_QK_DIM, sw_all), f32),
                      jax.ShapeDtypeStruct((n_seq, 1, LANES), f32)]
        out_specs += [s_spec, m_spec]
    res = pl.pallas_call(
        functools.partial(_mlstm_kernel, nc, has_state, emit_state),
        grid=(n_seq, 2, nc),
        in_specs=in_specs,
        out_specs=out_specs,
        out_shape=out_shape,
        scratch_shapes=[pltpu.VMEM((nc, ML_QK_DIM, sw_all), bf16), pltpu.VMEM((nc, SUBLANES, LANES), f32),
                        pltpu.VMEM((2, ML_QK_DIM, sw_all), f32), pltpu.VMEM((SUBLANES, LANES), f32)],
        compiler_params=pltpu.CompilerParams(
            dimension_semantics=("arbitrary", "arbitrary", "arbitrary"), vmem_limit_bytes=VMEM_LIMIT),
        name="mlstm_scan",
    )(*args)
    if not emit_state:
        return res[0]
    y, sfin, mfin = res
    tiles = sfin.reshape(n_seq, 2, ML_QK_DIM, ML_HEADS, ML_SW)
    c_fin = jnp.transpose(tiles[..., :ML_V_DIM], (0, 1, 3, 4, 2))
    n_fin = jnp.transpose(tiles[..., ML_V_DIM], (0, 1, 3, 2))
    m_fin = mfin[:, 0, :ML_DIRS].reshape(n_seq, 2, ML_HEADS)
    return y, c_fin, n_fin, m_fin


def kernel(x_prompt, x_sample, c, cache_na_k, cache_na_v, state_ssd, state_mlstm_c, state_mlstm_n, state_mlstm_m,
           c_ctx, w_mod, b_mod, norm_mix, norm_ffn, w_in_even, w_out_even, na_rpb, ssd_conv_w, ssd_conv_b,
           ssd_dt_bias, ssd_a_log, ssd_d, ssd_norm, w_in_odd, w_out_odd, ml_conv_w, ml_conv_b, ml_gate_b,
           w_ff1, w_ff2, norm_f):
    x = jnp.concatenate([x_prompt.reshape(N_PROMPT, D_MODEL), x_sample.reshape(DEC_BATCH * DEC_SEQ, D_MODEL)], axis=0)
    cond = jnp.concatenate([c_ctx[None, :], c, jnp.zeros((SUBLANES - N_COND, D_MODEL), f32)], axis=0)
    mod = adaln_all(cond, w_mod, b_mod)[:, :N_COND].reshape(DEPTH, N_COND, 1, N_MOD * D_MODEL)

    out_k, out_v, out_ssd, out_c, out_n, out_m = [], [], [], [], [], []
    for l in range(DEPTH):
        last = l == DEPTH - 1
        w1 = w_ff1[l].astype(bf16)
        w2 = w_ff2[l].astype(bf16)
        if l % 2 == 0:
            e = l // 2
            w = w_in_even[e]
            pad = jnp.zeros((D_MODEL, LANES - 2 * SSD_HEADS), f32)
            w = jnp.concatenate([w, pad], axis=1).astype(bf16)
            o0 = 3 * NA_WIDTH
            segs = ((0, NA_WIDTH, NA_WIDTH, NA_HEAD_DIM ** -0.5),
                    (NA_WIDTH, 2 * NA_WIDTH, NA_WIDTH, 1.0),
                    (2 * NA_WIDTH, 3 * NA_WIDTH, NA_WIDTH, 1.0),
                    (o0, o0 + SSD_INNER, SSD_INNER, 1.0),
                    (o0 + SSD_INNER, o0 + SSD_INNER + SSD_CONV_DIM, SSD_CONV_DIM, 1.0),
                    (o0 + SSD_INNER + SSD_CONV_DIM, o0 + SSD_INNER + SSD_CONV_DIM + LANES, LANES, 1.0))
            outs = ((0, bf16, False), (1, bf16, False), (2, bf16, False), (1, f32, True), (2, f32, True),
                    (3, f32, False), (4, f32, False), (5, f32, False))
            q, k, v, kp, vp, z, xbc, dt = in_proj(x, mod[l], norm_mix[l], w, segs, outs)
            heads = lambda t: t.reshape(BATCH, SEQ, NA_HEADS, NA_HEAD_DIM)
            out_k.append(jnp.swapaxes(heads(kp), 1, 2))
            out_v.append(jnp.swapaxes(heads(vp), 1, 2))
            xbc = conv_silu(xbc, ssd_conv_w[e], ssd_conv_b[e])
            ssd_w = (ssd_dt_bias[e], ssd_a_log[e], ssd_d[e], ssd_norm[e])
            y_ssd_p, s_fb = ssd_mix(xbc, dt, z, None, 0, BATCH, SEQ, *ssd_w, True)
            out_ssd.append(s_fb)
            y_ssd_s = ssd_mix(xbc, dt, z, state_ssd[:, e], N_PROMPT, DEC_BATCH, DEC_SEQ, *ssd_w, False)
            tokens = lambda t: jnp.swapaxes(t, 1, 2).reshape(DEC_BATCH, PAST_LEN, NA_WIDTH).astype(bf16)
            y_na_p = ctx_attention(q, k, v)
            y_na_s = na_latent(q, k, v, tokens(cache_na_k[:, e]), tokens(cache_na_v[:, e]), na_bias_table(na_rpb[e]))
            mixes = [jnp.concatenate([y_na_p, y_na_s], axis=0), jnp.concatenate([y_ssd_p, y_ssd_s], axis=0)]
            w_out = w_out_even[e].astype(bf16)
        else:
            o = l // 2
            w = w_in_odd[o]
            pad = jnp.zeros((D_MODEL, LANES - 4 * ML_HEADS), f32)
            w = jnp.concatenate([w, pad], axis=1).astype(bf16)
            a0 = 2 * ML_QK_WIDTH
            segs = ((0, a0, a0, 1.0),
                    (a0, a0 + ML_V_WIDTH, ML_V_WIDTH, 1.0),
                    (a0 + ML_V_WIDTH, a0 + 2 * ML_V_WIDTH, ML_V_WIDTH, 1.0),
                    (a0 + 2 * ML_V_WIDTH, a0 + 2 * ML_V_WIDTH + LANES, LANES, 1.0))
            outs = ((0, f32, False), (1, bf16, False), (2, f32, False), (3, f32, False))
            qk, v, og, gates = in_proj(x, mod[l], norm_mix[l], w, segs, outs)
            qk = conv_silu(qk, ml_conv_w[o], ml_conv_b[o])
            y_p, c_fin, n_fin, m_fin = mlstm_mix(qk, v, og, gates, ml_gate_b[o], None, 0, BATCH, SEQ, True)
            out_c.append(c_fin)
            out_n.append(n_fin)
            out_m.append(m_fin)
            state = (state_mlstm_c[:, o], state_mlstm_n[:, o], state_mlstm_m[:, o])
            y_s = mlstm_mix(qk, v, og, gates, ml_gate_b[o], state, N_PROMPT, DEC_BATCH, DEC_SEQ, False)
            mixes = [jnp.concatenate([y_p, y_s], axis=0)]
            w_out = w_out_odd[o].astype(bf16)
        x = out_mlp(x, mod[l], norm_ffn[l], mixes, w_out, w1, w2, norm_f if last else None)

    y_prompt = x[:N_PROMPT].reshape(BATCH, SEQ, D_MODEL)
    y_sample = x[N_PROMPT:].reshape(DEC_BATCH, DEC_SEQ, D_MODEL)
    return (y_prompt, y_sample, jnp.stack(out_k, axis=1), jnp.stack(out_v, axis=1), jnp.stack(out_ssd, axis=1),
            jnp.stack(out_c, axis=1), jnp.stack(out_n, axis=1), jnp.stack(out_m, axis=1))
```

```python
import functools
import math

import jax
import jax.numpy as jnp
from jax import lax
from jax.experimental import pallas as pl
from jax.experimental.pallas import tpu as pltpu

D_MODEL = 1024
BATCH = 32
SEQ = 256
DEPTH = 4
DEC_BATCH = 2
DEC_SEQ = 4096
PAST_LEN = 256
GRID_W = 64
N_EVEN = (DEPTH + 1) // 2
N_ODD = DEPTH // 2
RMS_EPS = 1e-6
N_MOD = 6
D_FF = 4 * D_MODEL
CONV_K = 3
Q_BLOCK = 128
NA_HEADS = 8
NA_HEAD_DIM = 64
NA_WIDTH = NA_HEADS * NA_HEAD_DIM
NA_WIN_ROWS = 8
NA_WIN_COLS = 16
NA_RPB_ROWS = 2 * NA_WIN_ROWS - 1
NA_RPB_COLS = 2 * NA_WIN_COLS - 1
SSD_INNER = D_MODEL
SSD_HEAD_DIM = 64
SSD_HEADS = SSD_INNER // SSD_HEAD_DIM
SSD_GROUPS = 2
SSD_RPG = SSD_HEADS // SSD_GROUPS
SSD_STATE = 128
SSD_GN = SSD_GROUPS * SSD_STATE
SSD_CONV_DIM = SSD_INNER + 2 * SSD_GN
SSD_CHUNK = 128
ML_HEADS = 8
ML_QK_DIM = D_MODEL // 16
ML_V_DIM = D_MODEL // 8
ML_QK_WIDTH = ML_HEADS * ML_QK_DIM
ML_V_WIDTH = ML_HEADS * ML_V_DIM
ML_CHUNK = 64
EVEN_MIX = NA_WIDTH + SSD_INNER

N_PROMPT = BATCH * SEQ
N_TOK = N_PROMPT + DEC_BATCH * DEC_SEQ
N_COND = 1 + DEC_BATCH
LANES = 128
SUBLANES = 8
VMEM_LIMIT = 56 * 1024 * 1024
TM = 512

f32 = jnp.float32
bf16 = jnp.bfloat16


def _cond_row(i, tm):
    start = i * tm
    return jnp.where(start < N_PROMPT, 0, (start - N_PROMPT) // DEC_SEQ + 1)


def _const_spec(shape):
    nd = len(shape)
    return pl.BlockSpec(shape, lambda i: (0,) * nd, pipeline_mode=pl.Buffered(1))


def _rms(x):
    return x * lax.rsqrt(jnp.mean(x * x, axis=-1, keepdims=True) + RMS_EPS)


def _modulated(x, g, mod, k):
    shift = mod[:, k * D_MODEL:(k + 1) * D_MODEL]
    scale = mod[:, (k + 1) * D_MODEL:(k + 2) * D_MODEL]
    return (_rms(x) * g) * (1.0 + scale) + shift


def _mod_kernel(c_ref, w_ref, b_ref, o_ref):
    c = c_ref[...]
    a = (c * jax.nn.sigmoid(c)).astype(bf16)
    o_ref[...] = jnp.dot(a, w_ref[...].astype(bf16), preferred_element_type=f32) + b_ref[...]


def adaln_all(cond, w_mod, b_mod):
    tn = 1536
    nj = N_MOD * D_MODEL // tn
    return pl.pallas_call(
        _mod_kernel,
        grid=(DEPTH, nj),
        in_specs=[
            pl.BlockSpec((SUBLANES, D_MODEL), lambda l, j: (0, 0)),
            pl.BlockSpec((None, D_MODEL, tn), lambda l, j: (l, 0, j)),
            pl.BlockSpec((None, 1, tn), lambda l, j: (l, 0, j)),
        ],
        out_specs=pl.BlockSpec((None, SUBLANES, tn), lambda l, j: (l, 0, j)),
        out_shape=jax.ShapeDtypeStruct((DEPTH, SUBLANES, N_MOD * D_MODEL), f32),
        compiler_params=pltpu.CompilerParams(
            dimension_semantics=("arbitrary", "arbitrary"), vmem_limit_bytes=VMEM_LIMIT),
        name="adaln",
    )(cond, w_mod, b_mod.reshape(DEPTH, 1, N_MOD * D_MODEL))


def _cast_kernel(x_ref, o_ref):
    o_ref[...] = x_ref[...].astype(o_ref.dtype)


def cast_bf16(w, cols=None):
    n_l, k, n = w.shape
    cols = n if cols is None else cols
    bk = 512
    spec = pl.BlockSpec((None, bk, cols), lambda l, i: (l, i, 0))
    return pl.pallas_call(
        _cast_kernel,
        grid=(n_l, k // bk),
        in_specs=[spec],
        out_specs=spec,
        out_shape=jax.ShapeDtypeStruct((n_l, k, cols), bf16),
        compiler_params=pltpu.CompilerParams(
            dimension_semantics=("arbitrary", "arbitrary"), vmem_limit_bytes=VMEM_LIMIT),
        name="cast_bf16",
    )(w)


def _layer_spec(w, l):
    nd = w.ndim - 1
    return pl.BlockSpec((None,) + w.shape[1:], lambda i: (l,) + (0,) * nd, pipeline_mode=pl.Buffered(1))


def _in_proj_kernel(segs, outs, x_ref, mod_ref, g_ref, w_ref, wt_ref, *o_refs):
    h = _modulated(x_ref[...], g_ref[...], mod_ref[...], 0).astype(bf16)
    ys = []
    for a, b, scale in segs:
        w = wt_ref[...] if a is None else w_ref[:, a:b]
        y = jnp.dot(h, w, preferred_element_type=f32)
        ys.append(y if scale == 1.0 else y * scale)
    for (si, _, prompt_only), o_ref in zip(outs, o_refs):
        if prompt_only:
            @pl.when(pl.program_id(0) < N_PROMPT // TM)
            def _(o_ref=o_ref, si=si):
                o_ref[...] = ys[si].astype(o_ref.dtype)
        else:
            o_ref[...] = ys[si].astype(o_ref.dtype)


def in_proj(x, mod_l, g, w, w_tail, l, segs, outs):
    grid = (N_TOK // TM,)
    last_prompt = N_PROMPT // TM - 1
    out_shape, out_specs = [], []
    for si, dt, prompt_only in outs:
        width = LANES if segs[si][0] is None else segs[si][1] - segs[si][0]
        if prompt_only:
            out_shape.append(jax.ShapeDtypeStruct((N_PROMPT, width), dt))
            out_specs.append(pl.BlockSpec((TM, width), lambda i: (jnp.minimum(i, last_prompt), 0)))
        else:
            out_shape.append(jax.ShapeDtypeStruct((N_TOK, width), dt))
            out_specs.append(pl.BlockSpec((TM, width), lambda i: (i, 0)))
    return pl.pallas_call(
        functools.partial(_in_proj_kernel, segs, outs),
        grid=grid,
        in_specs=[
            pl.BlockSpec((TM, D_MODEL), lambda i: (i, 0)),
            pl.BlockSpec((None, 1, N_MOD * D_MODEL), lambda i: (_cond_row(i, TM), 0, 0)),
            _const_spec((1, D_MODEL)),
            _layer_spec(w, l),
            _layer_spec(w_tail, l),
        ],
        out_specs=out_specs,
        out_shape=out_shape,
        compiler_params=pltpu.CompilerParams(
            dimension_semantics=("arbitrary",), vmem_limit_bytes=VMEM_LIMIT),
        name="in_proj",
    )(x, mod_l, g.reshape(1, D_MODEL), w, w_tail)


def _out_mlp_kernel(n_mix, final, x_ref, mod_ref, g_ref, *refs):
    mix_refs = refs[:n_mix]
    wo_ref, w1_ref, w2_ref = refs[n_mix:n_mix + 3]
    rest = refs[n_mix + 3:]
    if final:
        gf_ref, op_ref, os_ref = rest
    else:
        (o_ref,) = rest
    mod = mod_ref[...]
    m = None
    k0 = 0
    for r in mix_refs:
        kw = r.shape[-1]
        part = jnp.dot(r[...].astype(bf16), wo_ref[k0:k0 + kw, :], preferred_element_type=f32)
        m = part if m is None else m + part
        k0 += kw
    x1 = x_ref[...] + mod[:, 2 * D_MODEL:3 * D_MODEL] * m
    h2 = _modulated(x1, g_ref[...], mod, 3).astype(bf16)
    u = jnp.dot(h2, w1_ref[...], preferred_element_type=f32)
    a = jnp.square(jnp.maximum(u, 0.0)).astype(bf16)
    x2 = x1 + mod[:, 5 * D_MODEL:6 * D_MODEL] * jnp.dot(a, w2_ref[...], preferred_element_type=f32)
    if not final:
        o_ref[...] = x2
        return
    y = _rms(x2) * gf_ref[...]
    is_prompt = pl.program_id(0) < N_PROMPT // OUT_TM

    @pl.when(is_prompt)
    def _():
        op_ref[...] = y

    @pl.when(jnp.logical_not(is_prompt))
    def _():
        os_ref[...] = y


OUT_TM = 512


def out_mlp(x, mod_l, g_ffn, mixes, w_out, l_out, w1, w2, l, norm_f=None):
    final = norm_f is not None
    tm = OUT_TM
    grid = (N_TOK // tm,)
    in_specs = [
        pl.BlockSpec((tm, D_MODEL), lambda i: (i, 0)),
        pl.BlockSpec((None, 1, N_MOD * D_MODEL), lambda i: (_cond_row(i, tm), 0, 0)),
        _const_spec((1, D_MODEL)),
    ]
    in_specs += [pl.BlockSpec((tm, m.shape[-1]), lambda i: (i, 0)) for m in mixes]
    in_specs += [_layer_spec(w_out, l_out), _layer_spec(w1, l), _layer_spec(w2, l)]
    args = [x, mod_l, g_ffn.reshape(1, D_MODEL), *mixes, w_out, w1, w2]
    out_specs = pl.BlockSpec((tm, D_MODEL), lambda i: (i, 0))
    out_shape = jax.ShapeDtypeStruct((N_TOK, D_MODEL), f32)
    if final:
        in_specs.append(_const_spec((1, D_MODEL)))
        args.append(norm_f.reshape(1, D_MODEL))
        n_p = N_PROMPT // tm
        out_specs = [pl.BlockSpec((tm, D_MODEL), lambda i: (jnp.minimum(i, n_p - 1), 0)),
                     pl.BlockSpec((tm, D_MODEL), lambda i: (jnp.maximum(i - n_p, 0), 0))]
        out_shape = [jax.ShapeDtypeStruct((N_PROMPT, D_MODEL), f32),
                     jax.ShapeDtypeStruct((N_TOK - N_PROMPT, D_MODEL), f32)]
    return pl.pallas_call(
        functools.partial(_out_mlp_kernel, len(mixes), final),
        grid=grid,
        in_specs=in_specs,
        out_specs=out_specs,
        out_shape=out_shape,
        compiler_params=pltpu.CompilerParams(
            dimension_semantics=("arbitrary",), vmem_limit_bytes=VMEM_LIMIT),
        name="out_mlp",
    )(*args)


NA_PAIRS = NA_HEADS // 2
NA_ROWS = DEC_SEQ // GRID_W
NA_WIN = NA_WIN_ROWS * GRID_W


def _dot_nt(a, b):
    return lax.dot_general(a, b, (((1,), (1,)), ((), ())), preferred_element_type=f32)


def _pair_stack(x):
    lane = lax.broadcasted_iota(jnp.int32, x.shape, 1)
    zero = jnp.zeros_like(x)
    return jnp.concatenate([jnp.where(lane < NA_HEAD_DIM, x, zero), jnp.where(lane >= NA_HEAD_DIM, x, zero)], axis=0)


def _pair_unstack(o):
    n = o.shape[0] // 2
    lane = lax.broadcasted_iota(jnp.int32, (n, LANES), 1)
    return jnp.where(lane < NA_HEAD_DIM, o[:n], o[n:])


def _softmax_pv(scores, values):
    m = None
    for s in scores:
        mi = jnp.max(s, axis=1, keepdims=True)
        m = mi if m is None else jnp.maximum(m, mi)
    ps = [jnp.exp(s - m) for s in scores]
    l = None
    for p in ps:
        li = jnp.sum(p, axis=1, keepdims=True)
        l = li if l is None else l + li
    inv = 1.0 / l
    o = None
    for p, v in zip(ps, values):
        oi = jnp.dot((p * inv).astype(bf16), v, preferred_element_type=f32)
        o = oi if o is None else o + oi
    return o


def _ctx_attn_kernel(q_ref, k_ref, v_ref, o_ref):
    for p in range(NA_PAIRS):
        lanes = slice(p * LANES, (p + 1) * LANES)
        qq = _pair_stack(q_ref[:, lanes])
        s = _dot_nt(qq, k_ref[:, lanes])
        o = _softmax_pv([s], [v_ref[:, lanes]])
        o_ref[:, lanes] = _pair_unstack(o).astype(o_ref.dtype)


def ctx_attention(q, k, v):
    spec = pl.BlockSpec((SEQ, NA_WIDTH), lambda b: (b, 0))
    return pl.pallas_call(
        _ctx_attn_kernel,
        grid=(BATCH,),
        in_specs=[spec, spec, spec],
        out_specs=spec,
        out_shape=jax.ShapeDtypeStruct((N_TOK, NA_WIDTH), bf16),
        compiler_params=pltpu.CompilerParams(dimension_semantics=("arbitrary",), vmem_limit_bytes=VMEM_LIMIT),
        name="ctx_attn",
    )(q, k, v)


def _na_bias_kernel(rpb_ref, o_ref):
    pair = pl.program_id(0)
    shape = (GRID_W, LANES)
    qc = lax.broadcasted_iota(jnp.int32, shape, 0)
    lane = lax.broadcasted_iota(jnp.int32, shape, 1)
    kc = lane & (GRID_W - 1)
    low = lane < GRID_W
    col_start = jnp.clip(qc - NA_WIN_COLS // 2, 0, GRID_W - NA_WIN_COLS)
    valid = (kc >= col_start) & (kc < col_start + NA_WIN_COLS)
    rel_c = jnp.clip(kc - qc + NA_WIN_COLS - 1, 0, NA_RPB_COLS - 1)
    for e in range(2):
        base = (2 * pair + e) * NA_RPB_ROWS
        pieces = []
        for rr in range(NA_RPB_ROWS - 1):
            val = jnp.zeros(shape, f32)
            for t in range(NA_RPB_COLS):
                s_lo = rpb_ref[(base + rr) * NA_RPB_COLS + t]
                s_hi = rpb_ref[(base + rr + 1) * NA_RPB_COLS + t]
                val = jnp.where(rel_c == t, jnp.where(low, s_lo, s_hi), val)
            pieces.append(jnp.where(valid, val, -jnp.inf))
        for d in range(NA_WIN_ROWS):
            for i in range(0, NA_WIN_ROWS, 2):
                o_ref[d, e * GRID_W:(e + 1) * GRID_W, i * GRID_W:(i + 2) * GRID_W] = pieces[d + i]


def na_bias_table(rpb):
    return pl.pallas_call(
        _na_bias_kernel,
        grid=(NA_PAIRS,),
        in_specs=[pl.BlockSpec(memory_space=pltpu.SMEM)],
        out_specs=pl.BlockSpec((None, NA_WIN_ROWS, 2 * GRID_W, NA_WIN), lambda p: (p, 0, 0, 0)),
        out_shape=jax.ShapeDtypeStruct((NA_PAIRS, NA_WIN_ROWS, 2 * GRID_W, NA_WIN), f32),
        compiler_params=pltpu.CompilerParams(dimension_semantics=("arbitrary",), vmem_limit_bytes=VMEM_LIMIT),
        name="na_bias",
    )(rpb.reshape(-1))


NA_STEP_ROWS = 4


def _na_first_key_row(r):
    return jnp.clip(r - NA_WIN_ROWS // 2, 0, NA_ROWS - NA_WIN_ROWS)


def _na_kernel(q_ref, k_ref, v_ref, kc_ref, vc_ref, bias_ref, buf_ref, o_ref):
    del buf_ref
    for j in range(NA_STEP_ROWS):
        r = pl.program_id(1) * NA_STEP_ROWS + j
        first = _na_first_key_row(r)
        start = pl.multiple_of(first * GRID_W, GRID_W)
        shift = first - r + NA_WIN_ROWS - 1
        rows = slice(j * GRID_W, (j + 1) * GRID_W)
        for p in range(NA_PAIRS):
            lanes = slice(p * LANES, (p + 1) * LANES)
            qq = _pair_stack(q_ref[rows, lanes])
            s_loc = _dot_nt(qq, k_ref[pl.ds(start, NA_WIN), lanes]) + bias_ref[p, shift]
            s_ctx = _dot_nt(qq, kc_ref[:, lanes])
            o = _softmax_pv([s_loc, s_ctx], [v_ref[pl.ds(start, NA_WIN), lanes], vc_ref[:, lanes]])
            o_ref[rows, lanes] = _pair_unstack(o).astype(o_ref.dtype)


def na_latent(q, k, v, k_ctx, v_ctx, bias, buf):
    rows = NA_STEP_ROWS * GRID_W
    steps = NA_ROWS // NA_STEP_ROWS
    row0 = N_PROMPT // rows
    seq0 = N_PROMPT // DEC_SEQ
    kv_spec = pl.BlockSpec((DEC_SEQ, NA_WIDTH), lambda b, r: (seq0 + b, 0))
    ctx_spec = pl.BlockSpec((None, PAST_LEN, NA_WIDTH), lambda b, r: (b, 0, 0))
    return pl.pallas_call(
        _na_kernel,
        grid=(DEC_BATCH, steps),
        in_specs=[
            pl.BlockSpec((rows, NA_WIDTH), lambda b, r: (row0 + b * steps + r, 0)),
            kv_spec, kv_spec, ctx_spec, ctx_spec,
            pl.BlockSpec(bias.shape, lambda b, r: (0, 0, 0, 0), pipeline_mode=pl.Buffered(1)),
            pl.BlockSpec(memory_space=pl.ANY),
        ],
        out_specs=pl.BlockSpec((rows, NA_WIDTH), lambda b, r: (row0 + b * steps + r, 0)),
        out_shape=jax.ShapeDtypeStruct((N_TOK, NA_WIDTH), bf16),
        input_output_aliases={6: 0},
        compiler_params=pltpu.CompilerParams(
            dimension_semantics=("arbitrary", "arbitrary"), vmem_limit_bytes=VMEM_LIMIT),
        name="na_latent",
    )(q, k, v, k_ctx, v_ctx, bias, buf)


CONV_ROWS = SEQ


def _conv_silu_kernel(x_ref, prev_ref, next_ref, w_ref, b_ref, o_ref):
    i = pl.program_id(0)
    start = i * CONV_ROWS
    in_sample = start >= N_PROMPT
    off = start - N_PROMPT
    first = jnp.logical_or(jnp.logical_not(in_sample), off % DEC_SEQ == 0)
    last = jnp.logical_or(jnp.logical_not(in_sample), (off + CONV_ROWS) % DEC_SEQ == 0)
    x = x_ref[...]
    row = lax.broadcasted_iota(jnp.int32, x.shape, 0)
    prev_row = jnp.where(first, 0.0, prev_ref[SUBLANES - 1:SUBLANES, :])
    next_row = jnp.where(last, 0.0, next_ref[0:1, :])
    up = jnp.where(row == 0, prev_row, pltpu.roll(x, 1, axis=0))
    dn = jnp.where(row == CONV_ROWS - 1, next_row, pltpu.roll(x, CONV_ROWS - 1, axis=0))
    y = up * w_ref[0:1, :] + x * w_ref[1:2, :] + dn * w_ref[2:3, :] + b_ref[...]
    o_ref[...] = (y * jax.nn.sigmoid(y)).astype(o_ref.dtype)


def conv_silu(x, w, b):
    n, ch = x.shape
    per = CONV_ROWS // SUBLANES
    nblk8 = n // SUBLANES
    return pl.pallas_call(
        _conv_silu_kernel,
        grid=(n // CONV_ROWS,),
        in_specs=[
            pl.BlockSpec((CONV_ROWS, ch), lambda i: (i, 0)),
            pl.BlockSpec((SUBLANES, ch), lambda i: (jnp.maximum(i * per - 1, 0), 0)),
            pl.BlockSpec((SUBLANES, ch), lambda i: (jnp.minimum((i + 1) * per, nblk8 - 1), 0)),
            _const_spec((CONV_K, ch)),
            _const_spec((1, ch)),
        ],
        out_specs=pl.BlockSpec((CONV_ROWS, ch), lambda i: (i, 0)),
        out_shape=jax.ShapeDtypeStruct((n, ch), f32),
        compiler_params=pltpu.CompilerParams(dimension_semantics=("arbitrary",), vmem_limit_bytes=VMEM_LIMIT),
        name="conv_silu",
    )(x, x, x, w, b.reshape(1, ch))


CHUNK = 128


def _cumsum_rows(a, reverse=False):
    row = lax.broadcasted_iota(jnp.int32, a.shape, 0)
    s = 1
    while s < CHUNK:
        if reverse:
            a = a + jnp.where(row < CHUNK - s, pltpu.roll(a, CHUNK - s, axis=0), 0.0)
        else:
            a = a + jnp.where(row >= s, pltpu.roll(a, s, axis=0), 0.0)
        s *= 2
    return a


def _expand_heads(v, off, n_heads, width):
    shape = (v.shape[0], LANES)
    col = lambda h: jnp.broadcast_to(v[:, off + h:off + h + 1], shape)
    if width == LANES:
        return jnp.concatenate([col(h) for h in range(n_heads)], axis=1)
    assert 2 * width == LANES
    lane = lax.broadcasted_iota(jnp.int32, shape, 1)
    return jnp.concatenate(
        [jnp.where(lane < width, col(h), col(h + 1)) for h in range(0, n_heads, 2)], axis=1)


def _ssd_kernel(nc, has_h0, has_buf, emit_state, *refs):
    xbc_ref, dt_ref, z_ref = refs[:3]
    refs = refs[3:]
    if has_h0:
        h0_ref, refs = refs[0], refs[1:]
    dtb_ref, alog_ref, dskip_ref, g_ref = refs[:4]
    refs = refs[5:] if has_buf else refs[4:]
    y_ref, refs = refs[0], refs[1:]
    if emit_state:
        hfin_ref, refs = refs[0], refs[1:]
    hb_store, carry = refs

    phase = pl.program_id(1)
    c = pl.program_id(2)
    gw = SSD_RPG * SSD_HEAD_DIM

    xbc = xbc_ref[...]
    x = xbc[:, :SSD_INNER]
    x_bf = x.astype(bf16)
    b_mat = xbc[:, SSD_INNER:SSD_INNER + SSD_GN]
    dt = jax.nn.softplus(dt_ref[...] + dtb_ref[...])
    a = dt * (-jnp.exp(alog_ref[...]))

    def load_h0(d):
        if has_h0:
            return h0_ref[d].reshape(SSD_INNER, SSD_STATE).T
        return jnp.zeros((SSD_STATE, SSD_INNER), f32)

    def state_update(cum, edge, off):
        cum_e = _expand_heads(cum, off, SSD_HEADS, SSD_HEAD_DIM)
        edge_row = cum_e[edge:edge + 1, :]
        w_end = jnp.exp(edge_row - cum_e) * _expand_heads(dt, off, SSD_HEADS, SSD_HEAD_DIM)
        xw = (x * w_end).astype(bf16)
        parts = []
        for g in range(SSD_GROUPS):
            bt = b_mat[:, g * SSD_STATE:(g + 1) * SSD_STATE].T.astype(bf16)
            parts.append(jnp.dot(bt, xw[:, g * gw:(g + 1) * gw], preferred_element_type=f32))
        carry[...] = carry[...] * jnp.exp(edge_row) + jnp.concatenate(parts, axis=1)
        return cum_e

    @pl.when(phase == 0)
    def _backward_states():
        @pl.when(c == 0)
        def _():
            carry[...] = load_h0(1)

        hb_store[nc - 1 - c] = carry[...].astype(bf16)
        state_update(_cumsum_rows(a, reverse=True), 0, SSD_HEADS)

        if emit_state:
            @pl.when(c == nc - 1)
            def _():
                hfin_ref[1] = carry[...].T.reshape(SSD_HEADS, SSD_HEAD_DIM, SSD_STATE)

    @pl.when(phase == 1)
    def _forward_and_outputs():
        @pl.when(c == 0)
        def _():
            carry[...] = load_h0(0)

        c_mat = xbc[:, SSD_INNER + SSD_GN:].astype(bf16)
        b_bf = b_mat.astype(bf16)
        cum = _cumsum_rows(a)
        rcum = _cumsum_rows(a, reverse=True)
        cum_t, rcum_t, dt_t = cum.T, rcum.T, dt.T
        row = lax.broadcasted_iota(jnp.int32, (CHUNK, CHUNK), 0)
        col = lax.broadcasted_iota(jnp.int32, (CHUNK, CHUNK), 1)
        causal = col <= row
        anti = col >= row
        lane = lax.broadcasted_iota(jnp.int32, (CHUNK, LANES), 1)
        hf = carry[...].astype(bf16)
        hb = hb_store[c]
        cb = [_dot_nt(c_mat[:, g * SSD_STATE:(g + 1) * SSD_STATE], b_bf[:, g * SSD_STATE:(g + 1) * SSD_STATE])
              for g in range(SSD_GROUPS)]

        y_parts = []
        for p in range(SSD_HEADS // 2):
            g = (2 * p) // SSD_RPG
            ws = []
            for h in (2 * p, 2 * p + 1):
                hb_i = SSD_HEADS + h
                seg_f = jnp.where(causal, cum[:, h:h + 1] - cum_t[h:h + 1, :], -jnp.inf)
                seg_b = jnp.where(anti, rcum[:, hb_i:hb_i + 1] - rcum_t[hb_i:hb_i + 1, :], -jnp.inf)
                w = jnp.exp(seg_f) * dt_t[h:h + 1, :] + jnp.exp(seg_b) * dt_t[hb_i:hb_i + 1, :]
                ws.append((cb[g] * w).astype(bf16))
            xp = x_bf[:, p * LANES:(p + 1) * LANES]
            zero = jnp.zeros_like(xp)
            rhs = jnp.concatenate([jnp.where(lane < SSD_HEAD_DIM, xp, zero),
                                   jnp.where(lane >= SSD_HEAD_DIM, xp, zero)], axis=0)
            y_parts.append(jnp.dot(jnp.concatenate(ws, axis=1), rhs, preferred_element_type=f32))
        y = jnp.concatenate(y_parts, axis=1)

        inter_f = jnp.concatenate(
            [jnp.dot(c_mat[:, g * SSD_STATE:(g + 1) * SSD_STATE], hf[:, g * gw:(g + 1) * gw],
                     preferred_element_type=f32) for g in range(SSD_GROUPS)], axis=1)
        inter_b = jnp.concatenate(
            [jnp.dot(c_mat[:, g * SSD_STATE:(g + 1) * SSD_STATE], hb[:, g * gw:(g + 1) * gw],
                     preferred_element_type=f32) for g in range(SSD_GROUPS)], axis=1)
        cum_e = state_update(cum, CHUNK - 1, 0)
        rcum_e = _expand_heads(rcum, SSD_HEADS, SSD_HEADS, SSD_HEAD_DIM)
        y = y + inter_f * jnp.exp(cum_e) + inter_b * jnp.exp(rcum_e) + dskip_ref[...] * x

        zv = z_ref[...]
        yz = y * (zv * jax.nn.sigmoid(zv))
        y_ref[...] = (_rms(yz) * g_ref[...]).astype(y_ref.dtype)

        if emit_state:
            @pl.when(c == nc - 1)
            def _():
                hfin_ref[0] = carry[...].T.reshape(SSD_HEADS, SSD_HEAD_DIM, SSD_STATE)


def ssd_mix(xbc, dt, z, h0, row0, n_seq, seq_len, dt_bias, a_log, d_skip, norm_g, emit_state, out_buf=None):
    nc = seq_len // CHUNK
    blk0 = row0 // CHUNK
    has_h0 = h0 is not None

    def chunk_map(s, p, c):
        return (blk0 + s * nc + jnp.where(p == 0, nc - 1 - c, c), 0)

    state_spec = pl.BlockSpec((None, 2, SSD_HEADS, SSD_HEAD_DIM, SSD_STATE), lambda s, p, c: (s, 0, 0, 0, 0))
    vec = lambda n: pl.BlockSpec((1, n), lambda s, p, c: (0, 0))
    in_specs = [
        pl.BlockSpec((CHUNK, SSD_CONV_DIM), chunk_map),
        pl.BlockSpec((CHUNK, LANES), chunk_map),
        pl.BlockSpec((CHUNK, SSD_INNER), lambda s, p, c: (blk0 + s * nc + p * c, 0)),
    ]
    args = [xbc, dt, z]
    if has_h0:
        in_specs.append(state_spec)
        args.append(h0)
    in_specs += [vec(LANES), vec(LANES), vec(SSD_INNER), vec(SSD_INNER)]
    pad = lambda t: jnp.concatenate([t.reshape(1, -1), jnp.zeros((1, LANES - t.size), f32)], axis=1)
    args += [pad(dt_bias), pad(a_log), jnp.repeat(d_skip, SSD_HEAD_DIM).reshape(1, SSD_INNER),
             norm_g.reshape(1, SSD_INNER)]
    aliases = {}
    if out_buf is not None:
        aliases = {len(args): 0}
        in_specs.append(pl.BlockSpec(memory_space=pl.ANY))
        args.append(out_buf)
    out_shape = [jax.ShapeDtypeStruct((N_TOK, SSD_INNER), bf16)]
    out_specs = [pl.BlockSpec((CHUNK, SSD_INNER), lambda s, p, c: (blk0 + s * nc + p * c, 0))]
    if emit_state:
        out_shape.append(jax.ShapeDtypeStruct((n_seq, 2, SSD_HEADS, SSD_HEAD_DIM, SSD_STATE), f32))
        out_specs.append(state_spec)
    res = pl.pallas_call(
        functools.partial(_ssd_kernel, nc, has_h0, out_buf is not None, emit_state),
        input_output_aliases=aliases,
        grid=(n_seq, 2, nc),
        in_specs=in_specs,
        out_specs=out_specs,
        out_shape=out_shape,
        scratch_shapes=[pltpu.VMEM((nc, SSD_STATE, SSD_INNER), bf16), pltpu.VMEM((SSD_STATE, SSD_INNER), f32)],
        compiler_params=pltpu.CompilerParams(
            dimension_semantics=("arbitrary", "arbitrary", "arbitrary"), vmem_limit_bytes=VMEM_LIMIT),
        name="ssd_scan",
    )(*args)
    return res if emit_state else res[0]


ML_SW = 2 * ML_V_DIM
ML_DIRS = 2 * ML_HEADS


def _cummax_rows(a, reverse=False):
    row = lax.broadcasted_iota(jnp.int32, a.shape, 0)
    s = 1
    while s < CHUNK:
        if reverse:
            a = jnp.maximum(a, jnp.where(row < CHUNK - s, pltpu.roll(a, CHUNK - s, axis=0), -jnp.inf))
        else:
            a = jnp.maximum(a, jnp.where(row >= s, pltpu.roll(a, s, axis=0), -jnp.inf))
        s *= 2
    return a


def _mlstm_kernel(nc, has_state, has_buf, emit_state, *refs):
    qk_ref, v_ref, og_ref, gates_ref, gb_ref = refs[:5]
    refs = refs[5:]
    if has_state:
        s0_ref, m0_ref = refs[:2]
        refs = refs[2:]
    if has_buf:
        refs = refs[1:]
    y_ref, refs = refs[0], refs[1:]
    if emit_state:
        sfin_ref, mfin_ref = refs[:2]
        refs = refs[2:]
    s_store, m_store, s_carry, m_carry = refs

    phase = pl.program_id(1)
    c = pl.program_id(2)

    lane = lax.broadcasted_iota(jnp.int32, (CHUNK, LANES), 1)
    fwd_lane = lane < ML_HEADS
    fwd_row = lax.broadcasted_iota(jnp.int32, (1, LANES), 1) < ML_HEADS
    g = gates_ref[...] + gb_ref[...]
    li = g
    lf = pltpu.roll(jax.nn.log_sigmoid(g), LANES - ML_DIRS, axis=1)
    cum = jnp.where(fwd_lane, _cumsum_rows(lf), _cumsum_rows(lf, reverse=True))
    r = li - cum
    pm = jnp.where(fwd_lane, _cummax_rows(r), _cummax_rows(r, reverse=True))
    r_t = r.T

    qk = qk_ref[...]
    k_t = qk[:, ML_QK_WIDTH:].T
    ones = jnp.ones((CHUNK, ML_V_DIM), bf16)

    def state_update(m_row, big_m, d):
        edge = CHUNK - 1 if d == 0 else 0
        m_edge = big_m[edge:edge + 1, :]
        wc_row = jnp.exp(m_row - m_edge)
        wk_t = jnp.exp(r_t - big_m.T[:, edge:edge + 1])
        for h in range(ML_HEADS):
            cl = d * ML_HEADS + h
            kw = (k_t[h * ML_QK_DIM:(h + 1) * ML_QK_DIM, :] * wk_t[cl:cl + 1, :]).astype(bf16)
            rhs = jnp.concatenate([v_ref[:, h * ML_V_DIM:(h + 1) * ML_V_DIM], ones], axis=1)
            upd = jnp.dot(kw, rhs, preferred_element_type=f32)
            wc = jnp.broadcast_to(wc_row[:, cl:cl + 1], (ML_QK_DIM, ML_SW))
            s_carry[d, :, h * ML_SW:(h + 1) * ML_SW] = wc * s_carry[d, :, h * ML_SW:(h + 1) * ML_SW] + upd
        return cum[edge:edge + 1, :] + m_edge

    def init_state(d):
        if has_state:
            s_carry[d] = s0_ref[d]
        else:
            s_carry[d] = jnp.zeros((ML_QK_DIM, ML_HEADS * ML_SW), f32)

    @pl.when(jnp.logical_and(phase == 0, c == 0))
    def _():
        init_state(1)
        m_carry[...] = jnp.broadcast_to(m0_ref[...], m_carry.shape) if has_state else jnp.zeros(m_carry.shape, f32)

    @pl.when(phase == 0)
    def _backward_states():
        j = nc - 1 - c
        m_row = m_carry[0:1, :]
        s_store[j] = s_carry[1].astype(bf16)
        m_store[j] = m_carry[...]
        big_m = jnp.maximum(m_row, pm)
        m_new = state_update(m_row, big_m, 1)
        m_carry[...] = jnp.broadcast_to(jnp.where(fwd_row, m_row, m_new), m_carry.shape)

    @pl.when(phase == 1)
    def _forward_and_outputs():
        @pl.when(c == 0)
        def _():
            init_state(0)

        if emit_state:
            @pl.when(c == 0)
            def _():
                sfin_ref[1] = s_carry[1]

        m_both = m_carry[0:1, :]
        m_row = jnp.where(fwd_row, m_both, m_store[c][0:1, :])
        big_m = jnp.maximum(m_row, pm)
        w_inter = jnp.exp(m_row - big_m)
        floor = jnp.exp(-(cum + big_m))
        row = lax.broadcasted_iota(jnp.int32, (CHUNK, CHUNK), 0)
        col = lax.broadcasted_iota(jnp.int32, (CHUNK, CHUNK), 1)
        masks = (col <= row, col >= row)
        q = qk[:, :ML_QK_WIDTH] * (ML_QK_DIM ** -0.5)
        k_bf = qk[:, ML_QK_WIDTH:].astype(bf16)
        for h in range(ML_HEADS):
            p = h // 2
            qp = q[:, p * LANES:(p + 1) * LANES]
            keep = (lane < ML_QK_DIM) if h % 2 == 0 else (lane >= ML_QK_DIM)
            qm = jnp.where(keep, qp, 0.0).astype(bf16)
            s_raw = _dot_nt(qm, k_bf[:, p * LANES:(p + 1) * LANES])
            v_h = v_ref[:, h * ML_V_DIM:(h + 1) * ML_V_DIM]
            out = None
            for d in range(2):
                cl = d * ML_HEADS + h
                st = s_carry[0, :, h * ML_SW:(h + 1) * ML_SW].astype(bf16) if d == 0 \
                    else s_store[c, :, h * ML_SW:(h + 1) * ML_SW]
                inter = jnp.dot(qm, jnp.concatenate([st, st], axis=0), preferred_element_type=f32)
                wmat = jnp.exp(jnp.where(masks[d], r_t[cl:cl + 1, :] - big_m[:, cl:cl + 1], -jnp.inf))
                sw = s_raw * wmat
                wi = w_inter[:, cl:cl + 1]
                num = jnp.dot(sw.astype(bf16), v_h, preferred_element_type=f32) + wi * inter[:, :ML_V_DIM]
                den = jnp.sum(sw, axis=1, keepdims=True) + wi * inter[:, ML_V_DIM:]
                hd = num / jnp.maximum(jnp.abs(den), floor[:, cl:cl + 1])
                out = hd if out is None else out + hd
            og = og_ref[:, h * ML_V_DIM:(h + 1) * ML_V_DIM]
            y_ref[:, h * ML_V_DIM:(h + 1) * ML_V_DIM] = (out * jax.nn.sigmoid(og)).astype(y_ref.dtype)

        m_fin = jnp.where(fwd_row, state_update(m_row, big_m, 0), m_both)
        m_carry[...] = jnp.broadcast_to(m_fin, m_carry.shape)

        if emit_state:
            @pl.when(c == nc - 1)
            def _():
                sfin_ref[0] = s_carry[0]
                mfin_ref[...] = m_fin


def mlstm_mix(qk, v, og, gates, gate_b, state, row0, n_seq, seq_len, emit_state, out_buf=None):
    nc = seq_len // CHUNK
    blk0 = row0 // CHUNK
    has_state = state is not None
    sw_all = ML_HEADS * ML_SW

    def chunk_map(s, p, c):
        return (blk0 + s * nc + jnp.where(p == 0, nc - 1 - c, c), 0)

    s_spec = pl.BlockSpec((None, 2, ML_QK_DIM, sw_all), lambda s, p, c: (s, 0, 0, 0))
    m_spec = pl.BlockSpec((None, 1, LANES), lambda s, p, c: (s, 0, 0))
    in_specs = [
        pl.BlockSpec((CHUNK, 2 * ML_QK_WIDTH), chunk_map),
        pl.BlockSpec((CHUNK, ML_V_WIDTH), chunk_map),
        pl.BlockSpec((CHUNK, ML_V_WIDTH), lambda s, p, c: (blk0 + s * nc + p * c, 0)),
        pl.BlockSpec((CHUNK, LANES), chunk_map),
        pl.BlockSpec((1, LANES), lambda s, p, c: (0, 0)),
    ]
    gb =jnp.concatenate([gate_b.reshape(1, 2 * ML_DIRS), jnp.zeros((1, LANES - 2 * ML_DIRS), f32)], axis=1)
    args = [qk, v, og, gates, gb]
    if has_state:
        c0, n0, m0 = state
        tile = jnp.concatenate([jnp.swapaxes(c0, -1, -2),
                                jnp.broadcast_to(n0[..., None], n0.shape + (ML_V_DIM,))], axis=-1)
        s0 = jnp.transpose(tile, (0, 1, 3, 2, 4)).reshape(n_seq, 2, ML_QK_DIM, sw_all)
        m0 = jnp.concatenate([m0.reshape(n_seq, 1, ML_DIRS), jnp.zeros((n_seq, 1, LANES - ML_DIRS), f32)], axis=-1)
        in_specs += [s_spec, m_spec]
        args += [s0, m0]
    aliases = {}
    if out_buf is not None:
        aliases = {len(args): 0}
        in_specs.append(pl.BlockSpec(memory_space=pl.ANY))
        args.append(out_buf)
    out_shape = [jax.ShapeDtypeStruct((N_TOK, ML_V_WIDTH), bf16)]
    out_specs = [pl.BlockSpec((CHUNK, ML_V_WIDTH), lambda s, p, c: (blk0 + s * nc + p * c, 0))]
    if emit_state:
        out_shape += [jax.ShapeDtypeStruct((n_seq, 2, ML_QK_DIM, sw_all), f32),
                      jax.ShapeDtypeStruct((n_seq, 1, LANES), f32)]
        out_specs += [s_spec, m_spec]
    res = pl.pallas_call(
        functools.partial(_mlstm_kernel, nc, has_state, out_buf is not None, emit_state),
        input_output_aliases=aliases,
        grid=(n_seq, 2, nc),
        in_specs=in_specs,
        out_specs=out_specs,
        out_shape=out_shape,
        scratch_shapes=[pltpu.VMEM((nc, ML_QK_DIM, sw_all), bf16), pltpu.VMEM((nc, SUBLANES, LANES), f32),
                        pltpu.VMEM((2, ML_QK_DIM, sw_all), f32), pltpu.VMEM((SUBLANES, LANES), f32)],
        compiler_params=pltpu.CompilerParams(
            dimension_semantics=("arbitrary", "arbitrary", "arbitrary"), vmem_limit_bytes=VMEM_LIMIT),
        name="mlstm_scan",
    )(*args)
    if not emit_state:
        return res[0]
    y, sfin, mfin = res
    tiles = sfin.reshape(n_seq, 2, ML_QK_DIM, ML_HEADS, ML_SW)
    c_fin = jnp.transpose(tiles[..., :ML_V_DIM], (0, 1, 3, 4, 2))
    n_fin = jnp.transpose(tiles[..., ML_V_DIM], (0, 1, 3, 2))
    m_fin = mfin[:, 0, :ML_DIRS].reshape(n_seq, 2, ML_HEADS)
    return y, c_fin, n_fin, m_fin


def kernel(x_prompt, x_sample, c, cache_na_k, cache_na_v, state_ssd, state_mlstm_c, state_mlstm_n, state_mlstm_m,
           c_ctx, w_mod, b_mod, norm_mix, norm_ffn, w_in_even, w_out_even, na_rpb, ssd_conv_w, ssd_conv_b,
           ssd_dt_bias, ssd_a_log, ssd_d, ssd_norm, w_in_odd, w_out_odd, ml_conv_w, ml_conv_b, ml_gate_b,
           w_ff1, w_ff2, norm_f):
    x = jnp.concatenate([x_prompt.reshape(N_PROMPT, D_MODEL), x_sample.reshape(DEC_BATCH * DEC_SEQ, D_MODEL)], axis=0)
    cond = jnp.concatenate([c_ctx[None, :], c, jnp.zeros((SUBLANES - N_COND, D_MODEL), f32)], axis=0)
    mod = adaln_all(cond, w_mod, b_mod)[:, :N_COND].reshape(DEPTH, N_COND, 1, N_MOD * D_MODEL)

    even_main = 3 * NA_WIDTH + SSD_INNER + SSD_CONV_DIM
    odd_main = 2 * ML_QK_WIDTH + 2 * ML_V_WIDTH

    def tail_bf16(w, main):
        t = w[:, :, main:]
        return jnp.concatenate([t, jnp.zeros(t.shape[:2] + (LANES - t.shape[2],), f32)], axis=2).astype(bf16)

    wi_even, wt_even = cast_bf16(w_in_even, even_main), tail_bf16(w_in_even, even_main)
    wi_odd, wt_odd = cast_bf16(w_in_odd, odd_main), tail_bf16(w_in_odd, odd_main)
    wo_even, wo_odd = cast_bf16(w_out_even), cast_bf16(w_out_odd)
    w1_all, w2_all = cast_bf16(w_ff1), cast_bf16(w_ff2)

    out_k, out_v, out_ssd, out_c, out_n, out_m = [], [], [], [], [], []
    for l in range(DEPTH):
        norm_last = norm_f if l == DEPTH - 1 else None
        if l % 2 == 0:
            e = l // 2
            o0 = 3 * NA_WIDTH
            segs = ((0, NA_WIDTH, NA_HEAD_DIM ** -0.5),
                    (NA_WIDTH, 2 * NA_WIDTH, 1.0),
                    (2 * NA_WIDTH, 3 * NA_WIDTH, 1.0),
                    (o0, o0 + SSD_INNER, 1.0),
                    (o0 + SSD_INNER, o0 + SSD_INNER + SSD_CONV_DIM, 1.0),
                    (None, None, 1.0))
            outs = ((0, bf16, False), (1, bf16, False), (2, bf16, False), (1, f32, True), (2, f32, True),
                    (3, f32, False), (4, f32, False), (5, f32, False))
            q, k, v, kp, vp, z, xbc, dt = in_proj(x, mod[l], norm_mix[l], wi_even, wt_even, e, segs, outs)
            heads = lambda t: t.reshape(BATCH, SEQ, NA_HEADS, NA_HEAD_DIM)
            out_k.append(jnp.swapaxes(heads(kp), 1, 2))
            out_v.append(jnp.swapaxes(heads(vp), 1, 2))
            xbc = conv_silu(xbc, ssd_conv_w[e], ssd_conv_b[e])
            ssd_w = (ssd_dt_bias[e], ssd_a_log[e], ssd_d[e], ssd_norm[e])
            y_ssd, s_fb = ssd_mix(xbc, dt, z, None, 0, BATCH, SEQ, *ssd_w, True)
            out_ssd.append(s_fb)
            y_ssd = ssd_mix(xbc, dt, z, state_ssd[:, e], N_PROMPT, DEC_BATCH, DEC_SEQ, *ssd_w, False, out_buf=y_ssd)
            tokens = lambda t: jnp.swapaxes(t, 1, 2).reshape(DEC_BATCH, PAST_LEN, NA_WIDTH).astype(bf16)
            y_na = ctx_attention(q, k, v)
            y_na = na_latent(q, k, v, tokens(cache_na_k[:, e]), tokens(cache_na_v[:, e]),
                             na_bias_table(na_rpb[e]), y_na)
            res = out_mlp(x, mod[l], norm_ffn[l], [y_na, y_ssd], wo_even, e, w1_all, w2_all, l, norm_last)
        else:
            o = l // 2
            a0 = 2 * ML_QK_WIDTH
            segs = ((0, a0, 1.0),
                    (a0, a0 + ML_V_WIDTH, 1.0),
                    (a0 + ML_V_WIDTH, a0 + 2 * ML_V_WIDTH, 1.0),
                    (None, None, 1.0))
            outs = ((0, f32, False), (1, bf16, False), (2, f32, False), (3, f32, False))
            qk, v, og, gates = in_proj(x, mod[l], norm_mix[l], wi_odd, wt_odd, o, segs, outs)
            qk = conv_silu(qk, ml_conv_w[o], ml_conv_b[o])
            y_ml, c_fin, n_fin, m_fin = mlstm_mix(qk, v, og, gates, ml_gate_b[o], None, 0, BATCH, SEQ, True)
            out_c.append(c_fin)
            out_n.append(n_fin)
            out_m.append(m_fin)
            state = (state_mlstm_c[:, o], state_mlstm_n[:, o], state_mlstm_m[:, o])
            y_ml = mlstm_mix(qk, v, og, gates, ml_gate_b[o], state, N_PROMPT, DEC_BATCH, DEC_SEQ, False, out_buf=y_ml)
            res = out_mlp(x, mod[l], norm_ffn[l], [y_ml], wo_odd, o, w1_all, w2_all, l, norm_last)
        x = res

    y_prompt = x[0].reshape(BATCH, SEQ, D_MODEL)
    y_sample = x[1].reshape(DEC_BATCH, DEC_SEQ, D_MODEL)
    return (y_prompt, y_sample, jnp.stack(out_k, axis=1), jnp.stack(out_v, axis=1), jnp.stack(out_ssd, axis=1),
            jnp.stack(out_c, axis=1), jnp.stack(out_n, axis=1), jnp.stack(out_m, axis=1))
```

```python
import functools
import math

import jax
import jax.numpy as jnp
from jax import lax
from jax.experimental import pallas as pl
from jax.experimental.pallas import tpu as pltpu

D_MODEL = 1024
BATCH = 32
SEQ = 256
DEPTH = 4
DEC_BATCH = 2
DEC_SEQ = 4096
PAST_LEN = 256
GRID_W = 64
N_EVEN = (DEPTH + 1) // 2
N_ODD = DEPTH // 2
RMS_EPS = 1e-6
N_MOD = 6
D_FF = 4 * D_MODEL
CONV_K = 3
Q_BLOCK = 128
NA_HEADS = 8
NA_HEAD_DIM = 64
NA_WIDTH = NA_HEADS * NA_HEAD_DIM
NA_WIN_ROWS = 8
NA_WIN_COLS = 16
NA_RPB_ROWS = 2 * NA_WIN_ROWS - 1
NA_RPB_COLS = 2 * NA_WIN_COLS - 1
SSD_INNER = D_MODEL
SSD_HEAD_DIM = 64
SSD_HEADS = SSD_INNER // SSD_HEAD_DIM
SSD_GROUPS = 2
SSD_RPG = SSD_HEADS // SSD_GROUPS
SSD_STATE = 128
SSD_GN = SSD_GROUPS * SSD_STATE
SSD_CONV_DIM = SSD_INNER + 2 * SSD_GN
SSD_CHUNK = 128
ML_HEADS = 8
ML_QK_DIM = D_MODEL // 16
ML_V_DIM = D_MODEL // 8
ML_QK_WIDTH = ML_HEADS * ML_QK_DIM
ML_V_WIDTH = ML_HEADS * ML_V_DIM
ML_CHUNK = 64
EVEN_MIX = NA_WIDTH + SSD_INNER

N_PROMPT = BATCH * SEQ
N_TOK = N_PROMPT + DEC_BATCH * DEC_SEQ
N_COND = 1 + DEC_BATCH
LANES = 128
SUBLANES = 8
VMEM_LIMIT = 56 * 1024 * 1024
TM = 512

f32 = jnp.float32
bf16 = jnp.bfloat16


def _cond_row(i, tm):
    start = i * tm
    return jnp.where(start < N_PROMPT, 0, (start - N_PROMPT) // DEC_SEQ + 1)


def _const_spec(shape):
    nd = len(shape)
    return pl.BlockSpec(shape, lambda i: (0,) * nd, pipeline_mode=pl.Buffered(1))


def _rms(x):
    return x * lax.rsqrt(jnp.mean(x * x, axis=-1, keepdims=True) + RMS_EPS)


def _modulated(x, g, mod, k):
    shift = mod[:, k * D_MODEL:(k + 1) * D_MODEL]
    scale = mod[:, (k + 1) * D_MODEL:(k + 2) * D_MODEL]
    return (_rms(x) * g) * (1.0 + scale) + shift


def _mod_kernel(c_ref, w_ref, b_ref, o_ref):
    c = c_ref[...]
    a = (c * jax.nn.sigmoid(c)).astype(bf16)
    o_ref[...] = jnp.dot(a, w_ref[...].astype(bf16), preferred_element_type=f32) + b_ref[...]


def adaln_all(cond, w_mod, b_mod):
    tn = 1536
    nj = N_MOD * D_MODEL // tn
    return pl.pallas_call(
        _mod_kernel,
        grid=(DEPTH, nj),
        in_specs=[
            pl.BlockSpec((SUBLANES, D_MODEL), lambda l, j: (0, 0)),
            pl.BlockSpec((None, D_MODEL, tn), lambda l, j: (l, 0, j)),
            pl.BlockSpec((None, 1, tn), lambda l, j: (l, 0, j)),
        ],
        out_specs=pl.BlockSpec((None, SUBLANES, tn), lambda l, j: (l, 0, j)),
        out_shape=jax.ShapeDtypeStruct((DEPTH, SUBLANES, N_MOD * D_MODEL), f32),
        compiler_params=pltpu.CompilerParams(
            dimension_semantics=("arbitrary", "arbitrary"), vmem_limit_bytes=VMEM_LIMIT),
        name="adaln",
    )(cond, w_mod, b_mod.reshape(DEPTH, 1, N_MOD * D_MODEL))


def _cast_kernel(x_ref, o_ref):
    o_ref[...] = x_ref[...].astype(o_ref.dtype)


def cast_bf16(w, cols=None):
    n_l, k, n = w.shape
    cols = n if cols is None else cols
    bk = 512
    spec = pl.BlockSpec((None, bk, cols), lambda l, i: (l, i, 0))
    return pl.pallas_call(
        _cast_kernel,
        grid=(n_l, k // bk),
        in_specs=[spec],
        out_specs=spec,
        out_shape=jax.ShapeDtypeStruct((n_l, k, cols), bf16),
        compiler_params=pltpu.CompilerParams(
            dimension_semantics=("arbitrary", "arbitrary"), vmem_limit_bytes=VMEM_LIMIT),
        name="cast_bf16",
    )(w)


def _layer_spec(w, l):
    nd = w.ndim - 1
    return pl.BlockSpec((None,) + w.shape[1:], lambda i: (l,) + (0,) * nd, pipeline_mode=pl.Buffered(1))


def _tile_x(x_refs, tm):
    if len(x_refs) == 1:
        return x_refs[0][...]
    return jnp.where(pl.program_id(0) < N_PROMPT // tm, x_refs[0][...], x_refs[1][...])


def _conv_silu_tile(y, prev_row, next_row, w_ref, b_ref):
    i = pl.program_id(0)
    rows = y.shape[0]
    g = i * rows + lax.broadcasted_iota(jnp.int32, (rows, 1), 0)
    is_prompt = i < N_PROMPT // rows
    pos = jnp.where(is_prompt, g & (SEQ - 1), (g - N_PROMPT) & (DEC_SEQ - 1))
    seq_last = jnp.where(is_prompt, SEQ - 1, DEC_SEQ - 1)
    row = lax.broadcasted_iota(jnp.int32, y.shape, 0)
    up = jnp.where(row == 0, prev_row, pltpu.roll(y, 1, axis=0))
    dn = jnp.where(row == rows - 1, next_row, pltpu.roll(y, rows - 1, axis=0))
    up = jnp.where(pos == 0, 0.0, up)
    dn = jnp.where(pos == seq_last, 0.0, dn)
    c = up * w_ref[0:1, :] + y * w_ref[1:2, :] + dn * w_ref[2:3, :] + b_ref[...]
    return c * jax.nn.sigmoid(c)


def _in_proj_kernel(n_x, segs, outs, *refs):
    x_refs = refs[:n_x]
    prev_ref, next_ref, mod_ref, g_ref, w_ref, wt_ref, cw_ref, cb_ref = refs[n_x:n_x + 8]
    o_refs = refs[n_x + 8:]
    mod, g = mod_ref[...], g_ref[...]
    h = _modulated(_tile_x(x_refs, TM), g, mod, 0).astype(bf16)
    ys = []
    for a, b, scale, conv in segs:
        w = wt_ref[...] if a is None else w_ref[:, a:b]
        if conv:
            halo = jnp.concatenate([prev_ref[...], next_ref[...]], axis=0)
            hh = _modulated(halo, g, mod, 0).astype(bf16)
            y = jnp.dot(jnp.concatenate([h, hh], axis=0), w, preferred_element_type=f32)
            y = _conv_silu_tile(y[:TM], y[TM + SUBLANES - 1:TM + SUBLANES], y[TM + SUBLANES:TM + SUBLANES + 1],
                                cw_ref, cb_ref)
        else:
            y = jnp.dot(h, w, preferred_element_type=f32)
        ys.append(y if scale == 1.0 else y * scale)
    for (si, _, prompt_only), o_ref in zip(outs, o_refs):
        if prompt_only:
            @pl.when(pl.program_id(0) < N_PROMPT // TM)
            def _(o_ref=o_ref, si=si):
                o_ref[...] = ys[si].astype(o_ref.dtype)
        else:
            o_ref[...] = ys[si].astype(o_ref.dtype)


def _x_specs(xs, tm):
    if len(xs) == 1:
        return [pl.BlockSpec((tm, D_MODEL), lambda i: (i, 0))]
    n_p = N_PROMPT // tm
    return [pl.BlockSpec((tm, D_MODEL), lambda i: (jnp.minimum(i, n_p - 1), 0)),
            pl.BlockSpec((tm, D_MODEL), lambda i: (jnp.maximum(i - n_p, 0), 0))]


def in_proj(xs, mod_l, g, w, w_tail, l, segs, outs, conv_w, conv_b):
    grid = (N_TOK // TM,)
    last_prompt = N_PROMPT // TM - 1
    halo_src = xs[-1]
    per = TM // SUBLANES
    blk0 = (halo_src.shape[0] - DEC_BATCH * DEC_SEQ) // SUBLANES
    n_blk = halo_src.shape[0] // SUBLANES
    tile0 = N_PROMPT // TM

    def prev_map(i):
        return (jnp.clip(blk0 + (i - tile0) * per - 1, 0, n_blk - 1), 0)

    def next_map(i):
        return (jnp.clip(blk0 + (i - tile0 + 1) * per, 0, n_blk - 1), 0)

    ch = conv_w.shape[1]
    out_shape, out_specs = [], []
    for si, dt, prompt_only in outs:
        width = LANES if segs[si][0] is None else segs[si][1] - segs[si][0]
        if prompt_only:
            out_shape.append(jax.ShapeDtypeStruct((N_PROMPT, width), dt))
            out_specs.append(pl.BlockSpec((TM, width), lambda i: (jnp.minimum(i, last_prompt), 0)))
        else:
            out_shape.append(jax.ShapeDtypeStruct((N_TOK, width), dt))
            out_specs.append(pl.BlockSpec((TM, width), lambda i: (i, 0)))
    return pl.pallas_call(
        functools.partial(_in_proj_kernel, len(xs), segs, outs),
        grid=grid,
        in_specs=_x_specs(xs, TM) + [
            pl.BlockSpec((SUBLANES, D_MODEL), prev_map),
            pl.BlockSpec((SUBLANES, D_MODEL), next_map),
            pl.BlockSpec((None, 1, N_MOD * D_MODEL), lambda i: (_cond_row(i, TM), 0, 0)),
            _const_spec((1, D_MODEL)),
            _layer_spec(w, l),
            _layer_spec(w_tail, l),
            _const_spec((CONV_K, ch)),
            _const_spec((1, ch)),
        ],
        out_specs=out_specs,
        out_shape=out_shape,
        compiler_params=pltpu.CompilerParams(
            dimension_semantics=("arbitrary",), vmem_limit_bytes=VMEM_LIMIT),
        name="in_proj",
    )(*xs, halo_src, halo_src, mod_l, g.reshape(1, D_MODEL), w, w_tail, conv_w, conv_b.reshape(1, ch))


def _out_mlp_kernel(n_x, n_mix, final, *refs):
    x_refs = refs[:n_x]
    mod_ref, g_ref = refs[n_x:n_x + 2]
    refs = refs[n_x + 2:]
    mix_refs = refs[:n_mix]
    wo_ref, w1_ref, w2_ref = refs[n_mix:n_mix + 3]
    rest = refs[n_mix + 3:]
    if final:
        gf_ref, op_ref, os_ref = rest
    else:
        (o_ref,) = rest
    mod = mod_ref[...]
    m = None
    k0 = 0
    for r in mix_refs:
        kw = r.shape[-1]
        part = jnp.dot(r[...].astype(bf16), wo_ref[k0:k0 + kw, :], preferred_element_type=f32)
        m = part if m is None else m + part
        k0 += kw
    x1 = _tile_x(x_refs, OUT_TM) + mod[:, 2 * D_MODEL:3 * D_MODEL] * m
    h2 = _modulated(x1, g_ref[...], mod, 3).astype(bf16)
    u = jnp.dot(h2, w1_ref[...], preferred_element_type=f32)
    a = jnp.square(jnp.maximum(u, 0.0)).astype(bf16)
    x2 = x1 + mod[:, 5 * D_MODEL:6 * D_MODEL] * jnp.dot(a, w2_ref[...], preferred_element_type=f32)
    if not final:
        o_ref[...] = x2
        return
    y = _rms(x2) * gf_ref[...]
    is_prompt = pl.program_id(0) < N_PROMPT // OUT_TM

    @pl.when(is_prompt)
    def _():
        op_ref[...] = y

    @pl.when(jnp.logical_not(is_prompt))
    def _():
        os_ref[...] = y


OUT_TM = 512


def out_mlp(xs, mod_l, g_ffn, mixes, w_out, l_out, w1, w2, l, norm_f=None):
    final = norm_f is not None
    tm = OUT_TM
    grid = (N_TOK // tm,)
    in_specs = _x_specs(xs, tm) + [
        pl.BlockSpec((None, 1, N_MOD * D_MODEL), lambda i: (_cond_row(i, tm), 0, 0)),
        _const_spec((1, D_MODEL)),
    ]
    in_specs += [pl.BlockSpec((tm, m.shape[-1]), lambda i: (i, 0)) for m in mixes]
    in_specs += [_layer_spec(w_out, l_out), _layer_spec(w1, l), _layer_spec(w2, l)]
    args = [*xs, mod_l, g_ffn.reshape(1, D_MODEL), *mixes, w_out, w1, w2]
    out_specs = pl.BlockSpec((tm, D_MODEL), lambda i: (i, 0))
    out_shape = jax.ShapeDtypeStruct((N_TOK, D_MODEL), f32)
    if final:
        in_specs.append(_const_spec((1, D_MODEL)))
        args.append(norm_f.reshape(1, D_MODEL))
        n_p = N_PROMPT // tm
        out_specs = [pl.BlockSpec((tm, D_MODEL), lambda i: (jnp.minimum(i, n_p - 1), 0)),
                     pl.BlockSpec((tm, D_MODEL), lambda i: (jnp.maximum(i - n_p, 0), 0))]
        out_shape = [jax.ShapeDtypeStruct((N_PROMPT, D_MODEL), f32),
                     jax.ShapeDtypeStruct((N_TOK - N_PROMPT, D_MODEL), f32)]
    return pl.pallas_call(
        functools.partial(_out_mlp_kernel, len(xs), len(mixes), final),
        grid=grid,
        in_specs=in_specs,
        out_specs=out_specs,
        out_shape=out_shape,
        compiler_params=pltpu.CompilerParams(
            dimension_semantics=("arbitrary",), vmem_limit_bytes=VMEM_LIMIT),
        name="out_mlp",
    )(*args)


NA_PAIRS = NA_HEADS // 2
NA_ROWS = DEC_SEQ // GRID_W
NA_WIN = NA_WIN_ROWS * GRID_W


def _dot_nt(a, b):
    return lax.dot_general(a, b, (((1,), (1,)), ((), ())), preferred_element_type=f32)


def _pair_stack(x):
    lane = lax.broadcasted_iota(jnp.int32, x.shape, 1)
    zero = jnp.zeros_like(x)
    return jnp.concatenate([jnp.where(lane < NA_HEAD_DIM, x, zero), jnp.where(lane >= NA_HEAD_DIM, x, zero)], axis=0)


def _pair_unstack(o):
    n = o.shape[0] // 2
    lane = lax.broadcasted_iota(jnp.int32, (n, LANES), 1)
    return jnp.where(lane < NA_HEAD_DIM, o[:n], o[n:])


def _softmax_pv(scores, values):
    m = None
    for s in scores:
        mi = jnp.max(s, axis=1, keepdims=True)
        m = mi if m is None else jnp.maximum(m, mi)
    ps = [jnp.exp(s - m) for s in scores]
    l = None
    for p in ps:
        li = jnp.sum(p, axis=1, keepdims=True)
        l = li if l is None else l + li
    inv = 1.0 / l
    o = None
    for p, v in zip(ps, values):
        oi = jnp.dot((p * inv).astype(bf16), v, preferred_element_type=f32)
        o = oi if o is None else o + oi
    return o


def _ctx_attn_kernel(q_ref, k_ref, v_ref, o_ref):
    for p in range(NA_PAIRS):
        lanes = slice(p * LANES, (p + 1) * LANES)
        qq = _pair_stack(q_ref[:, lanes])
        s = _dot_nt(qq, k_ref[:, lanes])
        o = _softmax_pv([s], [v_ref[:, lanes]])
        o_ref[:, lanes] = _pair_unstack(o).astype(o_ref.dtype)


def ctx_attention(q, k, v):
    spec = pl.BlockSpec((SEQ, NA_WIDTH), lambda b: (b, 0))
    return pl.pallas_call(
        _ctx_attn_kernel,
        grid=(BATCH,),
        in_specs=[spec, spec, spec],
        out_specs=spec,
        out_shape=jax.ShapeDtypeStruct((N_TOK, NA_WIDTH), bf16),
        compiler_params=pltpu.CompilerParams(dimension_semantics=("arbitrary",), vmem_limit_bytes=VMEM_LIMIT),
        name="ctx_attn",
    )(q, k, v)


def _na_bias_kernel(rpb_ref, o_ref):
    pair = pl.program_id(0)
    shape = (GRID_W, LANES)
    qc = lax.broadcasted_iota(jnp.int32, shape, 0)
    lane = lax.broadcasted_iota(jnp.int32, shape, 1)
    kc = lane & (GRID_W - 1)
    low = lane < GRID_W
    col_start = jnp.clip(qc - NA_WIN_COLS // 2, 0, GRID_W - NA_WIN_COLS)
    valid = (kc >= col_start) & (kc < col_start + NA_WIN_COLS)
    rel_c = jnp.clip(kc - qc + NA_WIN_COLS - 1, 0, NA_RPB_COLS - 1)
    for e in range(2):
        base = (2 * pair + e) * NA_RPB_ROWS
        pieces = []
        for rr in range(NA_RPB_ROWS - 1):
            val = jnp.zeros(shape, f32)
            for t in range(NA_RPB_COLS):
                s_lo = rpb_ref[(base + rr) * NA_RPB_COLS + t]
                s_hi = rpb_ref[(base + rr + 1) * NA_RPB_COLS + t]
                val = jnp.where(rel_c == t, jnp.where(low, s_lo, s_hi), val)
            pieces.append(jnp.where(valid, val, -jnp.inf))
        for d in range(NA_WIN_ROWS):
            for i in range(0, NA_WIN_ROWS, 2):
                o_ref[d, e * GRID_W:(e + 1) * GRID_W, i * GRID_W:(i + 2) * GRID_W] = pieces[d + i]


def na_bias_table(rpb):
    return pl.pallas_call(
        _na_bias_kernel,
        grid=(NA_PAIRS,),
        in_specs=[pl.BlockSpec(memory_space=pltpu.SMEM)],
        out_specs=pl.BlockSpec((None, NA_WIN_ROWS, 2 * GRID_W, NA_WIN), lambda p: (p, 0, 0, 0)),
        out_shape=jax.ShapeDtypeStruct((NA_PAIRS, NA_WIN_ROWS, 2 * GRID_W, NA_WIN), f32),
        compiler_params=pltpu.CompilerParams(dimension_semantics=("arbitrary",), vmem_limit_bytes=VMEM_LIMIT),
        name="na_bias",
    )(rpb.reshape(-1))


NA_STEP_ROWS = 4


def _na_first_key_row(r):
    return jnp.clip(r - NA_WIN_ROWS // 2, 0, NA_ROWS - NA_WIN_ROWS)


def _na_kernel(q_ref, k_ref, v_ref, kc_ref, vc_ref, bias_ref, buf_ref, o_ref):
    del buf_ref
    for j in range(NA_STEP_ROWS):
        r = pl.program_id(1) * NA_STEP_ROWS + j
        first = _na_first_key_row(r)
        start = pl.multiple_of(first * GRID_W, GRID_W)
        shift = first - r + NA_WIN_ROWS - 1
        rows = slice(j * GRID_W, (j + 1) * GRID_W)
        for p in range(NA_PAIRS):
            lanes = slice(p * LANES, (p + 1) * LANES)
            qq = _pair_stack(q_ref[rows, lanes])
            s_loc = _dot_nt(qq, k_ref[pl.ds(start, NA_WIN), lanes]) + bias_ref[p, shift]
            s_ctx = _dot_nt(qq, kc_ref[:, lanes])
            o = _softmax_pv([s_loc, s_ctx], [v_ref[pl.ds(start, NA_WIN), lanes], vc_ref[:, lanes]])
            o_ref[rows, lanes] = _pair_unstack(o).astype(o_ref.dtype)


def na_latent(q, k, v, k_ctx, v_ctx, bias, buf):
    rows = NA_STEP_ROWS * GRID_W
    steps = NA_ROWS // NA_STEP_ROWS
    row0 = N_PROMPT // rows
    seq0 = N_PROMPT // DEC_SEQ
    kv_spec = pl.BlockSpec((DEC_SEQ, NA_WIDTH), lambda b, r: (seq0 + b, 0))
    ctx_spec = pl.BlockSpec((None, PAST_LEN, NA_WIDTH), lambda b, r: (b, 0, 0))
    return pl.pallas_call(
        _na_kernel,
        grid=(DEC_BATCH, steps),
        in_specs=[
            pl.BlockSpec((rows, NA_WIDTH), lambda b, r: (row0 + b * steps + r, 0)),
            kv_spec, kv_spec, ctx_spec, ctx_spec,
            pl.BlockSpec(bias.shape, lambda b, r: (0, 0, 0, 0), pipeline_mode=pl.Buffered(1)),
            pl.BlockSpec(memory_space=pl.ANY),
        ],
        out_specs=pl.BlockSpec((rows, NA_WIDTH), lambda b, r: (row0 + b * steps + r, 0)),
        out_shape=jax.ShapeDtypeStruct((N_TOK, NA_WIDTH), bf16),
        input_output_aliases={6: 0},
        compiler_params=pltpu.CompilerParams(
            dimension_semantics=("arbitrary", "arbitrary"), vmem_limit_bytes=VMEM_LIMIT),
        name="na_latent",
    )(q, k, v, k_ctx, v_ctx, bias, buf)


CHUNK = 128


def _cumsum_rows(a, reverse=False):
    row = lax.broadcasted_iota(jnp.int32, a.shape, 0)
    s = 1
    while s < CHUNK:
        if reverse:
            a = a + jnp.where(row < CHUNK - s, pltpu.roll(a, CHUNK - s, axis=0), 0.0)
        else:
            a = a + jnp.where(row >= s, pltpu.roll(a, s, axis=0), 0.0)
        s *= 2
    return a


def _expand_heads(v, off, n_heads, width):
    shape = (v.shape[0], LANES)
    col = lambda h: jnp.broadcast_to(v[:, off + h:off + h + 1], shape)
    if width == LANES:
        return jnp.concatenate([col(h) for h in range(n_heads)], axis=1)
    assert 2 * width == LANES
    lane = lax.broadcasted_iota(jnp.int32, shape, 1)
    return jnp.concatenate(
        [jnp.where(lane < width, col(h), col(h + 1)) for h in range(0, n_heads, 2)], axis=1)


def _ssd_kernel(nc, has_h0, n_buf, emit_state, *refs):
    xbc_ref, dt_ref, z_ref = refs[:3]
    refs = refs[3:]
    if has_h0:
        h0_ref, refs = refs[0], refs[1:]
    dtb_ref, alog_ref, dskip_ref, g_ref = refs[:4]
    refs = refs[4 + n_buf:]
    y_ref, refs = refs[0], refs[1:]
    if emit_state:
        hfin_ref, refs = refs[0], refs[1:]
    hb_store, carry = refs

    phase = pl.program_id(1)
    c = pl.program_id(2)
    gw = SSD_RPG * SSD_HEAD_DIM

    xbc = xbc_ref[...]
    x = xbc[:, :SSD_INNER]
    x_bf = x.astype(bf16)
    b_mat = xbc[:, SSD_INNER:SSD_INNER + SSD_GN]
    dt = jax.nn.softplus(dt_ref[...] + dtb_ref[...])
    a = dt * (-jnp.exp(alog_ref[...]))

    def load_h0(d):
        if has_h0:
            return h0_ref[d].reshape(SSD_INNER, SSD_STATE).T
        return jnp.zeros((SSD_STATE, SSD_INNER), f32)

    def state_update(cum, edge, off):
        cum_e = _expand_heads(cum, off, SSD_HEADS, SSD_HEAD_DIM)
        edge_row = cum_e[edge:edge + 1, :]
        w_end = jnp.exp(edge_row - cum_e) * _expand_heads(dt, off, SSD_HEADS, SSD_HEAD_DIM)
        xw = (x * w_end).astype(bf16)
        parts = []
        for g in range(SSD_GROUPS):
            bt = b_mat[:, g * SSD_STATE:(g + 1) * SSD_STATE].T.astype(bf16)
            parts.append(jnp.dot(bt, xw[:, g * gw:(g + 1) * gw], preferred_element_type=f32))
        carry[...] = carry[...] * jnp.exp(edge_row) + jnp.concatenate(parts, axis=1)
        return cum_e

    @pl.when(phase == 0)
    def _backward_states():
        @pl.when(c == 0)
        def _():
            carry[...] = load_h0(1)

        hb_store[nc - 1 - c] = carry[...].astype(bf16)
        state_update(_cumsum_rows(a, reverse=True), 0, SSD_HEADS)

        if emit_state:
            @pl.when(c == nc - 1)
            def _():
                hfin_ref[1] = carry[...].T.reshape(SSD_HEADS, SSD_HEAD_DIM, SSD_STATE)

    @pl.when(phase == 1)
    def _forward_and_outputs():
        @pl.when(c == 0)
        def _():
            carry[...] = load_h0(0)

        c_mat = xbc[:, SSD_INNER + SSD_GN:].astype(bf16)
        b_bf = b_mat.astype(bf16)
        cum = _cumsum_rows(a)
        rcum = _cumsum_rows(a, reverse=True)
        cum_t, rcum_t, dt_t = cum.T, rcum.T, dt.T
        row = lax.broadcasted_iota(jnp.int32, (CHUNK, CHUNK), 0)
        col = lax.broadcasted_iota(jnp.int32, (CHUNK, CHUNK), 1)
        causal = col <= row
        anti = col >= row
        lane = lax.broadcasted_iota(jnp.int32, (CHUNK, LANES), 1)
        hf = carry[...].astype(bf16)
        hb = hb_store[c]
        cb = [_dot_nt(c_mat[:, g * SSD_STATE:(g + 1) * SSD_STATE], b_bf[:, g * SSD_STATE:(g + 1) * SSD_STATE])
              for g in range(SSD_GROUPS)]

        y_parts = []
        for p in range(SSD_HEADS // 2):
            g = (2 * p) // SSD_RPG
            ws = []
            for h in (2 * p, 2 * p + 1):
                hb_i = SSD_HEADS + h
                seg_f = jnp.where(causal, cum[:, h:h + 1] - cum_t[h:h + 1, :], -jnp.inf)
                seg_b = jnp.where(anti, rcum[:, hb_i:hb_i + 1] - rcum_t[hb_i:hb_i + 1, :], -jnp.inf)
                w = jnp.exp(seg_f) * dt_t[h:h + 1, :] + jnp.exp(seg_b) * dt_t[hb_i:hb_i + 1, :]
                ws.append((cb[g] * w).astype(bf16))
            xp = x_bf[:, p * LANES:(p + 1) * LANES]
            zero = jnp.zeros_like(xp)
            rhs = jnp.concatenate([jnp.where(lane < SSD_HEAD_DIM, xp, zero),
                                   jnp.where(lane >= SSD_HEAD_DIM, xp, zero)], axis=0)
            y_parts.append(jnp.dot(jnp.concatenate(ws, axis=1), rhs, preferred_element_type=f32))
        y = jnp.concatenate(y_parts, axis=1)

        inter_f = jnp.concatenate(
            [jnp.dot(c_mat[:, g * SSD_STATE:(g + 1) * SSD_STATE], hf[:, g * gw:(g + 1) * gw],
                     preferred_element_type=f32) for g in range(SSD_GROUPS)], axis=1)
        inter_b = jnp.concatenate(
            [jnp.dot(c_mat[:, g * SSD_STATE:(g + 1) * SSD_STATE], hb[:, g * gw:(g + 1) * gw],
                     preferred_element_type=f32) for g in range(SSD_GROUPS)], axis=1)
        cum_e = state_update(cum, CHUNK - 1, 0)
        rcum_e = _expand_heads(rcum, SSD_HEADS, SSD_HEADS, SSD_HEAD_DIM)
        y = y + inter_f * jnp.exp(cum_e) + inter_b * jnp.exp(rcum_e) + dskip_ref[...] * x

        zv = z_ref[...]
        yz = y * (zv * jax.nn.sigmoid(zv))
        y_ref[...] = (_rms(yz) * g_ref[...]).astype(y_ref.dtype)

        if emit_state:
            @pl.when(c == nc - 1)
            def _():
                hfin_ref[0] = carry[...].T.reshape(SSD_HEADS, SSD_HEAD_DIM, SSD_STATE)


def ssd_mix(xbc, dt, z, h0, row0, n_seq, seq_len, dt_bias, a_log, d_skip, norm_g, emit_state, out_buf=None,
            layer=0, state_buf=None):
    nc = seq_len // CHUNK
    blk0 = row0 // CHUNK
    has_h0 = h0 is not None

    def chunk_map(s, p, c):
        return (blk0 + s * nc + jnp.where(p == 0, nc - 1 - c, c), 0)

    state_spec = pl.BlockSpec((None, 2, SSD_HEADS, SSD_HEAD_DIM, SSD_STATE), lambda s, p, c: (s, 0, 0, 0, 0))
    vec = lambda n: pl.BlockSpec((1, n), lambda s, p, c: (0, 0))
    in_specs = [
        pl.BlockSpec((CHUNK, SSD_CONV_DIM), chunk_map),
        pl.BlockSpec((CHUNK, LANES), chunk_map),
        pl.BlockSpec((CHUNK, SSD_INNER), lambda s, p, c: (blk0 + s * nc + p * c, 0)),
    ]
    args = [xbc, dt, z]
    if has_h0:
        in_specs.append(state_spec)
        args.append(h0)
    in_specs += [vec(LANES), vec(LANES), vec(SSD_INNER), vec(SSD_INNER)]
    pad = lambda t: jnp.concatenate([t.reshape(1, -1), jnp.zeros((1, LANES - t.size), f32)], axis=1)
    args += [pad(dt_bias), pad(a_log), jnp.repeat(d_skip, SSD_HEAD_DIM).reshape(1, SSD_INNER),
             norm_g.reshape(1, SSD_INNER)]
    aliases = {}
    if out_buf is not None:
        aliases = {len(args): 0}
        in_specs.append(pl.BlockSpec(memory_space=pl.ANY))
        args.append(out_buf)
    out_shape = [jax.ShapeDtypeStruct((N_TOK, SSD_INNER), bf16)]
    out_specs = [pl.BlockSpec((CHUNK, SSD_INNER), lambda s, p, c: (blk0 + s * nc + p * c, 0))]
    if emit_state:
        if state_buf is not None:
            aliases[len(args)] = 1
            in_specs.append(pl.BlockSpec(memory_space=pl.ANY))
            args.append(state_buf)
        out_shape.append(jax.ShapeDtypeStruct((n_seq, N_EVEN, 2, SSD_HEADS, SSD_HEAD_DIM, SSD_STATE), f32))
        out_specs.append(pl.BlockSpec((None, None, 2, SSD_HEADS, SSD_HEAD_DIM, SSD_STATE),
                                      lambda s, p, c: (s, layer, 0, 0, 0, 0)))
    res = pl.pallas_call(
        functools.partial(_ssd_kernel, nc, has_h0, len(aliases), emit_state),
        input_output_aliases=aliases,
        grid=(n_seq, 2, nc),
        in_specs=in_specs,
        out_specs=out_specs,
        out_shape=out_shape,
        scratch_shapes=[pltpu.VMEM((nc, SSD_STATE, SSD_INNER), bf16), pltpu.VMEM((SSD_STATE, SSD_INNER), f32)],
        compiler_params=pltpu.CompilerParams(
            dimension_semantics=("arbitrary", "arbitrary", "arbitrary"), vmem_limit_bytes=VMEM_LIMIT),
        name="ssd_scan",
    )(*args)
    return res if emit_state else res[0]


ML_SW = 2 * ML_V_DIM
ML_DIRS = 2 * ML_HEADS


def _cummax_rows(a, reverse=False):
    row = lax.broadcasted_iota(jnp.int32, a.shape, 0)
    s = 1
    while s < CHUNK:
        if reverse:
            a = jnp.maximum(a, jnp.where(row < CHUNK - s, pltpu.roll(a, CHUNK - s, axis=0), -jnp.inf))
        else:
            a = jnp.maximum(a, jnp.where(row >= s, pltpu.roll(a, s, axis=0), -jnp.inf))
        s *= 2
    return a


def _mlstm_kernel(nc, has_state, has_buf, emit_state, *refs):
    qk_ref, v_ref, og_ref, gates_ref, gb_ref = refs[:5]
    refs = refs[5:]
    if has_state:
        s0_ref, m0_ref = refs[:2]
        refs = refs[2:]
    if has_buf:
        refs = refs[1:]
    y_ref, refs = refs[0], refs[1:]
    if emit_state:
        sfin_ref, mfin_ref = refs[:2]
        refs = refs[2:]
    s_store, m_store, s_carry, m_carry = refs

    phase = pl.program_id(1)
    c = pl.program_id(2)

    lane = lax.broadcasted_iota(jnp.int32, (CHUNK, LANES), 1)
    fwd_lane = lane < ML_HEADS
    fwd_row = lax.broadcasted_iota(jnp.int32, (1, LANES), 1) < ML_HEADS
    g = gates_ref[...] + gb_ref[...]
    li = g
    lf = pltpu.roll(jax.nn.log_sigmoid(g), LANES - ML_DIRS, axis=1)
    cum = jnp.where(fwd_lane, _cumsum_rows(lf), _cumsum_rows(lf, reverse=True))
    r = li - cum
    pm = jnp.where(fwd_lane, _cummax_rows(r), _cummax_rows(r, reverse=True))
    r_t = r.T

    qk = qk_ref[...]
    k_t = qk[:, ML_QK_WIDTH:].T
    ones = jnp.ones((CHUNK, ML_V_DIM), bf16)

    def state_update(m_row, big_m, d):
        edge = CHUNK - 1 if d == 0 else 0
        m_edge = big_m[edge:edge + 1, :]
        wc_row = jnp.exp(m_row - m_edge)
        wk_t = jnp.exp(r_t - big_m.T[:, edge:edge + 1])
        for h in range(ML_HEADS):
            cl = d * ML_HEADS + h
            kw = (k_t[h * ML_QK_DIM:(h + 1) * ML_QK_DIM, :] * wk_t[cl:cl + 1, :]).astype(bf16)
            rhs = jnp.concatenate([v_ref[:, h * ML_V_DIM:(h + 1) * ML_V_DIM], ones], axis=1)
            upd = jnp.dot(kw, rhs, preferred_element_type=f32)
            wc = jnp.broadcast_to(wc_row[:, cl:cl + 1], (ML_QK_DIM, ML_SW))
            s_carry[d, :, h * ML_SW:(h + 1) * ML_SW] = wc * s_carry[d, :, h * ML_SW:(h + 1) * ML_SW] + upd
        return cum[edge:edge + 1, :] + m_edge

    def init_state(d):
        if has_state:
            s_carry[d] = s0_ref[d]
        else:
            s_carry[d] = jnp.zeros((ML_QK_DIM, ML_HEADS * ML_SW), f32)

    @pl.when(jnp.logical_and(phase == 0, c == 0))
    def _():
        init_state(1)
        m_carry[...] = jnp.broadcast_to(m0_ref[...], m_carry.shape) if has_state else jnp.zeros(m_carry.shape, f32)

    @pl.when(phase == 0)
    def _backward_states():
        j = nc - 1 - c
        m_row = m_carry[0:1, :]
        s_store[j] = s_carry[1].astype(bf16)
        m_store[j] = m_carry[...]
        big_m = jnp.maximum(m_row, pm)
        m_new = state_update(m_row, big_m, 1)
        m_carry[...] = jnp.broadcast_to(jnp.where(fwd_row, m_row, m_new), m_carry.shape)

    @pl.when(phase == 1)
    def _forward_and_outputs():
        @pl.when(c == 0)
        def _():
            init_state(0)

        if emit_state:
            @pl.when(c == 0)
            def _():
                sfin_ref[1] = s_carry[1]

        m_both = m_carry[0:1, :]
        m_row = jnp.where(fwd_row, m_both, m_store[c][0:1, :])
        big_m = jnp.maximum(m_row, pm)
        w_inter = jnp.exp(m_row - big_m)
        floor = jnp.exp(-(cum + big_m))
        row = lax.broadcasted_iota(jnp.int32, (CHUNK, CHUNK), 0)
        col = lax.broadcasted_iota(jnp.int32, (CHUNK, CHUNK), 1)
        masks = (col <= row, col >= row)
        q = qk[:, :ML_QK_WIDTH] * (ML_QK_DIM ** -0.5)
        k_bf = qk[:, ML_QK_WIDTH:].astype(bf16)
        for h in range(ML_HEADS):
            p = h // 2
            qp = q[:, p * LANES:(p + 1) * LANES]
            keep = (lane < ML_QK_DIM) if h % 2 == 0 else (lane >= ML_QK_DIM)
            qm = jnp.where(keep, qp, 0.0).astype(bf16)
            s_raw = _dot_nt(qm, k_bf[:, p * LANES:(p + 1) * LANES])
            v_h = v_ref[:, h * ML_V_DIM:(h + 1) * ML_V_DIM]
            out = None
            for d in range(2):
                cl = d * ML_HEADS + h
                st = s_carry[0, :, h * ML_SW:(h + 1) * ML_SW].astype(bf16) if d == 0 \
                    else s_store[c, :, h * ML_SW:(h + 1) * ML_SW]
                inter = jnp.dot(qm, jnp.concatenate([st, st], axis=0), preferred_element_type=f32)
                wmat = jnp.exp(jnp.where(masks[d], r_t[cl:cl + 1, :] - big_m[:, cl:cl + 1], -jnp.inf))
                sw = s_raw * wmat
                wi = w_inter[:, cl:cl + 1]
                num = jnp.dot(sw.astype(bf16), v_h, preferred_element_type=f32) + wi * inter[:, :ML_V_DIM]
                den = jnp.sum(sw, axis=1, keepdims=True) + wi * inter[:, ML_V_DIM:]
                hd = num / jnp.maximum(jnp.abs(den), floor[:, cl:cl + 1])
                out = hd if out is None else out + hd
            og = og_ref[:, h * ML_V_DIM:(h + 1) * ML_V_DIM]
            y_ref[:, h * ML_V_DIM:(h + 1) * ML_V_DIM] = (out * jax.nn.sigmoid(og)).astype(y_ref.dtype)

        m_fin = jnp.where(fwd_row, state_update(m_row, big_m, 0), m_both)
        m_carry[...] = jnp.broadcast_to(m_fin, m_carry.shape)

        if emit_state:
            @pl.when(c == nc - 1)
            def _():
                sfin_ref[0] = s_carry[0]
                mfin_ref[...] = m_fin


def mlstm_mix(qk, v, og, gates, gate_b, state, row0, n_seq, seq_len, emit_state, out_buf=None):
    nc = seq_len // CHUNK
    blk0 = row0 // CHUNK
    has_state = state is not None
    sw_all = ML_HEADS * ML_SW

    def chunk_map(s, p, c):
        return (blk0 + s * nc + jnp.where(p == 0, nc - 1 - c, c), 0)

    s_spec = pl.BlockSpec((None, 2, ML_QK_DIM, sw_all), lambda s, p, c: (s, 0, 0, 0))
    m_spec = pl.BlockSpec((None, 1, LANES), lambda s, p, c: (s, 0, 0))
    in_specs = [
        pl.BlockSpec((CHUNK, 2 * ML_QK_WIDTH), chunk_map),
        pl.BlockSpec((CHUNK, ML_V_WIDTH), chunk_map),
        pl.BlockSpec((CHUNK, ML_V_WIDTH), lambda s, p, c: (blk0 + s * nc + p * c, 0)),
        pl.BlockSpec((CHUNK, LANES), chunk_map),
        pl.BlockSpec((1, LANES), lambda s, p, c: (0, 0)),
    ]
    gb =jnp.concatenate([gate_b.reshape(1, 2 * ML_DIRS), jnp.zeros((1, LANES - 2 * ML_DIRS), f32)], axis=1)
    args = [qk, v, og, gates, gb]
    if has_state:
        c0, n0, m0 = state
        tile = jnp.concatenate([jnp.swapaxes(c0, -1, -2),
                                jnp.broadcast_to(n0[..., None], n0.shape + (ML_V_DIM,))], axis=-1)
        s0 = jnp.transpose(tile, (0, 1, 3, 2, 4)).reshape(n_seq, 2, ML_QK_DIM, sw_all)
        m0 = jnp.concatenate([m0.reshape(n_seq, 1, ML_DIRS), jnp.zeros((n_seq, 1, LANES - ML_DIRS), f32)], axis=-1)
        in_specs += [s_spec, m_spec]
        args += [s0, m0]
    aliases = {}
    if out_buf is not None:
        aliases = {len(args): 0}
        in_specs.append(pl.BlockSpec(memory_space=pl.ANY))
        args.append(out_buf)
    out_shape = [jax.ShapeDtypeStruct((N_TOK, ML_V_WIDTH), bf16)]
    out_specs = [pl.BlockSpec((CHUNK, ML_V_WIDTH), lambda s, p, c: (blk0 + s * nc + p * c, 0))]
    if emit_state:
        out_shape += [jax.ShapeDtypeStruct((n_seq, 2, ML_QK_DIM, sw_all), f32),
                      jax.ShapeDtypeStruct((n_seq, 1, LANES), f32)]
        out_specs += [s_spec, m_spec]
    res = pl.pallas_call(
        functools.partial(_mlstm_kernel, nc, has_state, out_buf is not None, emit_state),
        input_output_aliases=aliases,
        grid=(n_seq, 2, nc),
        in_specs=in_specs,
        out_specs=out_specs,
        out_shape=out_shape,
        scratch_shapes=[pltpu.VMEM((nc, ML_QK_DIM, sw_all), bf16), pltpu.VMEM((nc, SUBLANES, LANES), f32),
                        pltpu.VMEM((2, ML_QK_DIM, sw_all), f32), pltpu.VMEM((SUBLANES, LANES), f32)],
        compiler_params=pltpu.CompilerParams(
            dimension_semantics=("arbitrary", "arbitrary", "arbitrary"), vmem_limit_bytes=VMEM_LIMIT),
        name="mlstm_scan",
    )(*args)
    if not emit_state:
        return res[0]
    y, sfin, mfin = res
    tiles = sfin.reshape(n_seq, 2, ML_QK_DIM, ML_HEADS, ML_SW)
    c_fin = jnp.transpose(tiles[..., :ML_V_DIM], (0, 1, 3, 4, 2))
    n_fin = jnp.transpose(tiles[..., ML_V_DIM], (0, 1, 3, 2))
    m_fin = mfin[:, 0, :ML_DIRS].reshape(n_seq, 2, ML_HEADS)
    return y, c_fin, n_fin, m_fin


def kernel(x_prompt, x_sample, c, cache_na_k, cache_na_v, state_ssd, state_mlstm_c, state_mlstm_n, state_mlstm_m,
           c_ctx, w_mod, b_mod, norm_mix, norm_ffn, w_in_even, w_out_even, na_rpb, ssd_conv_w, ssd_conv_b,
           ssd_dt_bias, ssd_a_log, ssd_d, ssd_norm, w_in_odd, w_out_odd, ml_conv_w, ml_conv_b, ml_gate_b,
           w_ff1, w_ff2, norm_f):
    xs = [x_prompt.reshape(N_PROMPT, D_MODEL), x_sample.reshape(DEC_BATCH * DEC_SEQ, D_MODEL)]
    cond = jnp.concatenate([c_ctx[None, :], c, jnp.zeros((SUBLANES - N_COND, D_MODEL), f32)], axis=0)
    mod = adaln_all(cond, w_mod, b_mod)[:, :N_COND].reshape(DEPTH, N_COND, 1, N_MOD * D_MODEL)

    even_main = 3 * NA_WIDTH + SSD_INNER + SSD_CONV_DIM
    odd_main = 2 * ML_QK_WIDTH + 2 * ML_V_WIDTH

    def tail_bf16(w, main):
        t = w[:, :, main:]
        return jnp.concatenate([t, jnp.zeros(t.shape[:2] + (LANES - t.shape[2],), f32)], axis=2).astype(bf16)

    wi_even, wt_even = cast_bf16(w_in_even, even_main), tail_bf16(w_in_even, even_main)
    wi_odd, wt_odd = cast_bf16(w_in_odd, odd_main), tail_bf16(w_in_odd, odd_main)
    wo_even, wo_odd = cast_bf16(w_out_even), cast_bf16(w_out_odd)
    w1_all, w2_all = cast_bf16(w_ff1), cast_bf16(w_ff2)

    out_k, out_v, out_c, out_n, out_m = [], [], [], [], []
    new_ssd = None
    for l in range(DEPTH):
        norm_last = norm_f if l == DEPTH - 1 else None
        if l % 2 == 0:
            e = l // 2
            o0 = 3 * NA_WIDTH
            segs = ((0, NA_WIDTH, NA_HEAD_DIM ** -0.5, False),
                    (NA_WIDTH, 2 * NA_WIDTH, 1.0, False),
                    (2 * NA_WIDTH, 3 * NA_WIDTH, 1.0, False),
                    (o0, o0 + SSD_INNER, 1.0, False),
                    (o0 + SSD_INNER, o0 + SSD_INNER + SSD_CONV_DIM, 1.0, True),
                    (None, None, 1.0, False))
            outs = ((0, bf16, False), (1, bf16, False), (2, bf16, False), (1, f32, True), (2, f32, True),
                    (3, f32, False), (4, f32, False), (5, f32, False))
            q, k, v, kp, vp, z, xbc, dt = in_proj(xs, mod[l], norm_mix[l], wi_even, wt_even, e, segs, outs,
                                                  ssd_conv_w[e], ssd_conv_b[e])
            heads = lambda t: t.reshape(BATCH, SEQ, NA_HEADS, NA_HEAD_DIM)
            out_k.append(jnp.swapaxes(heads(kp), 1, 2))
            out_v.append(jnp.swapaxes(heads(vp), 1, 2))
            ssd_w = (ssd_dt_bias[e], ssd_a_log[e], ssd_d[e], ssd_norm[e])
            y_ssd, new_ssd = ssd_mix(xbc, dt, z, None, 0, BATCH, SEQ, *ssd_w, True, layer=e, state_buf=new_ssd)
            y_ssd = ssd_mix(xbc, dt, z, state_ssd[:, e], N_PROMPT, DEC_BATCH, DEC_SEQ, *ssd_w, False, out_buf=y_ssd)
            tokens = lambda t: jnp.swapaxes(t, 1, 2).reshape(DEC_BATCH, PAST_LEN, NA_WIDTH).astype(bf16)
            y_na = ctx_attention(q, k, v)
            y_na = na_latent(q, k, v, tokens(cache_na_k[:, e]), tokens(cache_na_v[:, e]),
                             na_bias_table(na_rpb[e]), y_na)
            res = out_mlp(xs, mod[l], norm_ffn[l], [y_na, y_ssd], wo_even, e, w1_all, w2_all, l, norm_last)
        else:
            o = l // 2
            a0 = 2 * ML_QK_WIDTH
            segs = ((0, a0, 1.0, True),
                    (a0, a0 + ML_V_WIDTH, 1.0, False),
                    (a0 + ML_V_WIDTH, a0 + 2 * ML_V_WIDTH, 1.0, False),
                    (None, None, 1.0, False))
            outs = ((0, f32, False), (1, bf16, False), (2, f32, False), (3, f32, False))
            qk, v, og, gates = in_proj(xs, mod[l], norm_mix[l], wi_odd, wt_odd, o, segs, outs,
                                       ml_conv_w[o], ml_conv_b[o])
            y_ml, c_fin, n_fin, m_fin = mlstm_mix(qk, v, og, gates, ml_gate_b[o], None, 0, BATCH, SEQ, True)
            out_c.append(c_fin)
            out_n.append(n_fin)
            out_m.append(m_fin)
            state = (state_mlstm_c[:, o], state_mlstm_n[:, o], state_mlstm_m[:, o])
            y_ml = mlstm_mix(qk, v, og, gates, ml_gate_b[o], state, N_PROMPT, DEC_BATCH, DEC_SEQ, False, out_buf=y_ml)
            res = out_mlp(xs, mod[l], norm_ffn[l], [y_ml], wo_odd, o, w1_all, w2_all, l, norm_last)
        xs = list(res) if norm_last is not None else [res]

    y_prompt = xs[0].reshape(BATCH, SEQ, D_MODEL)
    y_sample = xs[1].reshape(DEC_BATCH, DEC_SEQ, D_MODEL)
    return (y_prompt, y_sample, jnp.stack(out_k, axis=1), jnp.stack(out_v, axis=1), new_ssd,
            jnp.stack(out_c, axis=1), jnp.stack(out_n, axis=1), jnp.stack(out_m, axis=1))
```

```python
import functools
import math

import jax
import jax.numpy as jnp
from jax import lax
from jax.experimental import pallas as pl
from jax.experimental.pallas import tpu as pltpu

D_MODEL = 1024
BATCH = 32
SEQ = 256
DEPTH = 4
DEC_BATCH = 2
DEC_SEQ = 4096
PAST_LEN = 256
GRID_W = 64
N_EVEN = (DEPTH + 1) // 2
N_ODD = DEPTH // 2
RMS_EPS = 1e-6
N_MOD = 6
D_FF = 4 * D_MODEL
CONV_K = 3
Q_BLOCK = 128
NA_HEADS = 8
NA_HEAD_DIM = 64
NA_WIDTH = NA_HEADS * NA_HEAD_DIM
NA_WIN_ROWS = 8
NA_WIN_COLS = 16
NA_RPB_ROWS = 2 * NA_WIN_ROWS - 1
NA_RPB_COLS = 2 * NA_WIN_COLS - 1
SSD_INNER = D_MODEL
SSD_HEAD_DIM = 64
SSD_HEADS = SSD_INNER // SSD_HEAD_DIM
SSD_GROUPS = 2
SSD_RPG = SSD_HEADS // SSD_GROUPS
SSD_STATE = 128
SSD_GN = SSD_GROUPS * SSD_STATE
SSD_CONV_DIM = SSD_INNER + 2 * SSD_GN
SSD_CHUNK = 128
ML_HEADS = 8
ML_QK_DIM = D_MODEL // 16
ML_V_DIM = D_MODEL // 8
ML_QK_WIDTH = ML_HEADS * ML_QK_DIM
ML_V_WIDTH = ML_HEADS * ML_V_DIM
ML_CHUNK = 64
EVEN_MIX = NA_WIDTH + SSD_INNER

N_PROMPT = BATCH * SEQ
N_TOK = N_PROMPT + DEC_BATCH * DEC_SEQ
N_COND = 1 + DEC_BATCH
LANES = 128
SUBLANES = 8
VMEM_LIMIT = 56 * 1024 * 1024
TM = 512

f32 = jnp.float32
bf16 = jnp.bfloat16


def _cond_row(i, tm):
    start = i * tm
    return jnp.where(start < N_PROMPT, 0, (start - N_PROMPT) // DEC_SEQ + 1)


def _const_spec(shape):
    nd = len(shape)
    return pl.BlockSpec(shape, lambda i: (0,) * nd, pipeline_mode=pl.Buffered(1))


def _rms(x):
    return x * lax.rsqrt(jnp.mean(x * x, axis=-1, keepdims=True) + RMS_EPS)


def _modulated(x, g, mod, k):
    shift = mod[:, k * D_MODEL:(k + 1) * D_MODEL]
    scale = mod[:, (k + 1) * D_MODEL:(k + 2) * D_MODEL]
    return (_rms(x) * g) * (1.0 + scale) + shift


def _mod_kernel(c_ref, w_ref, b_ref, o_ref):
    c = c_ref[...]
    a = (c * jax.nn.sigmoid(c)).astype(bf16)
    o_ref[...] = jnp.dot(a, w_ref[...].astype(bf16), preferred_element_type=f32) + b_ref[...]


def adaln_all(cond, w_mod, b_mod):
    tn = 1536
    nj = N_MOD * D_MODEL // tn
    return pl.pallas_call(
        _mod_kernel,
        grid=(DEPTH, nj),
        in_specs=[
            pl.BlockSpec((SUBLANES, D_MODEL), lambda l, j: (0, 0)),
            pl.BlockSpec((None, D_MODEL, tn), lambda l, j: (l, 0, j)),
            pl.BlockSpec((None, 1, tn), lambda l, j: (l, 0, j)),
        ],
        out_specs=pl.BlockSpec((None, SUBLANES, tn), lambda l, j: (l, 0, j)),
        out_shape=jax.ShapeDtypeStruct((DEPTH, SUBLANES, N_MOD * D_MODEL), f32),
        compiler_params=pltpu.CompilerParams(
            dimension_semantics=("arbitrary", "arbitrary"), vmem_limit_bytes=VMEM_LIMIT),
        name="adaln",
    )(cond, w_mod, b_mod.reshape(DEPTH, 1, N_MOD * D_MODEL))


def _cast_kernel(x_ref, o_ref):
    o_ref[...] = x_ref[...].astype(o_ref.dtype)


def cast_bf16(w, cols=None):
    n_l, k, n = w.shape
    cols = n if cols is None else cols
    bk = 512
    spec = pl.BlockSpec((None, bk, cols), lambda l, i: (l, i, 0))
    return pl.pallas_call(
        _cast_kernel,
        grid=(n_l, k // bk),
        in_specs=[spec],
        out_specs=spec,
        out_shape=jax.ShapeDtypeStruct((n_l, k, cols), bf16),
        compiler_params=pltpu.CompilerParams(
            dimension_semantics=("arbitrary", "arbitrary"), vmem_limit_bytes=VMEM_LIMIT),
        name="cast_bf16",
    )(w)


def _layer_spec(w, l):
    nd = w.ndim - 1
    return pl.BlockSpec((None,) + w.shape[1:], lambda i: (l,) + (0,) * nd, pipeline_mode=pl.Buffered(1))


def _tile_x(x_refs, tm):
    if len(x_refs) == 1:
        return x_refs[0][...]
    return jnp.where(pl.program_id(0) < N_PROMPT // tm, x_refs[0][...], x_refs[1][...])


def _conv_silu_tile(y, prev_row, next_row, w, b):
    i = pl.program_id(0)
    rows = y.shape[0]
    g = i * rows + lax.broadcasted_iota(jnp.int32, (rows, 1), 0)
    is_prompt = i < N_PROMPT // rows
    pos = jnp.where(is_prompt, g & (SEQ - 1), (g - N_PROMPT) & (DEC_SEQ - 1))
    seq_last = jnp.where(is_prompt, SEQ - 1, DEC_SEQ - 1)
    row = lax.broadcasted_iota(jnp.int32, y.shape, 0)
    up = jnp.where(row == 0, prev_row, pltpu.roll(y, 1, axis=0))
    dn = jnp.where(row == rows - 1, next_row, pltpu.roll(y, rows - 1, axis=0))
    up = jnp.where(pos == 0, 0.0, up)
    dn = jnp.where(pos == seq_last, 0.0, dn)
    c = up * w[0:1, :] + y * w[1:2, :] + dn * w[2:3, :] + b
    return c * jax.nn.sigmoid(c)


def _in_proj_kernel(n_x, segs, outs, *refs):
    x_refs = refs[:n_x]
    prev_ref, next_ref, mod_ref, g_ref, w_ref, wt_ref, cw_ref, cb_ref = refs[n_x:n_x + 8]
    o_refs = refs[n_x + 8:]
    mod, g = mod_ref[...], g_ref[...]
    h = _modulated(_tile_x(x_refs, TM), g, mod, 0).astype(bf16)
    ys = []
    for a, b, scale, conv in segs:
        w = wt_ref[...] if a is None else w_ref[:, a:b]
        if conv:
            halo = jnp.concatenate([prev_ref[...], next_ref[...]], axis=0)
            hh = _modulated(halo, g, mod, 0).astype(bf16)
            y = jnp.dot(jnp.concatenate([h, hh], axis=0), w, preferred_element_type=f32)
            y = _conv_silu_tile(y[:TM], y[TM + SUBLANES - 1:TM + SUBLANES], y[TM + SUBLANES:TM + SUBLANES + 1],
                                cw_ref[...], cb_ref[...])
        else:
            y = jnp.dot(h, w, preferred_element_type=f32)
        ys.append(y if scale == 1.0 else y * scale)
    for (si, _, prompt_only), o_ref in zip(outs, o_refs):
        if prompt_only:
            @pl.when(pl.program_id(0) < N_PROMPT // TM)
            def _(o_ref=o_ref, si=si):
                o_ref[...] = ys[si].astype(o_ref.dtype)
        else:
            o_ref[...] = ys[si].astype(o_ref.dtype)


def _x_specs(xs, tm):
    if len(xs) == 1:
        return [pl.BlockSpec((tm, D_MODEL), lambda i: (i, 0))]
    n_p = N_PROMPT // tm
    return [pl.BlockSpec((tm, D_MODEL), lambda i: (jnp.minimum(i, n_p - 1), 0)),
            pl.BlockSpec((tm, D_MODEL), lambda i: (jnp.maximum(i - n_p, 0), 0))]


def in_proj(xs, mod_l, g, w, w_tail, l, segs, outs, conv_w, conv_b):
    grid = (N_TOK // TM,)
    last_prompt = N_PROMPT // TM - 1
    halo_src = xs[-1]
    per = TM // SUBLANES
    blk0 = (halo_src.shape[0] - DEC_BATCH * DEC_SEQ) // SUBLANES
    n_blk = halo_src.shape[0] // SUBLANES
    tile0 = N_PROMPT // TM

    def prev_map(i):
        return (jnp.clip(blk0 + (i - tile0) * per - 1, 0, n_blk - 1), 0)

    def next_map(i):
        return (jnp.clip(blk0 + (i - tile0 + 1) * per, 0, n_blk - 1), 0)

    ch = conv_w.shape[1]
    out_shape, out_specs = [], []
    for si, dt, prompt_only in outs:
        width = LANES if segs[si][0] is None else segs[si][1] - segs[si][0]
        if prompt_only:
            out_shape.append(jax.ShapeDtypeStruct((N_PROMPT, width), dt))
            out_specs.append(pl.BlockSpec((TM, width), lambda i: (jnp.minimum(i, last_prompt), 0)))
        else:
            out_shape.append(jax.ShapeDtypeStruct((N_TOK, width), dt))
            out_specs.append(pl.BlockSpec((TM, width), lambda i: (i, 0)))
    return pl.pallas_call(
        functools.partial(_in_proj_kernel, len(xs), segs, outs),
        grid=grid,
        in_specs=_x_specs(xs, TM) + [
            pl.BlockSpec((SUBLANES, D_MODEL), prev_map),
            pl.BlockSpec((SUBLANES, D_MODEL), next_map),
            pl.BlockSpec((None, 1, N_MOD * D_MODEL), lambda i: (_cond_row(i, TM), 0, 0)),
            _const_spec((1, D_MODEL)),
            _layer_spec(w, l),
            _layer_spec(w_tail, l),
            _const_spec((CONV_K, ch)),
            _const_spec((1, ch)),
        ],
        out_specs=out_specs,
        out_shape=out_shape,
        compiler_params=pltpu.CompilerParams(
            dimension_semantics=("arbitrary",), vmem_limit_bytes=VMEM_LIMIT),
        name="in_proj",
    )(*xs, halo_src, halo_src, mod_l, g.reshape(1, D_MODEL), w, w_tail, conv_w, conv_b.reshape(1, ch))


def _out_mlp_kernel(n_x, n_mix, final, *refs):
    x_refs = refs[:n_x]
    mod_ref, g_ref = refs[n_x:n_x + 2]
    refs = refs[n_x + 2:]
    mix_refs = refs[:n_mix]
    wo_ref, w1_ref, w2_ref = refs[n_mix:n_mix + 3]
    rest = refs[n_mix + 3:]
    if final:
        gf_ref, op_ref, os_ref = rest
    else:
        (o_ref,) = rest
    mod = mod_ref[...]
    m = None
    k0 = 0
    for r in mix_refs:
        kw = r.shape[-1]
        part = jnp.dot(r[...].astype(bf16), wo_ref[k0:k0 + kw, :], preferred_element_type=f32)
        m = part if m is None else m + part
        k0 += kw
    x1 = _tile_x(x_refs, OUT_TM) + mod[:, 2 * D_MODEL:3 * D_MODEL] * m
    h2 = _modulated(x1, g_ref[...], mod, 3).astype(bf16)
    u = jnp.dot(h2, w1_ref[...], preferred_element_type=f32)
    a = jnp.square(jnp.maximum(u, 0.0)).astype(bf16)
    x2 = x1 + mod[:, 5 * D_MODEL:6 * D_MODEL] * jnp.dot(a, w2_ref[...], preferred_element_type=f32)
    if not final:
        o_ref[...] = x2
        return
    y = _rms(x2) * gf_ref[...]
    is_prompt = pl.program_id(0) < N_PROMPT // OUT_TM

    @pl.when(is_prompt)
    def _():
        op_ref[...] = y

    @pl.when(jnp.logical_not(is_prompt))
    def _():
        os_ref[...] = y


OUT_TM = 512


def out_mlp(xs, mod_l, g_ffn, mixes, w_out, l_out, w1, w2, l, norm_f=None):
    final = norm_f is not None
    tm = OUT_TM
    grid = (N_TOK // tm,)
    in_specs = _x_specs(xs, tm) + [
        pl.BlockSpec((None, 1, N_MOD * D_MODEL), lambda i: (_cond_row(i, tm), 0, 0)),
        _const_spec((1, D_MODEL)),
    ]
    in_specs += [pl.BlockSpec((tm, m.shape[-1]), lambda i: (i, 0)) for m in mixes]
    in_specs += [_layer_spec(w_out, l_out), _layer_spec(w1, l), _layer_spec(w2, l)]
    args = [*xs, mod_l, g_ffn.reshape(1, D_MODEL), *mixes, w_out, w1, w2]
    out_specs = pl.BlockSpec((tm, D_MODEL), lambda i: (i, 0))
    out_shape = jax.ShapeDtypeStruct((N_TOK, D_MODEL), f32)
    if final:
        in_specs.append(_const_spec((1, D_MODEL)))
        args.append(norm_f.reshape(1, D_MODEL))
        n_p = N_PROMPT // tm
        out_specs = [pl.BlockSpec((tm, D_MODEL), lambda i: (jnp.minimum(i, n_p - 1), 0)),
                     pl.BlockSpec((tm, D_MODEL), lambda i: (jnp.maximum(i - n_p, 0), 0))]
        out_shape = [jax.ShapeDtypeStruct((N_PROMPT, D_MODEL), f32),
                     jax.ShapeDtypeStruct((N_TOK - N_PROMPT, D_MODEL), f32)]
    return pl.pallas_call(
        functools.partial(_out_mlp_kernel, len(xs), len(mixes), final),
        grid=grid,
        in_specs=in_specs,
        out_specs=out_specs,
        out_shape=out_shape,
        compiler_params=pltpu.CompilerParams(
            dimension_semantics=("arbitrary",), vmem_limit_bytes=VMEM_LIMIT),
        name="out_mlp",
    )(*args)


NA_PAIRS = NA_HEADS // 2
NA_ROWS = DEC_SEQ // GRID_W
NA_WIN = NA_WIN_ROWS * GRID_W


def _dot_nt(a, b):
    return lax.dot_general(a, b, (((1,), (1,)), ((), ())), preferred_element_type=f32)


def _pair_stack(x):
    lane = lax.broadcasted_iota(jnp.int32, x.shape, 1)
    zero = jnp.zeros_like(x)
    return jnp.concatenate([jnp.where(lane < NA_HEAD_DIM, x, zero), jnp.where(lane >= NA_HEAD_DIM, x, zero)], axis=0)


def _pair_unstack(o):
    n = o.shape[0] // 2
    lane = lax.broadcasted_iota(jnp.int32, (n, LANES), 1)
    return jnp.where(lane < NA_HEAD_DIM, o[:n], o[n:])


def _softmax_pv(scores, values):
    def lane_tiles(blocks):
        return [b[:, k * LANES:(k + 1) * LANES] for b in blocks for k in range(b.shape[1] // LANES)]

    ms = [jnp.max(functools.reduce(jnp.maximum, lane_tiles(s)), axis=1, keepdims=True) for s in scores]
    ps = [[jnp.exp(b - m) for b in s] for s, m in zip(scores, ms)]
    invs = [1.0 / jnp.sum(functools.reduce(jnp.add, lane_tiles(p)), axis=1, keepdims=True) for p in ps]
    pn = [[(b * inv).astype(bf16) for b in p] for p, inv in zip(ps, invs)]
    return [functools.reduce(jnp.add, [jnp.dot(b, v, preferred_element_type=f32) for b, v in zip(p, vs)])
            for p, vs in zip(pn, values)]


def _ctx_attn_kernel(q_ref, k_ref, v_ref, o_ref):
    pairs = [slice(p * LANES, (p + 1) * LANES) for p in range(NA_PAIRS)]
    scores = [[_dot_nt(_pair_stack(q_ref[:, lanes]), k_ref[:, lanes])] for lanes in pairs]
    outs = _softmax_pv(scores, [[v_ref[:, lanes]] for lanes in pairs])
    for lanes, o in zip(pairs, outs):
        o_ref[:, lanes] = _pair_unstack(o).astype(o_ref.dtype)


def ctx_attention(q, k, v):
    spec = pl.BlockSpec((SEQ, NA_WIDTH), lambda b: (b, 0))
    return pl.pallas_call(
        _ctx_attn_kernel,
        grid=(BATCH,),
        in_specs=[spec, spec, spec],
        out_specs=spec,
        out_shape=jax.ShapeDtypeStruct((N_TOK, NA_WIDTH), bf16),
        compiler_params=pltpu.CompilerParams(dimension_semantics=("arbitrary",), vmem_limit_bytes=VMEM_LIMIT),
        name="ctx_attn",
    )(q, k, v)


def _na_bias_kernel(rpb_ref, o_ref):
    pair = pl.program_id(0)
    shape = (GRID_W, LANES)
    qc = lax.broadcasted_iota(jnp.int32, shape, 0)
    lane = lax.broadcasted_iota(jnp.int32, shape, 1)
    kc = lane & (GRID_W - 1)
    low = lane < GRID_W
    col_start = jnp.clip(qc - NA_WIN_COLS // 2, 0, GRID_W - NA_WIN_COLS)
    valid = (kc >= col_start) & (kc < col_start + NA_WIN_COLS)
    rel_c = jnp.clip(kc - qc + NA_WIN_COLS - 1, 0, NA_RPB_COLS - 1)
    for e in range(2):
        base = (2 * pair + e) * NA_RPB_ROWS
        pieces = []
        for rr in range(NA_RPB_ROWS - 1):
            val = jnp.zeros(shape, f32)
            for t in range(NA_RPB_COLS):
                s_lo = rpb_ref[(base + rr) * NA_RPB_COLS + t]
                s_hi = rpb_ref[(base + rr + 1) * NA_RPB_COLS + t]
                val = jnp.where(rel_c == t, jnp.where(low, s_lo, s_hi), val)
            pieces.append(jnp.where(valid, val, -jnp.inf))
        for d in range(NA_WIN_ROWS):
            for i in range(0, NA_WIN_ROWS, 2):
                o_ref[d, e * GRID_W:(e + 1) * GRID_W, i * GRID_W:(i + 2) * GRID_W] = pieces[d + i]


def na_bias_table(rpb):
    return pl.pallas_call(
        _na_bias_kernel,
        grid=(NA_PAIRS,),
        in_specs=[pl.BlockSpec(memory_space=pltpu.SMEM)],
        out_specs=pl.BlockSpec((None, NA_WIN_ROWS, 2 * GRID_W, NA_WIN), lambda p: (p, 0, 0, 0)),
        out_shape=jax.ShapeDtypeStruct((NA_PAIRS, NA_WIN_ROWS, 2 * GRID_W, NA_WIN), f32),
        compiler_params=pltpu.CompilerParams(dimension_semantics=("arbitrary",), vmem_limit_bytes=VMEM_LIMIT),
        name="na_bias",
    )(rpb.reshape(-1))


NA_STEP_ROWS = 4


def _na_first_key_row(r):
    return jnp.clip(r - NA_WIN_ROWS // 2, 0, NA_ROWS - NA_WIN_ROWS)


def _na_kernel(q_ref, k_ref, v_ref, kc_ref, vc_ref, bias_ref, buf_ref, o_ref):
    del buf_ref
    where, scores, values = [], [], []
    for j in range(NA_STEP_ROWS):
        r = pl.program_id(1) * NA_STEP_ROWS + j
        first = _na_first_key_row(r)
        start = pl.multiple_of(first * GRID_W, GRID_W)
        shift = first - r + NA_WIN_ROWS - 1
        rows = slice(j * GRID_W, (j + 1) * GRID_W)
        for p in range(NA_PAIRS):
            lanes = slice(p * LANES, (p + 1) * LANES)
            qq = _pair_stack(q_ref[rows, lanes])
            where.append((rows, lanes))
            scores.append([_dot_nt(qq, k_ref[pl.ds(start, NA_WIN), lanes]) + bias_ref[p, shift],
                           _dot_nt(qq, kc_ref[:, lanes])])
            values.append([v_ref[pl.ds(start, NA_WIN), lanes], vc_ref[:, lanes]])
    for (rows, lanes), o in zip(where, _softmax_pv(scores, values)):
        o_ref[rows, lanes] = _pair_unstack(o).astype(o_ref.dtype)


def na_latent(q, k, v, k_ctx, v_ctx, bias, buf):
    rows = NA_STEP_ROWS * GRID_W
    steps = NA_ROWS // NA_STEP_ROWS
    row0 = N_PROMPT // rows
    seq0 = N_PROMPT // DEC_SEQ
    kv_spec = pl.BlockSpec((DEC_SEQ, NA_WIDTH), lambda b, r: (seq0 + b, 0))
    ctx_spec = pl.BlockSpec((None, PAST_LEN, NA_WIDTH), lambda b, r: (b, 0, 0))
    return pl.pallas_call(
        _na_kernel,
        grid=(DEC_BATCH, steps),
        in_specs=[
            pl.BlockSpec((rows, NA_WIDTH), lambda b, r: (row0 + b * steps + r, 0)),
            kv_spec, kv_spec, ctx_spec, ctx_spec,
            pl.BlockSpec(bias.shape, lambda b, r: (0, 0, 0, 0), pipeline_mode=pl.Buffered(1)),
            pl.BlockSpec(memory_space=pl.ANY),
        ],
        out_specs=pl.BlockSpec((rows, NA_WIDTH), lambda b, r: (row0 + b * steps + r, 0)),
        out_shape=jax.ShapeDtypeStruct((N_TOK, NA_WIDTH), bf16),
        input_output_aliases={6: 0},
        compiler_params=pltpu.CompilerParams(
            dimension_semantics=("arbitrary", "arbitrary"), vmem_limit_bytes=VMEM_LIMIT),
        name="na_latent",
    )(q, k, v, k_ctx, v_ctx, bias, buf)


CHUNK = 128


def _cumsum_rows(a, reverse=False):
    row = lax.broadcasted_iota(jnp.int32, a.shape, 0)
    s = 1
    while s < CHUNK:
        if reverse:
            a = a + jnp.where(row < CHUNK - s, pltpu.roll(a, CHUNK - s, axis=0), 0.0)
        else:
            a = a + jnp.where(row >= s, pltpu.roll(a, s, axis=0), 0.0)
        s *= 2
    return a


def _ssd_kernel(nc, has_h0, n_buf, emit_state, *refs):
    xbc_ref, dt_ref, z_ref = refs[:3]
    refs = refs[3:]
    if has_h0:
        h0_ref, refs = refs[0], refs[1:]
    dtb_ref, alog_ref, dskip_ref, g_ref = refs[:4]
    refs = refs[4 + n_buf:]
    y_ref, refs = refs[0], refs[1:]
    if emit_state:
        hfin_ref, refs = refs[0], refs[1:]
    hb_store, carry = refs

    phase = pl.program_id(1)
    c = pl.program_id(2)
    gw = SSD_RPG * SSD_HEAD_DIM

    xbc = xbc_ref[...]
    x = xbc[:, :SSD_INNER]
    x_bf = x.astype(bf16)
    b_mat = xbc[:, SSD_INNER:SSD_INNER + SSD_GN]
    dt = jax.nn.softplus(dt_ref[...] + dtb_ref[...])
    a = dt * (-jnp.exp(alog_ref[...]))

    x_t = jnp.concatenate([x[:, k * LANES:(k + 1) * LANES].T for k in range(SSD_INNER // LANES)], axis=0)
    b_bf = b_mat.astype(bf16)

    def load_h0(d):
        if has_h0:
            return h0_ref[d].reshape(SSD_INNER, SSD_STATE)
        return jnp.zeros((SSD_INNER, SSD_STATE), f32)

    def head_rows(v):
        return jnp.concatenate(
            [jnp.broadcast_to(v[h:h + 1, :], (SSD_HEAD_DIM, v.shape[1])) for h in range(SSD_HEADS)], axis=0)

    def state_update(cum_t, dt_t, edge):
        at_edge = jnp.broadcast_to(cum_t[:, edge:edge + 1], cum_t.shape)
        w_end = jnp.exp(at_edge - cum_t) * dt_t
        xw = (x_t * head_rows(w_end)).astype(bf16)
        upd = jnp.concatenate(
            [jnp.dot(xw[g * gw:(g + 1) * gw], b_bf[:, g * SSD_STATE:(g + 1) * SSD_STATE], preferred_element_type=f32)
             for g in range(SSD_GROUPS)], axis=0)
        carry[...] = carry[...] * head_rows(jnp.exp(at_edge)) + upd

    @pl.when(phase == 0)
    def _backward_states():
        @pl.when(c == 0)
        def _():
            carry[...] = load_h0(1)

        hb_store[nc - 1 - c] = carry[...].astype(bf16)
        rcum_t = _cumsum_rows(a, reverse=True).T
        state_update(rcum_t[SSD_HEADS:2 * SSD_HEADS], dt.T[SSD_HEADS:2 * SSD_HEADS], 0)

        if emit_state:
            @pl.when(c == nc - 1)
            def _():
                hfin_ref[1] = carry[...].reshape(SSD_HEADS, SSD_HEAD_DIM, SSD_STATE)

    @pl.when(phase == 1)
    def _forward_and_outputs():
        @pl.when(c == 0)
        def _():
            carry[...] = load_h0(0)

        c_mat = xbc[:, SSD_INNER + SSD_GN:].astype(bf16)
        cum = _cumsum_rows(a)
        rcum = _cumsum_rows(a, reverse=True)
        cum_t, rcum_t, dt_t = cum.T, rcum.T, dt.T
        row = lax.broadcasted_iota(jnp.int32, (CHUNK, CHUNK), 0)
        col = lax.broadcasted_iota(jnp.int32, (CHUNK, CHUNK), 1)
        causal = col <= row
        anti = col >= row
        lane = lax.broadcasted_iota(jnp.int32, (CHUNK, LANES), 1)
        hf = carry[...].astype(bf16)
        hb = hb_store[c]
        cb = [_dot_nt(c_mat[:, g * SSD_STATE:(g + 1) * SSD_STATE], b_bf[:, g * SSD_STATE:(g + 1) * SSD_STATE])
              for g in range(SSD_GROUPS)]

        heads = range(SSD_HEADS)
        seg_f = [jnp.where(causal, cum[:, h:h + 1] - cum_t[h:h + 1, :], -jnp.inf) for h in heads]
        seg_b = [jnp.where(anti, rcum[:, SSD_HEADS + h:SSD_HEADS + h + 1]
                           - rcum_t[SSD_HEADS + h:SSD_HEADS + h + 1, :], -jnp.inf) for h in heads]
        e_f = [jnp.exp(s) for s in seg_f]
        e_b = [jnp.exp(s) for s in seg_b]
        ws = [(cb[h // SSD_RPG] * (e_f[h] * dt_t[h:h + 1, :] + e_b[h] * dt_t[SSD_HEADS + h:SSD_HEADS + h + 1, :])
               ).astype(bf16) for h in heads]
        rhs = []
        for p in range(SSD_HEADS // 2):
            xp = x_bf[:, p * LANES:(p + 1) * LANES]
            zero = jnp.zeros_like(xp)
            rhs.append(jnp.concatenate([jnp.where(lane < SSD_HEAD_DIM, xp, zero),
                                        jnp.where(lane >= SSD_HEAD_DIM, xp, zero)], axis=0))
        y = jnp.concatenate(
            [jnp.dot(jnp.concatenate([ws[2 * p], ws[2 * p + 1]], axis=1), rhs[p], preferred_element_type=f32)
             for p in range(SSD_HEADS // 2)], axis=1)

        def inter_t(h_all):
            return jnp.concatenate(
                [_dot_nt(h_all[g * gw:(g + 1) * gw], c_mat[:, g * SSD_STATE:(g + 1) * SSD_STATE])
                 for g in range(SSD_GROUPS)], axis=0)

        cum_f, cum_b = cum_t[:SSD_HEADS], rcum_t[SSD_HEADS:2 * SSD_HEADS]
        y_t = inter_t(hf) * head_rows(jnp.exp(cum_f)) + inter_t(hb) * head_rows(jnp.exp(cum_b))
        y_inter = jnp.concatenate([y_t[k * LANES:(k + 1) * LANES].T for k in range(SSD_INNER // LANES)], axis=1)
        state_update(cum_f, dt_t[:SSD_HEADS], CHUNK - 1)
        y = y + y_inter + dskip_ref[...] * x

        zv = z_ref[...]
        yz = y * (zv * jax.nn.sigmoid(zv))
        y_ref[...] = (_rms(yz) * g_ref[...]).astype(y_ref.dtype)

        if emit_state:
            @pl.when(c == nc - 1)
            def _():
                hfin_ref[0] = carry[...].reshape(SSD_HEADS, SSD_HEAD_DIM, SSD_STATE)


def ssd_mix(xbc, dt, z, h0, row0, n_seq, seq_len, dt_bias, a_log, d_skip, norm_g, emit_state, out_buf=None,
            layer=0, state_buf=None):
    nc = seq_len // CHUNK
    blk0 = row0 // CHUNK
    has_h0 = h0 is not None

    def chunk_map(s, p, c):
        return (blk0 + s * nc + jnp.where(p == 0, nc - 1 - c, c), 0)

    state_spec = pl.BlockSpec((None, 2, SSD_HEADS, SSD_HEAD_DIM, SSD_STATE), lambda s, p, c: (s, 0, 0, 0, 0))
    vec = lambda n: pl.BlockSpec((1, n), lambda s, p, c: (0, 0))
    in_specs = [
        pl.BlockSpec((CHUNK, SSD_CONV_DIM), chunk_map),
        pl.BlockSpec((CHUNK, LANES), chunk_map),
        pl.BlockSpec((CHUNK, SSD_INNER), lambda s, p, c: (blk0 + s * nc + p * c, 0)),
    ]
    args = [xbc, dt, z]
    if has_h0:
        in_specs.append(state_spec)
        args.append(h0)
    in_specs += [vec(LANES), vec(LANES), vec(SSD_INNER), vec(SSD_INNER)]
    pad = lambda t: jnp.concatenate([t.reshape(1, -1), jnp.zeros((1, LANES - t.size), f32)], axis=1)
    args += [pad(dt_bias), pad(a_log), jnp.repeat(d_skip, SSD_HEAD_DIM).reshape(1, SSD_INNER),
             norm_g.reshape(1, SSD_INNER)]
    aliases = {}
    if out_buf is not None:
        aliases = {len(args): 0}
        in_specs.append(pl.BlockSpec(memory_space=pl.ANY))
        args.append(out_buf)
    out_shape = [jax.ShapeDtypeStruct((N_TOK, SSD_INNER), bf16)]
    out_specs = [pl.BlockSpec((CHUNK, SSD_INNER), lambda s, p, c: (blk0 + s * nc + p * c, 0))]
    if emit_state:
        if state_buf is not None:
            aliases[len(args)] = 1
            in_specs.append(pl.BlockSpec(memory_space=pl.ANY))
            args.append(state_buf)
        out_shape.append(jax.ShapeDtypeStruct((n_seq, N_EVEN, 2, SSD_HEADS, SSD_HEAD_DIM, SSD_STATE), f32))
        out_specs.append(pl.BlockSpec((None, None, 2, SSD_HEADS, SSD_HEAD_DIM, SSD_STATE),
                                      lambda s, p, c: (s, layer, 0, 0, 0, 0)))
    res = pl.pallas_call(
        functools.partial(_ssd_kernel, nc, has_h0, len(aliases), emit_state),
        input_output_aliases=aliases,
        grid=(n_seq, 2, nc),
        in_specs=in_specs,
        out_specs=out_specs,
        out_shape=out_shape,
        scratch_shapes=[pltpu.VMEM((nc, SSD_INNER, SSD_STATE), bf16), pltpu.VMEM((SSD_INNER, SSD_STATE), f32)],
        compiler_params=pltpu.CompilerParams(
            dimension_semantics=("arbitrary", "arbitrary", "arbitrary"), vmem_limit_bytes=VMEM_LIMIT),
        name="ssd_scan",
    )(*args)
    return res if emit_state else res[0]


ML_DIRS = 2 * ML_HEADS


def _cummax_rows(a, reverse=False):
    row = lax.broadcasted_iota(jnp.int32, a.shape, 0)
    s = 1
    while s < CHUNK:
        if reverse:
            a = jnp.maximum(a, jnp.where(row < CHUNK - s, pltpu.roll(a, CHUNK - s, axis=0), -jnp.inf))
        else:
            a = jnp.maximum(a, jnp.where(row >= s, pltpu.roll(a, s, axis=0), -jnp.inf))
        s *= 2
    return a


ML_ST = ML_V_DIM + 16


def _mlstm_t_kernel(nc, has_state, n_buf, emit_state, *refs):
    qk_ref, v_ref, og_ref, gates_ref, gb_ref = refs[:5]
    refs = refs[5:]
    if has_state:
        s0_ref, m0_ref = refs[:2]
        refs = refs[2:]
    refs = refs[n_buf:]
    y_ref, refs = refs[0], refs[1:]
    if emit_state:
        cfin_ref, nfin_ref, mfin_ref = refs[:3]
        refs = refs[3:]
    s_store, m_store, s_carry, m_carry = refs

    phase = pl.program_id(1)
    c = pl.program_id(2)

    lane = lax.broadcasted_iota(jnp.int32, (CHUNK, LANES), 1)
    fwd_lane = lane < ML_HEADS
    fwd_row = lax.broadcasted_iota(jnp.int32, (1, LANES), 1) < ML_HEADS
    g = gates_ref[...] + gb_ref[...]
    li = g
    lf = pltpu.roll(jax.nn.log_sigmoid(g), LANES - ML_DIRS, axis=1)
    cum = jnp.where(fwd_lane, _cumsum_rows(lf), _cumsum_rows(lf, reverse=True))
    r = li - cum
    pm = jnp.where(fwd_lane, _cummax_rows(r), _cummax_rows(r, reverse=True))

    qk = qk_ref[...]
    k = qk[:, ML_QK_WIDTH:]
    own_half = [(lane < ML_QK_DIM) if h % 2 == 0 else (lane >= ML_QK_DIM) for h in range(ML_HEADS)]

    k_own = [jnp.where(own_half[h], k[:, (h // 2) * LANES:(h // 2 + 1) * LANES], 0.0).astype(bf16)
             for h in range(ML_HEADS)]
    v_tr = [v_ref[:, h * ML_V_DIM:(h + 1) * ML_V_DIM].astype(f32).T for h in range(ML_HEADS)]

    def state_update(m_row, big_m, d):
        edge = CHUNK - 1 if d == 0 else 0
        m_edge = big_m[edge:edge + 1, :]
        wc_row = jnp.exp(m_row - m_edge)
        wk_t = jnp.exp(r.T - big_m.T[:, edge:edge + 1])
        lhs = []
        for h in range(ML_HEADS):
            wk = wk_t[d * ML_HEADS + h:d * ML_HEADS + h + 1, :]
            lhs.append(jnp.concatenate([v_tr[h] * wk, jnp.broadcast_to(wk, (ML_ST - ML_V_DIM, CHUNK))],
                                       axis=0).astype(bf16))
        upd = [jnp.dot(lhs[h], k_own[h], preferred_element_type=f32) for h in range(ML_HEADS)]
        for h in range(ML_HEADS):
            cl = d * ML_HEADS + h
            wc = jnp.broadcast_to(wc_row[:, cl:cl + 1], (ML_ST, LANES))
            s_carry[d, h] = wc * s_carry[d, h] + upd[h]
        return cum[edge:edge + 1, :] + m_edge

    def init_state(d):
        if has_state:
            s_carry[d] = s0_ref[d]
        else:
            s_carry[d] = jnp.zeros((ML_HEADS, ML_ST, LANES), f32)

    def emit_final(d):
        for h in range(ML_HEADS):
            tile = s_carry[d, h]
            if h % 2 == 1:
                tile = pltpu.roll(tile, ML_QK_DIM, axis=1)
            cfin_ref[d, h] = tile[:ML_V_DIM, :ML_QK_DIM]
            nfin_ref[d, h:h + 1, :] = tile[ML_V_DIM:ML_V_DIM + 1, :ML_QK_DIM]

    @pl.when(jnp.logical_and(phase == 0, c == 0))
    def _():
        init_state(1)
        m_carry[...] = jnp.broadcast_to(m0_ref[...], m_carry.shape) if has_state else jnp.zeros(m_carry.shape, f32)

    @pl.when(phase == 0)
    def _backward_states():
        j = nc - 1 - c
        m_row = m_carry[0:1, :]
        s_store[j] = s_carry[1].astype(bf16)
        m_store[j] = m_carry[...]
        big_m = jnp.maximum(m_row, pm)
        m_new = state_update(m_row, big_m, 1)
        m_carry[...] = jnp.broadcast_to(jnp.where(fwd_row, m_row, m_new), m_carry.shape)

    @pl.when(phase == 1)
    def _forward_and_outputs():
        @pl.when(c == 0)
        def _():
            init_state(0)

        if emit_state:
            @pl.when(c == 0)
            def _():
                emit_final(1)

        m_both = m_carry[0:1, :]
        m_row = jnp.where(fwd_row, m_both, m_store[c][0:1, :])
        big_m = jnp.maximum(m_row, pm)
        big_m_t = big_m.T
        w_inter_t = jnp.exp(m_row - big_m).T
        floor_t = jnp.exp(-(cum + big_m)).T
        key = lax.broadcasted_iota(jnp.int32, (CHUNK, CHUNK), 0)
        qry = lax.broadcasted_iota(jnp.int32, (CHUNK, CHUNK), 1)
        masks = (key <= qry, key >= qry)
        q = (qk[:, :ML_QK_WIDTH] * (ML_QK_DIM ** -0.5)).astype(bf16)
        heads = range(ML_HEADS)
        hd_pairs = [(h, d) for h in heads for d in range(2)]
        q_pair = [q[:, (h // 2) * LANES:(h // 2 + 1) * LANES] for h in heads]
        v_th = [v_tr[h].astype(bf16) for h in heads]
        s_raw_t = [_dot_nt(k_own[h], q_pair[h]) for h in heads]
        inter = {(h, d): _dot_nt(s_carry[0, h].astype(bf16) if d == 0 else s_store[c, h], q_pair[h])
                 for h, d in hd_pairs}
        w_t = {(h, d): jnp.exp(jnp.where(masks[d], r[:, d * ML_HEADS + h:d * ML_HEADS + h + 1]
                                         - big_m_t[d * ML_HEADS + h:d * ML_HEADS + h + 1, :], -jnp.inf))
               for h, d in hd_pairs}
        sw_t = {hd: s_raw_t[hd[0]] * w_t[hd] for hd in hd_pairs}
        num = {hd: jnp.dot(v_th[hd[0]], sw_t[hd].astype(bf16), preferred_element_type=f32) for hd in hd_pairs}
        out_t = []
        for h in heads:
            acc = None
            for d in range(2):
                cl = d * ML_HEADS + h
                wi = w_inter_t[cl:cl + 1, :]
                den = jnp.sum(sw_t[h, d], axis=0, keepdims=True) + wi * inter[h, d][ML_V_DIM:ML_V_DIM + 1]
                inv = 1.0 / jnp.maximum(jnp.abs(den), floor_t[cl:cl + 1, :])
                part = (num[h, d] + wi * inter[h, d][:ML_V_DIM]) * inv
                acc = part if acc is None else acc + part
            out_t.append(acc)
        for h in heads:
            og = og_ref[:, h * ML_V_DIM:(h + 1) * ML_V_DIM]
            y_ref[:, h * ML_V_DIM:(h + 1) * ML_V_DIM] = (out_t[h].T * jax.nn.sigmoid(og)).astype(y_ref.dtype)

        m_fin = jnp.where(fwd_row, state_update(m_row, big_m, 0), m_both)
        m_carry[...] = jnp.broadcast_to(m_fin, m_carry.shape)

        if emit_state:
            @pl.when(c == nc - 1)
            def _():
                emit_final(0)
                mfin_ref[...] = m_fin


def mlstm_mix(qk, v, og, gates, gate_b, state, row0, n_seq, seq_len, emit_state, out_buf=None,
              layer=0, state_bufs=None):
    nc = seq_len // CHUNK
    blk0 = row0 // CHUNK
    has_state = state is not None

    def chunk_map(s, p, c):
        return (blk0 + s * nc + jnp.where(p == 0, nc - 1 - c, c), 0)

    m_spec = pl.BlockSpec((None, 1, LANES), lambda s, p, c: (s, 0, 0))
    in_specs = [
        pl.BlockSpec((CHUNK, 2 * ML_QK_WIDTH), chunk_map),
        pl.BlockSpec((CHUNK, ML_V_WIDTH), chunk_map),
        pl.BlockSpec((CHUNK, ML_V_WIDTH), lambda s, p, c: (blk0 + s * nc + p * c, 0)),
        pl.BlockSpec((CHUNK, LANES), chunk_map),
        pl.BlockSpec((1, LANES), lambda s, p, c: (0, 0)),
    ]
    gb = jnp.concatenate([gate_b.reshape(1, 2 * ML_DIRS), jnp.zeros((1, LANES - 2 * ML_DIRS), f32)], axis=1)
    args = [qk, v, og, gates, gb]
    if has_state:
        c0, n0, m0 = state
        rows = jnp.concatenate([c0, jnp.broadcast_to(n0[..., None, :], n0.shape[:-1] + (ML_ST - ML_V_DIM, ML_QK_DIM))],
                               axis=-2)
        zeros = jnp.zeros_like(rows)
        odd = (jnp.arange(ML_HEADS) % 2 == 1)[:, None, None]
        s0 = jnp.where(odd, jnp.concatenate([zeros, rows], axis=-1), jnp.concatenate([rows, zeros], axis=-1))
        m0 = jnp.concatenate([m0.reshape(n_seq, 1, ML_DIRS), jnp.zeros((n_seq, 1, LANES - ML_DIRS), f32)], axis=-1)
        in_specs += [pl.BlockSpec((None, 2, ML_HEADS, ML_ST, LANES), lambda s, p, c: (s, 0, 0, 0, 0)), m_spec]
        args += [s0, m0]
    aliases = {}
    if out_buf is not None:
        aliases[len(args)] = 0
        in_specs.append(pl.BlockSpec(memory_space=pl.ANY))
        args.append(out_buf)
    out_shape = [jax.ShapeDtypeStruct((N_TOK, ML_V_WIDTH), bf16)]
    out_specs = [pl.BlockSpec((CHUNK, ML_V_WIDTH), lambda s, p, c: (blk0 + s * nc + p * c, 0))]
    if emit_state:
        if state_bufs is not None:
            for k_out, buf in enumerate(state_bufs):
                aliases[len(args)] = 1 + k_out
                in_specs.append(pl.BlockSpec(memory_space=pl.ANY))
                args.append(buf)
        out_shape += [jax.ShapeDtypeStruct((n_seq, N_ODD, 2, ML_HEADS, ML_V_DIM, ML_QK_DIM), f32),
                      jax.ShapeDtypeStruct((n_seq, N_ODD, 2, ML_HEADS, ML_QK_DIM), f32),
                      jax.ShapeDtypeStruct((n_seq, 1, LANES), f32)]
        out_specs += [pl.BlockSpec((None, None, 2, ML_HEADS, ML_V_DIM, ML_QK_DIM),
                                   lambda s, p, c: (s, layer, 0, 0, 0, 0)),
                      pl.BlockSpec((None, None, 2, ML_HEADS, ML_QK_DIM), lambda s, p, c: (s, layer, 0, 0, 0)),
                      m_spec]
    res = pl.pallas_call(
        functools.partial(_mlstm_t_kernel, nc, has_state, len(aliases), emit_state),
        input_output_aliases=aliases,
        grid=(n_seq, 2, nc),
        in_specs=in_specs,
        out_specs=out_specs,
        out_shape=out_shape,
        scratch_shapes=[pltpu.VMEM((nc, ML_HEADS, ML_ST, LANES), bf16), pltpu.VMEM((nc, SUBLANES, LANES), f32),
                        pltpu.VMEM((2, ML_HEADS, ML_ST, LANES), f32), pltpu.VMEM((SUBLANES, LANES), f32)],
        compiler_params=pltpu.CompilerParams(
            dimension_semantics=("arbitrary", "arbitrary", "arbitrary"), vmem_limit_bytes=VMEM_LIMIT),
        name="mlstm_scan",
    )(*args)
    if not emit_state:
        return res[0]
    y, c_fin, n_fin, mfin = res
    return y, c_fin, n_fin, mfin[:, 0, :ML_DIRS].reshape(n_seq, 2, ML_HEADS)


def kernel(x_prompt, x_sample, c, cache_na_k, cache_na_v, state_ssd, state_mlstm_c, state_mlstm_n, state_mlstm_m,
           c_ctx, w_mod, b_mod, norm_mix, norm_ffn, w_in_even, w_out_even, na_rpb, ssd_conv_w, ssd_conv_b,
           ssd_dt_bias, ssd_a_log, ssd_d, ssd_norm, w_in_odd, w_out_odd, ml_conv_w, ml_conv_b, ml_gate_b,
           w_ff1, w_ff2, norm_f):
    xs = [x_prompt.reshape(N_PROMPT, D_MODEL), x_sample.reshape(DEC_BATCH * DEC_SEQ, D_MODEL)]
    cond = jnp.concatenate([c_ctx[None, :], c, jnp.zeros((SUBLANES - N_COND, D_MODEL), f32)], axis=0)
    mod = adaln_all(cond, w_mod, b_mod)[:, :N_COND].reshape(DEPTH, N_COND, 1, N_MOD * D_MODEL)

    even_main = 3 * NA_WIDTH + SSD_INNER + SSD_CONV_DIM
    odd_main = 2 * ML_QK_WIDTH + 2 * ML_V_WIDTH

    def tail_bf16(w, main):
        t = w[:, :, main:]
        return jnp.concatenate([t, jnp.zeros(t.shape[:2] + (LANES - t.shape[2],), f32)], axis=2).astype(bf16)

    wi_even, wt_even = cast_bf16(w_in_even, even_main), tail_bf16(w_in_even, even_main)
    wi_odd, wt_odd = cast_bf16(w_in_odd, odd_main), tail_bf16(w_in_odd, odd_main)
    wo_even, wo_odd = cast_bf16(w_out_even), cast_bf16(w_out_odd)
    w1_all, w2_all = cast_bf16(w_ff1), cast_bf16(w_ff2)

    out_k, out_v, out_m = [], [], []
    new_ssd = new_c = new_n = None
    for l in range(DEPTH):
        norm_last = norm_f if l == DEPTH - 1 else None
        if l % 2 == 0:
            e = l // 2
            o0 = 3 * NA_WIDTH
            segs = ((0, NA_WIDTH, NA_HEAD_DIM ** -0.5, False),
                    (NA_WIDTH, 2 * NA_WIDTH, 1.0, False),
                    (2 * NA_WIDTH, 3 * NA_WIDTH, 1.0, False),
                    (o0, o0 + SSD_INNER, 1.0, False),
                    (o0 + SSD_INNER, o0 + SSD_INNER + SSD_CONV_DIM, 1.0, True),
                    (None, None, 1.0, False))
            outs = ((0, bf16, False), (1, bf16, False), (2, bf16, False), (1, f32, True), (2, f32, True),
                    (3, f32, False), (4, f32, False), (5, f32, False))
            q, k, v, kp, vp, z, xbc, dt = in_proj(xs, mod[l], norm_mix[l], wi_even, wt_even, e, segs, outs,
                                                  ssd_conv_w[e], ssd_conv_b[e])
            heads = lambda t: t.reshape(BATCH, SEQ, NA_HEADS, NA_HEAD_DIM)
            out_k.append(jnp.swapaxes(heads(kp), 1, 2))
            out_v.append(jnp.swapaxes(heads(vp), 1, 2))
            ssd_w = (ssd_dt_bias[e], ssd_a_log[e], ssd_d[e], ssd_norm[e])
            y_ssd, new_ssd = ssd_mix(xbc, dt, z, None, 0, BATCH, SEQ, *ssd_w, True, layer=e, state_buf=new_ssd)
            y_ssd = ssd_mix(xbc, dt, z, state_ssd[:, e], N_PROMPT, DEC_BATCH, DEC_SEQ, *ssd_w, False, out_buf=y_ssd)
            tokens = lambda t: jnp.swapaxes(t, 1, 2).reshape(DEC_BATCH, PAST_LEN, NA_WIDTH).astype(bf16)
            y_na = ctx_attention(q, k, v)
            y_na = na_latent(q, k, v, tokens(cache_na_k[:, e]), tokens(cache_na_v[:, e]),
                             na_bias_table(na_rpb[e]), y_na)
            res = out_mlp(xs, mod[l], norm_ffn[l], [y_na, y_ssd], wo_even, e, w1_all, w2_all, l, norm_last)
        else:
            o = l // 2
            a0 = 2 * ML_QK_WIDTH
            segs = ((0, a0, 1.0, True),
                    (a0, a0 + ML_V_WIDTH, 1.0, False),
                    (a0 + ML_V_WIDTH, a0 + 2 * ML_V_WIDTH, 1.0, False),
                    (None, None, 1.0, False))
            outs = ((0, f32, False), (1, bf16, False), (2, f32, False), (3, f32, False))
            qk, v, og, gates = in_proj(xs, mod[l], norm_mix[l], wi_odd, wt_odd, o, segs, outs,
                                       ml_conv_w[o], ml_conv_b[o])
            y_ml, new_c, new_n, m_fin = mlstm_mix(qk, v, og, gates, ml_gate_b[o], None, 0, BATCH, SEQ, True,
                                                  layer=o, state_bufs=None if new_c is None else (new_c, new_n))
            out_m.append(m_fin)
            state = (state_mlstm_c[:, o], state_mlstm_n[:, o], state_mlstm_m[:, o])
            y_ml = mlstm_mix(qk, v, og, gates, ml_gate_b[o], state, N_PROMPT, DEC_BATCH, DEC_SEQ, False, out_buf=y_ml)
            res = out_mlp(xs, mod[l], norm_ffn[l], [y_ml], wo_odd, o, w1_all, w2_all, l, norm_last)
        xs = list(res) if norm_last is not None else [res]

    y_prompt = xs[0].reshape(BATCH, SEQ, D_MODEL)
    y_sample = xs[1].reshape(DEC_BATCH, DEC_SEQ, D_MODEL)
    return (y_prompt, y_sample, jnp.stack(out_k, axis=1), jnp.stack(out_v, axis=1), new_ssd,
            new_c, new_n, jnp.stack(out_m, axis=1))
```

```python
import functools
import math

import jax
import jax.numpy as jnp
from jax import lax
from jax.experimental import pallas as pl
from jax.experimental.pallas import tpu as pltpu

D_MODEL = 1024
BATCH = 32
SEQ = 256
DEPTH = 4
DEC_BATCH = 2
DEC_SEQ = 4096
PAST_LEN = 256
GRID_W = 64
N_EVEN = (DEPTH + 1) // 2
N_ODD = DEPTH // 2
RMS_EPS = 1e-6
N_MOD = 6
D_FF = 4 * D_MODEL
CONV_K = 3
Q_BLOCK = 128
NA_HEADS = 8
NA_HEAD_DIM = 64
NA_WIDTH = NA_HEADS * NA_HEAD_DIM
NA_WIN_ROWS = 8
NA_WIN_COLS = 16
NA_RPB_ROWS = 2 * NA_WIN_ROWS - 1
NA_RPB_COLS = 2 * NA_WIN_COLS - 1
SSD_INNER = D_MODEL
SSD_HEAD_DIM = 64
SSD_HEADS = SSD_INNER // SSD_HEAD_DIM
SSD_GROUPS = 2
SSD_RPG = SSD_HEADS // SSD_GROUPS
SSD_STATE = 128
SSD_GN = SSD_GROUPS * SSD_STATE
SSD_CONV_DIM = SSD_INNER + 2 * SSD_GN
SSD_CHUNK = 128
ML_HEADS = 8
ML_QK_DIM = D_MODEL // 16
ML_V_DIM = D_MODEL // 8
ML_QK_WIDTH = ML_HEADS * ML_QK_DIM
ML_V_WIDTH = ML_HEADS * ML_V_DIM
ML_CHUNK = 64
EVEN_MIX = NA_WIDTH + SSD_INNER

N_PROMPT = BATCH * SEQ
N_TOK = N_PROMPT + DEC_BATCH * DEC_SEQ
N_COND = 1 + DEC_BATCH
LANES = 128
SUBLANES = 8
VMEM_LIMIT = 56 * 1024 * 1024
TM = 512

f32 = jnp.float32
bf16 = jnp.bfloat16


def _cond_row(i, tm):
    start = i * tm
    return jnp.where(start < N_PROMPT, 0, (start - N_PROMPT) // DEC_SEQ + 1)


def _const_spec(shape):
    nd = len(shape)
    return pl.BlockSpec(shape, lambda i: (0,) * nd, pipeline_mode=pl.Buffered(1))


def _rms(x):
    return x * lax.rsqrt(jnp.mean(x * x, axis=-1, keepdims=True) + RMS_EPS)


def _modulated(x, g, mod, k):
    shift = mod[:, k * D_MODEL:(k + 1) * D_MODEL]
    scale = mod[:, (k + 1) * D_MODEL:(k + 2) * D_MODEL]
    return (_rms(x) * g) * (1.0 + scale) + shift


def _mod_kernel(c_ref, w_ref, b_ref, o_ref):
    c = c_ref[...]
    a = (c * jax.nn.sigmoid(c)).astype(bf16)
    o_ref[...] = jnp.dot(a, w_ref[...].astype(bf16), preferred_element_type=f32) + b_ref[...]


def adaln_all(cond, w_mod, b_mod):
    tn = 1536
    nj = N_MOD * D_MODEL // tn
    return pl.pallas_call(
        _mod_kernel,
        grid=(DEPTH, nj),
        in_specs=[
            pl.BlockSpec((SUBLANES, D_MODEL), lambda l, j: (0, 0)),
            pl.BlockSpec((None, D_MODEL, tn), lambda l, j: (l, 0, j)),
            pl.BlockSpec((None, 1, tn), lambda l, j: (l, 0, j)),
        ],
        out_specs=pl.BlockSpec((None, SUBLANES, tn), lambda l, j: (l, 0, j)),
        out_shape=jax.ShapeDtypeStruct((DEPTH, SUBLANES, N_MOD * D_MODEL), f32),
        compiler_params=pltpu.CompilerParams(
            dimension_semantics=("arbitrary", "arbitrary"), vmem_limit_bytes=VMEM_LIMIT),
        name="adaln",
    )(cond, w_mod, b_mod.reshape(DEPTH, 1, N_MOD * D_MODEL))


def _cast_kernel(x_ref, o_ref):
    o_ref[...] = x_ref[...].astype(o_ref.dtype)


def cast_bf16(w, cols=None):
    n_l, k, n = w.shape
    cols = n if cols is None else cols
    bk = 512
    spec = pl.BlockSpec((None, bk, cols), lambda l, i: (l, i, 0))
    return pl.pallas_call(
        _cast_kernel,
        grid=(n_l, k // bk),
        in_specs=[spec],
        out_specs=spec,
        out_shape=jax.ShapeDtypeStruct((n_l, k, cols), bf16),
        compiler_params=pltpu.CompilerParams(
            dimension_semantics=("arbitrary", "arbitrary"), vmem_limit_bytes=VMEM_LIMIT),
        name="cast_bf16",
    )(w)


def _layer_spec(w, l):
    nd = w.ndim - 1
    return pl.BlockSpec((None,) + w.shape[1:], lambda i: (l,) + (0,) * nd, pipeline_mode=pl.Buffered(1))


def _tile_x(x_refs, tm):
    if len(x_refs) == 1:
        return x_refs[0][...]
    return jnp.where(pl.program_id(0) < N_PROMPT // tm, x_refs[0][...], x_refs[1][...])


def _conv_silu_tile(y, prev_row, next_row, w, b):
    i = pl.program_id(0)
    rows = y.shape[0]
    g = i * rows + lax.broadcasted_iota(jnp.int32, (rows, 1), 0)
    is_prompt = i < N_PROMPT // rows
    pos = jnp.where(is_prompt, g & (SEQ - 1), (g - N_PROMPT) & (DEC_SEQ - 1))
    seq_last = jnp.where(is_prompt, SEQ - 1, DEC_SEQ - 1)
    row = lax.broadcasted_iota(jnp.int32, y.shape, 0)
    up = jnp.where(row == 0, prev_row, pltpu.roll(y, 1, axis=0))
    dn = jnp.where(row == rows - 1, next_row, pltpu.roll(y, rows - 1, axis=0))
    up = jnp.where(pos == 0, 0.0, up)
    dn = jnp.where(pos == seq_last, 0.0, dn)
    c = up * w[0:1, :] + y * w[1:2, :] + dn * w[2:3, :] + b
    return c * jax.nn.sigmoid(c)


def _in_proj_kernel(n_x, n_buf, segs, outs, *refs):
    x_refs = refs[:n_x]
    prev_ref, next_ref, mod_ref, g_ref, w_ref, wt_ref, cw_ref, cb_ref = refs[n_x:n_x + 8]
    o_refs = refs[n_x + 8 + n_buf:]
    mod, g = mod_ref[...], g_ref[...]
    h = _modulated(_tile_x(x_refs, TM), g, mod, 0).astype(bf16)
    ys = []
    for a, b, scale, conv in segs:
        w = wt_ref[...] if a is None else w_ref[:, a:b]
        if conv:
            halo = jnp.concatenate([prev_ref[...], next_ref[...]], axis=0)
            hh = _modulated(halo, g, mod, 0).astype(bf16)
            y = jnp.dot(jnp.concatenate([h, hh], axis=0), w, preferred_element_type=f32)
            y = _conv_silu_tile(y[:TM], y[TM + SUBLANES - 1:TM + SUBLANES], y[TM + SUBLANES:TM + SUBLANES + 1],
                                cw_ref[...], cb_ref[...])
        else:
            y = jnp.dot(h, w, preferred_element_type=f32)
        ys.append(y if scale == 1.0 else y * scale)
    for (si, _, head_major), o_ref in zip(outs, o_refs):
        if head_major:
            @pl.when(pl.program_id(0) < N_PROMPT // TM)
            def _(o_ref=o_ref, si=si):
                for b in range(TM // SEQ):
                    for hd in range(NA_HEADS):
                        o_ref[b, hd] = ys[si][b * SEQ:(b + 1) * SEQ, hd * NA_HEAD_DIM:(hd + 1) * NA_HEAD_DIM]
        else:
            o_ref[...] = ys[si].astype(o_ref.dtype)


def _x_specs(xs, tm):
    if len(xs) == 1:
        return [pl.BlockSpec((tm, D_MODEL), lambda i: (i, 0))]
    n_p = N_PROMPT // tm
    return [pl.BlockSpec((tm, D_MODEL), lambda i: (jnp.minimum(i, n_p - 1), 0)),
            pl.BlockSpec((tm, D_MODEL), lambda i: (jnp.maximum(i - n_p, 0), 0))]


def in_proj(xs, mod_l, g, w, w_tail, l, segs, outs, conv_w, conv_b, cache_bufs=None):
    grid = (N_TOK // TM,)
    last_prompt = N_PROMPT // TM - 1
    halo_src = xs[-1]
    per = TM // SUBLANES
    blk0 = (halo_src.shape[0] - DEC_BATCH * DEC_SEQ) // SUBLANES
    n_blk = halo_src.shape[0] // SUBLANES
    tile0 = N_PROMPT // TM

    def prev_map(i):
        return (jnp.clip(blk0 + (i - tile0) * per - 1, 0, n_blk - 1), 0)

    def next_map(i):
        return (jnp.clip(blk0 + (i - tile0 + 1) * per, 0, n_blk - 1), 0)

    ch = conv_w.shape[1]
    out_shape, out_specs = [], []
    for si, dt, head_major in outs:
        width = LANES if segs[si][0] is None else segs[si][1] - segs[si][0]
        if head_major:
            out_shape.append(jax.ShapeDtypeStruct((BATCH, N_EVEN, NA_HEADS, SEQ, NA_HEAD_DIM), dt))
            out_specs.append(pl.BlockSpec((TM // SEQ, None, NA_HEADS, SEQ, NA_HEAD_DIM),
                                          lambda i: (jnp.minimum(i, last_prompt), l, 0, 0, 0)))
        else:
            out_shape.append(jax.ShapeDtypeStruct((N_TOK, width), dt))
            out_specs.append(pl.BlockSpec((TM, width), lambda i: (i, 0)))
    args = [*xs, halo_src, halo_src, mod_l, g.reshape(1, D_MODEL), w, w_tail, conv_w, conv_b.reshape(1, ch)]
    aliases = {}
    if cache_bufs is not None:
        head_major_outs = [k for k, o in enumerate(outs) if o[2]]
        for k_out, buf in zip(head_major_outs, cache_bufs):
            aliases[len(args)] = k_out
            args.append(buf)
    return pl.pallas_call(
        functools.partial(_in_proj_kernel, len(xs), len(aliases), segs, outs),
        input_output_aliases=aliases,
        grid=grid,
        in_specs=_x_specs(xs, TM) + [
            pl.BlockSpec((SUBLANES, D_MODEL), prev_map),
            pl.BlockSpec((SUBLANES, D_MODEL), next_map),
            pl.BlockSpec((None, 1, N_MOD * D_MODEL), lambda i: (_cond_row(i, TM), 0, 0)),
            _const_spec((1, D_MODEL)),
            _layer_spec(w, l),
            _layer_spec(w_tail, l),
            _const_spec((CONV_K, ch)),
            _const_spec((1, ch)),
        ] + [pl.BlockSpec(memory_space=pl.ANY)] * len(aliases),
        out_specs=out_specs,
        out_shape=out_shape,
        compiler_params=pltpu.CompilerParams(
            dimension_semantics=("arbitrary",), vmem_limit_bytes=VMEM_LIMIT),
        name="in_proj",
    )(*args)


def _out_mlp_kernel(n_x, n_mix, final, *refs):
    x_refs = refs[:n_x]
    mod_ref, g_ref = refs[n_x:n_x + 2]
    refs = refs[n_x + 2:]
    mix_refs = refs[:n_mix]
    wo_ref, w1_ref, w2_ref = refs[n_mix:n_mix + 3]
    rest = refs[n_mix + 3:]
    if final:
        gf_ref, op_ref, os_ref = rest
    else:
        (o_ref,) = rest
    mod = mod_ref[...]
    m = None
    k0 = 0
    for r in mix_refs:
        kw = r.shape[-1]
        part = jnp.dot(r[...].astype(bf16), wo_ref[k0:k0 + kw, :], preferred_element_type=f32)
        m = part if m is None else m + part
        k0 += kw
    x1 = _tile_x(x_refs, OUT_TM) + mod[:, 2 * D_MODEL:3 * D_MODEL] * m
    h2 = _modulated(x1, g_ref[...], mod, 3).astype(bf16)
    u = jnp.dot(h2, w1_ref[...], preferred_element_type=f32)
    a = jnp.square(jnp.maximum(u, 0.0)).astype(bf16)
    x2 = x1 + mod[:, 5 * D_MODEL:6 * D_MODEL] * jnp.dot(a, w2_ref[...], preferred_element_type=f32)
    if not final:
        o_ref[...] = x2
        return
    y = _rms(x2) * gf_ref[...]
    is_prompt = pl.program_id(0) < N_PROMPT // OUT_TM

    @pl.when(is_prompt)
    def _():
        op_ref[...] = y

    @pl.when(jnp.logical_not(is_prompt))
    def _():
        os_ref[...] = y


OUT_TM = 512


def out_mlp(xs, mod_l, g_ffn, mixes, w_out, l_out, w1, w2, l, norm_f=None):
    final = norm_f is not None
    tm = OUT_TM
    grid = (N_TOK // tm,)
    in_specs = _x_specs(xs, tm) + [
        pl.BlockSpec((None, 1, N_MOD * D_MODEL), lambda i: (_cond_row(i, tm), 0, 0)),
        _const_spec((1, D_MODEL)),
    ]
    in_specs += [pl.BlockSpec((tm, m.shape[-1]), lambda i: (i, 0)) for m in mixes]
    in_specs += [_layer_spec(w_out, l_out), _layer_spec(w1, l), _layer_spec(w2, l)]
    args = [*xs, mod_l, g_ffn.reshape(1, D_MODEL), *mixes, w_out, w1, w2]
    out_specs = pl.BlockSpec((tm, D_MODEL), lambda i: (i, 0))
    out_shape = jax.ShapeDtypeStruct((N_TOK, D_MODEL), f32)
    if final:
        in_specs.append(_const_spec((1, D_MODEL)))
        args.append(norm_f.reshape(1, D_MODEL))
        n_p = N_PROMPT // tm
        out_specs = [pl.BlockSpec((tm, D_MODEL), lambda i: (jnp.minimum(i, n_p - 1), 0)),
                     pl.BlockSpec((tm, D_MODEL), lambda i: (jnp.maximum(i - n_p, 0), 0))]
        out_shape = [jax.ShapeDtypeStruct((N_PROMPT, D_MODEL), f32),
                     jax.ShapeDtypeStruct((N_TOK - N_PROMPT, D_MODEL), f32)]
    return pl.pallas_call(
        functools.partial(_out_mlp_kernel, len(xs), len(mixes), final),
        grid=grid,
        in_specs=in_specs,
        out_specs=out_specs,
        out_shape=out_shape,
        compiler_params=pltpu.CompilerParams(
            dimension_semantics=("arbitrary",), vmem_limit_bytes=VMEM_LIMIT),
        name="out_mlp",
    )(*args)


NA_PAIRS = NA_HEADS // 2
NA_ROWS = DEC_SEQ // GRID_W
NA_WIN = NA_WIN_ROWS * GRID_W


def _dot_nt(a, b):
    return lax.dot_general(a, b, (((1,), (1,)), ((), ())), preferred_element_type=f32)


def _pair_stack(x):
    lane = lax.broadcasted_iota(jnp.int32, x.shape, 1)
    zero = jnp.zeros_like(x)
    return jnp.concatenate([jnp.where(lane < NA_HEAD_DIM, x, zero), jnp.where(lane >= NA_HEAD_DIM, x, zero)], axis=0)


def _pair_unstack(o):
    n = o.shape[0] // 2
    lane = lax.broadcasted_iota(jnp.int32, (n, LANES), 1)
    return jnp.where(lane < NA_HEAD_DIM, o[:n], o[n:])


def _softmax_pv(scores, values):
    def lane_tiles(blocks):
        return [b[:, k * LANES:(k + 1) * LANES] for b in blocks for k in range(b.shape[1] // LANES)]

    ms = [jnp.max(functools.reduce(jnp.maximum, lane_tiles(s)), axis=1, keepdims=True) for s in scores]
    ps = [[jnp.exp(b - m) for b in s] for s, m in zip(scores, ms)]
    invs = [1.0 / jnp.sum(functools.reduce(jnp.add, lane_tiles(p)), axis=1, keepdims=True) for p in ps]
    pn = [[(b * inv).astype(bf16) for b in p] for p, inv in zip(ps, invs)]
    return [functools.reduce(jnp.add, [jnp.dot(b, v, preferred_element_type=f32) for b, v in zip(p, vs)])
            for p, vs in zip(pn, values)]


def _ctx_attn_kernel(q_ref, k_ref, v_ref, o_ref):
    pairs = [slice(p * LANES, (p + 1) * LANES) for p in range(NA_PAIRS)]
    scores = [[_dot_nt(_pair_stack(q_ref[:, lanes]), k_ref[:, lanes])] for lanes in pairs]
    outs = _softmax_pv(scores, [[v_ref[:, lanes]] for lanes in pairs])
    for lanes, o in zip(pairs, outs):
        o_ref[:, lanes] = _pair_unstack(o).astype(o_ref.dtype)


def ctx_attention(q, k, v):
    spec = pl.BlockSpec((SEQ, NA_WIDTH), lambda b: (b, 0))
    return pl.pallas_call(
        _ctx_attn_kernel,
        grid=(BATCH,),
        in_specs=[spec, spec, spec],
        out_specs=spec,
        out_shape=jax.ShapeDtypeStruct((N_TOK, NA_WIDTH), bf16),
        compiler_params=pltpu.CompilerParams(dimension_semantics=("arbitrary",), vmem_limit_bytes=VMEM_LIMIT),
        name="ctx_attn",
    )(q, k, v)


def _na_bias_kernel(rpb_ref, o_ref):
    pair = pl.program_id(0)
    shape = (GRID_W, LANES)
    qc = lax.broadcasted_iota(jnp.int32, shape, 0)
    lane = lax.broadcasted_iota(jnp.int32, shape, 1)
    kc = lane & (GRID_W - 1)
    low = lane < GRID_W
    col_start = jnp.clip(qc - NA_WIN_COLS // 2, 0, GRID_W - NA_WIN_COLS)
    valid = (kc >= col_start) & (kc < col_start + NA_WIN_COLS)
    rel_c = jnp.clip(kc - qc + NA_WIN_COLS - 1, 0, NA_RPB_COLS - 1)
    for e in range(2):
        base = (2 * pair + e) * NA_RPB_ROWS
        pieces = []
        for rr in range(NA_RPB_ROWS - 1):
            val = jnp.zeros(shape, f32)
            for t in range(NA_RPB_COLS):
                s_lo = rpb_ref[(base + rr) * NA_RPB_COLS + t]
                s_hi = rpb_ref[(base + rr + 1) * NA_RPB_COLS + t]
                val = jnp.where(rel_c == t, jnp.where(low, s_lo, s_hi), val)
            pieces.append(jnp.where(valid, val, -jnp.inf))
        for d in range(NA_WIN_ROWS):
            for i in range(0, NA_WIN_ROWS, 2):
                o_ref[d, e * GRID_W:(e + 1) * GRID_W, i * GRID_W:(i + 2) * GRID_W] = pieces[d + i]


def na_bias_table(rpb):
    return pl.pallas_call(
        _na_bias_kernel,
        grid=(NA_PAIRS,),
        in_specs=[pl.BlockSpec(memory_space=pltpu.SMEM)],
        out_specs=pl.BlockSpec((None, NA_WIN_ROWS, 2 * GRID_W, NA_WIN), lambda p: (p, 0, 0, 0)),
        out_shape=jax.ShapeDtypeStruct((NA_PAIRS, NA_WIN_ROWS, 2 * GRID_W, NA_WIN), f32),
        compiler_params=pltpu.CompilerParams(dimension_semantics=("arbitrary",), vmem_limit_bytes=VMEM_LIMIT),
        name="na_bias",
    )(rpb.reshape(-1))


NA_STEP_ROWS = 4


def _na_first_key_row(r):
    return jnp.clip(r - NA_WIN_ROWS // 2, 0, NA_ROWS - NA_WIN_ROWS)


def _na_kernel(q_ref, k_ref, v_ref, kc_ref, vc_ref, bias_ref, buf_ref, o_ref):
    del buf_ref
    where, scores, values = [], [], []
    for j in range(NA_STEP_ROWS):
        r = pl.program_id(1) * NA_STEP_ROWS + j
        first = _na_first_key_row(r)
        start = pl.multiple_of(first * GRID_W, GRID_W)
        shift = first - r + NA_WIN_ROWS - 1
        rows = slice(j * GRID_W, (j + 1) * GRID_W)
        for p in range(NA_PAIRS):
            lanes = slice(p * LANES, (p + 1) * LANES)
            qq = _pair_stack(q_ref[rows, lanes])
            where.append((rows, lanes))
            scores.append([_dot_nt(qq, k_ref[pl.ds(start, NA_WIN), lanes]) + bias_ref[p, shift],
                           _dot_nt(qq, kc_ref[:, lanes])])
            values.append([v_ref[pl.ds(start, NA_WIN), lanes], vc_ref[:, lanes]])
    for (rows, lanes), o in zip(where, _softmax_pv(scores, values)):
        o_ref[rows, lanes] = _pair_unstack(o).astype(o_ref.dtype)


def na_latent(q, k, v, k_ctx, v_ctx, bias, buf):
    rows = NA_STEP_ROWS * GRID_W
    steps = NA_ROWS // NA_STEP_ROWS
    row0 = N_PROMPT // rows
    seq0 = N_PROMPT // DEC_SEQ
    kv_spec = pl.BlockSpec((DEC_SEQ, NA_WIDTH), lambda b, r: (seq0 + b, 0))
    ctx_spec = pl.BlockSpec((None, PAST_LEN, NA_WIDTH), lambda b, r: (b, 0, 0))
    return pl.pallas_call(
        _na_kernel,
        grid=(DEC_BATCH, steps),
        in_specs=[
            pl.BlockSpec((rows, NA_WIDTH), lambda b, r: (row0 + b * steps + r, 0)),
            kv_spec, kv_spec, ctx_spec, ctx_spec,
            pl.BlockSpec(bias.shape, lambda b, r: (0, 0, 0, 0), pipeline_mode=pl.Buffered(1)),
            pl.BlockSpec(memory_space=pl.ANY),
        ],
        out_specs=pl.BlockSpec((rows, NA_WIDTH), lambda b, r: (row0 + b * steps + r, 0)),
        out_shape=jax.ShapeDtypeStruct((N_TOK, NA_WIDTH), bf16),
        input_output_aliases={6: 0},
        compiler_params=pltpu.CompilerParams(
            dimension_semantics=("arbitrary", "arbitrary"), vmem_limit_bytes=VMEM_LIMIT),
        name="na_latent",
    )(q, k, v, k_ctx, v_ctx, bias, buf)


CHUNK = 128


def _cumsum_rows(a, reverse=False):
    row = lax.broadcasted_iota(jnp.int32, a.shape, 0)
    s = 1
    while s < CHUNK:
        if reverse:
            a = a + jnp.where(row < CHUNK - s, pltpu.roll(a, CHUNK - s, axis=0), 0.0)
        else:
            a = a + jnp.where(row >= s, pltpu.roll(a, s, axis=0), 0.0)
        s *= 2
    return a


def _ssd_kernel(nc, has_h0, n_buf, emit_state, *refs):
    xbc_ref, dt_ref, z_ref = refs[:3]
    refs = refs[3:]
    if has_h0:
        h0_ref, refs = refs[0], refs[1:]
    dtb_ref, alog_ref, dskip_ref, g_ref = refs[:4]
    refs = refs[4 + n_buf:]
    y_ref, refs = refs[0], refs[1:]
    if emit_state:
        hfin_ref, refs = refs[0], refs[1:]
    hb_store, carry = refs

    phase = pl.program_id(1)
    c = pl.program_id(2)
    gw = SSD_RPG * SSD_HEAD_DIM

    xbc = xbc_ref[...]
    x = xbc[:, :SSD_INNER]
    x_bf = x.astype(bf16)
    b_mat = xbc[:, SSD_INNER:SSD_INNER + SSD_GN]
    dt = jax.nn.softplus(dt_ref[...] + dtb_ref[...])
    a = dt * (-jnp.exp(alog_ref[...]))

    x_t = jnp.concatenate([x[:, k * LANES:(k + 1) * LANES].T for k in range(SSD_INNER // LANES)], axis=0)
    b_bf = b_mat.astype(bf16)

    def load_h0(d):
        if has_h0:
            return h0_ref[d].reshape(SSD_INNER, SSD_STATE)
        return jnp.zeros((SSD_INNER, SSD_STATE), f32)

    def head_rows(v):
        return jnp.concatenate(
            [jnp.broadcast_to(v[h:h + 1, :], (SSD_HEAD_DIM, v.shape[1])) for h in range(SSD_HEADS)], axis=0)

    def state_update(cum_t, dt_t, edge):
        at_edge = jnp.broadcast_to(cum_t[:, edge:edge + 1], cum_t.shape)
        w_end = jnp.exp(at_edge - cum_t) * dt_t
        xw = (x_t * head_rows(w_end)).astype(bf16)
        upd = jnp.concatenate(
            [jnp.dot(xw[g * gw:(g + 1) * gw], b_bf[:, g * SSD_STATE:(g + 1) * SSD_STATE], preferred_element_type=f32)
             for g in range(SSD_GROUPS)], axis=0)
        carry[...] = carry[...] * head_rows(jnp.exp(at_edge)) + upd

    @pl.when(phase == 0)
    def _backward_states():
        @pl.when(c == 0)
        def _():
            carry[...] = load_h0(1)

        hb_store[nc - 1 - c] = carry[...].astype(bf16)
        rcum_t = _cumsum_rows(a, reverse=True).T
        state_update(rcum_t[SSD_HEADS:2 * SSD_HEADS], dt.T[SSD_HEADS:2 * SSD_HEADS], 0)

        if emit_state:
            @pl.when(c == nc - 1)
            def _():
                hfin_ref[1] = carry[...].reshape(SSD_HEADS, SSD_HEAD_DIM, SSD_STATE)

    @pl.when(phase == 1)
    def _forward_and_outputs():
        @pl.when(c == 0)
        def _():
            carry[...] = load_h0(0)

        c_mat = xbc[:, SSD_INNER + SSD_GN:].astype(bf16)
        cum = _cumsum_rows(a)
        rcum = _cumsum_rows(a, reverse=True)
        cum_t, rcum_t, dt_t = cum.T, rcum.T, dt.T
        row = lax.broadcasted_iota(jnp.int32, (CHUNK, CHUNK), 0)
        col = lax.broadcasted_iota(jnp.int32, (CHUNK, CHUNK), 1)
        causal = col <= row
        anti = col >= row
        lane = lax.broadcasted_iota(jnp.int32, (CHUNK, LANES), 1)
        hf = carry[...].astype(bf16)
        hb = hb_store[c]
        cb = [_dot_nt(c_mat[:, g * SSD_STATE:(g + 1) * SSD_STATE], b_bf[:, g * SSD_STATE:(g + 1) * SSD_STATE])
              for g in range(SSD_GROUPS)]

        heads = range(SSD_HEADS)
        seg_f = [jnp.where(causal, cum[:, h:h + 1] - cum_t[h:h + 1, :], -jnp.inf) for h in heads]
        seg_b = [jnp.where(anti, rcum[:, SSD_HEADS + h:SSD_HEADS + h + 1]
                           - rcum_t[SSD_HEADS + h:SSD_HEADS + h + 1, :], -jnp.inf) for h in heads]
        e_f = [jnp.exp(s) for s in seg_f]
        e_b = [jnp.exp(s) for s in seg_b]
        ws = [(cb[h // SSD_RPG] * (e_f[h] * dt_t[h:h + 1, :] + e_b[h] * dt_t[SSD_HEADS + h:SSD_HEADS + h + 1, :])
               ).astype(bf16) for h in heads]
        rhs = []
        for p in range(SSD_HEADS // 2):
            xp = x_bf[:, p * LANES:(p + 1) * LANES]
            zero = jnp.zeros_like(xp)
            rhs.append(jnp.concatenate([jnp.where(lane < SSD_HEAD_DIM, xp, zero),
                                        jnp.where(lane >= SSD_HEAD_DIM, xp, zero)], axis=0))
        y = jnp.concatenate(
            [jnp.dot(jnp.concatenate([ws[2 * p], ws[2 * p + 1]], axis=1), rhs[p], preferred_element_type=f32)
             for p in range(SSD_HEADS // 2)], axis=1)

        def inter_t(h_all):
            return jnp.concatenate(
                [_dot_nt(h_all[g * gw:(g + 1) * gw], c_mat[:, g * SSD_STATE:(g + 1) * SSD_STATE])
                 for g in range(SSD_GROUPS)], axis=0)

        cum_f, cum_b = cum_t[:SSD_HEADS], rcum_t[SSD_HEADS:2 * SSD_HEADS]
        y_t = inter_t(hf) * head_rows(jnp.exp(cum_f)) + inter_t(hb) * head_rows(jnp.exp(cum_b))
        y_inter = jnp.concatenate([y_t[k * LANES:(k + 1) * LANES].T for k in range(SSD_INNER // LANES)], axis=1)
        state_update(cum_f, dt_t[:SSD_HEADS], CHUNK - 1)
        y = y + y_inter + dskip_ref[...] * x

        zv = z_ref[...]
        yz = y * (zv * jax.nn.sigmoid(zv))
        y_ref[...] = (_rms(yz) * g_ref[...]).astype(y_ref.dtype)

        if emit_state:
            @pl.when(c == nc - 1)
            def _():
                hfin_ref[0] = carry[...].reshape(SSD_HEADS, SSD_HEAD_DIM, SSD_STATE)


def ssd_mix(xbc, dt, z, h0, row0, n_seq, seq_len, dt_bias, a_log, d_skip, norm_g, emit_state, out_buf=None,
            layer=0, state_buf=None):
    nc = seq_len // CHUNK
    blk0 = row0 // CHUNK
    has_h0 = h0 is not None

    def chunk_map(s, p, c):
        return (blk0 + s * nc + jnp.where(p == 0, nc - 1 - c, c), 0)

    state_spec = pl.BlockSpec((None, 2, SSD_HEADS, SSD_HEAD_DIM, SSD_STATE), lambda s, p, c: (s, 0, 0, 0, 0))
    vec = lambda n: pl.BlockSpec((1, n), lambda s, p, c: (0, 0))
    in_specs = [
        pl.BlockSpec((CHUNK, SSD_CONV_DIM), chunk_map),
        pl.BlockSpec((CHUNK, LANES), chunk_map),
        pl.BlockSpec((CHUNK, SSD_INNER), lambda s, p, c: (blk0 + s * nc + p * c, 0)),
    ]
    args = [xbc, dt, z]
    if has_h0:
        in_specs.append(state_spec)
        args.append(h0)
    in_specs += [vec(LANES), vec(LANES), vec(SSD_INNER), vec(SSD_INNER)]
    pad = lambda t: jnp.concatenate([t.reshape(1, -1), jnp.zeros((1, LANES - t.size), f32)], axis=1)
    args += [pad(dt_bias), pad(a_log), jnp.repeat(d_skip, SSD_HEAD_DIM).reshape(1, SSD_INNER),
             norm_g.reshape(1, SSD_INNER)]
    aliases = {}
    if out_buf is not None:
        aliases = {len(args): 0}
        in_specs.append(pl.BlockSpec(memory_space=pl.ANY))
        args.append(out_buf)
    out_shape = [jax.ShapeDtypeStruct((N_TOK, SSD_INNER), bf16)]
    out_specs = [pl.BlockSpec((CHUNK, SSD_INNER), lambda s, p, c: (blk0 + s * nc + p * c, 0))]
    if emit_state:
        if state_buf is not None:
            aliases[len(args)] = 1
            in_specs.append(pl.BlockSpec(memory_space=pl.ANY))
            args.append(state_buf)
        out_shape.append(jax.ShapeDtypeStruct((n_seq, N_EVEN, 2, SSD_HEADS, SSD_HEAD_DIM, SSD_STATE), f32))
        out_specs.append(pl.BlockSpec((None, None, 2, SSD_HEADS, SSD_HEAD_DIM, SSD_STATE),
                                      lambda s, p, c: (s, layer, 0, 0, 0, 0)))
    res = pl.pallas_call(
        functools.partial(_ssd_kernel, nc, has_h0, len(aliases), emit_state),
        input_output_aliases=aliases,
        grid=(n_seq, 2, nc),
        in_specs=in_specs,
        out_specs=out_specs,
        out_shape=out_shape,
        scratch_shapes=[pltpu.VMEM((nc, SSD_INNER, SSD_STATE), bf16), pltpu.VMEM((SSD_INNER, SSD_STATE), f32)],
        compiler_params=pltpu.CompilerParams(
            dimension_semantics=("arbitrary", "arbitrary", "arbitrary"), vmem_limit_bytes=VMEM_LIMIT),
        name="ssd_scan",
    )(*args)
    return res if emit_state else res[0]


ML_DIRS = 2 * ML_HEADS


def _cummax_rows(a, reverse=False):
    row = lax.broadcasted_iota(jnp.int32, a.shape, 0)
    s = 1
    while s < CHUNK:
        if reverse:
            a = jnp.maximum(a, jnp.where(row < CHUNK - s, pltpu.roll(a, CHUNK - s, axis=0), -jnp.inf))
        else:
            a = jnp.maximum(a, jnp.where(row >= s, pltpu.roll(a, s, axis=0), -jnp.inf))
        s *= 2
    return a


ML_ST = ML_V_DIM + 16


def _mlstm_t_kernel(nc, has_state, n_buf, emit_state, *refs):
    qk_ref, v_ref, og_ref, gates_ref, gb_ref = refs[:5]
    refs = refs[5:]
    if has_state:
        s0_ref, m0_ref = refs[:2]
        refs = refs[2:]
    refs = refs[n_buf:]
    y_ref, refs = refs[0], refs[1:]
    if emit_state:
        cfin_ref, nfin_ref, mfin_ref = refs[:3]
        refs = refs[3:]
    s_store, m_store, s_carry, m_carry = refs

    phase = pl.program_id(1)
    c = pl.program_id(2)

    lane = lax.broadcasted_iota(jnp.int32, (CHUNK, LANES), 1)
    fwd_lane = lane < ML_HEADS
    fwd_row = lax.broadcasted_iota(jnp.int32, (1, LANES), 1) < ML_HEADS
    g = gates_ref[...] + gb_ref[...]
    li = g
    lf = pltpu.roll(jax.nn.log_sigmoid(g), LANES - ML_DIRS, axis=1)
    cum = jnp.where(fwd_lane, _cumsum_rows(lf), _cumsum_rows(lf, reverse=True))
    r = li - cum
    pm = jnp.where(fwd_lane, _cummax_rows(r), _cummax_rows(r, reverse=True))

    qk = qk_ref[...]
    k = qk[:, ML_QK_WIDTH:]
    own_half = [(lane < ML_QK_DIM) if h % 2 == 0 else (lane >= ML_QK_DIM) for h in range(ML_HEADS)]

    k_own = [jnp.where(own_half[h], k[:, (h // 2) * LANES:(h // 2 + 1) * LANES], 0.0).astype(bf16)
             for h in range(ML_HEADS)]
    v_tr = [v_ref[:, h * ML_V_DIM:(h + 1) * ML_V_DIM].astype(f32).T for h in range(ML_HEADS)]

    def state_update(m_row, big_m, d):
        edge = CHUNK - 1 if d == 0 else 0
        m_edge = big_m[edge:edge + 1, :]
        wc_row = jnp.exp(m_row - m_edge)
        wk_t = jnp.exp(r.T - big_m.T[:, edge:edge + 1])
        lhs = []
        for h in range(ML_HEADS):
            wk = wk_t[d * ML_HEADS + h:d * ML_HEADS + h + 1, :]
            lhs.append(jnp.concatenate([v_tr[h] * wk, jnp.broadcast_to(wk, (ML_ST - ML_V_DIM, CHUNK))],
                                       axis=0).astype(bf16))
        upd = [jnp.dot(lhs[h], k_own[h], preferred_element_type=f32) for h in range(ML_HEADS)]
        for h in range(ML_HEADS):
            cl = d * ML_HEADS + h
            wc = jnp.broadcast_to(wc_row[:, cl:cl + 1], (ML_ST, LANES))
            s_carry[d, h] = wc * s_carry[d, h] + upd[h]
        return cum[edge:edge + 1, :] + m_edge

    def init_state(d):
        if has_state:
            s_carry[d] = s0_ref[d]
        else:
            s_carry[d] = jnp.zeros((ML_HEADS, ML_ST, LANES), f32)

    def emit_final(d):
        for h in range(ML_HEADS):
            tile = s_carry[d, h]
            if h % 2 == 1:
                tile = pltpu.roll(tile, ML_QK_DIM, axis=1)
            cfin_ref[d, h] = tile[:ML_V_DIM, :ML_QK_DIM]
            nfin_ref[d, h:h + 1, :] = tile[ML_V_DIM:ML_V_DIM + 1, :ML_QK_DIM]

    @pl.when(jnp.logical_and(phase == 0, c == 0))
    def _():
        init_state(1)
        m_carry[...] = jnp.broadcast_to(m0_ref[...], m_carry.shape) if has_state else jnp.zeros(m_carry.shape, f32)

    @pl.when(phase == 0)
    def _backward_states():
        j = nc - 1 - c
        m_row = m_carry[0:1, :]
        s_store[j] = s_carry[1].astype(bf16)
        m_store[j] = m_carry[...]
        big_m = jnp.maximum(m_row, pm)
        m_new = state_update(m_row, big_m, 1)
        m_carry[...] = jnp.broadcast_to(jnp.where(fwd_row, m_row, m_new), m_carry.shape)

    @pl.when(phase == 1)
    def _forward_and_outputs():
        @pl.when(c == 0)
        def _():
            init_state(0)

        if emit_state:
            @pl.when(c == 0)
            def _():
                emit_final(1)

        m_both = m_carry[0:1, :]
        m_row = jnp.where(fwd_row, m_both, m_store[c][0:1, :])
        big_m = jnp.maximum(m_row, pm)
        big_m_t = big_m.T
        w_inter_t = jnp.exp(m_row - big_m).T
        floor_t = jnp.exp(-(cum + big_m)).T
        key = lax.broadcasted_iota(jnp.int32, (CHUNK, CHUNK), 0)
        qry = lax.broadcasted_iota(jnp.int32, (CHUNK, CHUNK), 1)
        masks = (key <= qry, key >= qry)
        q = (qk[:, :ML_QK_WIDTH] * (ML_QK_DIM ** -0.5)).astype(bf16)
        heads = range(ML_HEADS)
        hd_pairs = [(h, d) for h in heads for d in range(2)]
        q_pair = [q[:, (h // 2) * LANES:(h // 2 + 1) * LANES] for h in heads]
        v_th = [v_tr[h].astype(bf16) for h in heads]
        s_raw_t = [_dot_nt(k_own[h], q_pair[h]) for h in heads]
        inter = {(h, d): _dot_nt(s_carry[0, h].astype(bf16) if d == 0 else s_store[c, h], q_pair[h])
                 for h, d in hd_pairs}
        w_t = {(h, d): jnp.exp(jnp.where(masks[d], r[:, d * ML_HEADS + h:d * ML_HEADS + h + 1]
                                         - big_m_t[d * ML_HEADS + h:d * ML_HEADS + h + 1, :], -jnp.inf))
               for h, d in hd_pairs}
        sw_t = {hd: s_raw_t[hd[0]] * w_t[hd] for hd in hd_pairs}
        num = {hd: jnp.dot(v_th[hd[0]], sw_t[hd].astype(bf16), preferred_element_type=f32) for hd in hd_pairs}
        wi = {(h, d): w_inter_t[d * ML_HEADS + h:d * ML_HEADS + h + 1, :] for h, d in hd_pairs}
        den = {hd: jnp.sum(sw_t[hd], axis=0, keepdims=True) + wi[hd] * inter[hd][ML_V_DIM:ML_V_DIM + 1]
               for hd in hd_pairs}
        inv = {(h, d): 1.0 / jnp.maximum(jnp.abs(den[h, d]), floor_t[d * ML_HEADS + h:d * ML_HEADS + h + 1, :])
               for h, d in hd_pairs}
        part = {hd: (num[hd] + wi[hd] * inter[hd][:ML_V_DIM]) * inv[hd] for hd in hd_pairs}
        out = [(part[h, 0] + part[h, 1]).T for h in heads]
        gate = [jax.nn.sigmoid(og_ref[:, h * ML_V_DIM:(h + 1) * ML_V_DIM]) for h in heads]
        for h in heads:
            y_ref[:, h * ML_V_DIM:(h + 1) * ML_V_DIM] = (out[h] * gate[h]).astype(y_ref.dtype)

        m_fin = jnp.where(fwd_row, state_update(m_row, big_m, 0), m_both)
        m_carry[...] = jnp.broadcast_to(m_fin, m_carry.shape)

        if emit_state:
            @pl.when(c == nc - 1)
            def _():
                emit_final(0)
                mfin_ref[...] = m_fin


def mlstm_mix(qk, v, og, gates, gate_b, state, row0, n_seq, seq_len, emit_state, out_buf=None,
              layer=0, state_bufs=None):
    nc = seq_len // CHUNK
    blk0 = row0 // CHUNK
    has_state = state is not None

    def chunk_map(s, p, c):
        return (blk0 + s * nc + jnp.where(p == 0, nc - 1 - c, c), 0)

    m_spec = pl.BlockSpec((None, 1, LANES), lambda s, p, c: (s, 0, 0))
    in_specs = [
        pl.BlockSpec((CHUNK, 2 * ML_QK_WIDTH), chunk_map),
        pl.BlockSpec((CHUNK, ML_V_WIDTH), chunk_map),
        pl.BlockSpec((CHUNK, ML_V_WIDTH), lambda s, p, c: (blk0 + s * nc + p * c, 0)),
        pl.BlockSpec((CHUNK, LANES), chunk_map),
        pl.BlockSpec((1, LANES), lambda s, p, c: (0, 0)),
    ]
    gb = jnp.concatenate([gate_b.reshape(1, 2 * ML_DIRS), jnp.zeros((1, LANES - 2 * ML_DIRS), f32)], axis=1)
    args = [qk, v, og, gates, gb]
    if has_state:
        c0, n0, m0 = state
        rows = jnp.concatenate([c0, jnp.broadcast_to(n0[..., None, :], n0.shape[:-1] + (ML_ST - ML_V_DIM, ML_QK_DIM))],
                               axis=-2)
        zeros = jnp.zeros_like(rows)
        odd = (jnp.arange(ML_HEADS) % 2 == 1)[:, None, None]
        s0 = jnp.where(odd, jnp.concatenate([zeros, rows], axis=-1), jnp.concatenate([rows, zeros], axis=-1))
        m0 = jnp.concatenate([m0.reshape(n_seq, 1, ML_DIRS), jnp.zeros((n_seq, 1, LANES - ML_DIRS), f32)], axis=-1)
        in_specs += [pl.BlockSpec((None, 2, ML_HEADS, ML_ST, LANES), lambda s, p, c: (s, 0, 0, 0, 0)), m_spec]
        args += [s0, m0]
    aliases = {}
    if out_buf is not None:
        aliases[len(args)] = 0
        in_specs.append(pl.BlockSpec(memory_space=pl.ANY))
        args.append(out_buf)
    out_shape = [jax.ShapeDtypeStruct((N_TOK, ML_V_WIDTH), bf16)]
    out_specs = [pl.BlockSpec((CHUNK, ML_V_WIDTH), lambda s, p, c: (blk0 + s * nc + p * c, 0))]
    if emit_state:
        if state_bufs is not None:
            for k_out, buf in enumerate(state_bufs):
                aliases[len(args)] = 1 + k_out
                in_specs.append(pl.BlockSpec(memory_space=pl.ANY))
                args.append(buf)
        out_shape += [jax.ShapeDtypeStruct((n_seq, N_ODD, 2, ML_HEADS, ML_V_DIM, ML_QK_DIM), f32),
                      jax.ShapeDtypeStruct((n_seq, N_ODD, 2, ML_HEADS, ML_QK_DIM), f32),
                      jax.ShapeDtypeStruct((n_seq, 1, LANES), f32)]
        out_specs += [pl.BlockSpec((None, None, 2, ML_HEADS, ML_V_DIM, ML_QK_DIM),
                                   lambda s, p, c: (s, layer, 0, 0, 0, 0)),
                      pl.BlockSpec((None, None, 2, ML_HEADS, ML_QK_DIM), lambda s, p, c: (s, layer, 0, 0, 0)),
                      m_spec]
    res = pl.pallas_call(
        functools.partial(_mlstm_t_kernel, nc, has_state, len(aliases), emit_state),
        input_output_aliases=aliases,
        grid=(n_seq, 2, nc),
        in_specs=in_specs,
        out_specs=out_specs,
        out_shape=out_shape,
        scratch_shapes=[pltpu.VMEM((nc, ML_HEADS, ML_ST, LANES), bf16), pltpu.VMEM((nc, SUBLANES, LANES), f32),
                        pltpu.VMEM((2, ML_HEADS, ML_ST, LANES), f32), pltpu.VMEM((SUBLANES, LANES), f32)],
        compiler_params=pltpu.CompilerParams(
            dimension_semantics=("arbitrary", "arbitrary", "arbitrary"), vmem_limit_bytes=VMEM_LIMIT),
        name="mlstm_scan",
    )(*args)
    if not emit_state:
        return res[0]
    y, c_fin, n_fin, mfin = res
    return y, c_fin, n_fin, mfin[:, 0, :ML_DIRS].reshape(n_seq, 2, ML_HEADS)


def kernel(x_prompt, x_sample, c, cache_na_k, cache_na_v, state_ssd, state_mlstm_c, state_mlstm_n, state_mlstm_m,
           c_ctx, w_mod, b_mod, norm_mix, norm_ffn, w_in_even, w_out_even, na_rpb, ssd_conv_w, ssd_conv_b,
           ssd_dt_bias, ssd_a_log, ssd_d, ssd_norm, w_in_odd, w_out_odd, ml_conv_w, ml_conv_b, ml_gate_b,
           w_ff1, w_ff2, norm_f):
    xs = [x_prompt.reshape(N_PROMPT, D_MODEL), x_sample.reshape(DEC_BATCH * DEC_SEQ, D_MODEL)]
    cond = jnp.concatenate([c_ctx[None, :], c, jnp.zeros((SUBLANES - N_COND, D_MODEL), f32)], axis=0)
    mod = adaln_all(cond, w_mod, b_mod)[:, :N_COND].reshape(DEPTH, N_COND, 1, N_MOD * D_MODEL)

    even_main = 3 * NA_WIDTH + SSD_INNER + SSD_CONV_DIM
    odd_main = 2 * ML_QK_WIDTH + 2 * ML_V_WIDTH

    def tail_bf16(w, main):
        t = w[:, :, main:]
        return jnp.concatenate([t, jnp.zeros(t.shape[:2] + (LANES - t.shape[2],), f32)], axis=2).astype(bf16)

    wi_even, wt_even = cast_bf16(w_in_even, even_main), tail_bf16(w_in_even, even_main)
    wi_odd, wt_odd = cast_bf16(w_in_odd, odd_main), tail_bf16(w_in_odd, odd_main)
    wo_even, wo_odd = cast_bf16(w_out_even), cast_bf16(w_out_odd)
    w1_all, w2_all = cast_bf16(w_ff1), cast_bf16(w_ff2)

    out_m = []
    new_k = new_v = new_ssd = new_c = new_n = None
    for l in range(DEPTH):
        norm_last = norm_f if l == DEPTH - 1 else None
        if l % 2 == 0:
            e = l // 2
            o0 = 3 * NA_WIDTH
            segs = ((0, NA_WIDTH, NA_HEAD_DIM ** -0.5, False),
                    (NA_WIDTH, 2 * NA_WIDTH, 1.0, False),
                    (2 * NA_WIDTH, 3 * NA_WIDTH, 1.0, False),
                    (o0, o0 + SSD_INNER, 1.0, False),
                    (o0 + SSD_INNER, o0 + SSD_INNER + SSD_CONV_DIM, 1.0, True),
                    (None, None, 1.0, False))
            outs = ((0, bf16, False), (1, bf16, False), (2, bf16, False), (1, f32, True), (2, f32, True),
                    (3, f32, False), (4, f32, False), (5, f32, False))
            q, k, v, new_k, new_v, z, xbc, dt = in_proj(
                xs, mod[l], norm_mix[l], wi_even, wt_even, e, segs, outs, ssd_conv_w[e], ssd_conv_b[e],
                cache_bufs=None if new_k is None else (new_k, new_v))
            ssd_w = (ssd_dt_bias[e], ssd_a_log[e], ssd_d[e], ssd_norm[e])
            y_ssd, new_ssd = ssd_mix(xbc, dt, z, None, 0, BATCH, SEQ, *ssd_w, True, layer=e, state_buf=new_ssd)
            y_ssd = ssd_mix(xbc, dt, z, state_ssd[:, e], N_PROMPT, DEC_BATCH, DEC_SEQ, *ssd_w, False, out_buf=y_ssd)
            tokens = lambda t: jnp.swapaxes(t, 1, 2).reshape(DEC_BATCH, PAST_LEN, NA_WIDTH).astype(bf16)
            y_na = ctx_attention(q, k, v)
            y_na = na_latent(q, k, v, tokens(cache_na_k[:, e]), tokens(cache_na_v[:, e]),
                             na_bias_table(na_rpb[e]), y_na)
            res = out_mlp(xs, mod[l], norm_ffn[l], [y_na, y_ssd], wo_even, e, w1_all, w2_all, l, norm_last)
        else:
            o = l // 2
            a0 = 2 * ML_QK_WIDTH
            segs = ((0, a0, 1.0, True),
                    (a0, a0 + ML_V_WIDTH, 1.0, False),
                    (a0 + ML_V_WIDTH, a0 + 2 * ML_V_WIDTH, 1.0, False),
                    (None, None, 1.0, False))
            outs = ((0, f32, False), (1, bf16, False), (2, f32, False), (3, f32, False))
            qk, v, og, gates = in_proj(xs, mod[l], norm_mix[l], wi_odd, wt_odd, o, segs, outs,
                                       ml_conv_w[o], ml_conv_b[o])
            y_ml, new_c, new_n, m_fin = mlstm_mix(qk, v, og, gates, ml_gate_b[o], None, 0, BATCH, SEQ, True,
                                                  layer=o, state_bufs=None if new_c is None else (new_c, new_n))
            out_m.append(m_fin)
            state = (state_mlstm_c[:, o], state_mlstm_n[:, o], state_mlstm_m[:, o])
            y_ml = mlstm_mix(qk, v, og, gates, ml_gate_b[o], state, N_PROMPT, DEC_BATCH, DEC_SEQ, False, out_buf=y_ml)
            res = out_mlp(xs, mod[l], norm_ffn[l], [y_ml], wo_odd, o, w1_all, w2_all, l, norm_last)
        xs = list(res) if norm_last is not None else [res]

    y_prompt = xs[0].reshape(BATCH, SEQ, D_MODEL)
    y_sample = xs[1].reshape(DEC_BATCH, DEC_SEQ, D_MODEL)
    return (y_prompt, y_sample, new_k, new_v, new_ssd,
            new_c, new_n, jnp.stack(out_m, axis=1))
```

```python
import functools
import math

import jax
import jax.numpy as jnp
from jax import lax
from jax.experimental import pallas as pl
from jax.experimental.pallas import tpu as pltpu

D_MODEL = 1024
BATCH = 32
SEQ = 256
DEPTH = 4
DEC_BATCH = 2
DEC_SEQ = 4096
PAST_LEN = 256
GRID_W = 64
N_EVEN = (DEPTH + 1) // 2
N_ODD = DEPTH // 2
RMS_EPS = 1e-6
N_MOD = 6
D_FF = 4 * D_MODEL
CONV_K = 3
Q_BLOCK = 128
NA_HEADS = 8
NA_HEAD_DIM = 64
NA_WIDTH = NA_HEADS * NA_HEAD_DIM
NA_WIN_ROWS = 8
NA_WIN_COLS = 16
NA_RPB_ROWS = 2 * NA_WIN_ROWS - 1
NA_RPB_COLS = 2 * NA_WIN_COLS - 1
SSD_INNER = D_MODEL
SSD_HEAD_DIM = 64
SSD_HEADS = SSD_INNER // SSD_HEAD_DIM
SSD_GROUPS = 2
SSD_RPG = SSD_HEADS // SSD_GROUPS
SSD_STATE = 128
SSD_GN = SSD_GROUPS * SSD_STATE
SSD_CONV_DIM = SSD_INNER + 2 * SSD_GN
SSD_CHUNK = 128
ML_HEADS = 8
ML_QK_DIM = D_MODEL // 16
ML_V_DIM = D_MODEL // 8
ML_QK_WIDTH = ML_HEADS * ML_QK_DIM
ML_V_WIDTH = ML_HEADS * ML_V_DIM
ML_CHUNK = 64
EVEN_MIX = NA_WIDTH + SSD_INNER

N_PROMPT = BATCH * SEQ
N_TOK = N_PROMPT + DEC_BATCH * DEC_SEQ
N_COND = 1 + DEC_BATCH
LANES = 128
SUBLANES = 8
VMEM_LIMIT = 56 * 1024 * 1024
TM = 512

f32 = jnp.float32
bf16 = jnp.bfloat16


def _cond_row(i, tm):
    start = i * tm
    return jnp.where(start < N_PROMPT, 0, (start - N_PROMPT) // DEC_SEQ + 1)


def _const_spec(shape):
    nd = len(shape)
    return pl.BlockSpec(shape, lambda i: (0,) * nd, pipeline_mode=pl.Buffered(1))


def _rms(x):
    return x * lax.rsqrt(jnp.mean(x * x, axis=-1, keepdims=True) + RMS_EPS)


def _modulated(x, g, mod, k):
    shift = mod[:, k * D_MODEL:(k + 1) * D_MODEL]
    scale = mod[:, (k + 1) * D_MODEL:(k + 2) * D_MODEL]
    return (_rms(x) * g) * (1.0 + scale) + shift


def _mod_kernel(c_ref, w_ref, b_ref, o_ref):
    c = c_ref[...]
    a = (c * jax.nn.sigmoid(c)).astype(bf16)
    o_ref[...] = jnp.dot(a, w_ref[...].astype(bf16), preferred_element_type=f32) + b_ref[...]


def adaln_all(cond, w_mod, b_mod):
    tn = 1536
    nj = N_MOD * D_MODEL // tn
    return pl.pallas_call(
        _mod_kernel,
        grid=(DEPTH, nj),
        in_specs=[
            pl.BlockSpec((SUBLANES, D_MODEL), lambda l, j: (0, 0)),
            pl.BlockSpec((None, D_MODEL, tn), lambda l, j: (l, 0, j)),
            pl.BlockSpec((None, 1, tn), lambda l, j: (l, 0, j)),
        ],
        out_specs=pl.BlockSpec((None, SUBLANES, tn), lambda l, j: (l, 0, j)),
        out_shape=jax.ShapeDtypeStruct((DEPTH, SUBLANES, N_MOD * D_MODEL), f32),
        compiler_params=pltpu.CompilerParams(
            dimension_semantics=("arbitrary", "arbitrary"), vmem_limit_bytes=VMEM_LIMIT),
        name="adaln",
    )(cond, w_mod, b_mod.reshape(DEPTH, 1, N_MOD * D_MODEL))


def _cast_kernel(x_ref, o_ref):
    o_ref[...] = x_ref[...].astype(o_ref.dtype)


def cast_bf16(w, cols=None):
    n_l, k, n = w.shape
    cols = n if cols is None else cols
    bk = 512
    spec = pl.BlockSpec((None, bk, cols), lambda l, i: (l, i, 0))
    return pl.pallas_call(
        _cast_kernel,
        grid=(n_l, k // bk),
        in_specs=[spec],
        out_specs=spec,
        out_shape=jax.ShapeDtypeStruct((n_l, k, cols), bf16),
        compiler_params=pltpu.CompilerParams(
            dimension_semantics=("arbitrary", "arbitrary"), vmem_limit_bytes=VMEM_LIMIT),
        name="cast_bf16",
    )(w)


def _layer_spec(w, l):
    nd = w.ndim - 1
    return pl.BlockSpec((None,) + w.shape[1:], lambda i: (l,) + (0,) * nd, pipeline_mode=pl.Buffered(1))


def _tile_x(x_refs, tm):
    if len(x_refs) == 1:
        return x_refs[0][...]
    return jnp.where(pl.program_id(0) < N_PROMPT // tm, x_refs[0][...], x_refs[1][...])


def _conv_silu_tile(y, prev_row, next_row, w, b):
    i = pl.program_id(0)
    rows = y.shape[0]
    g = i * rows + lax.broadcasted_iota(jnp.int32, (rows, 1), 0)
    is_prompt = i < N_PROMPT // rows
    pos = jnp.where(is_prompt, g & (SEQ - 1), (g - N_PROMPT) & (DEC_SEQ - 1))
    seq_last = jnp.where(is_prompt, SEQ - 1, DEC_SEQ - 1)
    row = lax.broadcasted_iota(jnp.int32, y.shape, 0)
    up = jnp.where(row == 0, prev_row, pltpu.roll(y, 1, axis=0))
    dn = jnp.where(row == rows - 1, next_row, pltpu.roll(y, rows - 1, axis=0))
    up = jnp.where(pos == 0, 0.0, up)
    dn = jnp.where(pos == seq_last, 0.0, dn)
    c = up * w[0:1, :] + y * w[1:2, :] + dn * w[2:3, :] + b
    return c * jax.nn.sigmoid(c)


def _in_proj_kernel(n_x, n_buf, segs, outs, *refs):
    x_refs = refs[:n_x]
    prev_ref, next_ref, mod_ref, g_ref, w_ref, wt_ref, cw_ref, cb_ref = refs[n_x:n_x + 8]
    o_refs = refs[n_x + 8 + n_buf:]
    mod, g = mod_ref[...], g_ref[...]
    h = _modulated(_tile_x(x_refs, TM), g, mod, 0).astype(bf16)
    ys = []
    for a, b, scale, conv in segs:
        w = wt_ref[...] if a is None else w_ref[:, a:b]
        if conv:
            halo = jnp.concatenate([prev_ref[...], next_ref[...]], axis=0)
            hh = _modulated(halo, g, mod, 0).astype(bf16)
            y = jnp.dot(jnp.concatenate([h, hh], axis=0), w, preferred_element_type=f32)
            y = _conv_silu_tile(y[:TM], y[TM + SUBLANES - 1:TM + SUBLANES], y[TM + SUBLANES:TM + SUBLANES + 1],
                                cw_ref[...], cb_ref[...])
        else:
            y = jnp.dot(h, w, preferred_element_type=f32)
        ys.append(y if scale == 1.0 else y * scale)
    for (si, _, head_major), o_ref in zip(outs, o_refs):
        if head_major:
            @pl.when(pl.program_id(0) < N_PROMPT // TM)
            def _(o_ref=o_ref, si=si):
                for b in range(TM // SEQ):
                    for hd in range(NA_HEADS):
                        o_ref[b, hd] = ys[si][b * SEQ:(b + 1) * SEQ, hd * NA_HEAD_DIM:(hd + 1) * NA_HEAD_DIM]
        else:
            o_ref[...] = ys[si].astype(o_ref.dtype)


def _x_specs(xs, tm):
    if len(xs) == 1:
        return [pl.BlockSpec((tm, D_MODEL), lambda i: (i, 0))]
    n_p = N_PROMPT // tm
    return [pl.BlockSpec((tm, D_MODEL), lambda i: (jnp.minimum(i, n_p - 1), 0)),
            pl.BlockSpec((tm, D_MODEL), lambda i: (jnp.maximum(i - n_p, 0), 0))]


def in_proj(xs, mod_l, g, w, w_tail, l, segs, outs, conv_w, conv_b, cache_bufs=None):
    grid = (N_TOK // TM,)
    last_prompt = N_PROMPT // TM - 1
    halo_src = xs[-1]
    per = TM // SUBLANES
    blk0 = (halo_src.shape[0] - DEC_BATCH * DEC_SEQ) // SUBLANES
    n_blk = halo_src.shape[0] // SUBLANES
    tile0 = N_PROMPT // TM

    def prev_map(i):
        return (jnp.clip(blk0 + (i - tile0) * per - 1, 0, n_blk - 1), 0)

    def next_map(i):
        return (jnp.clip(blk0 + (i - tile0 + 1) * per, 0, n_blk - 1), 0)

    ch = conv_w.shape[1]
    out_shape, out_specs = [], []
    for si, dt, head_major in outs:
        width = LANES if segs[si][0] is None else segs[si][1] - segs[si][0]
        if head_major:
            out_shape.append(jax.ShapeDtypeStruct((BATCH, N_EVEN, NA_HEADS, SEQ, NA_HEAD_DIM), dt))
            out_specs.append(pl.BlockSpec((TM // SEQ, None, NA_HEADS, SEQ, NA_HEAD_DIM),
                                          lambda i: (jnp.minimum(i, last_prompt), l, 0, 0, 0)))
        else:
            out_shape.append(jax.ShapeDtypeStruct((N_TOK, width), dt))
            out_specs.append(pl.BlockSpec((TM, width), lambda i: (i, 0)))
    args = [*xs, halo_src, halo_src, mod_l, g.reshape(1, D_MODEL), w, w_tail, conv_w, conv_b.reshape(1, ch)]
    aliases = {}
    if cache_bufs is not None:
        head_major_outs = [k for k, o in enumerate(outs) if o[2]]
        for k_out, buf in zip(head_major_outs, cache_bufs):
            aliases[len(args)] = k_out
            args.append(buf)
    return pl.pallas_call(
        functools.partial(_in_proj_kernel, len(xs), len(aliases), segs, outs),
        input_output_aliases=aliases,
        grid=grid,
        in_specs=_x_specs(xs, TM) + [
            pl.BlockSpec((SUBLANES, D_MODEL), prev_map),
            pl.BlockSpec((SUBLANES, D_MODEL), next_map),
            pl.BlockSpec((None, 1, N_MOD * D_MODEL), lambda i: (_cond_row(i, TM), 0, 0)),
            _const_spec((1, D_MODEL)),
            _layer_spec(w, l),
            _layer_spec(w_tail, l),
            _const_spec((CONV_K, ch)),
            _const_spec((1, ch)),
        ] + [pl.BlockSpec(memory_space=pl.ANY)] * len(aliases),
        out_specs=out_specs,
        out_shape=out_shape,
        compiler_params=pltpu.CompilerParams(
            dimension_semantics=("arbitrary",), vmem_limit_bytes=VMEM_LIMIT),
        name="in_proj",
    )(*args)


def _out_mlp_kernel(n_x, n_mix, final, *refs):
    x_refs = refs[:n_x]
    mod_ref, g_ref = refs[n_x:n_x + 2]
    refs = refs[n_x + 2:]
    mix_refs = refs[:n_mix]
    wo_ref, w1_ref, w2_ref = refs[n_mix:n_mix + 3]
    rest = refs[n_mix + 3:]
    if final:
        gf_ref, op_ref, os_ref = rest
    else:
        (o_ref,) = rest
    mod = mod_ref[...]
    m = None
    k0 = 0
    for r in mix_refs:
        kw = r.shape[-1]
        part = jnp.dot(r[...].astype(bf16), wo_ref[k0:k0 + kw, :], preferred_element_type=f32)
        m = part if m is None else m + part
        k0 += kw
    x1 = _tile_x(x_refs, OUT_TM) + mod[:, 2 * D_MODEL:3 * D_MODEL] * m
    h2 = _modulated(x1, g_ref[...], mod, 3).astype(bf16)
    u = jnp.dot(h2, w1_ref[...], preferred_element_type=f32)
    a = jnp.square(jnp.maximum(u, 0.0)).astype(bf16)
    x2 = x1 + mod[:, 5 * D_MODEL:6 * D_MODEL] * jnp.dot(a, w2_ref[...], preferred_element_type=f32)
    if not final:
        o_ref[...] = x2
        return
    y = _rms(x2) * gf_ref[...]
    is_prompt = pl.program_id(0) < N_PROMPT // OUT_TM

    @pl.when(is_prompt)
    def _():
        op_ref[...] = y

    @pl.when(jnp.logical_not(is_prompt))
    def _():
        os_ref[...] = y


OUT_TM = 512


def out_mlp(xs, mod_l, g_ffn, mixes, w_out, l_out, w1, w2, l, norm_f=None):
    final = norm_f is not None
    tm = OUT_TM
    grid = (N_TOK // tm,)
    in_specs = _x_specs(xs, tm) + [
        pl.BlockSpec((None, 1, N_MOD * D_MODEL), lambda i: (_cond_row(i, tm), 0, 0)),
        _const_spec((1, D_MODEL)),
    ]
    in_specs += [pl.BlockSpec((tm, m.shape[-1]), lambda i: (i, 0)) for m in mixes]
    in_specs += [_layer_spec(w_out, l_out), _layer_spec(w1, l), _layer_spec(w2, l)]
    args = [*xs, mod_l, g_ffn.reshape(1, D_MODEL), *mixes, w_out, w1, w2]
    out_specs = pl.BlockSpec((tm, D_MODEL), lambda i: (i, 0))
    out_shape = jax.ShapeDtypeStruct((N_TOK, D_MODEL), f32)
    if final:
        in_specs.append(_const_spec((1, D_MODEL)))
        args.append(norm_f.reshape(1, D_MODEL))
        n_p = N_PROMPT // tm
        out_specs = [pl.BlockSpec((tm, D_MODEL), lambda i: (jnp.minimum(i, n_p - 1), 0)),
                     pl.BlockSpec((tm, D_MODEL), lambda i: (jnp.maximum(i - n_p, 0), 0))]
        out_shape = [jax.ShapeDtypeStruct((N_PROMPT, D_MODEL), f32),
                     jax.ShapeDtypeStruct((N_TOK - N_PROMPT, D_MODEL), f32)]
    return pl.pallas_call(
        functools.partial(_out_mlp_kernel, len(xs), len(mixes), final),
        grid=grid,
        in_specs=in_specs,
        out_specs=out_specs,
        out_shape=out_shape,
        compiler_params=pltpu.CompilerParams(
            dimension_semantics=("arbitrary",), vmem_limit_bytes=VMEM_LIMIT),
        name="out_mlp",
    )(*args)


NA_PAIRS = NA_HEADS // 2
NA_ROWS = DEC_SEQ // GRID_W
NA_WIN = NA_WIN_ROWS * GRID_W


def _dot_nt(a, b):
    return lax.dot_general(a, b, (((1,), (1,)), ((), ())), preferred_element_type=f32)


def _pair_stack(x):
    lane = lax.broadcasted_iota(jnp.int32, x.shape, 1)
    zero = jnp.zeros_like(x)
    return jnp.concatenate([jnp.where(lane < NA_HEAD_DIM, x, zero), jnp.where(lane >= NA_HEAD_DIM, x, zero)], axis=0)


def _pair_unstack(o):
    n = o.shape[0] // 2
    lane = lax.broadcasted_iota(jnp.int32, (n, LANES), 1)
    return jnp.where(lane < NA_HEAD_DIM, o[:n], o[n:])


def _softmax_pv(scores, values):
    def lane_tiles(blocks):
        return [b[:, k * LANES:(k + 1) * LANES] for b in blocks for k in range(b.shape[1] // LANES)]

    ms = [jnp.max(functools.reduce(jnp.maximum, lane_tiles(s)), axis=1, keepdims=True) for s in scores]
    ps = [[jnp.exp(b - m) for b in s] for s, m in zip(scores, ms)]
    invs = [1.0 / jnp.sum(functools.reduce(jnp.add, lane_tiles(p)), axis=1, keepdims=True) for p in ps]
    pn = [[(b * inv).astype(bf16) for b in p] for p, inv in zip(ps, invs)]
    return [functools.reduce(jnp.add, [jnp.dot(b, v, preferred_element_type=f32) for b, v in zip(p, vs)])
            for p, vs in zip(pn, values)]


def _ctx_attn_kernel(q_ref, k_ref, v_ref, o_ref):
    pairs = [slice(p * LANES, (p + 1) * LANES) for p in range(NA_PAIRS)]
    scores = [[_dot_nt(_pair_stack(q_ref[:, lanes]), k_ref[:, lanes])] for lanes in pairs]
    outs = _softmax_pv(scores, [[v_ref[:, lanes]] for lanes in pairs])
    for lanes, o in zip(pairs, outs):
        o_ref[:, lanes] = _pair_unstack(o).astype(o_ref.dtype)


def ctx_attention(q, k, v):
    spec = pl.BlockSpec((SEQ, NA_WIDTH), lambda b: (b, 0))
    return pl.pallas_call(
        _ctx_attn_kernel,
        grid=(BATCH,),
        in_specs=[spec, spec, spec],
        out_specs=spec,
        out_shape=jax.ShapeDtypeStruct((N_TOK, NA_WIDTH), bf16),
        compiler_params=pltpu.CompilerParams(dimension_semantics=("arbitrary",), vmem_limit_bytes=VMEM_LIMIT),
        name="ctx_attn",
    )(q, k, v)


def _na_bias_kernel(rpb_ref, o_ref):
    pair = pl.program_id(0)
    shape = (GRID_W, LANES)
    qc = lax.broadcasted_iota(jnp.int32, shape, 0)
    lane = lax.broadcasted_iota(jnp.int32, shape, 1)
    kc = lane & (GRID_W - 1)
    low = lane < GRID_W
    col_start = jnp.clip(qc - NA_WIN_COLS // 2, 0, GRID_W - NA_WIN_COLS)
    valid = (kc >= col_start) & (kc < col_start + NA_WIN_COLS)
    rel_c = jnp.clip(kc - qc + NA_WIN_COLS - 1, 0, NA_RPB_COLS - 1)
    for e in range(2):
        base = (2 * pair + e) * NA_RPB_ROWS
        pieces = []
        for rr in range(NA_RPB_ROWS - 1):
            val = jnp.zeros(shape, f32)
            for t in range(NA_RPB_COLS):
                s_lo = rpb_ref[(base + rr) * NA_RPB_COLS + t]
                s_hi = rpb_ref[(base + rr + 1) * NA_RPB_COLS + t]
                val = jnp.where(rel_c == t, jnp.where(low, s_lo, s_hi), val)
            pieces.append(jnp.where(valid, val, -jnp.inf))
        for d in range(NA_WIN_ROWS):
            for i in range(0, NA_WIN_ROWS, 2):
                o_ref[d, e * GRID_W:(e + 1) * GRID_W, i * GRID_W:(i + 2) * GRID_W] = pieces[d + i]


def na_bias_table(rpb):
    return pl.pallas_call(
        _na_bias_kernel,
        grid=(NA_PAIRS,),
        in_specs=[pl.BlockSpec(memory_space=pltpu.SMEM)],
        out_specs=pl.BlockSpec((None, NA_WIN_ROWS, 2 * GRID_W, NA_WIN), lambda p: (p, 0, 0, 0)),
        out_shape=jax.ShapeDtypeStruct((NA_PAIRS, NA_WIN_ROWS, 2 * GRID_W, NA_WIN), f32),
        compiler_params=pltpu.CompilerParams(dimension_semantics=("arbitrary",), vmem_limit_bytes=VMEM_LIMIT),
        name="na_bias",
    )(rpb.reshape(-1))


NA_STEP_ROWS = 4


def _na_first_key_row(r):
    return jnp.clip(r - NA_WIN_ROWS // 2, 0, NA_ROWS - NA_WIN_ROWS)


def _na_kernel(q_ref, k_ref, v_ref, kc_ref, vc_ref, bias_ref, buf_ref, o_ref):
    del buf_ref
    where, scores, values = [], [], []
    for j in range(NA_STEP_ROWS):
        r = pl.program_id(1) * NA_STEP_ROWS + j
        first = _na_first_key_row(r)
        start = pl.multiple_of(first * GRID_W, GRID_W)
        shift = first - r + NA_WIN_ROWS - 1
        rows = slice(j * GRID_W, (j + 1) * GRID_W)
        for p in range(NA_PAIRS):
            lanes = slice(p * LANES, (p + 1) * LANES)
            qq = _pair_stack(q_ref[rows, lanes])
            where.append((rows, lanes))
            scores.append([_dot_nt(qq, k_ref[pl.ds(start, NA_WIN), lanes]) + bias_ref[p, shift],
                           _dot_nt(qq, kc_ref[:, lanes])])
            values.append([v_ref[pl.ds(start, NA_WIN), lanes], vc_ref[:, lanes]])
    for (rows, lanes), o in zip(where, _softmax_pv(scores, values)):
        o_ref[rows, lanes] = _pair_unstack(o).astype(o_ref.dtype)


def na_latent(q, k, v, k_ctx, v_ctx, bias, buf):
    rows = NA_STEP_ROWS * GRID_W
    steps = NA_ROWS // NA_STEP_ROWS
    row0 = N_PROMPT // rows
    seq0 = N_PROMPT // DEC_SEQ
    kv_spec = pl.BlockSpec((DEC_SEQ, NA_WIDTH), lambda b, r: (seq0 + b, 0))
    ctx_spec = pl.BlockSpec((None, PAST_LEN, NA_WIDTH), lambda b, r: (b, 0, 0))
    return pl.pallas_call(
        _na_kernel,
        grid=(DEC_BATCH, steps),
        in_specs=[
            pl.BlockSpec((rows, NA_WIDTH), lambda b, r: (row0 + b * steps + r, 0)),
            kv_spec, kv_spec, ctx_spec, ctx_spec,
            pl.BlockSpec(bias.shape, lambda b, r: (0, 0, 0, 0), pipeline_mode=pl.Buffered(1)),
            pl.BlockSpec(memory_space=pl.ANY),
        ],
        out_specs=pl.BlockSpec((rows, NA_WIDTH), lambda b, r: (row0 + b * steps + r, 0)),
        out_shape=jax.ShapeDtypeStruct((N_TOK, NA_WIDTH), bf16),
        input_output_aliases={6: 0},
        compiler_params=pltpu.CompilerParams(
            dimension_semantics=("arbitrary", "arbitrary"), vmem_limit_bytes=VMEM_LIMIT),
        name="na_latent",
    )(q, k, v, k_ctx, v_ctx, bias, buf)


CHUNK = 128


def _cumsum_rows(a, reverse=False):
    row = lax.broadcasted_iota(jnp.int32, a.shape, 0)
    s = 1
    while s < CHUNK:
        if reverse:
            a = a + jnp.where(row < CHUNK - s, pltpu.roll(a, CHUNK - s, axis=0), 0.0)
        else:
            a = a + jnp.where(row >= s, pltpu.roll(a, s, axis=0), 0.0)
        s *= 2
    return a


def _ssd_kernel(nc, has_h0, n_buf, emit_state, *refs):
    xbc_ref, dt_ref, z_ref = refs[:3]
    refs = refs[3:]
    if has_h0:
        h0_ref, refs = refs[0], refs[1:]
    dtb_ref, alog_ref, dskip_ref, g_ref = refs[:4]
    refs = refs[4 + n_buf:]
    y_ref, refs = refs[0], refs[1:]
    if emit_state:
        hfin_ref, refs = refs[0], refs[1:]
    hb_store, xt_store, gate_store, carry = refs

    phase = pl.program_id(1)
    c = pl.program_id(2)
    gw = SSD_RPG * SSD_HEAD_DIM

    xbc = xbc_ref[...]
    x = xbc[:, :SSD_INNER]
    x_bf = x.astype(bf16)
    b_mat = xbc[:, SSD_INNER:SSD_INNER + SSD_GN]
    b_bf = b_mat.astype(bf16)

    def load_h0(d):
        if has_h0:
            return h0_ref[d].reshape(SSD_INNER, SSD_STATE)
        return jnp.zeros((SSD_INNER, SSD_STATE), f32)

    def head_rows(v):
        return jnp.concatenate(
            [jnp.broadcast_to(v[h:h + 1, :], (SSD_HEAD_DIM, v.shape[1])) for h in range(SSD_HEADS)], axis=0)

    def state_update(x_t, cum_t, dt_t, edge):
        at_edge = jnp.broadcast_to(cum_t[:, edge:edge + 1], cum_t.shape)
        w_end = jnp.exp(at_edge - cum_t) * dt_t
        xw = (x_t * head_rows(w_end)).astype(bf16)
        upd = jnp.concatenate(
            [jnp.dot(xw[g * gw:(g + 1) * gw], b_bf[:, g * SSD_STATE:(g + 1) * SSD_STATE], preferred_element_type=f32)
             for g in range(SSD_GROUPS)], axis=0)
        carry[...] = carry[...] * head_rows(jnp.exp(at_edge)) + upd

    @pl.when(phase == 0)
    def _backward_states():
        @pl.when(c == 0)
        def _():
            carry[...] = load_h0(1)

        j = nc - 1 - c
        dt = jax.nn.softplus(dt_ref[...] + dtb_ref[...])
        a = dt * (-jnp.exp(alog_ref[...]))
        x_t = jnp.concatenate([x[:, k * LANES:(k + 1) * LANES].T for k in range(SSD_INNER // LANES)], axis=0)
        rcum = _cumsum_rows(a, reverse=True)
        xt_store[j] = x_t
        gate_store[j, 0] = dt
        gate_store[j, 1] = _cumsum_rows(a)
        gate_store[j, 2] = rcum
        hb_store[j] = carry[...].astype(bf16)
        state_update(x_t, rcum.T[SSD_HEADS:2 * SSD_HEADS], dt.T[SSD_HEADS:2 * SSD_HEADS], 0)

        if emit_state:
            @pl.when(c == nc - 1)
            def _():
                hfin_ref[1] = carry[...].reshape(SSD_HEADS, SSD_HEAD_DIM, SSD_STATE)

    @pl.when(phase == 1)
    def _forward_and_outputs():
        @pl.when(c == 0)
        def _():
            carry[...] = load_h0(0)

        c_mat = xbc[:, SSD_INNER + SSD_GN:].astype(bf16)
        dt, cum, rcum = gate_store[c, 0], gate_store[c, 1], gate_store[c, 2]
        cum_t, rcum_t, dt_t = cum.T, rcum.T, dt.T
        row = lax.broadcasted_iota(jnp.int32, (CHUNK, CHUNK), 0)
        col = lax.broadcasted_iota(jnp.int32, (CHUNK, CHUNK), 1)
        causal = col <= row
        anti = col >= row
        lane = lax.broadcasted_iota(jnp.int32, (CHUNK, LANES), 1)
        hf = carry[...].astype(bf16)
        hb = hb_store[c]
        cb = [_dot_nt(c_mat[:, g * SSD_STATE:(g + 1) * SSD_STATE], b_bf[:, g * SSD_STATE:(g + 1) * SSD_STATE])
              for g in range(SSD_GROUPS)]

        heads = range(SSD_HEADS)
        seg_f = [jnp.where(causal, cum[:, h:h + 1] - cum_t[h:h + 1, :], -jnp.inf) for h in heads]
        seg_b = [jnp.where(anti, rcum[:, SSD_HEADS + h:SSD_HEADS + h + 1]
                           - rcum_t[SSD_HEADS + h:SSD_HEADS + h + 1, :], -jnp.inf) for h in heads]
        e_f = [jnp.exp(s) for s in seg_f]
        e_b = [jnp.exp(s) for s in seg_b]
        ws = [(cb[h // SSD_RPG] * (e_f[h] * dt_t[h:h + 1, :] + e_b[h] * dt_t[SSD_HEADS + h:SSD_HEADS + h + 1, :])
               ).astype(bf16) for h in heads]
        rhs = []
        for p in range(SSD_HEADS // 2):
            xp = x_bf[:, p * LANES:(p + 1) * LANES]
            zero = jnp.zeros_like(xp)
            rhs.append(jnp.concatenate([jnp.where(lane < SSD_HEAD_DIM, xp, zero),
                                        jnp.where(lane >= SSD_HEAD_DIM, xp, zero)], axis=0))
        y = jnp.concatenate(
            [jnp.dot(jnp.concatenate([ws[2 * p], ws[2 * p + 1]], axis=1), rhs[p], preferred_element_type=f32)
             for p in range(SSD_HEADS // 2)], axis=1)

        def inter_t(h_all):
            return jnp.concatenate(
                [_dot_nt(h_all[g * gw:(g + 1) * gw], c_mat[:, g * SSD_STATE:(g + 1) * SSD_STATE])
                 for g in range(SSD_GROUPS)], axis=0)

        cum_f, cum_b = cum_t[:SSD_HEADS], rcum_t[SSD_HEADS:2 * SSD_HEADS]
        y_t = inter_t(hf) * head_rows(jnp.exp(cum_f)) + inter_t(hb) * head_rows(jnp.exp(cum_b))
        y_inter = jnp.concatenate([y_t[k * LANES:(k + 1) * LANES].T for k in range(SSD_INNER // LANES)], axis=1)
        state_update(xt_store[c], cum_f, dt_t[:SSD_HEADS], CHUNK - 1)
        y = y + y_inter + dskip_ref[...] * x

        zv = z_ref[...]
        yz = y * (zv * jax.nn.sigmoid(zv))
        y_ref[...] = (_rms(yz) * g_ref[...]).astype(y_ref.dtype)

        if emit_state:
            @pl.when(c == nc - 1)
            def _():
                hfin_ref[0] = carry[...].reshape(SSD_HEADS, SSD_HEAD_DIM, SSD_STATE)


def ssd_mix(xbc, dt, z, h0, row0, n_seq, seq_len, dt_bias, a_log, d_skip, norm_g, emit_state, out_buf=None,
            layer=0, state_buf=None):
    nc = seq_len // CHUNK
    blk0 = row0 // CHUNK
    has_h0 = h0 is not None

    def chunk_map(s, p, c):
        return (blk0 + s * nc + jnp.where(p == 0, nc - 1 - c, c), 0)

    state_spec = pl.BlockSpec((None, 2, SSD_HEADS, SSD_HEAD_DIM, SSD_STATE), lambda s, p, c: (s, 0, 0, 0, 0))
    vec = lambda n: pl.BlockSpec((1, n), lambda s, p, c: (0, 0))
    in_specs = [
        pl.BlockSpec((CHUNK, SSD_CONV_DIM), chunk_map),
        pl.BlockSpec((CHUNK, LANES), chunk_map),
        pl.BlockSpec((CHUNK, SSD_INNER), lambda s, p, c: (blk0 + s * nc + p * c, 0)),
    ]
    args = [xbc, dt, z]
    if has_h0:
        in_specs.append(state_spec)
        args.append(h0)
    in_specs += [vec(LANES), vec(LANES), vec(SSD_INNER), vec(SSD_INNER)]
    pad = lambda t: jnp.concatenate([t.reshape(1, -1), jnp.zeros((1, LANES - t.size), f32)], axis=1)
    args += [pad(dt_bias), pad(a_log), jnp.repeat(d_skip, SSD_HEAD_DIM).reshape(1, SSD_INNER),
             norm_g.reshape(1, SSD_INNER)]
    aliases = {}
    if out_buf is not None:
        aliases = {len(args): 0}
        in_specs.append(pl.BlockSpec(memory_space=pl.ANY))
        args.append(out_buf)
    out_shape = [jax.ShapeDtypeStruct((N_TOK, SSD_INNER), bf16)]
    out_specs = [pl.BlockSpec((CHUNK, SSD_INNER), lambda s, p, c: (blk0 + s * nc + p * c, 0))]
    if emit_state:
        if state_buf is not None:
            aliases[len(args)] = 1
            in_specs.append(pl.BlockSpec(memory_space=pl.ANY))
            args.append(state_buf)
        out_shape.append(jax.ShapeDtypeStruct((n_seq, N_EVEN, 2, SSD_HEADS, SSD_HEAD_DIM, SSD_STATE), f32))
        out_specs.append(pl.BlockSpec((None, None, 2, SSD_HEADS, SSD_HEAD_DIM, SSD_STATE),
                                      lambda s, p, c: (s, layer, 0, 0, 0, 0)))
    res = pl.pallas_call(
        functools.partial(_ssd_kernel, nc, has_h0, len(aliases), emit_state),
        input_output_aliases=aliases,
        grid=(n_seq, 2, nc),
        in_specs=in_specs,
        out_specs=out_specs,
        out_shape=out_shape,
        scratch_shapes=[pltpu.VMEM((nc, SSD_INNER, SSD_STATE), bf16), pltpu.VMEM((nc, SSD_INNER, CHUNK), f32),
                        pltpu.VMEM((nc, 3, CHUNK, LANES), f32), pltpu.VMEM((SSD_INNER, SSD_STATE), f32)],
        compiler_params=pltpu.CompilerParams(
            dimension_semantics=("arbitrary", "arbitrary", "arbitrary"), vmem_limit_bytes=VMEM_LIMIT),
        name="ssd_scan",
    )(*args)
    return res if emit_state else res[0]


ML_DIRS = 2 * ML_HEADS


def _cummax_rows(a, reverse=False):
    row = lax.broadcasted_iota(jnp.int32, a.shape, 0)
    s = 1
    while s < CHUNK:
        if reverse:
            a = jnp.maximum(a, jnp.where(row < CHUNK - s, pltpu.roll(a, CHUNK - s, axis=0), -jnp.inf))
        else:
            a = jnp.maximum(a, jnp.where(row >= s, pltpu.roll(a, s, axis=0), -jnp.inf))
        s *= 2
    return a


ML_ST = ML_V_DIM + 16


def _mlstm_t_kernel(nc, has_state, n_buf, emit_state, *refs):
    qk_ref, v_ref, og_ref, gates_ref, gb_ref = refs[:5]
    refs = refs[5:]
    if has_state:
        s0_ref, m0_ref = refs[:2]
        refs = refs[2:]
    refs = refs[n_buf:]
    y_ref, refs = refs[0], refs[1:]
    if emit_state:
        cfin_ref, nfin_ref, mfin_ref = refs[:3]
        refs = refs[3:]
    s_store, m_store, gate_store, k_store, v_store, s_carry, m_carry = refs

    phase = pl.program_id(1)
    c = pl.program_id(2)
    fwd_row = lax.broadcasted_iota(jnp.int32, (1, LANES), 1) < ML_HEADS

    def state_update(cum, r, k_own, v_tr, m_row, big_m, d):
        edge = CHUNK - 1 if d == 0 else 0
        m_edge = big_m[edge:edge + 1, :]
        wc_row = jnp.exp(m_row - m_edge)
        wk_t = jnp.exp(r.T - big_m.T[:, edge:edge + 1])
        lhs = []
        for h in range(ML_HEADS):
            wk = wk_t[d * ML_HEADS + h:d * ML_HEADS + h + 1, :]
            lhs.append(jnp.concatenate([v_tr[h] * wk, jnp.broadcast_to(wk, (ML_ST - ML_V_DIM, CHUNK))],
                                       axis=0).astype(bf16))
        upd = [jnp.dot(lhs[h], k_own[h], preferred_element_type=f32) for h in range(ML_HEADS)]
        for h in range(ML_HEADS):
            cl = d * ML_HEADS + h
            wc = jnp.broadcast_to(wc_row[:, cl:cl + 1], (ML_ST, LANES))
            s_carry[d, h] = wc * s_carry[d, h] + upd[h]
        return cum[edge:edge + 1, :] + m_edge

    def init_state(d):
        if has_state:
            s_carry[d] = s0_ref[d]
        else:
            s_carry[d] = jnp.zeros((ML_HEADS, ML_ST, LANES), f32)

    def emit_final(d):
        for h in range(ML_HEADS):
            tile = s_carry[d, h]
            if h % 2 == 1:
                tile = pltpu.roll(tile, ML_QK_DIM, axis=1)
            cfin_ref[d, h] = tile[:ML_V_DIM, :ML_QK_DIM]
            nfin_ref[d, h:h + 1, :] = tile[ML_V_DIM:ML_V_DIM + 1, :ML_QK_DIM]

    @pl.when(jnp.logical_and(phase == 0, c == 0))
    def _():
        init_state(1)
        m_carry[...] = jnp.broadcast_to(m0_ref[...], m_carry.shape) if has_state else jnp.zeros(m_carry.shape, f32)

    @pl.when(phase == 0)
    def _backward_states():
        lane = lax.broadcasted_iota(jnp.int32, (CHUNK, LANES), 1)
        fwd_lane = lane < ML_HEADS
        g = gates_ref[...] + gb_ref[...]
        lf = pltpu.roll(jax.nn.log_sigmoid(g), LANES - ML_DIRS, axis=1)
        cum = jnp.where(fwd_lane, _cumsum_rows(lf), _cumsum_rows(lf, reverse=True))
        r = g - cum
        pm = jnp.where(fwd_lane, _cummax_rows(r), _cummax_rows(r, reverse=True))
        k = qk_ref[:, ML_QK_WIDTH:]
        k_own = [jnp.where((lane < ML_QK_DIM) if h % 2 == 0 else (lane >= ML_QK_DIM),
                           k[:, (h // 2) * LANES:(h // 2 + 1) * LANES], 0.0).astype(bf16) for h in range(ML_HEADS)]
        v_tr = [v_ref[:, h * ML_V_DIM:(h + 1) * ML_V_DIM].astype(f32).T for h in range(ML_HEADS)]

        j = nc - 1 - c
        gate_store[j, 0] = cum
        gate_store[j, 1] = r
        gate_store[j, 2] = pm
        for h in range(ML_HEADS):
            k_store[j, h] = k_own[h]
            v_store[j, h] = v_tr[h].astype(bf16)
        m_row = m_carry[0:1, :]
        s_store[j] = s_carry[1].astype(bf16)
        m_store[j] = m_carry[...]
        big_m = jnp.maximum(m_row, pm)
        m_new = state_update(cum, r, k_own, v_tr, m_row, big_m, 1)
        m_carry[...] = jnp.broadcast_to(jnp.where(fwd_row, m_row, m_new), m_carry.shape)

    @pl.when(phase == 1)
    def _forward_and_outputs():
        @pl.when(c == 0)
        def _():
            init_state(0)

        if emit_state:
            @pl.when(c == 0)
            def _():
                emit_final(1)

        cum, r, pm = gate_store[c, 0], gate_store[c, 1], gate_store[c, 2]
        k_own = [k_store[c, h] for h in range(ML_HEADS)]
        v_th = [v_store[c, h] for h in range(ML_HEADS)]
        m_both = m_carry[0:1, :]
        m_row = jnp.where(fwd_row, m_both, m_store[c][0:1, :])
        big_m = jnp.maximum(m_row, pm)
        big_m_t = big_m.T
        w_inter_t = jnp.exp(m_row - big_m).T
        floor_t = jnp.exp(-(cum + big_m)).T
        key = lax.broadcasted_iota(jnp.int32, (CHUNK, CHUNK), 0)
        qry = lax.broadcasted_iota(jnp.int32, (CHUNK, CHUNK), 1)
        masks = (key <= qry, key >= qry)
        q = (qk_ref[:, :ML_QK_WIDTH] * (ML_QK_DIM ** -0.5)).astype(bf16)
        heads = range(ML_HEADS)
        hd_pairs = [(h, d) for h in heads for d in range(2)]
        q_pair = [q[:, (h // 2) * LANES:(h // 2 + 1) * LANES] for h in heads]
        s_raw_t = [_dot_nt(k_own[h], q_pair[h]) for h in heads]
        inter = {(h, d): _dot_nt(s_carry[0, h].astype(bf16) if d == 0 else s_store[c, h], q_pair[h])
                 for h, d in hd_pairs}
        w_t = {(h, d): jnp.exp(jnp.where(masks[d], r[:, d * ML_HEADS + h:d * ML_HEADS + h + 1]
                                         - big_m_t[d * ML_HEADS + h:d * ML_HEADS + h + 1, :], -jnp.inf))
               for h, d in hd_pairs}
        sw_t = {hd: s_raw_t[hd[0]] * w_t[hd] for hd in hd_pairs}
        num = {hd: jnp.dot(v_th[hd[0]], sw_t[hd].astype(bf16), preferred_element_type=f32) for hd in hd_pairs}
        wi = {(h, d): w_inter_t[d * ML_HEADS + h:d * ML_HEADS + h + 1, :] for h, d in hd_pairs}
        den = {hd: jnp.sum(sw_t[hd], axis=0, keepdims=True) + wi[hd] * inter[hd][ML_V_DIM:ML_V_DIM + 1]
               for hd in hd_pairs}
        inv = {(h, d): 1.0 / jnp.maximum(jnp.abs(den[h, d]), floor_t[d * ML_HEADS + h:d * ML_HEADS + h + 1, :])
               for h, d in hd_pairs}
        part = {hd: (num[hd] + wi[hd] * inter[hd][:ML_V_DIM]) * inv[hd] for hd in hd_pairs}
        out = [(part[h, 0] + part[h, 1]).T for h in heads]
        gate = [jax.nn.sigmoid(og_ref[:, h * ML_V_DIM:(h + 1) * ML_V_DIM]) for h in heads]
        for h in heads:
            y_ref[:, h * ML_V_DIM:(h + 1) * ML_V_DIM] = (out[h] * gate[h]).astype(y_ref.dtype)

        v_tr = [v.astype(f32) for v in v_th]
        m_fin = jnp.where(fwd_row, state_update(cum, r, k_own, v_tr, m_row, big_m, 0), m_both)
        m_carry[...] = jnp.broadcast_to(m_fin, m_carry.shape)

        if emit_state:
            @pl.when(c == nc - 1)
            def _():
                emit_final(0)
                mfin_ref[...] = m_fin


def mlstm_mix(qk, v, og, gates, gate_b, state, row0, n_seq, seq_len, emit_state, out_buf=None,
              layer=0, state_bufs=None):
    nc = seq_len // CHUNK
    blk0 = row0 // CHUNK
    has_state = state is not None

    def chunk_map(s, p, c):
        return (blk0 + s * nc + jnp.where(p == 0, nc - 1 - c, c), 0)

    m_spec = pl.BlockSpec((None, 1, LANES), lambda s, p, c: (s, 0, 0))
    in_specs = [
        pl.BlockSpec((CHUNK, 2 * ML_QK_WIDTH), chunk_map),
        pl.BlockSpec((CHUNK, ML_V_WIDTH), chunk_map),
        pl.BlockSpec((CHUNK, ML_V_WIDTH), lambda s, p, c: (blk0 + s * nc + p * c, 0)),
        pl.BlockSpec((CHUNK, LANES), chunk_map),
        pl.BlockSpec((1, LANES), lambda s, p, c: (0, 0)),
    ]
    gb = jnp.concatenate([gate_b.reshape(1, 2 * ML_DIRS), jnp.zeros((1, LANES - 2 * ML_DIRS), f32)], axis=1)
    args = [qk, v, og, gates, gb]
    if has_state:
        c0, n0, m0 = state
        rows = jnp.concatenate([c0, jnp.broadcast_to(n0[..., None, :], n0.shape[:-1] + (ML_ST - ML_V_DIM, ML_QK_DIM))],
                               axis=-2)
        zeros = jnp.zeros_like(rows)
        odd = (jnp.arange(ML_HEADS) % 2 == 1)[:, None, None]
        s0 = jnp.where(odd, jnp.concatenate([zeros, rows], axis=-1), jnp.concatenate([rows, zeros], axis=-1))
        m0 = jnp.concatenate([m0.reshape(n_seq, 1, ML_DIRS), jnp.zeros((n_seq, 1, LANES - ML_DIRS), f32)], axis=-1)
        in_specs += [pl.BlockSpec((None, 2, ML_HEADS, ML_ST, LANES), lambda s, p, c: (s, 0, 0, 0, 0)), m_spec]
        args += [s0, m0]
    aliases = {}
    if out_buf is not None:
        aliases[len(args)] = 0
        in_specs.append(pl.BlockSpec(memory_space=pl.ANY))
        args.append(out_buf)
    out_shape = [jax.ShapeDtypeStruct((N_TOK, ML_V_WIDTH), bf16)]
    out_specs = [pl.BlockSpec((CHUNK, ML_V_WIDTH), lambda s, p, c: (blk0 + s * nc + p * c, 0))]
    if emit_state:
        if state_bufs is not None:
            for k_out, buf in enumerate(state_bufs):
                aliases[len(args)] = 1 + k_out
                in_specs.append(pl.BlockSpec(memory_space=pl.ANY))
                args.append(buf)
        out_shape += [jax.ShapeDtypeStruct((n_seq, N_ODD, 2, ML_HEADS, ML_V_DIM, ML_QK_DIM), f32),
                      jax.ShapeDtypeStruct((n_seq, N_ODD, 2, ML_HEADS, ML_QK_DIM), f32),
                      jax.ShapeDtypeStruct((n_seq, 1, LANES), f32)]
        out_specs += [pl.BlockSpec((None, None, 2, ML_HEADS, ML_V_DIM, ML_QK_DIM),
                                   lambda s, p, c: (s, layer, 0, 0, 0, 0)),
                      pl.BlockSpec((None, None, 2, ML_HEADS, ML_QK_DIM), lambda s, p, c: (s, layer, 0, 0, 0)),
                      m_spec]
    res = pl.pallas_call(
        functools.partial(_mlstm_t_kernel, nc, has_state, len(aliases), emit_state),
        input_output_aliases=aliases,
        grid=(n_seq, 2, nc),
        in_specs=in_specs,
        out_specs=out_specs,
        out_shape=out_shape,
        scratch_shapes=[pltpu.VMEM((nc, ML_HEADS, ML_ST, LANES), bf16), pltpu.VMEM((nc, SUBLANES, LANES), f32),
                        pltpu.VMEM((nc, 3, CHUNK, LANES), f32), pltpu.VMEM((nc, ML_HEADS, CHUNK, LANES), bf16),
                        pltpu.VMEM((nc, ML_HEADS, ML_V_DIM, CHUNK), bf16),
                        pltpu.VMEM((2, ML_HEADS, ML_ST, LANES), f32), pltpu.VMEM((SUBLANES, LANES), f32)],
        compiler_params=pltpu.CompilerParams(
            dimension_semantics=("arbitrary", "arbitrary", "arbitrary"), vmem_limit_bytes=VMEM_LIMIT),
        name="mlstm_scan",
    )(*args)
    if not emit_state:
        return res[0]
    y, c_fin, n_fin, mfin = res
    return y, c_fin, n_fin, mfin[:, 0, :ML_DIRS].reshape(n_seq, 2, ML_HEADS)


def kernel(x_prompt, x_sample, c, cache_na_k, cache_na_v, state_ssd, state_mlstm_c, state_mlstm_n, state_mlstm_m,
           c_ctx, w_mod, b_mod, norm_mix, norm_ffn, w_in_even, w_out_even, na_rpb, ssd_conv_w, ssd_conv_b,
           ssd_dt_bias, ssd_a_log, ssd_d, ssd_norm, w_in_odd, w_out_odd, ml_conv_w, ml_conv_b, ml_gate_b,
           w_ff1, w_ff2, norm_f):
    xs = [x_prompt.reshape(N_PROMPT, D_MODEL), x_sample.reshape(DEC_BATCH * DEC_SEQ, D_MODEL)]
    cond = jnp.concatenate([c_ctx[None, :], c, jnp.zeros((SUBLANES - N_COND, D_MODEL), f32)], axis=0)
    mod = adaln_all(cond, w_mod, b_mod)[:, :N_COND].reshape(DEPTH, N_COND, 1, N_MOD * D_MODEL)

    even_main = 3 * NA_WIDTH + SSD_INNER + SSD_CONV_DIM
    odd_main = 2 * ML_QK_WIDTH + 2 * ML_V_WIDTH

    def tail_bf16(w, main):
        t = w[:, :, main:]
        return jnp.concatenate([t, jnp.zeros(t.shape[:2] + (LANES - t.shape[2],), f32)], axis=2).astype(bf16)

    wi_even, wt_even = cast_bf16(w_in_even, even_main), tail_bf16(w_in_even, even_main)
    wi_odd, wt_odd = cast_bf16(w_in_odd, odd_main), tail_bf16(w_in_odd, odd_main)
    wo_even, wo_odd = cast_bf16(w_out_even), cast_bf16(w_out_odd)
    w1_all, w2_all = cast_bf16(w_ff1), cast_bf16(w_ff2)

    out_m = []
    new_k = new_v = new_ssd = new_c = new_n = None
    for l in range(DEPTH):
        norm_last = norm_f if l == DEPTH - 1 else None
        if l % 2 == 0:
            e = l // 2
            o0 = 3 * NA_WIDTH
            segs = ((0, NA_WIDTH, NA_HEAD_DIM ** -0.5, False),
                    (NA_WIDTH, 2 * NA_WIDTH, 1.0, False),
                    (2 * NA_WIDTH, 3 * NA_WIDTH, 1.0, False),
                    (o0, o0 + SSD_INNER, 1.0, False),
                    (o0 + SSD_INNER, o0 + SSD_INNER + SSD_CONV_DIM, 1.0, True),
                    (None, None, 1.0, False))
            outs = ((0, bf16, False), (1, bf16, False), (2, bf16, False), (1, f32, True), (2, f32, True),
                    (3, f32, False), (4, f32, False), (5, f32, False))
            q, k, v, new_k, new_v, z, xbc, dt = in_proj(
                xs, mod[l], norm_mix[l], wi_even, wt_even, e, segs, outs, ssd_conv_w[e], ssd_conv_b[e],
                cache_bufs=None if new_k is None else (new_k, new_v))
            ssd_w = (ssd_dt_bias[e], ssd_a_log[e], ssd_d[e], ssd_norm[e])
            y_ssd, new_ssd = ssd_mix(xbc, dt, z, None, 0, BATCH, SEQ, *ssd_w, True, layer=e, state_buf=new_ssd)
            y_ssd = ssd_mix(xbc, dt, z, state_ssd[:, e], N_PROMPT, DEC_BATCH, DEC_SEQ, *ssd_w, False, out_buf=y_ssd)
            tokens = lambda t: jnp.swapaxes(t, 1, 2).reshape(DEC_BATCH, PAST_LEN, NA_WIDTH).astype(bf16)
            y_na = ctx_attention(q, k, v)
            y_na = na_latent(q, k, v, tokens(cache_na_k[:, e]), tokens(cache_na_v[:, e]),
                             na_bias_table(na_rpb[e]), y_na)
            res = out_mlp(xs, mod[l], norm_ffn[l], [y_na, y_ssd], wo_even, e, w1_all, w2_all, l, norm_last)
        else:
            o = l // 2
            a0 = 2 * ML_QK_WIDTH
            segs = ((0, a0, 1.0, True),
                    (a0, a0 + ML_V_WIDTH, 1.0, False),
                    (a0 + ML_V_WIDTH, a0 + 2 * ML_V_WIDTH, 1.0, False),
                    (None, None, 1.0, False))
            outs = ((0, f32, False), (1, bf16, False), (2, f32, False), (3, f32, False))
            qk, v, og, gates = in_proj(xs, mod[l], norm_mix[l], wi_odd, wt_odd, o, segs, outs,
                                       ml_conv_w[o], ml_conv_b[o])
            y_ml, new_c, new_n, m_fin = mlstm_mix(qk, v, og, gates, ml_gate_b[o], None, 0, BATCH, SEQ, True,
                                                  layer=o, state_bufs=None if new_c is None else (new_c, new_n))
            out_m.append(m_fin)
            state = (state_mlstm_c[:, o], state_mlstm_n[:, o], state_mlstm_m[:, o])
            y_ml = mlstm_mix(qk, v, og, gates, ml_gate_b[o], state, N_PROMPT, DEC_BATCH, DEC_SEQ, False, out_buf=y_ml)
            res = out_mlp(xs, mod[l], norm_ffn[l], [y_ml], wo_odd, o, w1_all, w2_all, l, norm_last)
        xs = list(res) if norm_last is not None else [res]

    y_prompt = xs[0].reshape(BATCH, SEQ, D_MODEL)
    y_sample = xs[1].reshape(DEC_BATCH, DEC_SEQ, D_MODEL)
    return (y_prompt, y_sample, new_k, new_v, new_ssd,
            new_c, new_n, jnp.stack(out_m, axis=1))
```

```python
import functools
import math

import jax
import jax.numpy as jnp
from jax import lax
from jax.experimental import pallas as pl
from jax.experimental.pallas import tpu as pltpu

D_MODEL = 1024
BATCH = 32
SEQ = 256
DEPTH = 4
DEC_BATCH = 2
DEC_SEQ = 4096
PAST_LEN = 256
GRID_W = 64
N_EVEN = (DEPTH + 1) // 2
N_ODD = DEPTH // 2
RMS_EPS = 1e-6
N_MOD = 6
D_FF = 4 * D_MODEL
CONV_K = 3
Q_BLOCK = 128
NA_HEADS = 8
NA_HEAD_DIM = 64
NA_WIDTH = NA_HEADS * NA_HEAD_DIM
NA_WIN_ROWS = 8
NA_WIN_COLS = 16
NA_RPB_ROWS = 2 * NA_WIN_ROWS - 1
NA_RPB_COLS = 2 * NA_WIN_COLS - 1
SSD_INNER = D_MODEL
SSD_HEAD_DIM = 64
SSD_HEADS = SSD_INNER // SSD_HEAD_DIM
SSD_GROUPS = 2
SSD_RPG = SSD_HEADS // SSD_GROUPS
SSD_STATE = 128
SSD_GN = SSD_GROUPS * SSD_STATE
SSD_CONV_DIM = SSD_INNER + 2 * SSD_GN
SSD_CHUNK = 128
ML_HEADS = 8
ML_QK_DIM = D_MODEL // 16
ML_V_DIM = D_MODEL // 8
ML_QK_WIDTH = ML_HEADS * ML_QK_DIM
ML_V_WIDTH = ML_HEADS * ML_V_DIM
ML_CHUNK = 64
EVEN_MIX = NA_WIDTH + SSD_INNER

N_PROMPT = BATCH * SEQ
N_TOK = N_PROMPT + DEC_BATCH * DEC_SEQ
N_COND = 1 + DEC_BATCH
LANES = 128
SUBLANES = 8
VMEM_LIMIT = 56 * 1024 * 1024
TM = 512

f32 = jnp.float32
bf16 = jnp.bfloat16


def _cond_row(i, tm):
    start = i * tm
    return jnp.where(start < N_PROMPT, 0, (start - N_PROMPT) // DEC_SEQ + 1)


def _const_spec(shape):
    nd = len(shape)
    return pl.BlockSpec(shape, lambda i: (0,) * nd, pipeline_mode=pl.Buffered(1))


def _rms(x):
    return x * lax.rsqrt(jnp.mean(x * x, axis=-1, keepdims=True) + RMS_EPS)


def _modulated(x, g, mod, k):
    shift = mod[:, k * D_MODEL:(k + 1) * D_MODEL]
    scale = mod[:, (k + 1) * D_MODEL:(k + 2) * D_MODEL]
    return (_rms(x) * g) * (1.0 + scale) + shift


def _mod_kernel(c_ref, w_ref, b_ref, o_ref):
    c = c_ref[...]
    a = (c * jax.nn.sigmoid(c)).astype(bf16)
    o_ref[...] = jnp.dot(a, w_ref[...].astype(bf16), preferred_element_type=f32) + b_ref[...]


def adaln_all(cond, w_mod, b_mod):
    tn = 1536
    nj = N_MOD * D_MODEL // tn
    return pl.pallas_call(
        _mod_kernel,
        grid=(DEPTH, nj),
        in_specs=[
            pl.BlockSpec((SUBLANES, D_MODEL), lambda l, j: (0, 0)),
            pl.BlockSpec((None, D_MODEL, tn), lambda l, j: (l, 0, j)),
            pl.BlockSpec((None, 1, tn), lambda l, j: (l, 0, j)),
        ],
        out_specs=pl.BlockSpec((None, SUBLANES, tn), lambda l, j: (l, 0, j)),
        out_shape=jax.ShapeDtypeStruct((DEPTH, SUBLANES, N_MOD * D_MODEL), f32),
        compiler_params=pltpu.CompilerParams(
            dimension_semantics=("arbitrary", "arbitrary"), vmem_limit_bytes=VMEM_LIMIT),
        name="adaln",
    )(cond, w_mod, b_mod.reshape(DEPTH, 1, N_MOD * D_MODEL))


def _cast_kernel(x_ref, o_ref):
    o_ref[...] = x_ref[...].astype(o_ref.dtype)


def cast_bf16(w, cols=None):
    n_l, k, n = w.shape
    cols = n if cols is None else cols
    bk = 512
    spec = pl.BlockSpec((None, bk, cols), lambda l, i: (l, i, 0))
    return pl.pallas_call(
        _cast_kernel,
        grid=(n_l, k // bk),
        in_specs=[spec],
        out_specs=spec,
        out_shape=jax.ShapeDtypeStruct((n_l, k, cols), bf16),
        compiler_params=pltpu.CompilerParams(
            dimension_semantics=("arbitrary", "arbitrary"), vmem_limit_bytes=VMEM_LIMIT),
        name="cast_bf16",
    )(w)


def _layer_spec(w, l):
    nd = w.ndim - 1
    return pl.BlockSpec((None,) + w.shape[1:], lambda i: (l,) + (0,) * nd, pipeline_mode=pl.Buffered(1))


def _tile_x(x_refs, tm):
    if len(x_refs) == 1:
        return x_refs[0][...]
    return jnp.where(pl.program_id(0) < N_PROMPT // tm, x_refs[0][...], x_refs[1][...])


def _conv_silu_tile(y, prev_row, next_row, w, b):
    i = pl.program_id(0)
    rows = y.shape[0]
    g = i * rows + lax.broadcasted_iota(jnp.int32, (rows, 1), 0)
    is_prompt = i < N_PROMPT // rows
    pos = jnp.where(is_prompt, g & (SEQ - 1), (g - N_PROMPT) & (DEC_SEQ - 1))
    seq_last = jnp.where(is_prompt, SEQ - 1, DEC_SEQ - 1)
    row = lax.broadcasted_iota(jnp.int32, y.shape, 0)
    up = jnp.where(row == 0, prev_row, pltpu.roll(y, 1, axis=0))
    dn = jnp.where(row == rows - 1, next_row, pltpu.roll(y, rows - 1, axis=0))
    up = jnp.where(pos == 0, 0.0, up)
    dn = jnp.where(pos == seq_last, 0.0, dn)
    c = up * w[0:1, :] + y * w[1:2, :] + dn * w[2:3, :] + b
    return c * jax.nn.sigmoid(c)


def _in_proj_kernel(n_x, n_buf, segs, outs, *refs):
    x_refs = refs[:n_x]
    prev_ref, next_ref, mod_ref, g_ref, w_ref, wt_ref, cw_ref, cb_ref = refs[n_x:n_x + 8]
    o_refs = refs[n_x + 8 + n_buf:]
    mod, g = mod_ref[...], g_ref[...]
    h = _modulated(_tile_x(x_refs, TM), g, mod, 0).astype(bf16)
    ys = []
    for a, b, scale, conv in segs:
        w = wt_ref[...] if a is None else w_ref[:, a:b]
        if conv:
            halo = jnp.concatenate([prev_ref[...], next_ref[...]], axis=0)
            hh = _modulated(halo, g, mod, 0).astype(bf16)
            y = jnp.dot(jnp.concatenate([h, hh], axis=0), w, preferred_element_type=f32)
            y = _conv_silu_tile(y[:TM], y[TM + SUBLANES - 1:TM + SUBLANES], y[TM + SUBLANES:TM + SUBLANES + 1],
                                cw_ref[...], cb_ref[...])
        else:
            y = jnp.dot(h, w, preferred_element_type=f32)
        ys.append(y if scale == 1.0 else y * scale)
    for (si, _, head_major), o_ref in zip(outs, o_refs):
        if head_major:
            @pl.when(pl.program_id(0) < N_PROMPT // TM)
            def _(o_ref=o_ref, si=si):
                for b in range(TM // SEQ):
                    for hd in range(NA_HEADS):
                        o_ref[b, hd] = ys[si][b * SEQ:(b + 1) * SEQ, hd * NA_HEAD_DIM:(hd + 1) * NA_HEAD_DIM]
        else:
            o_ref[...] = ys[si].astype(o_ref.dtype)


def _x_specs(xs, tm):
    if len(xs) == 1:
        return [pl.BlockSpec((tm, D_MODEL), lambda i: (i, 0))]
    n_p = N_PROMPT // tm
    return [pl.BlockSpec((tm, D_MODEL), lambda i: (jnp.minimum(i, n_p - 1), 0)),
            pl.BlockSpec((tm, D_MODEL), lambda i: (jnp.maximum(i - n_p, 0), 0))]


def in_proj(xs, mod_l, g, w, w_tail, l, segs, outs, conv_w, conv_b, cache_bufs=None):
    grid = (N_TOK // TM,)
    last_prompt = N_PROMPT // TM - 1
    halo_src = xs[-1]
    per = TM // SUBLANES
    blk0 = (halo_src.shape[0] - DEC_BATCH * DEC_SEQ) // SUBLANES
    n_blk = halo_src.shape[0] // SUBLANES
    tile0 = N_PROMPT // TM

    def prev_map(i):
        return (jnp.clip(blk0 + (i - tile0) * per - 1, 0, n_blk - 1), 0)

    def next_map(i):
        return (jnp.clip(blk0 + (i - tile0 + 1) * per, 0, n_blk - 1), 0)

    ch = conv_w.shape[1]
    out_shape, out_specs = [], []
    for si, dt, head_major in outs:
        width = LANES if segs[si][0] is None else segs[si][1] - segs[si][0]
        if head_major:
            out_shape.append(jax.ShapeDtypeStruct((BATCH, N_EVEN, NA_HEADS, SEQ, NA_HEAD_DIM), dt))
            out_specs.append(pl.BlockSpec((TM // SEQ, None, NA_HEADS, SEQ, NA_HEAD_DIM),
                                          lambda i: (jnp.minimum(i, last_prompt), l, 0, 0, 0)))
        else:
            out_shape.append(jax.ShapeDtypeStruct((N_TOK, width), dt))
            out_specs.append(pl.BlockSpec((TM, width), lambda i: (i, 0)))
    args = [*xs, halo_src, halo_src, mod_l, g.reshape(1, D_MODEL), w, w_tail, conv_w, conv_b.reshape(1, ch)]
    aliases = {}
    if cache_bufs is not None:
        head_major_outs = [k for k, o in enumerate(outs) if o[2]]
        for k_out, buf in zip(head_major_outs, cache_bufs):
            aliases[len(args)] = k_out
            args.append(buf)
    return pl.pallas_call(
        functools.partial(_in_proj_kernel, len(xs), len(aliases), segs, outs),
        input_output_aliases=aliases,
        grid=grid,
        in_specs=_x_specs(xs, TM) + [
            pl.BlockSpec((SUBLANES, D_MODEL), prev_map),
            pl.BlockSpec((SUBLANES, D_MODEL), next_map),
            pl.BlockSpec((None, 1, N_MOD * D_MODEL), lambda i: (_cond_row(i, TM), 0, 0)),
            _const_spec((1, D_MODEL)),
            _layer_spec(w, l),
            _layer_spec(w_tail, l),
            _const_spec((CONV_K, ch)),
            _const_spec((1, ch)),
        ] + [pl.BlockSpec(memory_space=pl.ANY)] * len(aliases),
        out_specs=out_specs,
        out_shape=out_shape,
        compiler_params=pltpu.CompilerParams(
            dimension_semantics=("arbitrary",), vmem_limit_bytes=VMEM_LIMIT),
        name="in_proj",
    )(*args)


def _out_mlp_kernel(n_x, n_mix, final, *refs):
    x_refs = refs[:n_x]
    mod_ref, g_ref = refs[n_x:n_x + 2]
    refs = refs[n_x + 2:]
    mix_refs = refs[:n_mix]
    wo_ref, w1_ref, w2_ref = refs[n_mix:n_mix + 3]
    rest = refs[n_mix + 3:]
    if final:
        gf_ref, op_ref, os_ref = rest
    else:
        (o_ref,) = rest
    mod = mod_ref[...]
    m = None
    k0 = 0
    for r in mix_refs:
        kw = r.shape[-1]
        part = jnp.dot(r[...].astype(bf16), wo_ref[k0:k0 + kw, :], preferred_element_type=f32)
        m = part if m is None else m + part
        k0 += kw
    x1 = _tile_x(x_refs, OUT_TM) + mod[:, 2 * D_MODEL:3 * D_MODEL] * m
    h2 = _modulated(x1, g_ref[...], mod, 3).astype(bf16)
    u = jnp.dot(h2, w1_ref[...], preferred_element_type=f32)
    a = jnp.square(jnp.maximum(u, 0.0)).astype(bf16)
    x2 = x1 + mod[:, 5 * D_MODEL:6 * D_MODEL] * jnp.dot(a, w2_ref[...], preferred_element_type=f32)
    if not final:
        o_ref[...] = x2
        return
    y = _rms(x2) * gf_ref[...]
    is_prompt = pl.program_id(0) < N_PROMPT // OUT_TM

    @pl.when(is_prompt)
    def _():
        op_ref[...] = y

    @pl.when(jnp.logical_not(is_prompt))
    def _():
        os_ref[...] = y


OUT_TM = 512


def out_mlp(xs, mod_l, g_ffn, mixes, w_out, l_out, w1, w2, l, norm_f=None):
    final = norm_f is not None
    tm = OUT_TM
    grid = (N_TOK // tm,)
    in_specs = _x_specs(xs, tm) + [
        pl.BlockSpec((None, 1, N_MOD * D_MODEL), lambda i: (_cond_row(i, tm), 0, 0)),
        _const_spec((1, D_MODEL)),
    ]
    in_specs += [pl.BlockSpec((tm, m.shape[-1]), lambda i: (i, 0)) for m in mixes]
    in_specs += [_layer_spec(w_out, l_out), _layer_spec(w1, l), _layer_spec(w2, l)]
    args = [*xs, mod_l, g_ffn.reshape(1, D_MODEL), *mixes, w_out, w1, w2]
    out_specs = pl.BlockSpec((tm, D_MODEL), lambda i: (i, 0))
    out_shape = jax.ShapeDtypeStruct((N_TOK, D_MODEL), f32)
    if final:
        in_specs.append(_const_spec((1, D_MODEL)))
        args.append(norm_f.reshape(1, D_MODEL))
        n_p = N_PROMPT // tm
        out_specs = [pl.BlockSpec((tm, D_MODEL), lambda i: (jnp.minimum(i, n_p - 1), 0)),
                     pl.BlockSpec((tm, D_MODEL), lambda i: (jnp.maximum(i - n_p, 0), 0))]
        out_shape = [jax.ShapeDtypeStruct((N_PROMPT, D_MODEL), f32),
                     jax.ShapeDtypeStruct((N_TOK - N_PROMPT, D_MODEL), f32)]
    return pl.pallas_call(
        functools.partial(_out_mlp_kernel, len(xs), len(mixes), final),
        grid=grid,
        in_specs=in_specs,
        out_specs=out_specs,
        out_shape=out_shape,
        compiler_params=pltpu.CompilerParams(
            dimension_semantics=("arbitrary",), vmem_limit_bytes=VMEM_LIMIT),
        name="out_mlp",
    )(*args)


NA_PAIRS = NA_HEADS // 2
NA_ROWS = DEC_SEQ // GRID_W
NA_WIN = NA_WIN_ROWS * GRID_W


def _dot_nt(a, b):
    return lax.dot_general(a, b, (((1,), (1,)), ((), ())), preferred_element_type=f32)


def _pair_stack(x):
    lane = lax.broadcasted_iota(jnp.int32, x.shape, 1)
    zero = jnp.zeros_like(x)
    return jnp.concatenate([jnp.where(lane < NA_HEAD_DIM, x, zero), jnp.where(lane >= NA_HEAD_DIM, x, zero)], axis=0)


def _pair_unstack(o):
    n = o.shape[0] // 2
    lane = lax.broadcasted_iota(jnp.int32, (n, LANES), 1)
    return jnp.where(lane < NA_HEAD_DIM, o[:n], o[n:])


def _softmax_pv(scores, values):
    def lane_tiles(blocks):
        return [b[:, k * LANES:(k + 1) * LANES] for b in blocks for k in range(b.shape[1] // LANES)]

    ms = [jnp.max(functools.reduce(jnp.maximum, lane_tiles(s)), axis=1, keepdims=True) for s in scores]
    ps = [[jnp.exp(b - m) for b in s] for s, m in zip(scores, ms)]
    invs = [1.0 / jnp.sum(functools.reduce(jnp.add, lane_tiles(p)), axis=1, keepdims=True) for p in ps]
    pn = [[(b * inv).astype(bf16) for b in p] for p, inv in zip(ps, invs)]
    return [functools.reduce(jnp.add, [jnp.dot(b, v, preferred_element_type=f32) for b, v in zip(p, vs)])
            for p, vs in zip(pn, values)]


def _ctx_attn_kernel(q_ref, k_ref, v_ref, o_ref):
    pairs = [slice(p * LANES, (p + 1) * LANES) for p in range(NA_PAIRS)]
    scores = [[_dot_nt(_pair_stack(q_ref[:, lanes]), k_ref[:, lanes])] for lanes in pairs]
    outs = _softmax_pv(scores, [[v_ref[:, lanes]] for lanes in pairs])
    for lanes, o in zip(pairs, outs):
        o_ref[:, lanes] = _pair_unstack(o).astype(o_ref.dtype)


def ctx_attention(q, k, v):
    spec = pl.BlockSpec((SEQ, NA_WIDTH), lambda b: (b, 0))
    return pl.pallas_call(
        _ctx_attn_kernel,
        grid=(BATCH,),
        in_specs=[spec, spec, spec],
        out_specs=spec,
        out_shape=jax.ShapeDtypeStruct((N_TOK, NA_WIDTH), bf16),
        compiler_params=pltpu.CompilerParams(dimension_semantics=("arbitrary",), vmem_limit_bytes=VMEM_LIMIT),
        name="ctx_attn",
    )(q, k, v)


def _na_bias_kernel(rpb_ref, o_ref):
    pair = pl.program_id(0)
    shape = (GRID_W, LANES)
    qc = lax.broadcasted_iota(jnp.int32, shape, 0)
    lane = lax.broadcasted_iota(jnp.int32, shape, 1)
    kc = lane & (GRID_W - 1)
    low = lane < GRID_W
    col_start = jnp.clip(qc - NA_WIN_COLS // 2, 0, GRID_W - NA_WIN_COLS)
    valid = (kc >= col_start) & (kc < col_start + NA_WIN_COLS)
    rel_c = jnp.clip(kc - qc + NA_WIN_COLS - 1, 0, NA_RPB_COLS - 1)
    for e in range(2):
        base = (2 * pair + e) * NA_RPB_ROWS
        pieces = []
        for rr in range(NA_RPB_ROWS - 1):
            val = jnp.zeros(shape, f32)
            for t in range(NA_RPB_COLS):
                s_lo = rpb_ref[(base + rr) * NA_RPB_COLS + t]
                s_hi = rpb_ref[(base + rr + 1) * NA_RPB_COLS + t]
                val = jnp.where(rel_c == t, jnp.where(low, s_lo, s_hi), val)
            pieces.append(jnp.where(valid, val, -jnp.inf))
        for d in range(NA_WIN_ROWS):
            for i in range(0, NA_WIN_ROWS, 2):
                o_ref[d, e * GRID_W:(e + 1) * GRID_W, i * GRID_W:(i + 2) * GRID_W] = pieces[d + i]


def na_bias_table(rpb):
    return pl.pallas_call(
        _na_bias_kernel,
        grid=(NA_PAIRS,),
        in_specs=[pl.BlockSpec(memory_space=pltpu.SMEM)],
        out_specs=pl.BlockSpec((None, NA_WIN_ROWS, 2 * GRID_W, NA_WIN), lambda p: (p, 0, 0, 0)),
        out_shape=jax.ShapeDtypeStruct((NA_PAIRS, NA_WIN_ROWS, 2 * GRID_W, NA_WIN), f32),
        compiler_params=pltpu.CompilerParams(dimension_semantics=("arbitrary",), vmem_limit_bytes=VMEM_LIMIT),
        name="na_bias",
    )(rpb.reshape(-1))


NA_STEP_ROWS = 4


def _na_first_key_row(r):
    return jnp.clip(r - NA_WIN_ROWS // 2, 0, NA_ROWS - NA_WIN_ROWS)


def _na_kernel(q_ref, k_ref, v_ref, kc_ref, vc_ref, bias_ref, buf_ref, o_ref):
    del buf_ref
    where, scores, values = [], [], []
    for j in range(NA_STEP_ROWS):
        r = pl.program_id(1) * NA_STEP_ROWS + j
        first = _na_first_key_row(r)
        start = pl.multiple_of(first * GRID_W, GRID_W)
        shift = first - r + NA_WIN_ROWS - 1
        rows = slice(j * GRID_W, (j + 1) * GRID_W)
        for p in range(NA_PAIRS):
            lanes = slice(p * LANES, (p + 1) * LANES)
            qq = _pair_stack(q_ref[rows, lanes])
            where.append((rows, lanes))
            scores.append([_dot_nt(qq, k_ref[pl.ds(start, NA_WIN), lanes]) + bias_ref[p, shift],
                           _dot_nt(qq, kc_ref[:, lanes])])
            values.append([v_ref[pl.ds(start, NA_WIN), lanes], vc_ref[:, lanes]])
    for (rows, lanes), o in zip(where, _softmax_pv(scores, values)):
        o_ref[rows, lanes] = _pair_unstack(o).astype(o_ref.dtype)


def na_latent(q, k, v, k_ctx, v_ctx, bias, buf):
    rows = NA_STEP_ROWS * GRID_W
    steps = NA_ROWS // NA_STEP_ROWS
    row0 = N_PROMPT // rows
    seq0 = N_PROMPT // DEC_SEQ
    kv_spec = pl.BlockSpec((DEC_SEQ, NA_WIDTH), lambda b, r: (seq0 + b, 0))
    ctx_spec = pl.BlockSpec((None, PAST_LEN, NA_WIDTH), lambda b, r: (b, 0, 0))
    return pl.pallas_call(
        _na_kernel,
        grid=(DEC_BATCH, steps),
        in_specs=[
            pl.BlockSpec((rows, NA_WIDTH), lambda b, r: (row0 + b * steps + r, 0)),
            kv_spec, kv_spec, ctx_spec, ctx_spec,
            pl.BlockSpec(bias.shape, lambda b, r: (0, 0, 0, 0), pipeline_mode=pl.Buffered(1)),
            pl.BlockSpec(memory_space=pl.ANY),
        ],
        out_specs=pl.BlockSpec((rows, NA_WIDTH), lambda b, r: (row0 + b * steps + r, 0)),
        out_shape=jax.ShapeDtypeStruct((N_TOK, NA_WIDTH), bf16),
        input_output_aliases={6: 0},
        compiler_params=pltpu.CompilerParams(
            dimension_semantics=("arbitrary", "arbitrary"), vmem_limit_bytes=VMEM_LIMIT),
        name="na_latent",
    )(q, k, v, k_ctx, v_ctx, bias, buf)


CHUNK = 128
SCAN_G = 2
SCAN_ROWS = SCAN_G * CHUNK


def _cumsum_rows(a, reverse=False):
    row = lax.broadcasted_iota(jnp.int32, a.shape, 0)
    s = 1
    while s < CHUNK:
        if reverse:
            a = a + jnp.where(row < CHUNK - s, pltpu.roll(a, CHUNK - s, axis=0), 0.0)
        else:
            a = a + jnp.where(row >= s, pltpu.roll(a, s, axis=0), 0.0)
        s *= 2
    return a


def _ssd_kernel(nc, has_h0, n_buf, emit_state, *refs):
    xbc_ref, dt_ref, z_ref = refs[:3]
    refs = refs[3:]
    if has_h0:
        h0_ref, refs = refs[0], refs[1:]
    dtb_ref, alog_ref, dskip_ref, g_ref = refs[:4]
    refs = refs[4 + n_buf:]
    y_ref, refs = refs[0], refs[1:]
    if emit_state:
        hfin_ref, refs = refs[0], refs[1:]
    hb_store, xt_store, gate_store, carry = refs

    phase = pl.program_id(1)
    gi = pl.program_id(2)
    nb = nc // SCAN_G
    gw = SSD_RPG * SSD_HEAD_DIM

    def load_h0(d):
        if has_h0:
            return h0_ref[d].reshape(SSD_INNER, SSD_STATE)
        return jnp.zeros((SSD_INNER, SSD_STATE), f32)

    def head_rows(v):
        return jnp.concatenate(
            [jnp.broadcast_to(v[h:h + 1, :], (SSD_HEAD_DIM, v.shape[1])) for h in range(SSD_HEADS)], axis=0)

    def state_update(x_t, b_bf, cum_t, dt_t, edge):
        at_edge = jnp.broadcast_to(cum_t[:, edge:edge + 1], cum_t.shape)
        w_end = jnp.exp(at_edge - cum_t) * dt_t
        xw = (x_t * head_rows(w_end)).astype(bf16)
        upd = jnp.concatenate(
            [jnp.dot(xw[g * gw:(g + 1) * gw], b_bf[:, g * SSD_STATE:(g + 1) * SSD_STATE], preferred_element_type=f32)
             for g in range(SSD_GROUPS)], axis=0)
        carry[...] = carry[...] * head_rows(jnp.exp(at_edge)) + upd

    @pl.when(jnp.logical_and(phase == 0, gi == 0))
    def _():
        carry[...] = load_h0(1)

    def backward_chunk(sub):
        rows = slice(sub * CHUNK, (sub + 1) * CHUNK)
        j = SCAN_G * (nb - 1 - gi) + sub
        x = xbc_ref[rows, :SSD_INNER]
        b_bf = xbc_ref[rows, SSD_INNER:SSD_INNER + SSD_GN].astype(bf16)
        dt = jax.nn.softplus(dt_ref[rows, :] + dtb_ref[...])
        a = dt * (-jnp.exp(alog_ref[...]))
        x_t = jnp.concatenate([x[:, k * LANES:(k + 1) * LANES].T for k in range(SSD_INNER // LANES)], axis=0)
        rcum = _cumsum_rows(a, reverse=True)
        xt_store[j] = x_t
        gate_store[j, 0] = dt
        gate_store[j, 1] = _cumsum_rows(a)
        gate_store[j, 2] = rcum
        hb_store[j] = carry[...].astype(bf16)
        state_update(x_t, b_bf, rcum.T[SSD_HEADS:2 * SSD_HEADS], dt.T[SSD_HEADS:2 * SSD_HEADS], 0)

    @pl.when(phase == 0)
    def _backward_states():
        for sub in reversed(range(SCAN_G)):
            backward_chunk(sub)

        if emit_state:
            @pl.when(gi == nb - 1)
            def _():
                hfin_ref[1] = carry[...].reshape(SSD_HEADS, SSD_HEAD_DIM, SSD_STATE)

    @pl.when(jnp.logical_and(phase == 1, gi == 0))
    def _():
        carry[...] = load_h0(0)

    def forward_chunk(sub):
        rows = slice(sub * CHUNK, (sub + 1) * CHUNK)
        c = SCAN_G * gi + sub
        x = xbc_ref[rows, :SSD_INNER]
        x_bf = x.astype(bf16)
        b_bf = xbc_ref[rows, SSD_INNER:SSD_INNER + SSD_GN].astype(bf16)
        c_mat = xbc_ref[rows, SSD_INNER + SSD_GN:].astype(bf16)
        dt, cum, rcum = gate_store[c, 0], gate_store[c, 1], gate_store[c, 2]
        cum_t, rcum_t, dt_t = cum.T, rcum.T, dt.T
        row = lax.broadcasted_iota(jnp.int32, (CHUNK, CHUNK), 0)
        col = lax.broadcasted_iota(jnp.int32, (CHUNK, CHUNK), 1)
        causal = col <= row
        anti = col >= row
        lane = lax.broadcasted_iota(jnp.int32, (CHUNK, LANES), 1)
        hf = carry[...].astype(bf16)
        hb = hb_store[c]
        cb = [_dot_nt(c_mat[:, g * SSD_STATE:(g + 1) * SSD_STATE], b_bf[:, g * SSD_STATE:(g + 1) * SSD_STATE])
              for g in range(SSD_GROUPS)]

        heads = range(SSD_HEADS)
        seg_f = [jnp.where(causal, cum[:, h:h + 1] - cum_t[h:h + 1, :], -jnp.inf) for h in heads]
        seg_b = [jnp.where(anti, rcum[:, SSD_HEADS + h:SSD_HEADS + h + 1]
                           - rcum_t[SSD_HEADS + h:SSD_HEADS + h + 1, :], -jnp.inf) for h in heads]
        e_f = [jnp.exp(s) for s in seg_f]
        e_b = [jnp.exp(s) for s in seg_b]
        ws = [(cb[h // SSD_RPG] * (e_f[h] * dt_t[h:h + 1, :] + e_b[h] * dt_t[SSD_HEADS + h:SSD_HEADS + h + 1, :])
               ).astype(bf16) for h in heads]
        rhs = []
        for p in range(SSD_HEADS // 2):
            xp = x_bf[:, p * LANES:(p + 1) * LANES]
            zero = jnp.zeros_like(xp)
            rhs.append(jnp.concatenate([jnp.where(lane < SSD_HEAD_DIM, xp, zero),
                                        jnp.where(lane >= SSD_HEAD_DIM, xp, zero)], axis=0))
        y = jnp.concatenate(
            [jnp.dot(jnp.concatenate([ws[2 * p], ws[2 * p + 1]], axis=1), rhs[p], preferred_element_type=f32)
             for p in range(SSD_HEADS // 2)], axis=1)

        def inter_t(h_all):
            return jnp.concatenate(
                [_dot_nt(h_all[g * gw:(g + 1) * gw], c_mat[:, g * SSD_STATE:(g + 1) * SSD_STATE])
                 for g in range(SSD_GROUPS)], axis=0)

        cum_f, cum_b = cum_t[:SSD_HEADS], rcum_t[SSD_HEADS:2 * SSD_HEADS]
        y_t = inter_t(hf) * head_rows(jnp.exp(cum_f)) + inter_t(hb) * head_rows(jnp.exp(cum_b))
        y_inter = jnp.concatenate([y_t[k * LANES:(k + 1) * LANES].T for k in range(SSD_INNER // LANES)], axis=1)
        state_update(xt_store[c], b_bf, cum_f, dt_t[:SSD_HEADS], CHUNK - 1)
        y = y + y_inter + dskip_ref[...] * x

        zv = z_ref[rows, :]
        yz = y * (zv * jax.nn.sigmoid(zv))
        y_ref[rows, :] = (_rms(yz) * g_ref[...]).astype(y_ref.dtype)

    @pl.when(phase == 1)
    def _forward_and_outputs():
        for sub in range(SCAN_G):
            forward_chunk(sub)

        if emit_state:
            @pl.when(gi == nb - 1)
            def _():
                hfin_ref[0] = carry[...].reshape(SSD_HEADS, SSD_HEAD_DIM, SSD_STATE)


def ssd_mix(xbc, dt, z, h0, row0, n_seq, seq_len, dt_bias, a_log, d_skip, norm_g, emit_state, out_buf=None,
            layer=0, state_buf=None):
    nc = seq_len // CHUNK
    nb = seq_len // SCAN_ROWS
    blk0 = row0 // SCAN_ROWS
    has_h0 = h0 is not None

    def block_map(s, p, c):
        return (blk0 + s * nb + jnp.where(p == 0, nb - 1 - c, c), 0)

    state_spec = pl.BlockSpec((None, 2, SSD_HEADS, SSD_HEAD_DIM, SSD_STATE), lambda s, p, c: (s, 0, 0, 0, 0))
    vec = lambda n: pl.BlockSpec((1, n), lambda s, p, c: (0, 0))
    in_specs = [
        pl.BlockSpec((SCAN_ROWS, SSD_CONV_DIM), block_map),
        pl.BlockSpec((SCAN_ROWS, LANES), block_map),
        pl.BlockSpec((SCAN_ROWS, SSD_INNER), lambda s, p, c: (blk0 + s * nb + p * c, 0)),
    ]
    args = [xbc, dt, z]
    if has_h0:
        in_specs.append(state_spec)
        args.append(h0)
    in_specs += [vec(LANES), vec(LANES), vec(SSD_INNER), vec(SSD_INNER)]
    pad = lambda t: jnp.concatenate([t.reshape(1, -1), jnp.zeros((1, LANES - t.size), f32)], axis=1)
    args += [pad(dt_bias), pad(a_log), jnp.repeat(d_skip, SSD_HEAD_DIM).reshape(1, SSD_INNER),
             norm_g.reshape(1, SSD_INNER)]
    aliases = {}
    if out_buf is not None:
        aliases = {len(args): 0}
        in_specs.append(pl.BlockSpec(memory_space=pl.ANY))
        args.append(out_buf)
    out_shape = [jax.ShapeDtypeStruct((N_TOK, SSD_INNER), bf16)]
    out_specs = [pl.BlockSpec((SCAN_ROWS, SSD_INNER), lambda s, p, c: (blk0 + s * nb + p * c, 0))]
    if emit_state:
        if state_buf is not None:
            aliases[len(args)] = 1
            in_specs.append(pl.BlockSpec(memory_space=pl.ANY))
            args.append(state_buf)
        out_shape.append(jax.ShapeDtypeStruct((n_seq, N_EVEN, 2, SSD_HEADS, SSD_HEAD_DIM, SSD_STATE), f32))
        out_specs.append(pl.BlockSpec((None, None, 2, SSD_HEADS, SSD_HEAD_DIM, SSD_STATE),
                                      lambda s, p, c: (s, layer, 0, 0, 0, 0)))
    res = pl.pallas_call(
        functools.partial(_ssd_kernel, nc, has_h0, len(aliases), emit_state),
        input_output_aliases=aliases,
        grid=(n_seq, 2, nb),
        in_specs=in_specs,
        out_specs=out_specs,
        out_shape=out_shape,
        scratch_shapes=[pltpu.VMEM((nc, SSD_INNER, SSD_STATE), bf16), pltpu.VMEM((nc, SSD_INNER, CHUNK), f32),
                        pltpu.VMEM((nc, 3, CHUNK, LANES), f32), pltpu.VMEM((SSD_INNER, SSD_STATE), f32)],
        compiler_params=pltpu.CompilerParams(
            dimension_semantics=("arbitrary", "arbitrary", "arbitrary"), vmem_limit_bytes=VMEM_LIMIT),
        name="ssd_scan",
    )(*args)
    return res if emit_state else res[0]


ML_DIRS = 2 * ML_HEADS


def _cummax_rows(a, reverse=False):
    row = lax.broadcasted_iota(jnp.int32, a.shape, 0)
    s = 1
    while s < CHUNK:
        if reverse:
            a = jnp.maximum(a, jnp.where(row < CHUNK - s, pltpu.roll(a, CHUNK - s, axis=0), -jnp.inf))
        else:
            a = jnp.maximum(a, jnp.where(row >= s, pltpu.roll(a, s, axis=0), -jnp.inf))
        s *= 2
    return a


ML_ST = ML_V_DIM + 16


def _mlstm_t_kernel(nc, has_state, n_buf, emit_state, *refs):
    qk_ref, v_ref, og_ref, gates_ref, gb_ref = refs[:5]
    refs = refs[5:]
    if has_state:
        s0_ref, m0_ref = refs[:2]
        refs = refs[2:]
    refs = refs[n_buf:]
    y_ref, refs = refs[0], refs[1:]
    if emit_state:
        cfin_ref, nfin_ref, mfin_ref = refs[:3]
        refs = refs[3:]
    s_store, m_store, gate_store, k_store, v_store, s_carry, m_carry = refs

    phase = pl.program_id(1)
    gi = pl.program_id(2)
    nb = nc // SCAN_G
    fwd_row = lax.broadcasted_iota(jnp.int32, (1, LANES), 1) < ML_HEADS

    def state_update(cum, r, k_own, v_tr, m_row, big_m, d):
        edge = CHUNK - 1 if d == 0 else 0
        m_edge = big_m[edge:edge + 1, :]
        wc_row = jnp.exp(m_row - m_edge)
        wk_t = jnp.exp(r.T - big_m.T[:, edge:edge + 1])
        lhs = []
        for h in range(ML_HEADS):
            wk = wk_t[d * ML_HEADS + h:d * ML_HEADS + h + 1, :]
            lhs.append(jnp.concatenate([v_tr[h] * wk, jnp.broadcast_to(wk, (ML_ST - ML_V_DIM, CHUNK))],
                                       axis=0).astype(bf16))
        upd = [jnp.dot(lhs[h], k_own[h], preferred_element_type=f32) for h in range(ML_HEADS)]
        for h in range(ML_HEADS):
            cl = d * ML_HEADS + h
            wc = jnp.broadcast_to(wc_row[:, cl:cl + 1], (ML_ST, LANES))
            s_carry[d, h] = wc * s_carry[d, h] + upd[h]
        return cum[edge:edge + 1, :] + m_edge

    def init_state(d):
        if has_state:
            s_carry[d] = s0_ref[d]
        else:
            s_carry[d] = jnp.zeros((ML_HEADS, ML_ST, LANES), f32)

    def emit_final(d):
        for h in range(ML_HEADS):
            tile = s_carry[d, h]
            if h % 2 == 1:
                tile = pltpu.roll(tile, ML_QK_DIM, axis=1)
            cfin_ref[d, h] = tile[:ML_V_DIM, :ML_QK_DIM]
            nfin_ref[d, h:h + 1, :] = tile[ML_V_DIM:ML_V_DIM + 1, :ML_QK_DIM]

    @pl.when(jnp.logical_and(phase == 0, gi == 0))
    def _():
        init_state(1)
        m_carry[...] = jnp.broadcast_to(m0_ref[...], m_carry.shape) if has_state else jnp.zeros(m_carry.shape, f32)

    def backward_chunk(sub):
        rows = slice(sub * CHUNK, (sub + 1) * CHUNK)
        lane = lax.broadcasted_iota(jnp.int32, (CHUNK, LANES), 1)
        fwd_lane = lane < ML_HEADS
        g = gates_ref[rows, :] + gb_ref[...]
        lf = pltpu.roll(jax.nn.log_sigmoid(g), LANES - ML_DIRS, axis=1)
        cum = jnp.where(fwd_lane, _cumsum_rows(lf), _cumsum_rows(lf, reverse=True))
        r = g - cum
        pm = jnp.where(fwd_lane, _cummax_rows(r), _cummax_rows(r, reverse=True))
        k = qk_ref[rows, ML_QK_WIDTH:]
        k_own = [jnp.where((lane < ML_QK_DIM) if h % 2 == 0 else (lane >= ML_QK_DIM),
                           k[:, (h // 2) * LANES:(h // 2 + 1) * LANES], 0.0).astype(bf16) for h in range(ML_HEADS)]
        v_tr = [v_ref[rows, h * ML_V_DIM:(h + 1) * ML_V_DIM].astype(f32).T for h in range(ML_HEADS)]

        j = SCAN_G * (nb - 1 - gi) + sub
        gate_store[j, 0] = cum
        gate_store[j, 1] = r
        gate_store[j, 2] = pm
        for h in range(ML_HEADS):
            k_store[j, h] = k_own[h]
            v_store[j, h] = v_tr[h].astype(bf16)
        m_row = m_carry[0:1, :]
        s_store[j] = s_carry[1].astype(bf16)
        m_store[j] = m_carry[...]
        big_m = jnp.maximum(m_row, pm)
        m_new = state_update(cum, r, k_own, v_tr, m_row, big_m, 1)
        m_carry[...] = jnp.broadcast_to(jnp.where(fwd_row, m_row, m_new), m_carry.shape)

    @pl.when(phase == 0)
    def _backward_states():
        for sub in reversed(range(SCAN_G)):
            backward_chunk(sub)

    @pl.when(jnp.logical_and(phase == 1, gi == 0))
    def _():
        init_state(0)
        if emit_state:
            emit_final(1)

    def forward_chunk(sub):
        rows = slice(sub * CHUNK, (sub + 1) * CHUNK)
        c = SCAN_G * gi + sub
        cum, r, pm = gate_store[c, 0], gate_store[c, 1], gate_store[c, 2]
        k_own = [k_store[c, h] for h in range(ML_HEADS)]
        v_th = [v_store[c, h] for h in range(ML_HEADS)]
        m_both = m_carry[0:1, :]
        m_row = jnp.where(fwd_row, m_both, m_store[c][0:1, :])
        big_m = jnp.maximum(m_row, pm)
        big_m_t = big_m.T
        w_inter_t = jnp.exp(m_row - big_m).T
        floor_t = jnp.exp(-(cum + big_m)).T
        key = lax.broadcasted_iota(jnp.int32, (CHUNK, CHUNK), 0)
        qry = lax.broadcasted_iota(jnp.int32, (CHUNK, CHUNK), 1)
        masks = (key <= qry, key >= qry)
        q = (qk_ref[rows, :ML_QK_WIDTH] * (ML_QK_DIM ** -0.5)).astype(bf16)
        heads = range(ML_HEADS)
        hd_pairs = [(h, d) for h in heads for d in range(2)]
        q_pair = [q[:, (h // 2) * LANES:(h // 2 + 1) * LANES] for h in heads]
        s_raw_t = [_dot_nt(k_own[h], q_pair[h]) for h in heads]
        inter = {(h, d): _dot_nt(s_carry[0, h].astype(bf16) if d == 0 else s_store[c, h], q_pair[h])
                 for h, d in hd_pairs}
        w_t = {(h, d): jnp.exp(jnp.where(masks[d], r[:, d * ML_HEADS + h:d * ML_HEADS + h + 1]
                                         - big_m_t[d * ML_HEADS + h:d * ML_HEADS + h + 1, :], -jnp.inf))
               for h, d in hd_pairs}
        sw_t = {hd: s_raw_t[hd[0]] * w_t[hd] for hd in hd_pairs}
        num = {hd: jnp.dot(v_th[hd[0]], sw_t[hd].astype(bf16), preferred_element_type=f32) for hd in hd_pairs}
        wi = {(h, d): w_inter_t[d * ML_HEADS + h:d * ML_HEADS + h + 1, :] for h, d in hd_pairs}
        den = {hd: jnp.sum(sw_t[hd], axis=0, keepdims=True) + wi[hd] * inter[hd][ML_V_DIM:ML_V_DIM + 1]
               for hd in hd_pairs}
        inv = {(h, d): 1.0 / jnp.maximum(jnp.abs(den[h, d]), floor_t[d * ML_HEADS + h:d * ML_HEADS + h + 1, :])
               for h, d in hd_pairs}
        part = {hd: (num[hd] + wi[hd] * inter[hd][:ML_V_DIM]) * inv[hd] for hd in hd_pairs}
        out = [(part[h, 0] + part[h, 1]).T for h in heads]
        gate = [jax.nn.sigmoid(og_ref[rows, h * ML_V_DIM:(h + 1) * ML_V_DIM]) for h in heads]
        for h in heads:
            y_ref[rows, h * ML_V_DIM:(h + 1) * ML_V_DIM] = (out[h] * gate[h]).astype(y_ref.dtype)

        v_tr = [v.astype(f32) for v in v_th]
        m_fin = jnp.where(fwd_row, state_update(cum, r, k_own, v_tr, m_row, big_m, 0), m_both)
        m_carry[...] = jnp.broadcast_to(m_fin, m_carry.shape)

        if emit_state and sub == SCAN_G - 1:
            @pl.when(gi == nb - 1)
            def _():
                emit_final(0)
                mfin_ref[...] = m_fin

    @pl.when(phase == 1)
    def _forward_and_outputs():
        for sub in range(SCAN_G):
            forward_chunk(sub)


def mlstm_mix(qk, v, og, gates, gate_b, state, row0, n_seq, seq_len, emit_state, out_buf=None,
              layer=0, state_bufs=None):
    nc = seq_len // CHUNK
    nb = seq_len // SCAN_ROWS
    blk0 = row0 // SCAN_ROWS
    has_state = state is not None

    def block_map(s, p, c):
        return (blk0 + s * nb + jnp.where(p == 0, nb - 1 - c, c), 0)

    m_spec = pl.BlockSpec((None, 1, LANES), lambda s, p, c: (s, 0, 0))
    in_specs = [
        pl.BlockSpec((SCAN_ROWS, 2 * ML_QK_WIDTH), block_map),
        pl.BlockSpec((SCAN_ROWS, ML_V_WIDTH), block_map),
        pl.BlockSpec((SCAN_ROWS, ML_V_WIDTH), lambda s, p, c: (blk0 + s * nb + p * c, 0)),
        pl.BlockSpec((SCAN_ROWS, LANES), block_map),
        pl.BlockSpec((1, LANES), lambda s, p, c: (0, 0)),
    ]
    gb = jnp.concatenate([gate_b.reshape(1, 2 * ML_DIRS), jnp.zeros((1, LANES - 2 * ML_DIRS), f32)], axis=1)
    args = [qk, v, og, gates, gb]
    if has_state:
        c0, n0, m0 = state
        rows = jnp.concatenate([c0, jnp.broadcast_to(n0[..., None, :], n0.shape[:-1] + (ML_ST - ML_V_DIM, ML_QK_DIM))],
                               axis=-2)
        zeros = jnp.zeros_like(rows)
        odd = (jnp.arange(ML_HEADS) % 2 == 1)[:, None, None]
        s0 = jnp.where(odd, jnp.concatenate([zeros, rows], axis=-1), jnp.concatenate([rows, zeros], axis=-1))
        m0 = jnp.concatenate([m0.reshape(n_seq, 1, ML_DIRS), jnp.zeros((n_seq, 1, LANES - ML_DIRS), f32)], axis=-1)
        in_specs += [pl.BlockSpec((None, 2, ML_HEADS, ML_ST, LANES), lambda s, p, c: (s, 0, 0, 0, 0)), m_spec]
        args += [s0, m0]
    aliases = {}
    if out_buf is not None:
        aliases[len(args)] = 0
        in_specs.append(pl.BlockSpec(memory_space=pl.ANY))
        args.append(out_buf)
    out_shape = [jax.ShapeDtypeStruct((N_TOK, ML_V_WIDTH), bf16)]
    out_specs = [pl.BlockSpec((SCAN_ROWS, ML_V_WIDTH), lambda s, p, c: (blk0 + s * nb + p * c, 0))]
    if emit_state:
        if state_bufs is not None:
            for k_out, buf in enumerate(state_bufs):
                aliases[len(args)] = 1 + k_out
                in_specs.append(pl.BlockSpec(memory_space=pl.ANY))
                args.append(buf)
        out_shape += [jax.ShapeDtypeStruct((n_seq, N_ODD, 2, ML_HEADS, ML_V_DIM, ML_QK_DIM), f32),
                      jax.ShapeDtypeStruct((n_seq, N_ODD, 2, ML_HEADS, ML_QK_DIM), f32),
                      jax.ShapeDtypeStruct((n_seq, 1, LANES), f32)]
        out_specs += [pl.BlockSpec((None, None, 2, ML_HEADS, ML_V_DIM, ML_QK_DIM),
                                   lambda s, p, c: (s, layer, 0, 0, 0, 0)),
                      pl.BlockSpec((None, None, 2, ML_HEADS, ML_QK_DIM), lambda s, p, c: (s, layer, 0, 0, 0)),
                      m_spec]
    res = pl.pallas_call(
        functools.partial(_mlstm_t_kernel, nc, has_state, len(aliases), emit_state),
        input_output_aliases=aliases,
        grid=(n_seq, 2, nb),
        in_specs=in_specs,
        out_specs=out_specs,
        out_shape=out_shape,
        scratch_shapes=[pltpu.VMEM((nc, ML_HEADS, ML_ST, LANES), bf16), pltpu.VMEM((nc, SUBLANES, LANES), f32),
                        pltpu.VMEM((nc, 3, CHUNK, LANES), f32), pltpu.VMEM((nc, ML_HEADS, CHUNK, LANES), bf16),
                        pltpu.VMEM((nc, ML_HEADS, ML_V_DIM, CHUNK), bf16),
                        pltpu.VMEM((2, ML_HEADS, ML_ST, LANES), f32), pltpu.VMEM((SUBLANES, LANES), f32)],
        compiler_params=pltpu.CompilerParams(
            dimension_semantics=("arbitrary", "arbitrary", "arbitrary"), vmem_limit_bytes=VMEM_LIMIT),
        name="mlstm_scan",
    )(*args)
    if not emit_state:
        return res[0]
    y, c_fin, n_fin, mfin = res
    return y, c_fin, n_fin, mfin[:, 0, :ML_DIRS].reshape(n_seq, 2, ML_HEADS)


def kernel(x_prompt, x_sample, c, cache_na_k, cache_na_v, state_ssd, state_mlstm_c, state_mlstm_n, state_mlstm_m,
           c_ctx, w_mod, b_mod, norm_mix, norm_ffn, w_in_even, w_out_even, na_rpb, ssd_conv_w, ssd_conv_b,
           ssd_dt_bias, ssd_a_log, ssd_d, ssd_norm, w_in_odd, w_out_odd, ml_conv_w, ml_conv_b, ml_gate_b,
           w_ff1, w_ff2, norm_f):
    xs = [x_prompt.reshape(N_PROMPT, D_MODEL), x_sample.reshape(DEC_BATCH * DEC_SEQ, D_MODEL)]
    cond = jnp.concatenate([c_ctx[None, :], c, jnp.zeros((SUBLANES - N_COND, D_MODEL), f32)], axis=0)
    mod = adaln_all(cond, w_mod, b_mod)[:, :N_COND].reshape(DEPTH, N_COND, 1, N_MOD * D_MODEL)

    even_main = 3 * NA_WIDTH + SSD_INNER + SSD_CONV_DIM
    odd_main = 2 * ML_QK_WIDTH + 2 * ML_V_WIDTH

    def tail_bf16(w, main):
        t = w[:, :, main:]
        return jnp.concatenate([t, jnp.zeros(t.shape[:2] + (LANES - t.shape[2],), f32)], axis=2).astype(bf16)

    wi_even, wt_even = cast_bf16(w_in_even, even_main), tail_bf16(w_in_even, even_main)
    wi_odd, wt_odd = cast_bf16(w_in_odd, odd_main), tail_bf16(w_in_odd, odd_main)
    wo_even, wo_odd = cast_bf16(w_out_even), cast_bf16(w_out_odd)
    w1_all, w2_all = cast_bf16(w_ff1), cast_bf16(w_ff2)

    out_m = []
    new_k = new_v = new_ssd = new_c = new_n = None
    for l in range(DEPTH):
        norm_last = norm_f if l == DEPTH - 1 else None
        if l % 2 == 0:
            e = l // 2
            o0 = 3 * NA_WIDTH
            segs = ((0, NA_WIDTH, NA_HEAD_DIM ** -0.5, False),
                    (NA_WIDTH, 2 * NA_WIDTH, 1.0, False),
                    (2 * NA_WIDTH, 3 * NA_WIDTH, 1.0, False),
                    (o0, o0 + SSD_INNER, 1.0, False),
                    (o0 + SSD_INNER, o0 + SSD_INNER + SSD_CONV_DIM, 1.0, True),
                    (None, None, 1.0, False))
            outs = ((0, bf16, False), (1, bf16, False), (2, bf16, False), (1, f32, True), (2, f32, True),
                    (3, f32, False), (4, f32, False), (5, f32, False))
            q, k, v, new_k, new_v, z, xbc, dt = in_proj(
                xs, mod[l], norm_mix[l], wi_even, wt_even, e, segs, outs, ssd_conv_w[e], ssd_conv_b[e],
                cache_bufs=None if new_k is None else (new_k, new_v))
            ssd_w = (ssd_dt_bias[e], ssd_a_log[e], ssd_d[e], ssd_norm[e])
            y_ssd, new_ssd = ssd_mix(xbc, dt, z, None, 0, BATCH, SEQ, *ssd_w, True, layer=e, state_buf=new_ssd)
            y_ssd = ssd_mix(xbc, dt, z, state_ssd[:, e], N_PROMPT, DEC_BATCH, DEC_SEQ, *ssd_w, False, out_buf=y_ssd)
            tokens = lambda t: jnp.swapaxes(t, 1, 2).reshape(DEC_BATCH, PAST_LEN, NA_WIDTH).astype(bf16)
            y_na = ctx_attention(q, k, v)
            y_na = na_latent(q, k, v, tokens(cache_na_k[:, e]), tokens(cache_na_v[:, e]),
                             na_bias_table(na_rpb[e]), y_na)
            res = out_mlp(xs, mod[l], norm_ffn[l], [y_na, y_ssd], wo_even, e, w1_all, w2_all, l, norm_last)
        else:
            o = l // 2
            a0 = 2 * ML_QK_WIDTH
            segs = ((0, a0, 1.0, True),
                    (a0, a0 + ML_V_WIDTH, 1.0, False),
                    (a0 + ML_V_WIDTH, a0 + 2 * ML_V_WIDTH, 1.0, False),
                    (None, None, 1.0, False))
            outs = ((0, f32, False), (1, bf16, False), (2, f32, False), (3, f32, False))
            qk, v, og, gates = in_proj(xs, mod[l], norm_mix[l], wi_odd, wt_odd, o, segs, outs,
                                       ml_conv_w[o], ml_conv_b[o])
            y_ml, new_c, new_n, m_fin = mlstm_mix(qk, v, og, gates, ml_gate_b[o], None, 0, BATCH, SEQ, True,
                                                  layer=o, state_bufs=None if new_c is None else (new_c, new_n))
            out_m.append(m_fin)
            state = (state_mlstm_c[:, o], state_mlstm_n[:, o], state_mlstm_m[:, o])
            y_ml = mlstm_mix(qk, v, og, gates, ml_gate_b[o], state, N_PROMPT, DEC_BATCH, DEC_SEQ, False, out_buf=y_ml)
            res = out_mlp(xs, mod[l], norm_ffn[l], [y_ml], wo_odd, o, w1_all, w2_all, l, norm_last)
        xs = list(res) if norm_last is not None else [res]

    y_prompt = xs[0].reshape(BATCH, SEQ, D_MODEL)
    y_sample = xs[1].reshape(DEC_BATCH, DEC_SEQ, D_MODEL)
    return (y_prompt, y_sample, new_k, new_v, new_ssd,
            new_c, new_n, jnp.stack(out_m, axis=1))
```

```python
import functools
import math

import jax
import jax.numpy as jnp
from jax import lax
from jax.experimental import pallas as pl
from jax.experimental.pallas import tpu as pltpu

D_MODEL = 1024
BATCH = 32
SEQ = 256
DEPTH = 4
DEC_BATCH = 2
DEC_SEQ = 4096
PAST_LEN = 256
GRID_W = 64
N_EVEN = (DEPTH + 1) // 2
N_ODD = DEPTH // 2
RMS_EPS = 1e-6
N_MOD = 6
D_FF = 4 * D_MODEL
CONV_K = 3
Q_BLOCK = 128
NA_HEADS = 8
NA_HEAD_DIM = 64
NA_WIDTH = NA_HEADS * NA_HEAD_DIM
NA_WIN_ROWS = 8
NA_WIN_COLS = 16
NA_RPB_ROWS = 2 * NA_WIN_ROWS - 1
NA_RPB_COLS = 2 * NA_WIN_COLS - 1
SSD_INNER = D_MODEL
SSD_HEAD_DIM = 64
SSD_HEADS = SSD_INNER // SSD_HEAD_DIM
SSD_GROUPS = 2
SSD_RPG = SSD_HEADS // SSD_GROUPS
SSD_STATE = 128
SSD_GN = SSD_GROUPS * SSD_STATE
SSD_CONV_DIM = SSD_INNER + 2 * SSD_GN
SSD_CHUNK = 128
ML_HEADS = 8
ML_QK_DIM = D_MODEL // 16
ML_V_DIM = D_MODEL // 8
ML_QK_WIDTH = ML_HEADS * ML_QK_DIM
ML_V_WIDTH = ML_HEADS * ML_V_DIM
ML_CHUNK = 64
EVEN_MIX = NA_WIDTH + SSD_INNER

N_PROMPT = BATCH * SEQ
N_TOK = N_PROMPT + DEC_BATCH * DEC_SEQ
N_COND = 1 + DEC_BATCH
LANES = 128
SUBLANES = 8
VMEM_LIMIT = 56 * 1024 * 1024
TM = 512

f32 = jnp.float32
bf16 = jnp.bfloat16


def _cond_row(i, tm):
    start = i * tm
    return jnp.where(start < N_PROMPT, 0, (start - N_PROMPT) // DEC_SEQ + 1)


def _const_spec(shape):
    nd = len(shape)
    return pl.BlockSpec(shape, lambda i: (0,) * nd, pipeline_mode=pl.Buffered(1))


def _rms(x):
    return x * lax.rsqrt(jnp.mean(x * x, axis=-1, keepdims=True) + RMS_EPS)


def _modulated(x, g, mod, k):
    shift = mod[:, k * D_MODEL:(k + 1) * D_MODEL]
    scale = mod[:, (k + 1) * D_MODEL:(k + 2) * D_MODEL]
    return (_rms(x) * g) * (1.0 + scale) + shift


def _mod_kernel(c_ref, w_ref, b_ref, o_ref):
    c = c_ref[...]
    a = (c * jax.nn.sigmoid(c)).astype(bf16)
    o_ref[...] = jnp.dot(a, w_ref[...].astype(bf16), preferred_element_type=f32) + b_ref[...]


def adaln_all(cond, w_mod, b_mod):
    tn = 1536
    nj = N_MOD * D_MODEL // tn
    return pl.pallas_call(
        _mod_kernel,
        grid=(DEPTH, nj),
        in_specs=[
            pl.BlockSpec((SUBLANES, D_MODEL), lambda l, j: (0, 0)),
            pl.BlockSpec((None, D_MODEL, tn), lambda l, j: (l, 0, j)),
            pl.BlockSpec((None, 1, tn), lambda l, j: (l, 0, j)),
        ],
        out_specs=pl.BlockSpec((None, SUBLANES, tn), lambda l, j: (l, 0, j)),
        out_shape=jax.ShapeDtypeStruct((DEPTH, SUBLANES, N_MOD * D_MODEL), f32),
        compiler_params=pltpu.CompilerParams(
            dimension_semantics=("arbitrary", "arbitrary"), vmem_limit_bytes=VMEM_LIMIT),
        name="adaln",
    )(cond, w_mod, b_mod.reshape(DEPTH, 1, N_MOD * D_MODEL))


def _cast_kernel(x_ref, o_ref):
    o_ref[...] = x_ref[...].astype(o_ref.dtype)


def cast_bf16(w, cols=None):
    n_l, k, n = w.shape
    cols = n if cols is None else cols
    bk = 512
    spec = pl.BlockSpec((None, bk, cols), lambda l, i: (l, i, 0))
    return pl.pallas_call(
        _cast_kernel,
        grid=(n_l, k // bk),
        in_specs=[spec],
        out_specs=spec,
        out_shape=jax.ShapeDtypeStruct((n_l, k, cols), bf16),
        compiler_params=pltpu.CompilerParams(
            dimension_semantics=("arbitrary", "arbitrary"), vmem_limit_bytes=VMEM_LIMIT),
        name="cast_bf16",
    )(w)


def _layer_spec(w, l):
    nd = w.ndim - 1
    return pl.BlockSpec((None,) + w.shape[1:], lambda i: (l,) + (0,) * nd, pipeline_mode=pl.Buffered(1))


def _tile_x(x_refs, tm):
    if len(x_refs) == 1:
        return x_refs[0][...]
    return jnp.where(pl.program_id(0) < N_PROMPT // tm, x_refs[0][...], x_refs[1][...])


def _conv_silu_tile(y, prev_row, next_row, w, b):
    i = pl.program_id(0)
    rows = y.shape[0]
    g = i * rows + lax.broadcasted_iota(jnp.int32, (rows, 1), 0)
    is_prompt = i < N_PROMPT // rows
    pos = jnp.where(is_prompt, g & (SEQ - 1), (g - N_PROMPT) & (DEC_SEQ - 1))
    seq_last = jnp.where(is_prompt, SEQ - 1, DEC_SEQ - 1)
    row = lax.broadcasted_iota(jnp.int32, y.shape, 0)
    up = jnp.where(row == 0, prev_row, pltpu.roll(y, 1, axis=0))
    dn = jnp.where(row == rows - 1, next_row, pltpu.roll(y, rows - 1, axis=0))
    up = jnp.where(pos == 0, 0.0, up)
    dn = jnp.where(pos == seq_last, 0.0, dn)
    c = up * w[0:1, :] + y * w[1:2, :] + dn * w[2:3, :] + b
    return c * jax.nn.sigmoid(c)


def _in_proj_kernel(n_x, n_buf, segs, outs, *refs):
    x_refs = refs[:n_x]
    prev_ref, next_ref, mod_ref, g_ref, w_ref, wt_ref, cw_ref, cb_ref = refs[n_x:n_x + 8]
    o_refs = refs[n_x + 8 + n_buf:]
    mod, g = mod_ref[...], g_ref[...]
    h = _modulated(_tile_x(x_refs, TM), g, mod, 0).astype(bf16)
    ys = []
    for a, b, scale, conv in segs:
        w = wt_ref[...] if a is None else w_ref[:, a:b]
        if conv:
            halo = jnp.concatenate([prev_ref[...], next_ref[...]], axis=0)
            hh = _modulated(halo, g, mod, 0).astype(bf16)
            y = jnp.dot(jnp.concatenate([h, hh], axis=0), w, preferred_element_type=f32)
            y = _conv_silu_tile(y[:TM], y[TM + SUBLANES - 1:TM + SUBLANES], y[TM + SUBLANES:TM + SUBLANES + 1],
                                cw_ref[...], cb_ref[...])
        else:
            y = jnp.dot(h, w, preferred_element_type=f32)
        ys.append(y if scale == 1.0 else y * scale)
    for (si, _, head_major), o_ref in zip(outs, o_refs):
        if head_major:
            @pl.when(pl.program_id(0) < N_PROMPT // TM)
            def _(o_ref=o_ref, si=si):
                for b in range(TM // SEQ):
                    for hd in range(NA_HEADS):
                        o_ref[b, hd] = ys[si][b * SEQ:(b + 1) * SEQ, hd * NA_HEAD_DIM:(hd + 1) * NA_HEAD_DIM]
        else:
            o_ref[...] = ys[si].astype(o_ref.dtype)


def _x_specs(xs, tm):
    if len(xs) == 1:
        return [pl.BlockSpec((tm, D_MODEL), lambda i: (i, 0))]
    n_p = N_PROMPT // tm
    return [pl.BlockSpec((tm, D_MODEL), lambda i: (jnp.minimum(i, n_p - 1), 0)),
            pl.BlockSpec((tm, D_MODEL), lambda i: (jnp.maximum(i - n_p, 0), 0))]


def in_proj(xs, mod_l, g, w, w_tail, l, segs, outs, conv_w, conv_b, cache_bufs=None):
    grid = (N_TOK // TM,)
    last_prompt = N_PROMPT // TM - 1
    halo_src = xs[-1]
    per = TM // SUBLANES
    blk0 = (halo_src.shape[0] - DEC_BATCH * DEC_SEQ) // SUBLANES
    n_blk = halo_src.shape[0] // SUBLANES
    tile0 = N_PROMPT // TM

    def prev_map(i):
        return (jnp.clip(blk0 + (i - tile0) * per - 1, 0, n_blk - 1), 0)

    def next_map(i):
        return (jnp.clip(blk0 + (i - tile0 + 1) * per, 0, n_blk - 1), 0)

    ch = conv_w.shape[1]
    out_shape, out_specs = [], []
    for si, dt, head_major in outs:
        width = LANES if segs[si][0] is None else segs[si][1] - segs[si][0]
        if head_major:
            out_shape.append(jax.ShapeDtypeStruct((BATCH, N_EVEN, NA_HEADS, SEQ, NA_HEAD_DIM), dt))
            out_specs.append(pl.BlockSpec((TM // SEQ, None, NA_HEADS, SEQ, NA_HEAD_DIM),
                                          lambda i: (jnp.minimum(i, last_prompt), l, 0, 0, 0)))
        else:
            out_shape.append(jax.ShapeDtypeStruct((N_TOK, width), dt))
            out_specs.append(pl.BlockSpec((TM, width), lambda i: (i, 0)))
    args = [*xs, halo_src, halo_src, mod_l, g.reshape(1, D_MODEL), w, w_tail, conv_w, conv_b.reshape(1, ch)]
    aliases = {}
    if cache_bufs is not None:
        head_major_outs = [k for k, o in enumerate(outs) if o[2]]
        for k_out, buf in zip(head_major_outs, cache_bufs):
            aliases[len(args)] = k_out
            args.append(buf)
    return pl.pallas_call(
        functools.partial(_in_proj_kernel, len(xs), len(aliases), segs, outs),
        input_output_aliases=aliases,
        grid=grid,
        in_specs=_x_specs(xs, TM) + [
            pl.BlockSpec((SUBLANES, D_MODEL), prev_map),
            pl.BlockSpec((SUBLANES, D_MODEL), next_map),
            pl.BlockSpec((None, 1, N_MOD * D_MODEL), lambda i: (_cond_row(i, TM), 0, 0)),
            _const_spec((1, D_MODEL)),
            _layer_spec(w, l),
            _layer_spec(w_tail, l),
            _const_spec((CONV_K, ch)),
            _const_spec((1, ch)),
        ] + [pl.BlockSpec(memory_space=pl.ANY)] * len(aliases),
        out_specs=out_specs,
        out_shape=out_shape,
        compiler_params=pltpu.CompilerParams(
            dimension_semantics=("arbitrary",), vmem_limit_bytes=VMEM_LIMIT),
        name="in_proj",
    )(*args)


def _out_mlp_kernel(n_x, n_mix, final, *refs):
    x_refs = refs[:n_x]
    mod_ref, g_ref = refs[n_x:n_x + 2]
    refs = refs[n_x + 2:]
    mix_refs = refs[:n_mix]
    wo_ref, w1_ref, w2_ref = refs[n_mix:n_mix + 3]
    rest = refs[n_mix + 3:]
    if final:
        gf_ref, op_ref, os_ref = rest
    else:
        (o_ref,) = rest
    mod = mod_ref[...]
    m = None
    k0 = 0
    for r in mix_refs:
        kw = r.shape[-1]
        part = jnp.dot(r[...].astype(bf16), wo_ref[k0:k0 + kw, :], preferred_element_type=f32)
        m = part if m is None else m + part
        k0 += kw
    x1 = _tile_x(x_refs, OUT_TM) + mod[:, 2 * D_MODEL:3 * D_MODEL] * m
    h2 = _modulated(x1, g_ref[...], mod, 3).astype(bf16)
    u = jnp.dot(h2, w1_ref[...], preferred_element_type=f32)
    a = jnp.square(jnp.maximum(u, 0.0)).astype(bf16)
    x2 = x1 + mod[:, 5 * D_MODEL:6 * D_MODEL] * jnp.dot(a, w2_ref[...], preferred_element_type=f32)
    if not final:
        o_ref[...] = x2
        return
    y = _rms(x2) * gf_ref[...]
    is_prompt = pl.program_id(0) < N_PROMPT // OUT_TM

    @pl.when(is_prompt)
    def _():
        op_ref[...] = y

    @pl.when(jnp.logical_not(is_prompt))
    def _():
        os_ref[...] = y


OUT_TM = 512


def out_mlp(xs, mod_l, g_ffn, mixes, w_out, l_out, w1, w2, l, norm_f=None):
    final = norm_f is not None
    tm = OUT_TM
    grid = (N_TOK // tm,)
    in_specs = _x_specs(xs, tm) + [
        pl.BlockSpec((None, 1, N_MOD * D_MODEL), lambda i: (_cond_row(i, tm), 0, 0)),
        _const_spec((1, D_MODEL)),
    ]
    in_specs += [pl.BlockSpec((tm, m.shape[-1]), lambda i: (i, 0)) for m in mixes]
    in_specs += [_layer_spec(w_out, l_out), _layer_spec(w1, l), _layer_spec(w2, l)]
    args = [*xs, mod_l, g_ffn.reshape(1, D_MODEL), *mixes, w_out, w1, w2]
    out_specs = pl.BlockSpec((tm, D_MODEL), lambda i: (i, 0))
    out_shape = jax.ShapeDtypeStruct((N_TOK, D_MODEL), f32)
    if final:
        in_specs.append(_const_spec((1, D_MODEL)))
        args.append(norm_f.reshape(1, D_MODEL))
        n_p = N_PROMPT // tm
        out_specs = [pl.BlockSpec((tm, D_MODEL), lambda i: (jnp.minimum(i, n_p - 1), 0)),
                     pl.BlockSpec((tm, D_MODEL), lambda i: (jnp.maximum(i - n_p, 0), 0))]
        out_shape = [jax.ShapeDtypeStruct((N_PROMPT, D_MODEL), f32),
                     jax.ShapeDtypeStruct((N_TOK - N_PROMPT, D_MODEL), f32)]
    return pl.pallas_call(
        functools.partial(_out_mlp_kernel, len(xs), len(mixes), final),
        grid=grid,
        in_specs=in_specs,
        out_specs=out_specs,
        out_shape=out_shape,
        compiler_params=pltpu.CompilerParams(
            dimension_semantics=("arbitrary",), vmem_limit_bytes=VMEM_LIMIT),
        name="out_mlp",
    )(*args)


NA_PAIRS = NA_HEADS // 2
NA_ROWS = DEC_SEQ // GRID_W
NA_WIN = NA_WIN_ROWS * GRID_W


def _dot_nt(a, b):
    return lax.dot_general(a, b, (((1,), (1,)), ((), ())), preferred_element_type=f32)


def _pair_stack(x):
    lane = lax.broadcasted_iota(jnp.int32, x.shape, 1)
    zero = jnp.zeros_like(x)
    return jnp.concatenate([jnp.where(lane < NA_HEAD_DIM, x, zero), jnp.where(lane >= NA_HEAD_DIM, x, zero)], axis=0)


def _pair_unstack(o):
    n = o.shape[0] // 2
    lane = lax.broadcasted_iota(jnp.int32, (n, LANES), 1)
    return jnp.where(lane < NA_HEAD_DIM, o[:n], o[n:])


def _softmax_pv(scores, values):
    def lane_tiles(blocks):
        return [b[:, k * LANES:(k + 1) * LANES] for b in blocks for k in range(b.shape[1] // LANES)]

    ms = [jnp.max(functools.reduce(jnp.maximum, lane_tiles(s)), axis=1, keepdims=True) for s in scores]
    ps = [[jnp.exp(b - m) for b in s] for s, m in zip(scores, ms)]
    invs = [1.0 / jnp.sum(functools.reduce(jnp.add, lane_tiles(p)), axis=1, keepdims=True) for p in ps]
    pn = [[(b * inv).astype(bf16) for b in p] for p, inv in zip(ps, invs)]
    return [functools.reduce(jnp.add, [jnp.dot(b, v, preferred_element_type=f32) for b, v in zip(p, vs)])
            for p, vs in zip(pn, values)]


def _ctx_attn_kernel(q_ref, k_ref, v_ref, o_ref):
    pairs = [slice(p * LANES, (p + 1) * LANES) for p in range(NA_PAIRS)]
    scores = [[_dot_nt(_pair_stack(q_ref[:, lanes]), k_ref[:, lanes])] for lanes in pairs]
    outs = _softmax_pv(scores, [[v_ref[:, lanes]] for lanes in pairs])
    for lanes, o in zip(pairs, outs):
        o_ref[:, lanes] = _pair_unstack(o).astype(o_ref.dtype)


def ctx_attention(q, k, v):
    spec = pl.BlockSpec((SEQ, NA_WIDTH), lambda b: (b, 0))
    return pl.pallas_call(
        _ctx_attn_kernel,
        grid=(BATCH,),
        in_specs=[spec, spec, spec],
        out_specs=spec,
        out_shape=jax.ShapeDtypeStruct((N_TOK, NA_WIDTH), bf16),
        compiler_params=pltpu.CompilerParams(dimension_semantics=("arbitrary",), vmem_limit_bytes=VMEM_LIMIT),
        name="ctx_attn",
    )(q, k, v)


def _na_bias_kernel(rpb_ref, o_ref):
    pair = pl.program_id(0)
    shape = (GRID_W, LANES)
    qc = lax.broadcasted_iota(jnp.int32, shape, 0)
    lane = lax.broadcasted_iota(jnp.int32, shape, 1)
    kc = lane & (GRID_W - 1)
    low = lane < GRID_W
    col_start = jnp.clip(qc - NA_WIN_COLS // 2, 0, GRID_W - NA_WIN_COLS)
    valid = (kc >= col_start) & (kc < col_start + NA_WIN_COLS)
    rel_c = jnp.clip(kc - qc + NA_WIN_COLS - 1, 0, NA_RPB_COLS - 1)
    for e in range(2):
        base = (2 * pair + e) * NA_RPB_ROWS
        pieces = []
        for rr in range(NA_RPB_ROWS - 1):
            val = jnp.zeros(shape, f32)
            for t in range(NA_RPB_COLS):
                s_lo = rpb_ref[(base + rr) * NA_RPB_COLS + t]
                s_hi = rpb_ref[(base + rr + 1) * NA_RPB_COLS + t]
                val = jnp.where(rel_c == t, jnp.where(low, s_lo, s_hi), val)
            pieces.append(jnp.where(valid, val, -jnp.inf))
        for d in range(NA_WIN_ROWS):
            for i in range(0, NA_WIN_ROWS, 2):
                o_ref[d, e * GRID_W:(e + 1) * GRID_W, i * GRID_W:(i + 2) * GRID_W] = pieces[d + i]


def na_bias_table(rpb):
    return pl.pallas_call(
        _na_bias_kernel,
        grid=(NA_PAIRS,),
        in_specs=[pl.BlockSpec(memory_space=pltpu.SMEM)],
        out_specs=pl.BlockSpec((None, NA_WIN_ROWS, 2 * GRID_W, NA_WIN), lambda p: (p, 0, 0, 0)),
        out_shape=jax.ShapeDtypeStruct((NA_PAIRS, NA_WIN_ROWS, 2 * GRID_W, NA_WIN), f32),
        compiler_params=pltpu.CompilerParams(dimension_semantics=("arbitrary",), vmem_limit_bytes=VMEM_LIMIT),
        name="na_bias",
    )(rpb.reshape(-1))


NA_STEP_ROWS = 8


def _na_first_key_row(r):
    return jnp.clip(r - NA_WIN_ROWS // 2, 0, NA_ROWS - NA_WIN_ROWS)


def _na_kernel(q_ref, k_ref, v_ref, kc_ref, vc_ref, bias_ref, buf_ref, o_ref):
    del buf_ref
    where, scores, values = [], [], []
    for j in range(NA_STEP_ROWS):
        r = pl.program_id(1) * NA_STEP_ROWS + j
        first = _na_first_key_row(r)
        start = pl.multiple_of(first * GRID_W, GRID_W)
        shift = first - r + NA_WIN_ROWS - 1
        rows = slice(j * GRID_W, (j + 1) * GRID_W)
        for p in range(NA_PAIRS):
            lanes = slice(p * LANES, (p + 1) * LANES)
            qq = _pair_stack(q_ref[rows, lanes])
            where.append((rows, lanes))
            scores.append([_dot_nt(qq, k_ref[pl.ds(start, NA_WIN), lanes]) + bias_ref[p, shift],
                           _dot_nt(qq, kc_ref[:, lanes])])
            values.append([v_ref[pl.ds(start, NA_WIN), lanes], vc_ref[:, lanes]])
    for (rows, lanes), o in zip(where, _softmax_pv(scores, values)):
        o_ref[rows, lanes] = _pair_unstack(o).astype(o_ref.dtype)


def na_latent(q, k, v, k_ctx, v_ctx, bias, buf):
    rows = NA_STEP_ROWS * GRID_W
    steps = NA_ROWS // NA_STEP_ROWS
    row0 = N_PROMPT // rows
    seq0 = N_PROMPT // DEC_SEQ
    kv_spec = pl.BlockSpec((DEC_SEQ, NA_WIDTH), lambda b, r: (seq0 + b, 0))
    ctx_spec = pl.BlockSpec((None, PAST_LEN, NA_WIDTH), lambda b, r: (b, 0, 0))
    return pl.pallas_call(
        _na_kernel,
        grid=(DEC_BATCH, steps),
        in_specs=[
            pl.BlockSpec((rows, NA_WIDTH), lambda b, r: (row0 + b * steps + r, 0)),
            kv_spec, kv_spec, ctx_spec, ctx_spec,
            pl.BlockSpec(bias.shape, lambda b, r: (0, 0, 0, 0), pipeline_mode=pl.Buffered(1)),
            pl.BlockSpec(memory_space=pl.ANY),
        ],
        out_specs=pl.BlockSpec((rows, NA_WIDTH), lambda b, r: (row0 + b * steps + r, 0)),
        out_shape=jax.ShapeDtypeStruct((N_TOK, NA_WIDTH), bf16),
        input_output_aliases={6: 0},
        compiler_params=pltpu.CompilerParams(
            dimension_semantics=("arbitrary", "arbitrary"), vmem_limit_bytes=VMEM_LIMIT),
        name="na_latent",
    )(q, k, v, k_ctx, v_ctx, bias, buf)


CHUNK = 128
SCAN_G_MAX = 4


def _cumsum_rows(a, reverse=False):
    row = lax.broadcasted_iota(jnp.int32, a.shape, 0)
    s = 1
    while s < CHUNK:
        if reverse:
            a = a + jnp.where(row < CHUNK - s, pltpu.roll(a, CHUNK - s, axis=0), 0.0)
        else:
            a = a + jnp.where(row >= s, pltpu.roll(a, s, axis=0), 0.0)
        s *= 2
    return a


def _ssd_kernel(nc, grp, has_h0, n_buf, emit_state, *refs):
    xbc_ref, dt_ref, z_ref = refs[:3]
    refs = refs[3:]
    if has_h0:
        h0_ref, refs = refs[0], refs[1:]
    dtb_ref, alog_ref, dskip_ref, g_ref = refs[:4]
    refs = refs[4 + n_buf:]
    y_ref, refs = refs[0], refs[1:]
    if emit_state:
        hfin_ref, refs = refs[0], refs[1:]
    hb_store, xt_store, gate_store, carry = refs

    phase = pl.program_id(1)
    gi = pl.program_id(2)
    nb = nc // grp
    gw = SSD_RPG * SSD_HEAD_DIM

    def load_h0(d):
        if has_h0:
            return h0_ref[d].reshape(SSD_INNER, SSD_STATE)
        return jnp.zeros((SSD_INNER, SSD_STATE), f32)

    def head_rows(v):
        return jnp.concatenate(
            [jnp.broadcast_to(v[h:h + 1, :], (SSD_HEAD_DIM, v.shape[1])) for h in range(SSD_HEADS)], axis=0)

    def state_update(x_t, b_bf, cum_t, dt_t, edge):
        at_edge = jnp.broadcast_to(cum_t[:, edge:edge + 1], cum_t.shape)
        w_end = jnp.exp(at_edge - cum_t) * dt_t
        xw = (x_t * head_rows(w_end)).astype(bf16)
        upd = jnp.concatenate(
            [jnp.dot(xw[g * gw:(g + 1) * gw], b_bf[:, g * SSD_STATE:(g + 1) * SSD_STATE], preferred_element_type=f32)
             for g in range(SSD_GROUPS)], axis=0)
        carry[...] = carry[...] * head_rows(jnp.exp(at_edge)) + upd

    @pl.when(jnp.logical_and(phase == 0, gi == 0))
    def _():
        carry[...] = load_h0(1)

    def backward_chunk(sub):
        rows = slice(sub * CHUNK, (sub + 1) * CHUNK)
        j = grp * (nb - 1 - gi) + sub
        x = xbc_ref[rows, :SSD_INNER]
        b_bf = xbc_ref[rows, SSD_INNER:SSD_INNER + SSD_GN].astype(bf16)
        dt = jax.nn.softplus(dt_ref[rows, :] + dtb_ref[...])
        a = dt * (-jnp.exp(alog_ref[...]))
        x_t = jnp.concatenate([x[:, k * LANES:(k + 1) * LANES].T for k in range(SSD_INNER // LANES)], axis=0)
        rcum = _cumsum_rows(a, reverse=True)
        xt_store[j] = x_t
        gate_store[j, 0] = dt
        gate_store[j, 1] = _cumsum_rows(a)
        gate_store[j, 2] = rcum
        hb_store[j] = carry[...].astype(bf16)
        state_update(x_t, b_bf, rcum.T[SSD_HEADS:2 * SSD_HEADS], dt.T[SSD_HEADS:2 * SSD_HEADS], 0)

    @pl.when(phase == 0)
    def _backward_states():
        for sub in reversed(range(grp)):
            backward_chunk(sub)

        if emit_state:
            @pl.when(gi == nb - 1)
            def _():
                hfin_ref[1] = carry[...].reshape(SSD_HEADS, SSD_HEAD_DIM, SSD_STATE)

    @pl.when(jnp.logical_and(phase == 1, gi == 0))
    def _():
        carry[...] = load_h0(0)

    def forward_chunk(sub):
        rows = slice(sub * CHUNK, (sub + 1) * CHUNK)
        c = grp * gi + sub
        x = xbc_ref[rows, :SSD_INNER]
        x_bf = x.astype(bf16)
        b_bf = xbc_ref[rows, SSD_INNER:SSD_INNER + SSD_GN].astype(bf16)
        c_mat = xbc_ref[rows, SSD_INNER + SSD_GN:].astype(bf16)
        dt, cum, rcum = gate_store[c, 0], gate_store[c, 1], gate_store[c, 2]
        cum_t, rcum_t, dt_t = cum.T, rcum.T, dt.T
        row = lax.broadcasted_iota(jnp.int32, (CHUNK, CHUNK), 0)
        col = lax.broadcasted_iota(jnp.int32, (CHUNK, CHUNK), 1)
        causal = col <= row
        anti = col >= row
        lane = lax.broadcasted_iota(jnp.int32, (CHUNK, LANES), 1)
        hf = carry[...].astype(bf16)
        hb = hb_store[c]
        cb = [_dot_nt(c_mat[:, g * SSD_STATE:(g + 1) * SSD_STATE], b_bf[:, g * SSD_STATE:(g + 1) * SSD_STATE])
              for g in range(SSD_GROUPS)]

        heads = range(SSD_HEADS)
        seg_f = [jnp.where(causal, cum[:, h:h + 1] - cum_t[h:h + 1, :], -jnp.inf) for h in heads]
        seg_b = [jnp.where(anti, rcum[:, SSD_HEADS + h:SSD_HEADS + h + 1]
                           - rcum_t[SSD_HEADS + h:SSD_HEADS + h + 1, :], -jnp.inf) for h in heads]
        e_f = [jnp.exp(s) for s in seg_f]
        e_b = [jnp.exp(s) for s in seg_b]
        ws = [(cb[h // SSD_RPG] * (e_f[h] * dt_t[h:h + 1, :] + e_b[h] * dt_t[SSD_HEADS + h:SSD_HEADS + h + 1, :])
               ).astype(bf16) for h in heads]
        rhs = []
        for p in range(SSD_HEADS // 2):
            xp = x_bf[:, p * LANES:(p + 1) * LANES]
            zero = jnp.zeros_like(xp)
            rhs.append(jnp.concatenate([jnp.where(lane < SSD_HEAD_DIM, xp, zero),
                                        jnp.where(lane >= SSD_HEAD_DIM, xp, zero)], axis=0))
        y = jnp.concatenate(
            [jnp.dot(jnp.concatenate([ws[2 * p], ws[2 * p + 1]], axis=1), rhs[p], preferred_element_type=f32)
             for p in range(SSD_HEADS // 2)], axis=1)

        def inter_t(h_all):
            return jnp.concatenate(
                [_dot_nt(h_all[g * gw:(g + 1) * gw], c_mat[:, g * SSD_STATE:(g + 1) * SSD_STATE])
                 for g in range(SSD_GROUPS)], axis=0)

        cum_f, cum_b = cum_t[:SSD_HEADS], rcum_t[SSD_HEADS:2 * SSD_HEADS]
        y_t = inter_t(hf) * head_rows(jnp.exp(cum_f)) + inter_t(hb) * head_rows(jnp.exp(cum_b))
        y_inter = jnp.concatenate([y_t[k * LANES:(k + 1) * LANES].T for k in range(SSD_INNER // LANES)], axis=1)
        state_update(xt_store[c], b_bf, cum_f, dt_t[:SSD_HEADS], CHUNK - 1)
        y = y + y_inter + dskip_ref[...] * x

        zv = z_ref[rows, :]
        yz = y * (zv * jax.nn.sigmoid(zv))
        y_ref[rows, :] = (_rms(yz) * g_ref[...]).astype(y_ref.dtype)

    @pl.when(phase == 1)
    def _forward_and_outputs():
        for sub in range(grp):
            forward_chunk(sub)

        if emit_state:
            @pl.when(gi == nb - 1)
            def _():
                hfin_ref[0] = carry[...].reshape(SSD_HEADS, SSD_HEAD_DIM, SSD_STATE)


def ssd_mix(xbc, dt, z, h0, row0, n_seq, seq_len, dt_bias, a_log, d_skip, norm_g, emit_state, out_buf=None,
            layer=0, state_buf=None):
    nc = seq_len // CHUNK
    grp = math.gcd(nc, SCAN_G_MAX)
    blk_rows = grp * CHUNK
    nb = seq_len // blk_rows
    blk0 = row0 // blk_rows
    has_h0 = h0 is not None

    def block_map(s, p, c):
        return (blk0 + s * nb + jnp.where(p == 0, nb - 1 - c, c), 0)

    state_spec = pl.BlockSpec((None, 2, SSD_HEADS, SSD_HEAD_DIM, SSD_STATE), lambda s, p, c: (s, 0, 0, 0, 0))
    vec = lambda n: pl.BlockSpec((1, n), lambda s, p, c: (0, 0))
    in_specs = [
        pl.BlockSpec((blk_rows, SSD_CONV_DIM), block_map),
        pl.BlockSpec((blk_rows, LANES), block_map),
        pl.BlockSpec((blk_rows, SSD_INNER), lambda s, p, c: (blk0 + s * nb + p * c, 0)),
    ]
    args = [xbc, dt, z]
    if has_h0:
        in_specs.append(state_spec)
        args.append(h0)
    in_specs += [vec(LANES), vec(LANES), vec(SSD_INNER), vec(SSD_INNER)]
    pad = lambda t: jnp.concatenate([t.reshape(1, -1), jnp.zeros((1, LANES - t.size), f32)], axis=1)
    args += [pad(dt_bias), pad(a_log), jnp.repeat(d_skip, SSD_HEAD_DIM).reshape(1, SSD_INNER),
             norm_g.reshape(1, SSD_INNER)]
    aliases = {}
    if out_buf is not None:
        aliases = {len(args): 0}
        in_specs.append(pl.BlockSpec(memory_space=pl.ANY))
        args.append(out_buf)
    out_shape = [jax.ShapeDtypeStruct((N_TOK, SSD_INNER), bf16)]
    out_specs = [pl.BlockSpec((blk_rows, SSD_INNER), lambda s, p, c: (blk0 + s * nb + p * c, 0))]
    if emit_state:
        if state_buf is not None:
            aliases[len(args)] = 1
            in_specs.append(pl.BlockSpec(memory_space=pl.ANY))
            args.append(state_buf)
        out_shape.append(jax.ShapeDtypeStruct((n_seq, N_EVEN, 2, SSD_HEADS, SSD_HEAD_DIM, SSD_STATE), f32))
        out_specs.append(pl.BlockSpec((None, None, 2, SSD_HEADS, SSD_HEAD_DIM, SSD_STATE),
                                      lambda s, p, c: (s, layer, 0, 0, 0, 0)))
    res = pl.pallas_call(
        functools.partial(_ssd_kernel, nc, grp, has_h0, len(aliases), emit_state),
        input_output_aliases=aliases,
        grid=(n_seq, 2, nb),
        in_specs=in_specs,
        out_specs=out_specs,
        out_shape=out_shape,
        scratch_shapes=[pltpu.VMEM((nc, SSD_INNER, SSD_STATE), bf16), pltpu.VMEM((nc, SSD_INNER, CHUNK), f32),
                        pltpu.VMEM((nc, 3, CHUNK, LANES), f32), pltpu.VMEM((SSD_INNER, SSD_STATE), f32)],
        compiler_params=pltpu.CompilerParams(
            dimension_semantics=("arbitrary", "arbitrary", "arbitrary"), vmem_limit_bytes=VMEM_LIMIT),
        name="ssd_scan",
    )(*args)
    return res if emit_state else res[0]


ML_DIRS = 2 * ML_HEADS


def _cummax_rows(a, reverse=False):
    row = lax.broadcasted_iota(jnp.int32, a.shape, 0)
    s = 1
    while s < CHUNK:
        if reverse:
            a = jnp.maximum(a, jnp.where(row < CHUNK - s, pltpu.roll(a, CHUNK - s, axis=0), -jnp.inf))
        else:
            a = jnp.maximum(a, jnp.where(row >= s, pltpu.roll(a, s, axis=0), -jnp.inf))
        s *= 2
    return a


ML_ST = ML_V_DIM + 16


def _mlstm_t_kernel(nc, grp, has_state, n_buf, emit_state, *refs):
    qk_ref, v_ref, og_ref, gates_ref, gb_ref = refs[:5]
    refs = refs[5:]
    if has_state:
        s0_ref, m0_ref = refs[:2]
        refs = refs[2:]
    refs = refs[n_buf:]
    y_ref, refs = refs[0], refs[1:]
    if emit_state:
        cfin_ref, nfin_ref, mfin_ref = refs[:3]
        refs = refs[3:]
    s_store, m_store, gate_store, k_store, v_store, s_carry, m_carry = refs

    phase = pl.program_id(1)
    gi = pl.program_id(2)
    nb = nc // grp
    fwd_row = lax.broadcasted_iota(jnp.int32, (1, LANES), 1) < ML_HEADS

    def state_update(cum, r, k_own, v_tr, m_row, big_m, d):
        edge = CHUNK - 1 if d == 0 else 0
        m_edge = big_m[edge:edge + 1, :]
        wc_row = jnp.exp(m_row - m_edge)
        wk_t = jnp.exp(r.T - big_m.T[:, edge:edge + 1])
        lhs = []
        for h in range(ML_HEADS):
            wk = wk_t[d * ML_HEADS + h:d * ML_HEADS + h + 1, :]
            lhs.append(jnp.concatenate([v_tr[h] * wk, jnp.broadcast_to(wk, (ML_ST - ML_V_DIM, CHUNK))],
                                       axis=0).astype(bf16))
        upd = [jnp.dot(lhs[h], k_own[h], preferred_element_type=f32) for h in range(ML_HEADS)]
        for h in range(ML_HEADS):
            cl = d * ML_HEADS + h
            wc = jnp.broadcast_to(wc_row[:, cl:cl + 1], (ML_ST, LANES))
            s_carry[d, h] = wc * s_carry[d, h] + upd[h]
        return cum[edge:edge + 1, :] + m_edge

    def init_state(d):
        if has_state:
            s_carry[d] = s0_ref[d]
        else:
            s_carry[d] = jnp.zeros((ML_HEADS, ML_ST, LANES), f32)

    def emit_final(d):
        for h in range(ML_HEADS):
            tile = s_carry[d, h]
            if h % 2 == 1:
                tile = pltpu.roll(tile, ML_QK_DIM, axis=1)
            cfin_ref[d, h] = tile[:ML_V_DIM, :ML_QK_DIM]
            nfin_ref[d, h:h + 1, :] = tile[ML_V_DIM:ML_V_DIM + 1, :ML_QK_DIM]

    @pl.when(jnp.logical_and(phase == 0, gi == 0))
    def _():
        init_state(1)
        m_carry[...] = jnp.broadcast_to(m0_ref[...], m_carry.shape) if has_state else jnp.zeros(m_carry.shape, f32)

    def backward_chunk(sub):
        rows = slice(sub * CHUNK, (sub + 1) * CHUNK)
        lane = lax.broadcasted_iota(jnp.int32, (CHUNK, LANES), 1)
        fwd_lane = lane < ML_HEADS
        g = gates_ref[rows, :] + gb_ref[...]
        lf = pltpu.roll(jax.nn.log_sigmoid(g), LANES - ML_DIRS, axis=1)
        cum = jnp.where(fwd_lane, _cumsum_rows(lf), _cumsum_rows(lf, reverse=True))
        r = g - cum
        pm = jnp.where(fwd_lane, _cummax_rows(r), _cummax_rows(r, reverse=True))
        k = qk_ref[rows, ML_QK_WIDTH:]
        k_own = [jnp.where((lane < ML_QK_DIM) if h % 2 == 0 else (lane >= ML_QK_DIM),
                           k[:, (h // 2) * LANES:(h // 2 + 1) * LANES], 0.0).astype(bf16) for h in range(ML_HEADS)]
        v_tr = [v_ref[rows, h * ML_V_DIM:(h + 1) * ML_V_DIM].astype(f32).T for h in range(ML_HEADS)]

        j = grp * (nb - 1 - gi) + sub
        gate_store[j, 0] = cum
        gate_store[j, 1] = r
        gate_store[j, 2] = pm
        for h in range(ML_HEADS):
            k_store[j, h] = k_own[h]
            v_store[j, h] = v_tr[h].astype(bf16)
        m_row = m_carry[0:1, :]
        s_store[j] = s_carry[1].astype(bf16)
        m_store[j] = m_carry[...]
        big_m = jnp.maximum(m_row, pm)
        m_new = state_update(cum, r, k_own, v_tr, m_row, big_m, 1)
        m_carry[...] = jnp.broadcast_to(jnp.where(fwd_row, m_row, m_new), m_carry.shape)

    @pl.when(phase == 0)
    def _backward_states():
        for sub in reversed(range(grp)):
            backward_chunk(sub)

    @pl.when(jnp.logical_and(phase == 1, gi == 0))
    def _():
        init_state(0)
        if emit_state:
            emit_final(1)

    def forward_chunk(sub):
        rows = slice(sub * CHUNK, (sub + 1) * CHUNK)
        c = grp * gi + sub
        cum, r, pm = gate_store[c, 0], gate_store[c, 1], gate_store[c, 2]
        k_own = [k_store[c, h] for h in range(ML_HEADS)]
        v_th = [v_store[c, h] for h in range(ML_HEADS)]
        m_both = m_carry[0:1, :]
        m_row = jnp.where(fwd_row, m_both, m_store[c][0:1, :])
        big_m = jnp.maximum(m_row, pm)
        big_m_t = big_m.T
        w_inter_t = jnp.exp(m_row - big_m).T
        floor_t = jnp.exp(-(cum + big_m)).T
        key = lax.broadcasted_iota(jnp.int32, (CHUNK, CHUNK), 0)
        qry = lax.broadcasted_iota(jnp.int32, (CHUNK, CHUNK), 1)
        masks = (key <= qry, key >= qry)
        q = (qk_ref[rows, :ML_QK_WIDTH] * (ML_QK_DIM ** -0.5)).astype(bf16)
        heads = range(ML_HEADS)
        hd_pairs = [(h, d) for h in heads for d in range(2)]
        q_pair = [q[:, (h // 2) * LANES:(h // 2 + 1) * LANES] for h in heads]
        s_raw_t = [_dot_nt(k_own[h], q_pair[h]) for h in heads]
        inter = {(h, d): _dot_nt(s_carry[0, h].astype(bf16) if d == 0 else s_store[c, h], q_pair[h])
                 for h, d in hd_pairs}
        w_t = {(h, d): jnp.exp(jnp.where(masks[d], r[:, d * ML_HEADS + h:d * ML_HEADS + h + 1]
                                         - big_m_t[d * ML_HEADS + h:d * ML_HEADS + h + 1, :], -jnp.inf))
               for h, d in hd_pairs}
        sw_t = {hd: s_raw_t[hd[0]] * w_t[hd] for hd in hd_pairs}
        num = {hd: jnp.dot(v_th[hd[0]], sw_t[hd].astype(bf16), preferred_element_type=f32) for hd in hd_pairs}
        wi = {(h, d): w_inter_t[d * ML_HEADS + h:d * ML_HEADS + h + 1, :] for h, d in hd_pairs}
        den = {hd: jnp.sum(sw_t[hd], axis=0, keepdims=True) + wi[hd] * inter[hd][ML_V_DIM:ML_V_DIM + 1]
               for hd in hd_pairs}
        inv = {(h, d): 1.0 / jnp.maximum(jnp.abs(den[h, d]), floor_t[d * ML_HEADS + h:d * ML_HEADS + h + 1, :])
               for h, d in hd_pairs}
        part = {hd: (num[hd] + wi[hd] * inter[hd][:ML_V_DIM]) * inv[hd] for hd in hd_pairs}
        out = [(part[h, 0] + part[h, 1]).T for h in heads]
        gate = [jax.nn.sigmoid(og_ref[rows, h * ML_V_DIM:(h + 1) * ML_V_DIM]) for h in heads]
        for h in heads:
            y_ref[rows, h * ML_V_DIM:(h + 1) * ML_V_DIM] = (out[h] * gate[h]).astype(y_ref.dtype)

        v_tr = [v.astype(f32) for v in v_th]
        m_fin = jnp.where(fwd_row, state_update(cum, r, k_own, v_tr, m_row, big_m, 0), m_both)
        m_carry[...] = jnp.broadcast_to(m_fin, m_carry.shape)

        if emit_state and sub == grp - 1:
            @pl.when(gi == nb - 1)
            def _():
                emit_final(0)
                mfin_ref[...] = m_fin

    @pl.when(phase == 1)
    def _forward_and_outputs():
        for sub in range(grp):
            forward_chunk(sub)


def mlstm_mix(qk, v, og, gates, gate_b, state, row0, n_seq, seq_len, emit_state, out_buf=None,
              layer=0, state_bufs=None):
    nc = seq_len // CHUNK
    grp = math.gcd(nc, SCAN_G_MAX)
    blk_rows = grp * CHUNK
    nb = seq_len // blk_rows
    blk0 = row0 // blk_rows
    has_state = state is not None

    def block_map(s, p, c):
        return (blk0 + s * nb + jnp.where(p == 0, nb - 1 - c, c), 0)

    m_spec = pl.BlockSpec((None, 1, LANES), lambda s, p, c: (s, 0, 0))
    in_specs = [
        pl.BlockSpec((blk_rows, 2 * ML_QK_WIDTH), block_map),
        pl.BlockSpec((blk_rows, ML_V_WIDTH), block_map),
        pl.BlockSpec((blk_rows, ML_V_WIDTH), lambda s, p, c: (blk0 + s * nb + p * c, 0)),
        pl.BlockSpec((blk_rows, LANES), block_map),
        pl.BlockSpec((1, LANES), lambda s, p, c: (0, 0)),
    ]
    gb = jnp.concatenate([gate_b.reshape(1, 2 * ML_DIRS), jnp.zeros((1, LANES - 2 * ML_DIRS), f32)], axis=1)
    args = [qk, v, og, gates, gb]
    if has_state:
        c0, n0, m0 = state
        rows = jnp.concatenate([c0, jnp.broadcast_to(n0[..., None, :], n0.shape[:-1] + (ML_ST - ML_V_DIM, ML_QK_DIM))],
                               axis=-2)
        zeros = jnp.zeros_like(rows)
        odd = (jnp.arange(ML_HEADS) % 2 == 1)[:, None, None]
        s0 = jnp.where(odd, jnp.concatenate([zeros, rows], axis=-1), jnp.concatenate([rows, zeros], axis=-1))
        m0 = jnp.concatenate([m0.reshape(n_seq, 1, ML_DIRS), jnp.zeros((n_seq, 1, LANES - ML_DIRS), f32)], axis=-1)
        in_specs += [pl.BlockSpec((None, 2, ML_HEADS, ML_ST, LANES), lambda s, p, c: (s, 0, 0, 0, 0)), m_spec]
        args += [s0, m0]
    aliases = {}
    if out_buf is not None:
        aliases[len(args)] = 0
        in_specs.append(pl.BlockSpec(memory_space=pl.ANY))
        args.append(out_buf)
    out_shape = [jax.ShapeDtypeStruct((N_TOK, ML_V_WIDTH), bf16)]
    out_specs = [pl.BlockSpec((blk_rows, ML_V_WIDTH), lambda s, p, c: (blk0 + s * nb + p * c, 0))]
    if emit_state:
        if state_bufs is not None:
            for k_out, buf in enumerate(state_bufs):
                aliases[len(args)] = 1 + k_out
                in_specs.append(pl.BlockSpec(memory_space=pl.ANY))
                args.append(buf)
        out_shape += [jax.ShapeDtypeStruct((n_seq, N_ODD, 2, ML_HEADS, ML_V_DIM, ML_QK_DIM), f32),
                      jax.ShapeDtypeStruct((n_seq, N_ODD, 2, ML_HEADS, ML_QK_DIM), f32),
                      jax.ShapeDtypeStruct((n_seq, 1, LANES), f32)]
        out_specs += [pl.BlockSpec((None, None, 2, ML_HEADS, ML_V_DIM, ML_QK_DIM),
                                   lambda s, p, c: (s, layer, 0, 0, 0, 0)),
                      pl.BlockSpec((None, None, 2, ML_HEADS, ML_QK_DIM), lambda s, p, c: (s, layer, 0, 0, 0)),
                      m_spec]
    res = pl.pallas_call(
        functools.partial(_mlstm_t_kernel, nc, grp, has_state, len(aliases), emit_state),
        input_output_aliases=aliases,
        grid=(n_seq, 2, nb),
        in_specs=in_specs,
        out_specs=out_specs,
        out_shape=out_shape,
        scratch_shapes=[pltpu.VMEM((nc, ML_HEADS, ML_ST, LANES), bf16), pltpu.VMEM((nc, SUBLANES, LANES), f32),
                        pltpu.VMEM((nc, 3, CHUNK, LANES), f32), pltpu.VMEM((nc, ML_HEADS, CHUNK, LANES), bf16),
                        pltpu.VMEM((nc, ML_HEADS, ML_V_DIM, CHUNK), bf16),
                        pltpu.VMEM((2, ML_HEADS, ML_ST, LANES), f32), pltpu.VMEM((SUBLANES, LANES), f32)],
        compiler_params=pltpu.CompilerParams(
            dimension_semantics=("arbitrary", "arbitrary", "arbitrary"), vmem_limit_bytes=VMEM_LIMIT),
        name="mlstm_scan",
    )(*args)
    if not emit_state:
        return res[0]
    y, c_fin, n_fin, mfin = res
    return y, c_fin, n_fin, mfin[:, 0, :ML_DIRS].reshape(n_seq, 2, ML_HEADS)


def kernel(x_prompt, x_sample, c, cache_na_k, cache_na_v, state_ssd, state_mlstm_c, state_mlstm_n, state_mlstm_m,
           c_ctx, w_mod, b_mod, norm_mix, norm_ffn, w_in_even, w_out_even, na_rpb, ssd_conv_w, ssd_conv_b,
           ssd_dt_bias, ssd_a_log, ssd_d, ssd_norm, w_in_odd, w_out_odd, ml_conv_w, ml_conv_b, ml_gate_b,
           w_ff1, w_ff2, norm_f):
    xs = [x_prompt.reshape(N_PROMPT, D_MODEL), x_sample.reshape(DEC_BATCH * DEC_SEQ, D_MODEL)]
    cond = jnp.concatenate([c_ctx[None, :], c, jnp.zeros((SUBLANES - N_COND, D_MODEL), f32)], axis=0)
    mod = adaln_all(cond, w_mod, b_mod)[:, :N_COND].reshape(DEPTH, N_COND, 1, N_MOD * D_MODEL)

    even_main = 3 * NA_WIDTH + SSD_INNER + SSD_CONV_DIM
    odd_main = 2 * ML_QK_WIDTH + 2 * ML_V_WIDTH

    def tail_bf16(w, main):
        t = w[:, :, main:]
        return jnp.concatenate([t, jnp.zeros(t.shape[:2] + (LANES - t.shape[2],), f32)], axis=2).astype(bf16)

    wi_even, wt_even = cast_bf16(w_in_even, even_main), tail_bf16(w_in_even, even_main)
    wi_odd, wt_odd = cast_bf16(w_in_odd, odd_main), tail_bf16(w_in_odd, odd_main)
    wo_even, wo_odd = cast_bf16(w_out_even), cast_bf16(w_out_odd)
    w1_all, w2_all = cast_bf16(w_ff1), cast_bf16(w_ff2)

    out_m = []
    new_k = new_v = new_ssd = new_c = new_n = None
    for l in range(DEPTH):
        norm_last = norm_f if l == DEPTH - 1 else None
        if l % 2 == 0:
            e = l // 2
            o0 = 3 * NA_WIDTH
            segs = ((0, NA_WIDTH, NA_HEAD_DIM ** -0.5, False),
                    (NA_WIDTH, 2 * NA_WIDTH, 1.0, False),
                    (2 * NA_WIDTH, 3 * NA_WIDTH, 1.0, False),
                    (o0, o0 + SSD_INNER, 1.0, False),
                    (o0 + SSD_INNER, o0 + SSD_INNER + SSD_CONV_DIM, 1.0, True),
                    (None, None, 1.0, False))
            outs = ((0, bf16, False), (1, bf16, False), (2, bf16, False), (1, f32, True), (2, f32, True),
                    (3, f32, False), (4, f32, False), (5, f32, False))
            q, k, v, new_k, new_v, z, xbc, dt = in_proj(
                xs, mod[l], norm_mix[l], wi_even, wt_even, e, segs, outs, ssd_conv_w[e], ssd_conv_b[e],
                cache_bufs=None if new_k is None else (new_k, new_v))
            ssd_w = (ssd_dt_bias[e], ssd_a_log[e], ssd_d[e], ssd_norm[e])
            y_ssd, new_ssd = ssd_mix(xbc, dt, z, None, 0, BATCH, SEQ, *ssd_w, True, layer=e, state_buf=new_ssd)
            y_ssd = ssd_mix(xbc, dt, z, state_ssd[:, e], N_PROMPT, DEC_BATCH, DEC_SEQ, *ssd_w, False, out_buf=y_ssd)
            tokens = lambda t: jnp.swapaxes(t, 1, 2).reshape(DEC_BATCH, PAST_LEN, NA_WIDTH).astype(bf16)
            y_na = ctx_attention(q, k, v)
            y_na = na_latent(q, k, v, tokens(cache_na_k[:, e]), tokens(cache_na_v[:, e]),
                             na_bias_table(na_rpb[e]), y_na)
            res = out_mlp(xs, mod[l], norm_ffn[l], [y_na, y_ssd], wo_even, e, w1_all, w2_all, l, norm_last)
        else:
            o = l // 2
            a0 = 2 * ML_QK_WIDTH
            segs = ((0, a0, 1.0, True),
                    (a0, a0 + ML_V_WIDTH, 1.0, False),
                    (a0 + ML_V_WIDTH, a0 + 2 * ML_V_WIDTH, 1.0, False),
                    (None, None, 1.0, False))
            outs = ((0, f32, False), (1, bf16, False), (2, f32, False), (3, f32, False))
            qk, v, og, gates = in_proj(xs, mod[l], norm_mix[l], wi_odd, wt_odd, o, segs, outs,
                                       ml_conv_w[o], ml_conv_b[o])
            y_ml, new_c, new_n, m_fin = mlstm_mix(qk, v, og, gates, ml_gate_b[o], None, 0, BATCH, SEQ, True,
                                                  layer=o, state_bufs=None if new_c is None else (new_c, new_n))
            out_m.append(m_fin)
            state = (state_mlstm_c[:, o], state_mlstm_n[:, o], state_mlstm_m[:, o])
            y_ml = mlstm_mix(qk, v, og, gates, ml_gate_b[o], state, N_PROMPT, DEC_BATCH, DEC_SEQ, False, out_buf=y_ml)
            res = out_mlp(xs, mod[l], norm_ffn[l], [y_ml], wo_odd, o, w1_all, w2_all, l, norm_last)
        xs = list(res) if norm_last is not None else [res]

    y_prompt = xs[0].reshape(BATCH, SEQ, D_MODEL)
    y_sample = xs[1].reshape(DEC_BATCH, DEC_SEQ, D_MODEL)
    return (y_prompt, y_sample, new_k, new_v, new_ssd,
            new_c, new_n, jnp.stack(out_m, axis=1))
```

```python
import functools
import math

import jax
import jax.numpy as jnp
from jax import lax
from jax.experimental import pallas as pl
from jax.experimental.pallas import tpu as pltpu

D_MODEL = 1024
BATCH = 32
SEQ = 256
DEPTH = 4
DEC_BATCH = 2
DEC_SEQ = 4096
PAST_LEN = 256
GRID_W = 64
N_EVEN = (DEPTH + 1) // 2
N_ODD = DEPTH // 2
RMS_EPS = 1e-6
N_MOD = 6
D_FF = 4 * D_MODEL
CONV_K = 3
Q_BLOCK = 128
NA_HEADS = 8
NA_HEAD_DIM = 64
NA_WIDTH = NA_HEADS * NA_HEAD_DIM
NA_WIN_ROWS = 8
NA_WIN_COLS = 16
NA_RPB_ROWS = 2 * NA_WIN_ROWS - 1
NA_RPB_COLS = 2 * NA_WIN_COLS - 1
SSD_INNER = D_MODEL
SSD_HEAD_DIM = 64
SSD_HEADS = SSD_INNER // SSD_HEAD_DIM
SSD_GROUPS = 2
SSD_RPG = SSD_HEADS // SSD_GROUPS
SSD_STATE = 128
SSD_GN = SSD_GROUPS * SSD_STATE
SSD_CONV_DIM = SSD_INNER + 2 * SSD_GN
SSD_CHUNK = 128
ML_HEADS = 8
ML_QK_DIM = D_MODEL // 16
ML_V_DIM = D_MODEL // 8
ML_QK_WIDTH = ML_HEADS * ML_QK_DIM
ML_V_WIDTH = ML_HEADS * ML_V_DIM
ML_CHUNK = 64
EVEN_MIX = NA_WIDTH + SSD_INNER

N_PROMPT = BATCH * SEQ
N_TOK = N_PROMPT + DEC_BATCH * DEC_SEQ
N_COND = 1 + DEC_BATCH
LANES = 128
SUBLANES = 8
VMEM_LIMIT = 56 * 1024 * 1024
TM = 512
ODD_ROW_BLOCK = 128

f32 = jnp.float32
bf16 = jnp.bfloat16


def _cond_row(i, tm):
    start = i * tm
    return jnp.where(start < N_PROMPT, 0, (start - N_PROMPT) // DEC_SEQ + 1)


def _const_spec(shape):
    nd = len(shape)
    return pl.BlockSpec(shape, lambda i: (0,) * nd, pipeline_mode=pl.Buffered(1))


def _rms(x):
    return x * lax.rsqrt(jnp.mean(x * x, axis=-1, keepdims=True) + RMS_EPS)


def _modulated(x, g, mod, k):
    shift = mod[:, k * D_MODEL:(k + 1) * D_MODEL]
    scale = mod[:, (k + 1) * D_MODEL:(k + 2) * D_MODEL]
    return (_rms(x) * g) * (1.0 + scale) + shift


def _mod_kernel(c_ref, w_ref, b_ref, o_ref):
    c = c_ref[...]
    a = (c * jax.nn.sigmoid(c)).astype(bf16)
    o_ref[...] = jnp.dot(a, w_ref[...].astype(bf16), preferred_element_type=f32) + b_ref[...]


def adaln_all(cond, w_mod, b_mod):
    tn = 1536
    nj = N_MOD * D_MODEL // tn
    return pl.pallas_call(
        _mod_kernel,
        grid=(DEPTH, nj),
        in_specs=[
            pl.BlockSpec((SUBLANES, D_MODEL), lambda l, j: (0, 0)),
            pl.BlockSpec((None, D_MODEL, tn), lambda l, j: (l, 0, j)),
            pl.BlockSpec((None, 1, tn), lambda l, j: (l, 0, j)),
        ],
        out_specs=pl.BlockSpec((None, SUBLANES, tn), lambda l, j: (l, 0, j)),
        out_shape=jax.ShapeDtypeStruct((DEPTH, SUBLANES, N_MOD * D_MODEL), f32),
        compiler_params=pltpu.CompilerParams(
            dimension_semantics=("arbitrary", "arbitrary"), vmem_limit_bytes=VMEM_LIMIT),
        name="adaln",
    )(cond, w_mod, b_mod.reshape(DEPTH, 1, N_MOD * D_MODEL))


def _cast_kernel(x_ref, o_ref):
    o_ref[...] = x_ref[...].astype(o_ref.dtype)


def cast_bf16(w, cols=None):
    n_l, k, n = w.shape
    cols = n if cols is None else cols
    bk = 512
    spec = pl.BlockSpec((None, bk, cols), lambda l, i: (l, i, 0))
    return pl.pallas_call(
        _cast_kernel,
        grid=(n_l, k // bk),
        in_specs=[spec],
        out_specs=spec,
        out_shape=jax.ShapeDtypeStruct((n_l, k, cols), bf16),
        compiler_params=pltpu.CompilerParams(
            dimension_semantics=("arbitrary", "arbitrary"), vmem_limit_bytes=VMEM_LIMIT),
        name="cast_bf16",
    )(w)


def _layer_spec(w, l):
    nd = w.ndim - 1
    return pl.BlockSpec((None,) + w.shape[1:], lambda i: (l,) + (0,) * nd, pipeline_mode=pl.Buffered(1))


def _tile_x(x_refs, tm):
    if len(x_refs) == 1:
        return x_refs[0][...]
    return jnp.where(pl.program_id(0) < N_PROMPT // tm, x_refs[0][...], x_refs[1][...])


def _conv_silu_tile(y, prev_row, next_row, w, b):
    i = pl.program_id(0)
    rows = y.shape[0]
    g = i * rows + lax.broadcasted_iota(jnp.int32, (rows, 1), 0)
    is_prompt = i < N_PROMPT // rows
    pos = jnp.where(is_prompt, g & (SEQ - 1), (g - N_PROMPT) & (DEC_SEQ - 1))
    seq_last = jnp.where(is_prompt, SEQ - 1, DEC_SEQ - 1)
    row = lax.broadcasted_iota(jnp.int32, y.shape, 0)
    up = jnp.where(row == 0, prev_row, pltpu.roll(y, 1, axis=0))
    dn = jnp.where(row == rows - 1, next_row, pltpu.roll(y, rows - 1, axis=0))
    up = jnp.where(pos == 0, 0.0, up)
    dn = jnp.where(pos == seq_last, 0.0, dn)
    c = up * w[0:1, :] + y * w[1:2, :] + dn * w[2:3, :] + b
    return c * jax.nn.sigmoid(c)


def _in_proj_kernel(n_x, n_buf, row_block, segs, outs, *refs):
    x_refs = refs[:n_x]
    prev_ref, next_ref, mod_ref, g_ref, w_ref, wt_ref, cw_ref, cb_ref = refs[n_x:n_x + 8]
    o_refs = refs[n_x + 8 + n_buf:]
    mod, g = mod_ref[...], g_ref[...]
    x = _tile_x(x_refs, TM)
    n_blocks = TM // row_block
    halo = _modulated(jnp.concatenate([prev_ref[...], next_ref[...]], axis=0), g, mod, 0).astype(bf16)
    parts = [[] for _ in segs]
    for r in range(n_blocks):
        hb = _modulated(x[r * row_block:(r + 1) * row_block], g, mod, 0).astype(bf16)
        for i, (a, b, scale, conv) in enumerate(segs):
            w = wt_ref[...] if a is None else w_ref[:, a:b]
            lhs = jnp.concatenate([hb, halo], axis=0) if conv and r == n_blocks - 1 else hb
            parts[i].append(jnp.dot(lhs, w, preferred_element_type=f32))
    ys = []
    for (a, b, scale, conv), p in zip(segs, parts):
        y = jnp.concatenate(p, axis=0)
        if conv:
            y = _conv_silu_tile(y[:TM], y[TM + SUBLANES - 1:TM + SUBLANES], y[TM + SUBLANES:TM + SUBLANES + 1],
                                cw_ref[...], cb_ref[...])
        ys.append(y if scale == 1.0 else y * scale)
    for (si, _, head_major), o_ref in zip(outs, o_refs):
        if head_major:
            @pl.when(pl.program_id(0) < N_PROMPT // TM)
            def _(o_ref=o_ref, si=si):
                for b in range(TM // SEQ):
                    for hd in range(NA_HEADS):
                        o_ref[b, hd] = ys[si][b * SEQ:(b + 1) * SEQ, hd * NA_HEAD_DIM:(hd + 1) * NA_HEAD_DIM]
        else:
            o_ref[...] = ys[si].astype(o_ref.dtype)


def _x_specs(xs, tm):
    if len(xs) == 1:
        return [pl.BlockSpec((tm, D_MODEL), lambda i: (i, 0))]
    n_p = N_PROMPT // tm
    return [pl.BlockSpec((tm, D_MODEL), lambda i: (jnp.minimum(i, n_p - 1), 0)),
            pl.BlockSpec((tm, D_MODEL), lambda i: (jnp.maximum(i - n_p, 0), 0))]


def in_proj(xs, mod_l, g, w, w_tail, l, segs, outs, conv_w, conv_b, cache_bufs=None, row_block=TM):
    grid = (N_TOK // TM,)
    last_prompt = N_PROMPT // TM - 1
    halo_src = xs[-1]
    per = TM // SUBLANES
    blk0 = (halo_src.shape[0] - DEC_BATCH * DEC_SEQ) // SUBLANES
    n_blk = halo_src.shape[0] // SUBLANES
    tile0 = N_PROMPT // TM

    def prev_map(i):
        return (jnp.clip(blk0 + (i - tile0) * per - 1, 0, n_blk - 1), 0)

    def next_map(i):
        return (jnp.clip(blk0 + (i - tile0 + 1) * per, 0, n_blk - 1), 0)

    ch = conv_w.shape[1]
    out_shape, out_specs = [], []
    for si, dt, head_major in outs:
        width = LANES if segs[si][0] is None else segs[si][1] - segs[si][0]
        if head_major:
            out_shape.append(jax.ShapeDtypeStruct((BATCH, N_EVEN, NA_HEADS, SEQ, NA_HEAD_DIM), dt))
            out_specs.append(pl.BlockSpec((TM // SEQ, None, NA_HEADS, SEQ, NA_HEAD_DIM),
                                          lambda i: (jnp.minimum(i, last_prompt), l, 0, 0, 0)))
        else:
            out_shape.append(jax.ShapeDtypeStruct((N_TOK, width), dt))
            out_specs.append(pl.BlockSpec((TM, width), lambda i: (i, 0)))
    args = [*xs, halo_src, halo_src, mod_l, g.reshape(1, D_MODEL), w, w_tail, conv_w, conv_b.reshape(1, ch)]
    aliases = {}
    if cache_bufs is not None:
        head_major_outs = [k for k, o in enumerate(outs) if o[2]]
        for k_out, buf in zip(head_major_outs, cache_bufs):
            aliases[len(args)] = k_out
            args.append(buf)
    return pl.pallas_call(
        functools.partial(_in_proj_kernel, len(xs), len(aliases), row_block, segs, outs),
        input_output_aliases=aliases,
        grid=grid,
        in_specs=_x_specs(xs, TM) + [
            pl.BlockSpec((SUBLANES, D_MODEL), prev_map),
            pl.BlockSpec((SUBLANES, D_MODEL), next_map),
            pl.BlockSpec((None, 1, N_MOD * D_MODEL), lambda i: (_cond_row(i, TM), 0, 0)),
            _const_spec((1, D_MODEL)),
            _layer_spec(w, l),
            _layer_spec(w_tail, l),
            _const_spec((CONV_K, ch)),
            _const_spec((1, ch)),
        ] + [pl.BlockSpec(memory_space=pl.ANY)] * len(aliases),
        out_specs=out_specs,
        out_shape=out_shape,
        compiler_params=pltpu.CompilerParams(
            dimension_semantics=("arbitrary",), vmem_limit_bytes=VMEM_LIMIT),
        name="in_proj",
    )(*args)


def _out_mlp_kernel(n_x, n_mix, final, *refs):
    x_refs = refs[:n_x]
    mod_ref, g_ref = refs[n_x:n_x + 2]
    refs = refs[n_x + 2:]
    mix_refs = refs[:n_mix]
    wo_ref, w1_ref, w2_ref = refs[n_mix:n_mix + 3]
    rest = refs[n_mix + 3:]
    if final:
        gf_ref, op_ref, os_ref = rest
    else:
        (o_ref,) = rest
    mod = mod_ref[...]
    m = None
    k0 = 0
    for r in mix_refs:
        kw = r.shape[-1]
        part = jnp.dot(r[...].astype(bf16), wo_ref[k0:k0 + kw, :], preferred_element_type=f32)
        m = part if m is None else m + part
        k0 += kw
    x1 = _tile_x(x_refs, OUT_TM) + mod[:, 2 * D_MODEL:3 * D_MODEL] * m
    h2 = _modulated(x1, g_ref[...], mod, 3).astype(bf16)
    u = jnp.dot(h2, w1_ref[...], preferred_element_type=f32)
    a = jnp.square(jnp.maximum(u, 0.0)).astype(bf16)
    x2 = x1 + mod[:, 5 * D_MODEL:6 * D_MODEL] * jnp.dot(a, w2_ref[...], preferred_element_type=f32)
    if not final:
        o_ref[...] = x2
        return
    y = _rms(x2) * gf_ref[...]
    is_prompt = pl.program_id(0) < N_PROMPT // OUT_TM

    @pl.when(is_prompt)
    def _():
        op_ref[...] = y

    @pl.when(jnp.logical_not(is_prompt))
    def _():
        os_ref[...] = y


OUT_TM = 512


def out_mlp(xs, mod_l, g_ffn, mixes, w_out, l_out, w1, w2, l, norm_f=None):
    final = norm_f is not None
    tm = OUT_TM
    grid = (N_TOK // tm,)
    in_specs = _x_specs(xs, tm) + [
        pl.BlockSpec((None, 1, N_MOD * D_MODEL), lambda i: (_cond_row(i, tm), 0, 0)),
        _const_spec((1, D_MODEL)),
    ]
    in_specs += [pl.BlockSpec((tm, m.shape[-1]), lambda i: (i, 0)) for m in mixes]
    in_specs += [_layer_spec(w_out, l_out), _layer_spec(w1, l), _layer_spec(w2, l)]
    args = [*xs, mod_l, g_ffn.reshape(1, D_MODEL), *mixes, w_out, w1, w2]
    out_specs = pl.BlockSpec((tm, D_MODEL), lambda i: (i, 0))
    out_shape = jax.ShapeDtypeStruct((N_TOK, D_MODEL), f32)
    if final:
        in_specs.append(_const_spec((1, D_MODEL)))
        args.append(norm_f.reshape(1, D_MODEL))
        n_p = N_PROMPT // tm
        out_specs = [pl.BlockSpec((tm, D_MODEL), lambda i: (jnp.minimum(i, n_p - 1), 0)),
                     pl.BlockSpec((tm, D_MODEL), lambda i: (jnp.maximum(i - n_p, 0), 0))]
        out_shape = [jax.ShapeDtypeStruct((N_PROMPT, D_MODEL), f32),
                     jax.ShapeDtypeStruct((N_TOK - N_PROMPT, D_MODEL), f32)]
    return pl.pallas_call(
        functools.partial(_out_mlp_kernel, len(xs), len(mixes), final),
        grid=grid,
        in_specs=in_specs,
        out_specs=out_specs,
        out_shape=out_shape,
        compiler_params=pltpu.CompilerParams(
            dimension_semantics=("arbitrary",), vmem_limit_bytes=VMEM_LIMIT),
        name="out_mlp",
    )(*args)


NA_PAIRS = NA_HEADS // 2
NA_ROWS = DEC_SEQ // GRID_W
NA_WIN = NA_WIN_ROWS * GRID_W


def _dot_nt(a, b):
    return lax.dot_general(a, b, (((1,), (1,)), ((), ())), preferred_element_type=f32)


def _pair_stack(x):
    lane = lax.broadcasted_iota(jnp.int32, x.shape, 1)
    zero = jnp.zeros_like(x)
    return jnp.concatenate([jnp.where(lane < NA_HEAD_DIM, x, zero), jnp.where(lane >= NA_HEAD_DIM, x, zero)], axis=0)


def _pair_unstack(o):
    n = o.shape[0] // 2
    lane = lax.broadcasted_iota(jnp.int32, (n, LANES), 1)
    return jnp.where(lane < NA_HEAD_DIM, o[:n], o[n:])


def _softmax_pv(scores, values):
    def lane_tiles(blocks):
        return [b[:, k * LANES:(k + 1) * LANES] for b in blocks for k in range(b.shape[1] // LANES)]

    ms = [jnp.max(functools.reduce(jnp.maximum, lane_tiles(s)), axis=1, keepdims=True) for s in scores]
    ps = [[jnp.exp(b - m) for b in s] for s, m in zip(scores, ms)]
    invs = [1.0 / jnp.sum(functools.reduce(jnp.add, lane_tiles(p)), axis=1, keepdims=True) for p in ps]
    pn = [[(b * inv).astype(bf16) for b in p] for p, inv in zip(ps, invs)]
    return [functools.reduce(jnp.add, [jnp.dot(b, v, preferred_element_type=f32) for b, v in zip(p, vs)])
            for p, vs in zip(pn, values)]


def _ctx_attn_kernel(q_ref, k_ref, v_ref, o_ref):
    pairs = [slice(p * LANES, (p + 1) * LANES) for p in range(NA_PAIRS)]
    scores = [[_dot_nt(_pair_stack(q_ref[:, lanes]), k_ref[:, lanes])] for lanes in pairs]
    outs = _softmax_pv(scores, [[v_ref[:, lanes]] for lanes in pairs])
    for lanes, o in zip(pairs, outs):
        o_ref[:, lanes] = _pair_unstack(o).astype(o_ref.dtype)


def ctx_attention(q, k, v):
    spec = pl.BlockSpec((SEQ, NA_WIDTH), lambda b: (b, 0))
    return pl.pallas_call(
        _ctx_attn_kernel,
        grid=(BATCH,),
        in_specs=[spec, spec, spec],
        out_specs=spec,
        out_shape=jax.ShapeDtypeStruct((N_TOK, NA_WIDTH), bf16),
        compiler_params=pltpu.CompilerParams(dimension_semantics=("arbitrary",), vmem_limit_bytes=VMEM_LIMIT),
        name="ctx_attn",
    )(q, k, v)


def _na_bias_kernel(rpb_ref, o_ref):
    pair = pl.program_id(0)
    shape = (GRID_W, LANES)
    qc = lax.broadcasted_iota(jnp.int32, shape, 0)
    lane = lax.broadcasted_iota(jnp.int32, shape, 1)
    kc = lane & (GRID_W - 1)
    low = lane < GRID_W
    col_start = jnp.clip(qc - NA_WIN_COLS // 2, 0, GRID_W - NA_WIN_COLS)
    valid = (kc >= col_start) & (kc < col_start + NA_WIN_COLS)
    rel_c = jnp.clip(kc - qc + NA_WIN_COLS - 1, 0, NA_RPB_COLS - 1)
    for e in range(2):
        base = (2 * pair + e) * NA_RPB_ROWS
        pieces = []
        for rr in range(NA_RPB_ROWS - 1):
            val = jnp.zeros(shape, f32)
            for t in range(NA_RPB_COLS):
                s_lo = rpb_ref[(base + rr) * NA_RPB_COLS + t]
                s_hi = rpb_ref[(base + rr + 1) * NA_RPB_COLS + t]
                val = jnp.where(rel_c == t, jnp.where(low, s_lo, s_hi), val)
            pieces.append(jnp.where(valid, val, -jnp.inf))
        for d in range(NA_WIN_ROWS):
            for i in range(0, NA_WIN_ROWS, 2):
                o_ref[d, e * GRID_W:(e + 1) * GRID_W, i * GRID_W:(i + 2) * GRID_W] = pieces[d + i]


def na_bias_table(rpb):
    return pl.pallas_call(
        _na_bias_kernel,
        grid=(NA_PAIRS,),
        in_specs=[pl.BlockSpec(memory_space=pltpu.SMEM)],
        out_specs=pl.BlockSpec((None, NA_WIN_ROWS, 2 * GRID_W, NA_WIN), lambda p: (p, 0, 0, 0)),
        out_shape=jax.ShapeDtypeStruct((NA_PAIRS, NA_WIN_ROWS, 2 * GRID_W, NA_WIN), f32),
        compiler_params=pltpu.CompilerParams(dimension_semantics=("arbitrary",), vmem_limit_bytes=VMEM_LIMIT),
        name="na_bias",
    )(rpb.reshape(-1))


NA_STEP_ROWS = 8


def _na_first_key_row(r):
    return jnp.clip(r - NA_WIN_ROWS // 2, 0, NA_ROWS - NA_WIN_ROWS)


def _na_kernel(q_ref, k_ref, v_ref, kc_ref, vc_ref, bias_ref, buf_ref, o_ref):
    del buf_ref
    where, scores, values = [], [], []
    for j in range(NA_STEP_ROWS):
        r = pl.program_id(1) * NA_STEP_ROWS + j
        first = _na_first_key_row(r)
        start = pl.multiple_of(first * GRID_W, GRID_W)
        shift = first - r + NA_WIN_ROWS - 1
        rows = slice(j * GRID_W, (j + 1) * GRID_W)
        for p in range(NA_PAIRS):
            lanes = slice(p * LANES, (p + 1) * LANES)
            qq = _pair_stack(q_ref[rows, lanes])
            where.append((rows, lanes))
            scores.append([_dot_nt(qq, k_ref[pl.ds(start, NA_WIN), lanes]) + bias_ref[p, shift],
                           _dot_nt(qq, kc_ref[:, lanes])])
            values.append([v_ref[pl.ds(start, NA_WIN), lanes], vc_ref[:, lanes]])
    for (rows, lanes), o in zip(where, _softmax_pv(scores, values)):
        o_ref[rows, lanes] = _pair_unstack(o).astype(o_ref.dtype)


def na_latent(q, k, v, k_ctx, v_ctx, bias, buf):
    rows = NA_STEP_ROWS * GRID_W
    steps = NA_ROWS // NA_STEP_ROWS
    row0 = N_PROMPT // rows
    seq0 = N_PROMPT // DEC_SEQ
    kv_spec = pl.BlockSpec((DEC_SEQ, NA_WIDTH), lambda b, r: (seq0 + b, 0))
    ctx_spec = pl.BlockSpec((None, PAST_LEN, NA_WIDTH), lambda b, r: (b, 0, 0))
    return pl.pallas_call(
        _na_kernel,
        grid=(DEC_BATCH, steps),
        in_specs=[
            pl.BlockSpec((rows, NA_WIDTH), lambda b, r: (row0 + b * steps + r, 0)),
            kv_spec, kv_spec, ctx_spec, ctx_spec,
            pl.BlockSpec(bias.shape, lambda b, r: (0, 0, 0, 0), pipeline_mode=pl.Buffered(1)),
            pl.BlockSpec(memory_space=pl.ANY),
        ],
        out_specs=pl.BlockSpec((rows, NA_WIDTH), lambda b, r: (row0 + b * steps + r, 0)),
        out_shape=jax.ShapeDtypeStruct((N_TOK, NA_WIDTH), bf16),
        input_output_aliases={6: 0},
        compiler_params=pltpu.CompilerParams(
            dimension_semantics=("arbitrary", "arbitrary"), vmem_limit_bytes=VMEM_LIMIT),
        name="na_latent",
    )(q, k, v, k_ctx, v_ctx, bias, buf)


CHUNK = 128
SCAN_G_MAX = 4


def _cumsum_rows(a, reverse=False):
    row = lax.broadcasted_iota(jnp.int32, a.shape, 0)
    s = 1
    while s < CHUNK:
        if reverse:
            a = a + jnp.where(row < CHUNK - s, pltpu.roll(a, CHUNK - s, axis=0), 0.0)
        else:
            a = a + jnp.where(row >= s, pltpu.roll(a, s, axis=0), 0.0)
        s *= 2
    return a


def _ssd_kernel(nc, grp, has_h0, n_buf, emit_state, *refs):
    xbc_ref, dt_ref, z_ref = refs[:3]
    refs = refs[3:]
    if has_h0:
        h0_ref, refs = refs[0], refs[1:]
    dtb_ref, alog_ref, dskip_ref, g_ref = refs[:4]
    refs = refs[4 + n_buf:]
    y_ref, refs = refs[0], refs[1:]
    if emit_state:
        hfin_ref, refs = refs[0], refs[1:]
    hb_store, xt_store, gate_store, carry = refs

    phase = pl.program_id(1)
    gi = pl.program_id(2)
    nb = nc // grp
    gw = SSD_RPG * SSD_HEAD_DIM

    def load_h0(d):
        if has_h0:
            return h0_ref[d].reshape(SSD_INNER, SSD_STATE)
        return jnp.zeros((SSD_INNER, SSD_STATE), f32)

    def head_rows(v):
        return jnp.concatenate(
            [jnp.broadcast_to(v[h:h + 1, :], (SSD_HEAD_DIM, v.shape[1])) for h in range(SSD_HEADS)], axis=0)

    def state_update(x_t, b_bf, cum_t, dt_t, edge):
        at_edge = jnp.broadcast_to(cum_t[:, edge:edge + 1], cum_t.shape)
        w_end = jnp.exp(at_edge - cum_t) * dt_t
        xw = (x_t * head_rows(w_end)).astype(bf16)
        upd = jnp.concatenate(
            [jnp.dot(xw[g * gw:(g + 1) * gw], b_bf[:, g * SSD_STATE:(g + 1) * SSD_STATE], preferred_element_type=f32)
             for g in range(SSD_GROUPS)], axis=0)
        carry[...] = carry[...] * head_rows(jnp.exp(at_edge)) + upd

    @pl.when(jnp.logical_and(phase == 0, gi == 0))
    def _():
        carry[...] = load_h0(1)

    def backward_chunk(sub):
        rows = slice(sub * CHUNK, (sub + 1) * CHUNK)
        j = grp * (nb - 1 - gi) + sub
        x = xbc_ref[rows, :SSD_INNER]
        b_bf = xbc_ref[rows, SSD_INNER:SSD_INNER + SSD_GN].astype(bf16)
        dt = jax.nn.softplus(dt_ref[rows, :] + dtb_ref[...])
        a = dt * (-jnp.exp(alog_ref[...]))
        x_t = jnp.concatenate([x[:, k * LANES:(k + 1) * LANES].T for k in range(SSD_INNER // LANES)], axis=0)
        rcum = _cumsum_rows(a, reverse=True)
        xt_store[j] = x_t
        gate_store[j, 0] = dt
        gate_store[j, 1] = _cumsum_rows(a)
        gate_store[j, 2] = rcum
        hb_store[j] = carry[...].astype(bf16)
        state_update(x_t, b_bf, rcum.T[SSD_HEADS:2 * SSD_HEADS], dt.T[SSD_HEADS:2 * SSD_HEADS], 0)

    @pl.when(phase == 0)
    def _backward_states():
        for sub in reversed(range(grp)):
            backward_chunk(sub)

        if emit_state:
            @pl.when(gi == nb - 1)
            def _():
                hfin_ref[1] = carry[...].reshape(SSD_HEADS, SSD_HEAD_DIM, SSD_STATE)

    @pl.when(jnp.logical_and(phase == 1, gi == 0))
    def _():
        carry[...] = load_h0(0)

    def forward_chunk(sub):
        rows = slice(sub * CHUNK, (sub + 1) * CHUNK)
        c = grp * gi + sub
        x = xbc_ref[rows, :SSD_INNER]
        x_bf = x.astype(bf16)
        b_bf = xbc_ref[rows, SSD_INNER:SSD_INNER + SSD_GN].astype(bf16)
        c_mat = xbc_ref[rows, SSD_INNER + SSD_GN:].astype(bf16)
        dt, cum, rcum = gate_store[c, 0], gate_store[c, 1], gate_store[c, 2]
        cum_t, rcum_t, dt_t = cum.T, rcum.T, dt.T
        row = lax.broadcasted_iota(jnp.int32, (CHUNK, CHUNK), 0)
        col = lax.broadcasted_iota(jnp.int32, (CHUNK, CHUNK), 1)
        causal = col <= row
        anti = col >= row
        lane = lax.broadcasted_iota(jnp.int32, (CHUNK, LANES), 1)
        hf = carry[...].astype(bf16)
        hb = hb_store[c]
        cb = [_dot_nt(c_mat[:, g * SSD_STATE:(g + 1) * SSD_STATE], b_bf[:, g * SSD_STATE:(g + 1) * SSD_STATE])
              for g in range(SSD_GROUPS)]

        heads = range(SSD_HEADS)
        key_f = cum_t - jnp.log(dt_t)
        key_b = rcum_t - jnp.log(dt_t)
        seg_f = [jnp.where(causal, cum[:, h:h + 1] - key_f[h:h + 1, :], -jnp.inf) for h in heads]
        seg_b = [jnp.where(anti, rcum[:, SSD_HEADS + h:SSD_HEADS + h + 1]
                           - key_b[SSD_HEADS + h:SSD_HEADS + h + 1, :], -jnp.inf) for h in heads]
        e_f = [jnp.exp(s) for s in seg_f]
        e_b = [jnp.exp(s) for s in seg_b]
        ws = [(cb[h // SSD_RPG] * (e_f[h] + e_b[h])).astype(bf16) for h in heads]
        rhs = []
        for p in range(SSD_HEADS // 2):
            xp = x_bf[:, p * LANES:(p + 1) * LANES]
            zero = jnp.zeros_like(xp)
            rhs.append(jnp.concatenate([jnp.where(lane < SSD_HEAD_DIM, xp, zero),
                                        jnp.where(lane >= SSD_HEAD_DIM, xp, zero)], axis=0))
        y = jnp.concatenate(
            [jnp.dot(jnp.concatenate([ws[2 * p], ws[2 * p + 1]], axis=1), rhs[p], preferred_element_type=f32)
             for p in range(SSD_HEADS // 2)], axis=1)

        def inter_t(h_all):
            return jnp.concatenate(
                [_dot_nt(h_all[g * gw:(g + 1) * gw], c_mat[:, g * SSD_STATE:(g + 1) * SSD_STATE])
                 for g in range(SSD_GROUPS)], axis=0)

        cum_f, cum_b = cum_t[:SSD_HEADS], rcum_t[SSD_HEADS:2 * SSD_HEADS]
        y_t = inter_t(hf) * head_rows(jnp.exp(cum_f)) + inter_t(hb) * head_rows(jnp.exp(cum_b))
        y_inter = jnp.concatenate([y_t[k * LANES:(k + 1) * LANES].T for k in range(SSD_INNER // LANES)], axis=1)
        state_update(xt_store[c], b_bf, cum_f, dt_t[:SSD_HEADS], CHUNK - 1)
        y = y + y_inter + dskip_ref[...] * x

        zv = z_ref[rows, :]
        yz = y * (zv * jax.nn.sigmoid(zv))
        y_ref[rows, :] = (_rms(yz) * g_ref[...]).astype(y_ref.dtype)

    @pl.when(phase == 1)
    def _forward_and_outputs():
        for sub in range(grp):
            forward_chunk(sub)

        if emit_state:
            @pl.when(gi == nb - 1)
            def _():
                hfin_ref[0] = carry[...].reshape(SSD_HEADS, SSD_HEAD_DIM, SSD_STATE)


def ssd_mix(xbc, dt, z, h0, row0, n_seq, seq_len, dt_bias, a_log, d_skip, norm_g, emit_state, out_buf=None,
            layer=0, state_buf=None):
    nc = seq_len // CHUNK
    grp = math.gcd(nc, SCAN_G_MAX)
    blk_rows = grp * CHUNK
    nb = seq_len // blk_rows
    blk0 = row0 // blk_rows
    has_h0 = h0 is not None

    def block_map(s, p, c):
        return (blk0 + s * nb + jnp.where(p == 0, nb - 1 - c, c), 0)

    state_spec = pl.BlockSpec((None, 2, SSD_HEADS, SSD_HEAD_DIM, SSD_STATE), lambda s, p, c: (s, 0, 0, 0, 0))
    vec = lambda n: pl.BlockSpec((1, n), lambda s, p, c: (0, 0))
    in_specs = [
        pl.BlockSpec((blk_rows, SSD_CONV_DIM), block_map),
        pl.BlockSpec((blk_rows, LANES), block_map),
        pl.BlockSpec((blk_rows, SSD_INNER), lambda s, p, c: (blk0 + s * nb + p * c, 0)),
    ]
    args = [xbc, dt, z]
    if has_h0:
        in_specs.append(state_spec)
        args.append(h0)
    in_specs += [vec(LANES), vec(LANES), vec(SSD_INNER), vec(SSD_INNER)]
    pad = lambda t: jnp.concatenate([t.reshape(1, -1), jnp.zeros((1, LANES - t.size), f32)], axis=1)
    args += [pad(dt_bias), pad(a_log), jnp.repeat(d_skip, SSD_HEAD_DIM).reshape(1, SSD_INNER),
             norm_g.reshape(1, SSD_INNER)]
    aliases = {}
    if out_buf is not None:
        aliases = {len(args): 0}
        in_specs.append(pl.BlockSpec(memory_space=pl.ANY))
        args.append(out_buf)
    out_shape = [jax.ShapeDtypeStruct((N_TOK, SSD_INNER), bf16)]
    out_specs = [pl.BlockSpec((blk_rows, SSD_INNER), lambda s, p, c: (blk0 + s * nb + p * c, 0))]
    if emit_state:
        if state_buf is not None:
            aliases[len(args)] = 1
            in_specs.append(pl.BlockSpec(memory_space=pl.ANY))
            args.append(state_buf)
        out_shape.append(jax.ShapeDtypeStruct((n_seq, N_EVEN, 2, SSD_HEADS, SSD_HEAD_DIM, SSD_STATE), f32))
        out_specs.append(pl.BlockSpec((None, None, 2, SSD_HEADS, SSD_HEAD_DIM, SSD_STATE),
                                      lambda s, p, c: (s, layer, 0, 0, 0, 0)))
    res = pl.pallas_call(
        functools.partial(_ssd_kernel, nc, grp, has_h0, len(aliases), emit_state),
        input_output_aliases=aliases,
        grid=(n_seq, 2, nb),
        in_specs=in_specs,
        out_specs=out_specs,
        out_shape=out_shape,
        scratch_shapes=[pltpu.VMEM((nc, SSD_INNER, SSD_STATE), bf16), pltpu.VMEM((nc, SSD_INNER, CHUNK), f32),
                        pltpu.VMEM((nc, 3, CHUNK, LANES), f32), pltpu.VMEM((SSD_INNER, SSD_STATE), f32)],
        compiler_params=pltpu.CompilerParams(
            dimension_semantics=("arbitrary", "arbitrary", "arbitrary"), vmem_limit_bytes=VMEM_LIMIT),
        name="ssd_scan",
    )(*args)
    return res if emit_state else res[0]


ML_DIRS = 2 * ML_HEADS


def _cummax_rows(a, reverse=False):
    row = lax.broadcasted_iota(jnp.int32, a.shape, 0)
    s = 1
    while s < CHUNK:
        if reverse:
            a = jnp.maximum(a, jnp.where(row < CHUNK - s, pltpu.roll(a, CHUNK - s, axis=0), -jnp.inf))
        else:
            a = jnp.maximum(a, jnp.where(row >= s, pltpu.roll(a, s, axis=0), -jnp.inf))
        s *= 2
    return a


ML_ST = ML_V_DIM + 16


def _mlstm_t_kernel(nc, grp, has_state, n_buf, emit_state, *refs):
    qk_ref, v_ref, og_ref, gates_ref, gb_ref = refs[:5]
    refs = refs[5:]
    if has_state:
        s0_ref, m0_ref = refs[:2]
        refs = refs[2:]
    refs = refs[n_buf:]
    y_ref, refs = refs[0], refs[1:]
    if emit_state:
        cfin_ref, nfin_ref, mfin_ref = refs[:3]
        refs = refs[3:]
    s_store, m_store, gate_store, k_store, v_store, s_carry, m_carry = refs

    phase = pl.program_id(1)
    gi = pl.program_id(2)
    nb = nc // grp
    fwd_row = lax.broadcasted_iota(jnp.int32, (1, LANES), 1) < ML_HEADS

    def state_update(cum, r, k_own, v_tr, m_row, big_m, d):
        edge = CHUNK - 1 if d == 0 else 0
        m_edge = big_m[edge:edge + 1, :]
        wc_row = jnp.exp(m_row - m_edge)
        wk_t = jnp.exp(r.T - big_m.T[:, edge:edge + 1])
        lhs = []
        for h in range(ML_HEADS):
            wk = wk_t[d * ML_HEADS + h:d * ML_HEADS + h + 1, :]
            lhs.append(jnp.concatenate([v_tr[h] * wk, jnp.broadcast_to(wk, (ML_ST - ML_V_DIM, CHUNK))],
                                       axis=0).astype(bf16))
        upd = [jnp.dot(lhs[h], k_own[h], preferred_element_type=f32) for h in range(ML_HEADS)]
        for h in range(ML_HEADS):
            cl = d * ML_HEADS + h
            wc = jnp.broadcast_to(wc_row[:, cl:cl + 1], (ML_ST, LANES))
            s_carry[d, h] = wc * s_carry[d, h] + upd[h]
        return cum[edge:edge + 1, :] + m_edge

    def init_state(d):
        if has_state:
            s_carry[d] = s0_ref[d]
        else:
            s_carry[d] = jnp.zeros((ML_HEADS, ML_ST, LANES), f32)

    def emit_final(d):
        for h in range(ML_HEADS):
            tile = s_carry[d, h]
            if h % 2 == 1:
                tile = pltpu.roll(tile, ML_QK_DIM, axis=1)
            cfin_ref[d, h] = tile[:ML_V_DIM, :ML_QK_DIM]
            nfin_ref[d, h:h + 1, :] = tile[ML_V_DIM:ML_V_DIM + 1, :ML_QK_DIM]

    @pl.when(jnp.logical_and(phase == 0, gi == 0))
    def _():
        init_state(1)
        m_carry[...] = jnp.broadcast_to(m0_ref[...], m_carry.shape) if has_state else jnp.zeros(m_carry.shape, f32)

    def backward_chunk(sub):
        rows = slice(sub * CHUNK, (sub + 1) * CHUNK)
        lane = lax.broadcasted_iota(jnp.int32, (CHUNK, LANES), 1)
        fwd_lane = lane < ML_HEADS
        g = gates_ref[rows, :] + gb_ref[...]
        lf = pltpu.roll(jax.nn.log_sigmoid(g), LANES - ML_DIRS, axis=1)
        cum = jnp.where(fwd_lane, _cumsum_rows(lf), _cumsum_rows(lf, reverse=True))
        r = g - cum
        pm = jnp.where(fwd_lane, _cummax_rows(r), _cummax_rows(r, reverse=True))
        k = qk_ref[rows, ML_QK_WIDTH:]
        k_own = [jnp.where((lane < ML_QK_DIM) if h % 2 == 0 else (lane >= ML_QK_DIM),
                           k[:, (h // 2) * LANES:(h // 2 + 1) * LANES], 0.0).astype(bf16) for h in range(ML_HEADS)]
        v_tr = [v_ref[rows, h * ML_V_DIM:(h + 1) * ML_V_DIM].astype(f32).T for h in range(ML_HEADS)]

        j = grp * (nb - 1 - gi) + sub
        gate_store[j, 0] = cum
        gate_store[j, 1] = r
        gate_store[j, 2] = pm
        for h in range(ML_HEADS):
            k_store[j, h] = k_own[h]
            v_store[j, h] = v_tr[h].astype(bf16)
        m_row = m_carry[0:1, :]
        s_store[j] = s_carry[1].astype(bf16)
        m_store[j] = m_carry[...]
        big_m = jnp.maximum(m_row, pm)
        m_new = state_update(cum, r, k_own, v_tr, m_row, big_m, 1)
        m_carry[...] = jnp.broadcast_to(jnp.where(fwd_row, m_row, m_new), m_carry.shape)

    @pl.when(phase == 0)
    def _backward_states():
        for sub in reversed(range(grp)):
            backward_chunk(sub)

    @pl.when(jnp.logical_and(phase == 1, gi == 0))
    def _():
        init_state(0)
        if emit_state:
            emit_final(1)

    def forward_chunk(sub):
        rows = slice(sub * CHUNK, (sub + 1) * CHUNK)
        c = grp * gi + sub
        cum, r, pm = gate_store[c, 0], gate_store[c, 1], gate_store[c, 2]
        k_own = [k_store[c, h] for h in range(ML_HEADS)]
        v_th = [v_store[c, h] for h in range(ML_HEADS)]
        m_both = m_carry[0:1, :]
        m_row = jnp.where(fwd_row, m_both, m_store[c][0:1, :])
        big_m = jnp.maximum(m_row, pm)
        big_m_t = big_m.T
        w_inter_t = jnp.exp(m_row - big_m).T
        floor_t = jnp.exp(-(cum + big_m)).T
        key = lax.broadcasted_iota(jnp.int32, (CHUNK, CHUNK), 0)
        qry = lax.broadcasted_iota(jnp.int32, (CHUNK, CHUNK), 1)
        masks = (key <= qry, key >= qry)
        q = (qk_ref[rows, :ML_QK_WIDTH] * (ML_QK_DIM ** -0.5)).astype(bf16)
        heads = range(ML_HEADS)
        hd_pairs = [(h, d) for h in heads for d in range(2)]
        q_pair = [q[:, (h // 2) * LANES:(h // 2 + 1) * LANES] for h in heads]
        s_raw_t = [_dot_nt(k_own[h], q_pair[h]) for h in heads]
        inter = {(h, d): _dot_nt(s_carry[0, h].astype(bf16) if d == 0 else s_store[c, h], q_pair[h])
                 for h, d in hd_pairs}
        w_t = {(h, d): jnp.exp(jnp.where(masks[d], r[:, d * ML_HEADS + h:d * ML_HEADS + h + 1]
                                         - big_m_t[d * ML_HEADS + h:d * ML_HEADS + h + 1, :], -jnp.inf))
               for h, d in hd_pairs}
        sw_t = {hd: s_raw_t[hd[0]] * w_t[hd] for hd in hd_pairs}
        num = {hd: jnp.dot(v_th[hd[0]], sw_t[hd].astype(bf16), preferred_element_type=f32) for hd in hd_pairs}
        wi = {(h, d): w_inter_t[d * ML_HEADS + h:d * ML_HEADS + h + 1, :] for h, d in hd_pairs}
        den = {hd: jnp.sum(sw_t[hd], axis=0, keepdims=True) + wi[hd] * inter[hd][ML_V_DIM:ML_V_DIM + 1]
               for hd in hd_pairs}
        inv = {(h, d): 1.0 / jnp.maximum(jnp.abs(den[h, d]), floor_t[d * ML_HEADS + h:d * ML_HEADS + h + 1, :])
               for h, d in hd_pairs}
        part = {hd: (num[hd] + wi[hd] * inter[hd][:ML_V_DIM]) * inv[hd] for hd in hd_pairs}
        out = [(part[h, 0] + part[h, 1]).T for h in heads]
        gate = [jax.nn.sigmoid(og_ref[rows, h * ML_V_DIM:(h + 1) * ML_V_DIM]) for h in heads]
        for h in heads:
            y_ref[rows, h * ML_V_DIM:(h + 1) * ML_V_DIM] = (out[h] * gate[h]).astype(y_ref.dtype)

        v_tr = [v.astype(f32) for v in v_th]
        m_fin = jnp.where(fwd_row, state_update(cum, r, k_own, v_tr, m_row, big_m, 0), m_both)
        m_carry[...] = jnp.broadcast_to(m_fin, m_carry.shape)

        if emit_state and sub == grp - 1:
            @pl.when(gi == nb - 1)
            def _():
                emit_final(0)
                mfin_ref[...] = m_fin

    @pl.when(phase == 1)
    def _forward_and_outputs():
        for sub in range(grp):
            forward_chunk(sub)


def mlstm_mix(qk, v, og, gates, gate_b, state, row0, n_seq, seq_len, emit_state, out_buf=None,
              layer=0, state_bufs=None):
    nc = seq_len // CHUNK
    grp = math.gcd(nc, SCAN_G_MAX)
    blk_rows = grp * CHUNK
    nb = seq_len // blk_rows
    blk0 = row0 // blk_rows
    has_state = state is not None

    def block_map(s, p, c):
        return (blk0 + s * nb + jnp.where(p == 0, nb - 1 - c, c), 0)

    m_spec = pl.BlockSpec((None, 1, LANES), lambda s, p, c: (s, 0, 0))
    in_specs = [
        pl.BlockSpec((blk_rows, 2 * ML_QK_WIDTH), block_map),
        pl.BlockSpec((blk_rows, ML_V_WIDTH), block_map),
        pl.BlockSpec((blk_rows, ML_V_WIDTH), lambda s, p, c: (blk0 + s * nb + p * c, 0)),
        pl.BlockSpec((blk_rows, LANES), block_map),
        pl.BlockSpec((1, LANES), lambda s, p, c: (0, 0)),
    ]
    gb = jnp.concatenate([gate_b.reshape(1, 2 * ML_DIRS), jnp.zeros((1, LANES - 2 * ML_DIRS), f32)], axis=1)
    args = [qk, v, og, gates, gb]
    if has_state:
        c0, n0, m0 = state
        rows = jnp.concatenate([c0, jnp.broadcast_to(n0[..., None, :], n0.shape[:-1] + (ML_ST - ML_V_DIM, ML_QK_DIM))],
                               axis=-2)
        zeros = jnp.zeros_like(rows)
        odd = (jnp.arange(ML_HEADS) % 2 == 1)[:, None, None]
        s0 = jnp.where(odd, jnp.concatenate([zeros, rows], axis=-1), jnp.concatenate([rows, zeros], axis=-1))
        m0 = jnp.concatenate([m0.reshape(n_seq, 1, ML_DIRS), jnp.zeros((n_seq, 1, LANES - ML_DIRS), f32)], axis=-1)
        in_specs += [pl.BlockSpec((None, 2, ML_HEADS, ML_ST, LANES), lambda s, p, c: (s, 0, 0, 0, 0)), m_spec]
        args += [s0, m0]
    aliases = {}
    if out_buf is not None:
        aliases[len(args)] = 0
        in_specs.append(pl.BlockSpec(memory_space=pl.ANY))
        args.append(out_buf)
    out_shape = [jax.ShapeDtypeStruct((N_TOK, ML_V_WIDTH), bf16)]
    out_specs = [pl.BlockSpec((blk_rows, ML_V_WIDTH), lambda s, p, c: (blk0 + s * nb + p * c, 0))]
    if emit_state:
        if state_bufs is not None:
            for k_out, buf in enumerate(state_bufs):
                aliases[len(args)] = 1 + k_out
                in_specs.append(pl.BlockSpec(memory_space=pl.ANY))
                args.append(buf)
        out_shape += [jax.ShapeDtypeStruct((n_seq, N_ODD, 2, ML_HEADS, ML_V_DIM, ML_QK_DIM), f32),
                      jax.ShapeDtypeStruct((n_seq, N_ODD, 2, ML_HEADS, ML_QK_DIM), f32),
                      jax.ShapeDtypeStruct((n_seq, 1, LANES), f32)]
        out_specs += [pl.BlockSpec((None, None, 2, ML_HEADS, ML_V_DIM, ML_QK_DIM),
                                   lambda s, p, c: (s, layer, 0, 0, 0, 0)),
                      pl.BlockSpec((None, None, 2, ML_HEADS, ML_QK_DIM), lambda s, p, c: (s, layer, 0, 0, 0)),
                      m_spec]
    res = pl.pallas_call(
        functools.partial(_mlstm_t_kernel, nc, grp, has_state, len(aliases), emit_state),
        input_output_aliases=aliases,
        grid=(n_seq, 2, nb),
        in_specs=in_specs,
        out_specs=out_specs,
        out_shape=out_shape,
        scratch_shapes=[pltpu.VMEM((nc, ML_HEADS, ML_ST, LANES), bf16), pltpu.VMEM((nc, SUBLANES, LANES), f32),
                        pltpu.VMEM((nc, 3, CHUNK, LANES), f32), pltpu.VMEM((nc, ML_HEADS, CHUNK, LANES), bf16),
                        pltpu.VMEM((nc, ML_HEADS, ML_V_DIM, CHUNK), bf16),
                        pltpu.VMEM((2, ML_HEADS, ML_ST, LANES), f32), pltpu.VMEM((SUBLANES, LANES), f32)],
        compiler_params=pltpu.CompilerParams(
            dimension_semantics=("arbitrary", "arbitrary", "arbitrary"), vmem_limit_bytes=VMEM_LIMIT),
        name="mlstm_scan",
    )(*args)
    if not emit_state:
        return res[0]
    y, c_fin, n_fin, mfin = res
    return y, c_fin, n_fin, mfin[:, 0, :ML_DIRS].reshape(n_seq, 2, ML_HEADS)


def kernel(x_prompt, x_sample, c, cache_na_k, cache_na_v, state_ssd, state_mlstm_c, state_mlstm_n, state_mlstm_m,
           c_ctx, w_mod, b_mod, norm_mix, norm_ffn, w_in_even, w_out_even, na_rpb, ssd_conv_w, ssd_conv_b,
           ssd_dt_bias, ssd_a_log, ssd_d, ssd_norm, w_in_odd, w_out_odd, ml_conv_w, ml_conv_b, ml_gate_b,
           w_ff1, w_ff2, norm_f):
    xs = [x_prompt.reshape(N_PROMPT, D_MODEL), x_sample.reshape(DEC_BATCH * DEC_SEQ, D_MODEL)]
    cond = jnp.concatenate([c_ctx[None, :], c, jnp.zeros((SUBLANES - N_COND, D_MODEL), f32)], axis=0)
    mod = adaln_all(cond, w_mod, b_mod)[:, :N_COND].reshape(DEPTH, N_COND, 1, N_MOD * D_MODEL)

    even_main = 3 * NA_WIDTH + SSD_INNER + SSD_CONV_DIM
    odd_main = 2 * ML_QK_WIDTH + 2 * ML_V_WIDTH

    def tail_bf16(w, main):
        t = w[:, :, main:]
        return jnp.concatenate([t, jnp.zeros(t.shape[:2] + (LANES - t.shape[2],), f32)], axis=2).astype(bf16)

    wi_even, wt_even = cast_bf16(w_in_even, even_main), tail_bf16(w_in_even, even_main)
    wi_odd, wt_odd = cast_bf16(w_in_odd, odd_main), tail_bf16(w_in_odd, odd_main)
    wo_even, wo_odd = cast_bf16(w_out_even), cast_bf16(w_out_odd)
    w1_all, w2_all = cast_bf16(w_ff1), cast_bf16(w_ff2)

    out_m = []
    new_k = new_v = new_ssd = new_c = new_n = None
    for l in range(DEPTH):
        norm_last = norm_f if l == DEPTH - 1 else None
        if l % 2 == 0:
            e = l // 2
            o0 = 3 * NA_WIDTH
            segs = ((0, NA_WIDTH, NA_HEAD_DIM ** -0.5, False),
                    (NA_WIDTH, 2 * NA_WIDTH, 1.0, False),
                    (2 * NA_WIDTH, 3 * NA_WIDTH, 1.0, False),
                    (o0, o0 + SSD_INNER, 1.0, False),
                    (o0 + SSD_INNER, o0 + SSD_INNER + SSD_CONV_DIM, 1.0, True),
                    (None, None, 1.0, False))
            outs = ((0, bf16, False), (1, bf16, False), (2, bf16, False), (1, f32, True), (2, f32, True),
                    (3, f32, False), (4, f32, False), (5, f32, False))
            q, k, v, new_k, new_v, z, xbc, dt = in_proj(
                xs, mod[l], norm_mix[l], wi_even, wt_even, e, segs, outs, ssd_conv_w[e], ssd_conv_b[e],
                cache_bufs=None if new_k is None else (new_k, new_v))
            ssd_w = (ssd_dt_bias[e], ssd_a_log[e], ssd_d[e], ssd_norm[e])
            y_ssd, new_ssd = ssd_mix(xbc, dt, z, None, 0, BATCH, SEQ, *ssd_w, True, layer=e, state_buf=new_ssd)
            y_ssd = ssd_mix(xbc, dt, z, state_ssd[:, e], N_PROMPT, DEC_BATCH, DEC_SEQ, *ssd_w, False, out_buf=y_ssd)
            tokens = lambda t: jnp.swapaxes(t, 1, 2).reshape(DEC_BATCH, PAST_LEN, NA_WIDTH).astype(bf16)
            y_na = ctx_attention(q, k, v)
            y_na = na_latent(q, k, v, tokens(cache_na_k[:, e]), tokens(cache_na_v[:, e]),
                             na_bias_table(na_rpb[e]), y_na)
            res = out_mlp(xs, mod[l], norm_ffn[l], [y_na, y_ssd], wo_even, e, w1_all, w2_all, l, norm_last)
        else:
            o = l // 2
            a0 = 2 * ML_QK_WIDTH
            segs = ((0, a0, 1.0, True),
                    (a0, a0 + ML_V_WIDTH, 1.0, False),
                    (a0 + ML_V_WIDTH, a0 + 2 * ML_V_WIDTH, 1.0, False),
                    (None, None, 1.0, False))
            outs = ((0, f32, False), (1, bf16, False), (2, f32, False), (3, f32, False))
            qk, v, og, gates = in_proj(xs, mod[l], norm_mix[l], wi_odd, wt_odd, o, segs, outs,
                                       ml_conv_w[o], ml_conv_b[o], row_block=ODD_ROW_BLOCK)
            y_ml, new_c, new_n, m_fin = mlstm_mix(qk, v, og, gates, ml_gate_b[o], None, 0, BATCH, SEQ, True,
                                                  layer=o, state_bufs=None if new_c is None else (new_c, new_n))
            out_m.append(m_fin)
            state = (state_mlstm_c[:, o], state_mlstm_n[:, o], state_mlstm_m[:, o])
            y_ml = mlstm_mix(qk, v, og, gates, ml_gate_b[o], state, N_PROMPT, DEC_BATCH, DEC_SEQ, False, out_buf=y_ml)
            res = out_mlp(xs, mod[l], norm_ffn[l], [y_ml], wo_odd, o, w1_all, w2_all, l, norm_last)
        xs = list(res) if norm_last is not None else [res]

    y_prompt = xs[0].reshape(BATCH, SEQ, D_MODEL)
    y_sample = xs[1].reshape(DEC_BATCH, DEC_SEQ, D_MODEL)
    return (y_prompt, y_sample, new_k, new_v, new_ssd,
            new_c, new_n, jnp.stack(out_m, axis=1))
```

```python
import functools
import math

import jax
import jax.numpy as jnp
from jax import lax
from jax.experimental import pallas as pl
from jax.experimental.pallas import tpu as pltpu

D_MODEL = 1024
BATCH = 32
SEQ = 256
DEPTH = 4
DEC_BATCH = 2
DEC_SEQ = 4096
PAST_LEN = 256
GRID_W = 64
N_EVEN = (DEPTH + 1) // 2
N_ODD = DEPTH // 2
RMS_EPS = 1e-6
N_MOD = 6
D_FF = 4 * D_MODEL
CONV_K = 3
Q_BLOCK = 128
NA_HEADS = 8
NA_HEAD_DIM = 64
NA_WIDTH = NA_HEADS * NA_HEAD_DIM
NA_WIN_ROWS = 8
NA_WIN_COLS = 16
NA_RPB_ROWS = 2 * NA_WIN_ROWS - 1
NA_RPB_COLS = 2 * NA_WIN_COLS - 1
SSD_INNER = D_MODEL
SSD_HEAD_DIM = 64
SSD_HEADS = SSD_INNER // SSD_HEAD_DIM
SSD_GROUPS = 2
SSD_RPG = SSD_HEADS // SSD_GROUPS
SSD_STATE = 128
SSD_GN = SSD_GROUPS * SSD_STATE
SSD_CONV_DIM = SSD_INNER + 2 * SSD_GN
SSD_CHUNK = 128
ML_HEADS = 8
ML_QK_DIM = D_MODEL // 16
ML_V_DIM = D_MODEL // 8
ML_QK_WIDTH = ML_HEADS * ML_QK_DIM
ML_V_WIDTH = ML_HEADS * ML_V_DIM
ML_CHUNK = 64
EVEN_MIX = NA_WIDTH + SSD_INNER

N_PROMPT = BATCH * SEQ
N_TOK = N_PROMPT + DEC_BATCH * DEC_SEQ
N_COND = 1 + DEC_BATCH
LANES = 128
SUBLANES = 8
VMEM_LIMIT = 56 * 1024 * 1024
TM = 512
ODD_ROW_BLOCK = 128

f32 = jnp.float32
bf16 = jnp.bfloat16


def _cond_row(i, tm):
    start = i * tm
    return jnp.where(start < N_PROMPT, 0, (start - N_PROMPT) // DEC_SEQ + 1)


def _const_spec(shape):
    nd = len(shape)
    return pl.BlockSpec(shape, lambda i: (0,) * nd, pipeline_mode=pl.Buffered(1))


def _rms(x):
    return x * lax.rsqrt(jnp.mean(x * x, axis=-1, keepdims=True) + RMS_EPS)


def _modulated(x, g, mod, k):
    shift = mod[:, k * D_MODEL:(k + 1) * D_MODEL]
    scale = mod[:, (k + 1) * D_MODEL:(k + 2) * D_MODEL]
    return (_rms(x) * g) * (1.0 + scale) + shift


def _mod_kernel(c_ref, w_ref, b_ref, o_ref):
    c = c_ref[...]
    a = (c * jax.nn.sigmoid(c)).astype(bf16)
    o_ref[...] = jnp.dot(a, w_ref[...].astype(bf16), preferred_element_type=f32) + b_ref[...]


def adaln_all(cond, w_mod, b_mod):
    tn = 1536
    nj = N_MOD * D_MODEL // tn
    return pl.pallas_call(
        _mod_kernel,
        grid=(DEPTH, nj),
        in_specs=[
            pl.BlockSpec((SUBLANES, D_MODEL), lambda l, j: (0, 0)),
            pl.BlockSpec((None, D_MODEL, tn), lambda l, j: (l, 0, j)),
            pl.BlockSpec((None, 1, tn), lambda l, j: (l, 0, j)),
        ],
        out_specs=pl.BlockSpec((None, SUBLANES, tn), lambda l, j: (l, 0, j)),
        out_shape=jax.ShapeDtypeStruct((DEPTH, SUBLANES, N_MOD * D_MODEL), f32),
        compiler_params=pltpu.CompilerParams(
            dimension_semantics=("arbitrary", "arbitrary"), vmem_limit_bytes=VMEM_LIMIT),
        name="adaln",
    )(cond, w_mod, b_mod.reshape(DEPTH, 1, N_MOD * D_MODEL))


def _cast_kernel(x_ref, o_ref):
    o_ref[...] = x_ref[...].astype(o_ref.dtype)


def cast_bf16(w, cols=None):
    n_l, k, n = w.shape
    cols = n if cols is None else cols
    bk = 512
    spec = pl.BlockSpec((None, bk, cols), lambda l, i: (l, i, 0))
    return pl.pallas_call(
        _cast_kernel,
        grid=(n_l, k // bk),
        in_specs=[spec],
        out_specs=spec,
        out_shape=jax.ShapeDtypeStruct((n_l, k, cols), bf16),
        compiler_params=pltpu.CompilerParams(
            dimension_semantics=("arbitrary", "arbitrary"), vmem_limit_bytes=VMEM_LIMIT),
        name="cast_bf16",
    )(w)


def _layer_spec(w, l):
    nd = w.ndim - 1
    return pl.BlockSpec((None,) + w.shape[1:], lambda i: (l,) + (0,) * nd, pipeline_mode=pl.Buffered(1))


def _tile_x(x_refs, tm):
    if len(x_refs) == 1:
        return x_refs[0][...]
    return jnp.where(pl.program_id(0) < N_PROMPT // tm, x_refs[0][...], x_refs[1][...])


def _conv_silu_tile(y, prev_row, next_row, w, b):
    i = pl.program_id(0)
    rows = y.shape[0]
    g = i * rows + lax.broadcasted_iota(jnp.int32, (rows, 1), 0)
    is_prompt = i < N_PROMPT // rows
    pos = jnp.where(is_prompt, g & (SEQ - 1), (g - N_PROMPT) & (DEC_SEQ - 1))
    seq_last = jnp.where(is_prompt, SEQ - 1, DEC_SEQ - 1)
    row = lax.broadcasted_iota(jnp.int32, y.shape, 0)
    up = jnp.where(row == 0, prev_row, pltpu.roll(y, 1, axis=0))
    dn = jnp.where(row == rows - 1, next_row, pltpu.roll(y, rows - 1, axis=0))
    up = jnp.where(pos == 0, 0.0, up)
    dn = jnp.where(pos == seq_last, 0.0, dn)
    c = up * w[0:1, :] + y * w[1:2, :] + dn * w[2:3, :] + b
    return c * jax.nn.sigmoid(c)


def _in_proj_kernel(n_x, n_buf, row_block, segs, outs, *refs):
    x_refs = refs[:n_x]
    prev_ref, next_ref, mod_ref, g_ref, w_ref, wt_ref, cw_ref, cb_ref = refs[n_x:n_x + 8]
    o_refs = refs[n_x + 8 + n_buf:]
    mod, g = mod_ref[...], g_ref[...]
    x = _tile_x(x_refs, TM)
    n_blocks = TM // row_block
    halo = _modulated(jnp.concatenate([prev_ref[...], next_ref[...]], axis=0), g, mod, 0).astype(bf16)
    parts = [[] for _ in segs]
    for r in range(n_blocks):
        hb = _modulated(x[r * row_block:(r + 1) * row_block], g, mod, 0).astype(bf16)
        for i, (a, b, scale, conv) in enumerate(segs):
            w = wt_ref[...] if a is None else w_ref[:, a:b]
            lhs = jnp.concatenate([hb, halo], axis=0) if conv and r == n_blocks - 1 else hb
            parts[i].append(jnp.dot(lhs, w, preferred_element_type=f32))
    ys = []
    for (a, b, scale, conv), p in zip(segs, parts):
        y = jnp.concatenate(p, axis=0)
        if conv:
            y = _conv_silu_tile(y[:TM], y[TM + SUBLANES - 1:TM + SUBLANES], y[TM + SUBLANES:TM + SUBLANES + 1],
                                cw_ref[...], cb_ref[...])
        ys.append(y if scale == 1.0 else y * scale)
    for (si, _, head_major), o_ref in zip(outs, o_refs):
        if head_major:
            @pl.when(pl.program_id(0) < N_PROMPT // TM)
            def _(o_ref=o_ref, si=si):
                for b in range(TM // SEQ):
                    for hd in range(NA_HEADS):
                        o_ref[b, hd] = ys[si][b * SEQ:(b + 1) * SEQ, hd * NA_HEAD_DIM:(hd + 1) * NA_HEAD_DIM]
        else:
            o_ref[...] = ys[si].astype(o_ref.dtype)


def _x_specs(xs, tm):
    if len(xs) == 1:
        return [pl.BlockSpec((tm, D_MODEL), lambda i: (i, 0))]
    n_p = N_PROMPT // tm
    return [pl.BlockSpec((tm, D_MODEL), lambda i: (jnp.minimum(i, n_p - 1), 0)),
            pl.BlockSpec((tm, D_MODEL), lambda i: (jnp.maximum(i - n_p, 0), 0))]


def in_proj(xs, mod_l, g, w, w_tail, l, segs, outs, conv_w, conv_b, cache_bufs=None, row_block=TM):
    grid = (N_TOK // TM,)
    last_prompt = N_PROMPT // TM - 1
    halo_src = xs[-1]
    per = TM // SUBLANES
    blk0 = (halo_src.shape[0] - DEC_BATCH * DEC_SEQ) // SUBLANES
    n_blk = halo_src.shape[0] // SUBLANES
    tile0 = N_PROMPT // TM

    def prev_map(i):
        return (jnp.clip(blk0 + (i - tile0) * per - 1, 0, n_blk - 1), 0)

    def next_map(i):
        return (jnp.clip(blk0 + (i - tile0 + 1) * per, 0, n_blk - 1), 0)

    ch = conv_w.shape[1]
    out_shape, out_specs = [], []
    for si, dt, head_major in outs:
        width = LANES if segs[si][0] is None else segs[si][1] - segs[si][0]
        if head_major:
            out_shape.append(jax.ShapeDtypeStruct((BATCH, N_EVEN, NA_HEADS, SEQ, NA_HEAD_DIM), dt))
            out_specs.append(pl.BlockSpec((TM // SEQ, None, NA_HEADS, SEQ, NA_HEAD_DIM),
                                          lambda i: (jnp.minimum(i, last_prompt), l, 0, 0, 0)))
        else:
            out_shape.append(jax.ShapeDtypeStruct((N_TOK, width), dt))
            out_specs.append(pl.BlockSpec((TM, width), lambda i: (i, 0)))
    args = [*xs, halo_src, halo_src, mod_l, g.reshape(1, D_MODEL), w, w_tail, conv_w, conv_b.reshape(1, ch)]
    aliases = {}
    if cache_bufs is not None:
        head_major_outs = [k for k, o in enumerate(outs) if o[2]]
        for k_out, buf in zip(head_major_outs, cache_bufs):
            aliases[len(args)] = k_out
            args.append(buf)
    return pl.pallas_call(
        functools.partial(_in_proj_kernel, len(xs), len(aliases), row_block, segs, outs),
        input_output_aliases=aliases,
        grid=grid,
        in_specs=_x_specs(xs, TM) + [
            pl.BlockSpec((SUBLANES, D_MODEL), prev_map),
            pl.BlockSpec((SUBLANES, D_MODEL), next_map),
            pl.BlockSpec((None, 1, N_MOD * D_MODEL), lambda i: (_cond_row(i, TM), 0, 0)),
            _const_spec((1, D_MODEL)),
            _layer_spec(w, l),
            _layer_spec(w_tail, l),
            _const_spec((CONV_K, ch)),
            _const_spec((1, ch)),
        ] + [pl.BlockSpec(memory_space=pl.ANY)] * len(aliases),
        out_specs=out_specs,
        out_shape=out_shape,
        compiler_params=pltpu.CompilerParams(
            dimension_semantics=("arbitrary",), vmem_limit_bytes=VMEM_LIMIT),
        name="in_proj",
    )(*args)


def _out_mlp_kernel(n_x, n_mix, final, *refs):
    x_refs = refs[:n_x]
    mod_ref, g_ref = refs[n_x:n_x + 2]
    refs = refs[n_x + 2:]
    mix_refs = refs[:n_mix]
    wo_ref, w1_ref, w2_ref = refs[n_mix:n_mix + 3]
    rest = refs[n_mix + 3:]
    if final:
        gf_ref, op_ref, os_ref = rest
    else:
        (o_ref,) = rest
    mod = mod_ref[...]
    m = None
    k0 = 0
    for r in mix_refs:
        kw = r.shape[-1]
        part = jnp.dot(r[...].astype(bf16), wo_ref[k0:k0 + kw, :], preferred_element_type=f32)
        m = part if m is None else m + part
        k0 += kw
    x1 = _tile_x(x_refs, OUT_TM) + mod[:, 2 * D_MODEL:3 * D_MODEL] * m
    h2 = _modulated(x1, g_ref[...], mod, 3).astype(bf16)
    u = jnp.dot(h2, w1_ref[...], preferred_element_type=f32)
    a = jnp.square(jnp.maximum(u, 0.0)).astype(bf16)
    x2 = x1 + mod[:, 5 * D_MODEL:6 * D_MODEL] * jnp.dot(a, w2_ref[...], preferred_element_type=f32)
    if not final:
        o_ref[...] = x2
        return
    y = _rms(x2) * gf_ref[...]
    is_prompt = pl.program_id(0) < N_PROMPT // OUT_TM

    @pl.when(is_prompt)
    def _():
        op_ref[...] = y

    @pl.when(jnp.logical_not(is_prompt))
    def _():
        os_ref[...] = y


OUT_TM = 512


def out_mlp(xs, mod_l, g_ffn, mixes, w_out, l_out, w1, w2, l, norm_f=None):
    final = norm_f is not None
    tm = OUT_TM
    grid = (N_TOK // tm,)
    in_specs = _x_specs(xs, tm) + [
        pl.BlockSpec((None, 1, N_MOD * D_MODEL), lambda i: (_cond_row(i, tm), 0, 0)),
        _const_spec((1, D_MODEL)),
    ]
    in_specs += [pl.BlockSpec((tm, m.shape[-1]), lambda i: (i, 0)) for m in mixes]
    in_specs += [_layer_spec(w_out, l_out), _layer_spec(w1, l), _layer_spec(w2, l)]
    args = [*xs, mod_l, g_ffn.reshape(1, D_MODEL), *mixes, w_out, w1, w2]
    out_specs = pl.BlockSpec((tm, D_MODEL), lambda i: (i, 0))
    out_shape = jax.ShapeDtypeStruct((N_TOK, D_MODEL), f32)
    if final:
        in_specs.append(_const_spec((1, D_MODEL)))
        args.append(norm_f.reshape(1, D_MODEL))
        n_p = N_PROMPT // tm
        out_specs = [pl.BlockSpec((tm, D_MODEL), lambda i: (jnp.minimum(i, n_p - 1), 0)),
                     pl.BlockSpec((tm, D_MODEL), lambda i: (jnp.maximum(i - n_p, 0), 0))]
        out_shape = [jax.ShapeDtypeStruct((N_PROMPT, D_MODEL), f32),
                     jax.ShapeDtypeStruct((N_TOK - N_PROMPT, D_MODEL), f32)]
    return pl.pallas_call(
        functools.partial(_out_mlp_kernel, len(xs), len(mixes), final),
        grid=grid,
        in_specs=in_specs,
        out_specs=out_specs,
        out_shape=out_shape,
        compiler_params=pltpu.CompilerParams(
            dimension_semantics=("arbitrary",), vmem_limit_bytes=VMEM_LIMIT),
        name="out_mlp",
    )(*args)


NA_PAIRS = NA_HEADS // 2
NA_ROWS = DEC_SEQ // GRID_W
NA_WIN = NA_WIN_ROWS * GRID_W


def _dot_nt(a, b):
    return lax.dot_general(a, b, (((1,), (1,)), ((), ())), preferred_element_type=f32)


def _pair_stack(x):
    lane = lax.broadcasted_iota(jnp.int32, x.shape, 1)
    zero = jnp.zeros_like(x)
    return jnp.concatenate([jnp.where(lane < NA_HEAD_DIM, x, zero), jnp.where(lane >= NA_HEAD_DIM, x, zero)], axis=0)


def _pair_unstack(o):
    n = o.shape[0] // 2
    lane = lax.broadcasted_iota(jnp.int32, (n, LANES), 1)
    return jnp.where(lane < NA_HEAD_DIM, o[:n], o[n:])


def _softmax_pv(scores, values):
    def lane_tiles(blocks):
        return [b[:, k * LANES:(k + 1) * LANES] for b in blocks for k in range(b.shape[1] // LANES)]

    ms = [jnp.max(functools.reduce(jnp.maximum, lane_tiles(s)), axis=1, keepdims=True) for s in scores]
    ps = [[jnp.exp(b - m) for b in s] for s, m in zip(scores, ms)]
    invs = [1.0 / jnp.sum(functools.reduce(jnp.add, lane_tiles(p)), axis=1, keepdims=True) for p in ps]
    pn = [[(b * inv).astype(bf16) for b in p] for p, inv in zip(ps, invs)]
    return [functools.reduce(jnp.add, [jnp.dot(b, v, preferred_element_type=f32) for b, v in zip(p, vs)])
            for p, vs in zip(pn, values)]


def _ctx_attn_kernel(q_ref, k_ref, v_ref, o_ref):
    pairs = [slice(p * LANES, (p + 1) * LANES) for p in range(NA_PAIRS)]
    scores = [[_dot_nt(_pair_stack(q_ref[:, lanes]), k_ref[:, lanes])] for lanes in pairs]
    outs = _softmax_pv(scores, [[v_ref[:, lanes]] for lanes in pairs])
    for lanes, o in zip(pairs, outs):
        o_ref[:, lanes] = _pair_unstack(o).astype(o_ref.dtype)


def ctx_attention(q, k, v):
    spec = pl.BlockSpec((SEQ, NA_WIDTH), lambda b: (b, 0))
    return pl.pallas_call(
        _ctx_attn_kernel,
        grid=(BATCH,),
        in_specs=[spec, spec, spec],
        out_specs=spec,
        out_shape=jax.ShapeDtypeStruct((N_TOK, NA_WIDTH), bf16),
        compiler_params=pltpu.CompilerParams(dimension_semantics=("arbitrary",), vmem_limit_bytes=VMEM_LIMIT),
        name="ctx_attn",
    )(q, k, v)


def _na_bias_kernel(rpb_ref, o_ref):
    pair = pl.program_id(0)
    shape = (GRID_W, LANES)
    qc = lax.broadcasted_iota(jnp.int32, shape, 0)
    lane = lax.broadcasted_iota(jnp.int32, shape, 1)
    kc = lane & (GRID_W - 1)
    low = lane < GRID_W
    col_start = jnp.clip(qc - NA_WIN_COLS // 2, 0, GRID_W - NA_WIN_COLS)
    valid = (kc >= col_start) & (kc < col_start + NA_WIN_COLS)
    rel_c = jnp.clip(kc - qc + NA_WIN_COLS - 1, 0, NA_RPB_COLS - 1)
    for e in range(2):
        base = (2 * pair + e) * NA_RPB_ROWS
        pieces = []
        for rr in range(NA_RPB_ROWS - 1):
            val = jnp.zeros(shape, f32)
            for t in range(NA_RPB_COLS):
                s_lo = rpb_ref[(base + rr) * NA_RPB_COLS + t]
                s_hi = rpb_ref[(base + rr + 1) * NA_RPB_COLS + t]
                val = jnp.where(rel_c == t, jnp.where(low, s_lo, s_hi), val)
            pieces.append(jnp.where(valid, val, -jnp.inf))
        for d in range(NA_WIN_ROWS):
            for i in range(0, NA_WIN_ROWS, 2):
                o_ref[d, e * GRID_W:(e + 1) * GRID_W, i * GRID_W:(i + 2) * GRID_W] = pieces[d + i]


def na_bias_table(rpb):
    return pl.pallas_call(
        _na_bias_kernel,
        grid=(NA_PAIRS,),
        in_specs=[pl.BlockSpec(memory_space=pltpu.SMEM)],
        out_specs=pl.BlockSpec((None, NA_WIN_ROWS, 2 * GRID_W, NA_WIN), lambda p: (p, 0, 0, 0)),
        out_shape=jax.ShapeDtypeStruct((NA_PAIRS, NA_WIN_ROWS, 2 * GRID_W, NA_WIN), f32),
        compiler_params=pltpu.CompilerParams(dimension_semantics=("arbitrary",), vmem_limit_bytes=VMEM_LIMIT),
        name="na_bias",
    )(rpb.reshape(-1))


NA_STEP_ROWS = 8


def _na_first_key_row(r):
    return jnp.clip(r - NA_WIN_ROWS // 2, 0, NA_ROWS - NA_WIN_ROWS)


def _na_kernel(q_ref, k_ref, v_ref, kc_ref, vc_ref, bias_ref, buf_ref, o_ref):
    del buf_ref
    where, scores, values = [], [], []
    for j in range(NA_STEP_ROWS):
        r = pl.program_id(1) * NA_STEP_ROWS + j
        first = _na_first_key_row(r)
        start = pl.multiple_of(first * GRID_W, GRID_W)
        shift = first - r + NA_WIN_ROWS - 1
        rows = slice(j * GRID_W, (j + 1) * GRID_W)
        for p in range(NA_PAIRS):
            lanes = slice(p * LANES, (p + 1) * LANES)
            qq = _pair_stack(q_ref[rows, lanes])
            where.append((rows, lanes))
            scores.append([_dot_nt(qq, k_ref[pl.ds(start, NA_WIN), lanes]) + bias_ref[p, shift],
                           _dot_nt(qq, kc_ref[:, lanes])])
            values.append([v_ref[pl.ds(start, NA_WIN), lanes], vc_ref[:, lanes]])
    for (rows, lanes), o in zip(where, _softmax_pv(scores, values)):
        o_ref[rows, lanes] = _pair_unstack(o).astype(o_ref.dtype)


def na_latent(q, k, v, k_ctx, v_ctx, bias, buf):
    rows = NA_STEP_ROWS * GRID_W
    steps = NA_ROWS // NA_STEP_ROWS
    row0 = N_PROMPT // rows
    seq0 = N_PROMPT // DEC_SEQ
    kv_spec = pl.BlockSpec((DEC_SEQ, NA_WIDTH), lambda b, r: (seq0 + b, 0))
    ctx_spec = pl.BlockSpec((None, PAST_LEN, NA_WIDTH), lambda b, r: (b, 0, 0))
    return pl.pallas_call(
        _na_kernel,
        grid=(DEC_BATCH, steps),
        in_specs=[
            pl.BlockSpec((rows, NA_WIDTH), lambda b, r: (row0 + b * steps + r, 0)),
            kv_spec, kv_spec, ctx_spec, ctx_spec,
            pl.BlockSpec(bias.shape, lambda b, r: (0, 0, 0, 0), pipeline_mode=pl.Buffered(1)),
            pl.BlockSpec(memory_space=pl.ANY),
        ],
        out_specs=pl.BlockSpec((rows, NA_WIDTH), lambda b, r: (row0 + b * steps + r, 0)),
        out_shape=jax.ShapeDtypeStruct((N_TOK, NA_WIDTH), bf16),
        input_output_aliases={6: 0},
        compiler_params=pltpu.CompilerParams(
            dimension_semantics=("arbitrary", "arbitrary"), vmem_limit_bytes=VMEM_LIMIT),
        name="na_latent",
    )(q, k, v, k_ctx, v_ctx, bias, buf)


CHUNK = 128
SCAN_G_MAX = 4


def _cumsum_rows(a, reverse=False):
    row = lax.broadcasted_iota(jnp.int32, a.shape, 0)
    s = 1
    while s < CHUNK:
        if reverse:
            a = a + jnp.where(row < CHUNK - s, pltpu.roll(a, CHUNK - s, axis=0), 0.0)
        else:
            a = a + jnp.where(row >= s, pltpu.roll(a, s, axis=0), 0.0)
        s *= 2
    return a


def _ssd_kernel(nc, grp, has_h0, n_buf, emit_state, *refs):
    xbc_ref, dt_ref, z_ref = refs[:3]
    refs = refs[3:]
    if has_h0:
        h0_ref, refs = refs[0], refs[1:]
    dtb_ref, alog_ref, dskip_ref, g_ref = refs[:4]
    refs = refs[4 + n_buf:]
    y_ref, refs = refs[0], refs[1:]
    if emit_state:
        hfin_ref, refs = refs[0], refs[1:]
    hb_store, xt_store, gate_store, carry = refs

    phase = pl.program_id(1)
    gi = pl.program_id(2)
    nb = nc // grp
    gw = SSD_RPG * SSD_HEAD_DIM

    def load_h0(d):
        if has_h0:
            return h0_ref[d].reshape(SSD_INNER, SSD_STATE)
        return jnp.zeros((SSD_INNER, SSD_STATE), f32)

    def head_rows(v):
        return jnp.concatenate(
            [jnp.broadcast_to(v[h:h + 1, :], (SSD_HEAD_DIM, v.shape[1])) for h in range(SSD_HEADS)], axis=0)

    def state_update(x_t, b_bf, cum_t, dt_t, edge):
        at_edge = jnp.broadcast_to(cum_t[:, edge:edge + 1], cum_t.shape)
        w_end = jnp.exp(at_edge - cum_t) * dt_t
        xw = (x_t * head_rows(w_end)).astype(bf16)
        upd = jnp.concatenate(
            [jnp.dot(xw[g * gw:(g + 1) * gw], b_bf[:, g * SSD_STATE:(g + 1) * SSD_STATE], preferred_element_type=f32)
             for g in range(SSD_GROUPS)], axis=0)
        carry[...] = carry[...] * head_rows(jnp.exp(at_edge)) + upd

    @pl.when(jnp.logical_and(phase == 0, gi == 0))
    def _():
        carry[...] = load_h0(1)

    def backward_chunk(sub):
        rows = slice(sub * CHUNK, (sub + 1) * CHUNK)
        j = grp * (nb - 1 - gi) + sub
        x = xbc_ref[rows, :SSD_INNER]
        b_bf = xbc_ref[rows, SSD_INNER:SSD_INNER + SSD_GN].astype(bf16)
        dt = jax.nn.softplus(dt_ref[rows, :] + dtb_ref[...])
        a = dt * (-jnp.exp(alog_ref[...]))
        x_t = jnp.concatenate([x[:, k * LANES:(k + 1) * LANES].T for k in range(SSD_INNER // LANES)], axis=0)
        rcum = _cumsum_rows(a, reverse=True)
        xt_store[j] = x_t
        gate_store[j, 0] = dt
        gate_store[j, 1] = _cumsum_rows(a)
        gate_store[j, 2] = rcum
        hb_store[j] = carry[...].astype(bf16)
        state_update(x_t, b_bf, rcum.T[SSD_HEADS:2 * SSD_HEADS], dt.T[SSD_HEADS:2 * SSD_HEADS], 0)

    @pl.when(phase == 0)
    def _backward_states():
        for sub in reversed(range(grp)):
            backward_chunk(sub)

        if emit_state:
            @pl.when(gi == nb - 1)
            def _():
                hfin_ref[1] = carry[...].reshape(SSD_HEADS, SSD_HEAD_DIM, SSD_STATE)

    @pl.when(jnp.logical_and(phase == 1, gi == 0))
    def _():
        carry[...] = load_h0(0)

    @pl.when(phase == 1)
    def _forward_and_outputs():
        subs = range(grp)
        heads = range(SSD_HEADS)
        sub_heads = [(sub, h) for sub in subs for h in heads]
        rows = [slice(sub * CHUNK, (sub + 1) * CHUNK) for sub in subs]
        cidx = [grp * gi + sub for sub in subs]
        row = lax.broadcasted_iota(jnp.int32, (CHUNK, CHUNK), 0)
        col = lax.broadcasted_iota(jnp.int32, (CHUNK, CHUNK), 1)
        causal = col <= row
        anti = col >= row
        lane = lax.broadcasted_iota(jnp.int32, (CHUNK, LANES), 1)
        x = [xbc_ref[rw, :SSD_INNER] for rw in rows]
        x_bf = [xs.astype(bf16) for xs in x]
        b_bf = [xbc_ref[rw, SSD_INNER:SSD_INNER + SSD_GN].astype(bf16) for rw in rows]
        c_mat = [xbc_ref[rw, SSD_INNER + SSD_GN:].astype(bf16) for rw in rows]
        dt = [gate_store[c, 0] for c in cidx]
        cum = [gate_store[c, 1] for c in cidx]
        rcum = [gate_store[c, 2] for c in cidx]
        cum_t = [t.T for t in cum]
        rcum_t = [t.T for t in rcum]
        dt_t = [t.T for t in dt]
        cb = {(sub, g): _dot_nt(c_mat[sub][:, g * SSD_STATE:(g + 1) * SSD_STATE],
                                b_bf[sub][:, g * SSD_STATE:(g + 1) * SSD_STATE])
              for sub in subs for g in range(SSD_GROUPS)}
        log_dt = [jnp.log(t) for t in dt_t]
        key_f = [cum_t[sub] - log_dt[sub] for sub in subs]
        key_b = [rcum_t[sub] - log_dt[sub] for sub in subs]
        seg_f = {(sub, h): jnp.where(causal, cum[sub][:, h:h + 1] - key_f[sub][h:h + 1, :], -jnp.inf)
                 for sub, h in sub_heads}
        seg_b = {(sub, h): jnp.where(anti, rcum[sub][:, SSD_HEADS + h:SSD_HEADS + h + 1]
                                     - key_b[sub][SSD_HEADS + h:SSD_HEADS + h + 1, :], -jnp.inf)
                 for sub, h in sub_heads}
        e_f = {sh: jnp.exp(seg_f[sh]) for sh in sub_heads}
        e_b = {sh: jnp.exp(seg_b[sh]) for sh in sub_heads}
        ws = {(sub, h): (cb[sub, h // SSD_RPG] * (e_f[sub, h] + e_b[sub, h])).astype(bf16) for sub, h in sub_heads}
        rhs = {}
        for sub in subs:
            for p in range(SSD_HEADS // 2):
                xp = x_bf[sub][:, p * LANES:(p + 1) * LANES]
                zero = jnp.zeros_like(xp)
                rhs[sub, p] = jnp.concatenate([jnp.where(lane < SSD_HEAD_DIM, xp, zero),
                                               jnp.where(lane >= SSD_HEAD_DIM, xp, zero)], axis=0)
        y_intra = [jnp.concatenate(
            [jnp.dot(jnp.concatenate([ws[sub, 2 * p], ws[sub, 2 * p + 1]], axis=1), rhs[sub, p],
                     preferred_element_type=f32) for p in range(SSD_HEADS // 2)], axis=1) for sub in subs]

        for sub in subs:
            hf = carry[...].astype(bf16)
            hb = hb_store[cidx[sub]]

            def inter_t(h_all):
                return jnp.concatenate(
                    [_dot_nt(h_all[g * gw:(g + 1) * gw], c_mat[sub][:, g * SSD_STATE:(g + 1) * SSD_STATE])
                     for g in range(SSD_GROUPS)], axis=0)

            cum_f, cum_b = cum_t[sub][:SSD_HEADS], rcum_t[sub][SSD_HEADS:2 * SSD_HEADS]
            y_t = inter_t(hf) * head_rows(jnp.exp(cum_f)) + inter_t(hb) * head_rows(jnp.exp(cum_b))
            y_inter = jnp.concatenate([y_t[k * LANES:(k + 1) * LANES].T for k in range(SSD_INNER // LANES)], axis=1)
            state_update(xt_store[cidx[sub]], b_bf[sub], cum_f, dt_t[sub][:SSD_HEADS], CHUNK - 1)
            y = y_intra[sub] + y_inter + dskip_ref[...] * x[sub]

            zv = z_ref[rows[sub], :]
            yz = y * (zv * jax.nn.sigmoid(zv))
            y_ref[rows[sub], :] = (_rms(yz) * g_ref[...]).astype(y_ref.dtype)

        if emit_state:
            @pl.when(gi == nb - 1)
            def _():
                hfin_ref[0] = carry[...].reshape(SSD_HEADS, SSD_HEAD_DIM, SSD_STATE)


def ssd_mix(xbc, dt, z, h0, row0, n_seq, seq_len, dt_bias, a_log, d_skip, norm_g, emit_state, out_buf=None,
            layer=0, state_buf=None):
    nc = seq_len // CHUNK
    grp = math.gcd(nc, SCAN_G_MAX)
    blk_rows = grp * CHUNK
    nb = seq_len // blk_rows
    blk0 = row0 // blk_rows
    has_h0 = h0 is not None

    def block_map(s, p, c):
        return (blk0 + s * nb + jnp.where(p == 0, nb - 1 - c, c), 0)

    state_spec = pl.BlockSpec((None, 2, SSD_HEADS, SSD_HEAD_DIM, SSD_STATE), lambda s, p, c: (s, 0, 0, 0, 0))
    vec = lambda n: pl.BlockSpec((1, n), lambda s, p, c: (0, 0))
    in_specs = [
        pl.BlockSpec((blk_rows, SSD_CONV_DIM), block_map),
        pl.BlockSpec((blk_rows, LANES), block_map),
        pl.BlockSpec((blk_rows, SSD_INNER), lambda s, p, c: (blk0 + s * nb + p * c, 0)),
    ]
    args = [xbc, dt, z]
    if has_h0:
        in_specs.append(state_spec)
        args.append(h0)
    in_specs += [vec(LANES), vec(LANES), vec(SSD_INNER), vec(SSD_INNER)]
    pad = lambda t: jnp.concatenate([t.reshape(1, -1), jnp.zeros((1, LANES - t.size), f32)], axis=1)
    args += [pad(dt_bias), pad(a_log), jnp.repeat(d_skip, SSD_HEAD_DIM).reshape(1, SSD_INNER),
             norm_g.reshape(1, SSD_INNER)]
    aliases = {}
    if out_buf is not None:
        aliases = {len(args): 0}
        in_specs.append(pl.BlockSpec(memory_space=pl.ANY))
        args.append(out_buf)
    out_shape = [jax.ShapeDtypeStruct((N_TOK, SSD_INNER), bf16)]
    out_specs = [pl.BlockSpec((blk_rows, SSD_INNER), lambda s, p, c: (blk0 + s * nb + p * c, 0))]
    if emit_state:
        if state_buf is not None:
            aliases[len(args)] = 1
            in_specs.append(pl.BlockSpec(memory_space=pl.ANY))
            args.append(state_buf)
        out_shape.append(jax.ShapeDtypeStruct((n_seq, N_EVEN, 2, SSD_HEADS, SSD_HEAD_DIM, SSD_STATE), f32))
        out_specs.append(pl.BlockSpec((None, None, 2, SSD_HEADS, SSD_HEAD_DIM, SSD_STATE),
                                      lambda s, p, c: (s, layer, 0, 0, 0, 0)))
    res = pl.pallas_call(
        functools.partial(_ssd_kernel, nc, grp, has_h0, len(aliases), emit_state),
        input_output_aliases=aliases,
        grid=(n_seq, 2, nb),
        in_specs=in_specs,
        out_specs=out_specs,
        out_shape=out_shape,
        scratch_shapes=[pltpu.VMEM((nc, SSD_INNER, SSD_STATE), bf16), pltpu.VMEM((nc, SSD_INNER, CHUNK), f32),
                        pltpu.VMEM((nc, 3, CHUNK, LANES), f32), pltpu.VMEM((SSD_INNER, SSD_STATE), f32)],
        compiler_params=pltpu.CompilerParams(
            dimension_semantics=("arbitrary", "arbitrary", "arbitrary"), vmem_limit_bytes=VMEM_LIMIT),
        name="ssd_scan",
    )(*args)
    return res if emit_state else res[0]


ML_DIRS = 2 * ML_HEADS


def _cummax_rows(a, reverse=False):
    row = lax.broadcasted_iota(jnp.int32, a.shape, 0)
    s = 1
    while s < CHUNK:
        if reverse:
            a = jnp.maximum(a, jnp.where(row < CHUNK - s, pltpu.roll(a, CHUNK - s, axis=0), -jnp.inf))
        else:
            a = jnp.maximum(a, jnp.where(row >= s, pltpu.roll(a, s, axis=0), -jnp.inf))
        s *= 2
    return a


ML_ST = ML_V_DIM + 16


def _mlstm_t_kernel(nc, grp, has_state, n_buf, emit_state, *refs):
    qk_ref, v_ref, og_ref, gates_ref, gb_ref = refs[:5]
    refs = refs[5:]
    if has_state:
        s0_ref, m0_ref = refs[:2]
        refs = refs[2:]
    refs = refs[n_buf:]
    y_ref, refs = refs[0], refs[1:]
    if emit_state:
        cfin_ref, nfin_ref, mfin_ref = refs[:3]
        refs = refs[3:]
    s_store, m_store, gate_store, k_store, v_store, s_carry, m_carry = refs

    phase = pl.program_id(1)
    gi = pl.program_id(2)
    nb = nc // grp
    fwd_row = lax.broadcasted_iota(jnp.int32, (1, LANES), 1) < ML_HEADS

    def state_update(cum, r, k_own, v_tr, m_row, big_m, d):
        edge = CHUNK - 1 if d == 0 else 0
        m_edge = big_m[edge:edge + 1, :]
        wc_row = jnp.exp(m_row - m_edge)
        wk_t = jnp.exp(r.T - big_m.T[:, edge:edge + 1])
        lhs = []
        for h in range(ML_HEADS):
            wk = wk_t[d * ML_HEADS + h:d * ML_HEADS + h + 1, :]
            lhs.append(jnp.concatenate([v_tr[h] * wk, jnp.broadcast_to(wk, (ML_ST - ML_V_DIM, CHUNK))],
                                       axis=0).astype(bf16))
        upd = [jnp.dot(lhs[h], k_own[h], preferred_element_type=f32) for h in range(ML_HEADS)]
        for h in range(ML_HEADS):
            cl = d * ML_HEADS + h
            wc = jnp.broadcast_to(wc_row[:, cl:cl + 1], (ML_ST, LANES))
            s_carry[d, h] = wc * s_carry[d, h] + upd[h]
        return cum[edge:edge + 1, :] + m_edge

    def init_state(d):
        if has_state:
            s_carry[d] = s0_ref[d]
        else:
            s_carry[d] = jnp.zeros((ML_HEADS, ML_ST, LANES), f32)

    def emit_final(d):
        for h in range(ML_HEADS):
            tile = s_carry[d, h]
            if h % 2 == 1:
                tile = pltpu.roll(tile, ML_QK_DIM, axis=1)
            cfin_ref[d, h] = tile[:ML_V_DIM, :ML_QK_DIM]
            nfin_ref[d, h:h + 1, :] = tile[ML_V_DIM:ML_V_DIM + 1, :ML_QK_DIM]

    @pl.when(jnp.logical_and(phase == 0, gi == 0))
    def _():
        init_state(1)
        m_carry[...] = jnp.broadcast_to(m0_ref[...], m_carry.shape) if has_state else jnp.zeros(m_carry.shape, f32)

    def backward_chunk(sub):
        rows = slice(sub * CHUNK, (sub + 1) * CHUNK)
        lane = lax.broadcasted_iota(jnp.int32, (CHUNK, LANES), 1)
        fwd_lane = lane < ML_HEADS
        g = gates_ref[rows, :] + gb_ref[...]
        lf = pltpu.roll(jax.nn.log_sigmoid(g), LANES - ML_DIRS, axis=1)
        cum = jnp.where(fwd_lane, _cumsum_rows(lf), _cumsum_rows(lf, reverse=True))
        r = g - cum
        pm = jnp.where(fwd_lane, _cummax_rows(r), _cummax_rows(r, reverse=True))
        k = qk_ref[rows, ML_QK_WIDTH:]
        k_own = [jnp.where((lane < ML_QK_DIM) if h % 2 == 0 else (lane >= ML_QK_DIM),
                           k[:, (h // 2) * LANES:(h // 2 + 1) * LANES], 0.0).astype(bf16) for h in range(ML_HEADS)]
        v_tr = [v_ref[rows, h * ML_V_DIM:(h + 1) * ML_V_DIM].astype(f32).T for h in range(ML_HEADS)]

        j = grp * (nb - 1 - gi) + sub
        gate_store[j, 0] = cum
        gate_store[j, 1] = r
        gate_store[j, 2] = pm
        for h in range(ML_HEADS):
            k_store[j, h] = k_own[h]
            v_store[j, h] = v_tr[h].astype(bf16)
        m_row = m_carry[0:1, :]
        s_store[j] = s_carry[1].astype(bf16)
        m_store[j] = m_carry[...]
        big_m = jnp.maximum(m_row, pm)
        m_new = state_update(cum, r, k_own, v_tr, m_row, big_m, 1)
        m_carry[...] = jnp.broadcast_to(jnp.where(fwd_row, m_row, m_new), m_carry.shape)

    @pl.when(phase == 0)
    def _backward_states():
        for sub in reversed(range(grp)):
            backward_chunk(sub)

    @pl.when(jnp.logical_and(phase == 1, gi == 0))
    def _():
        init_state(0)
        if emit_state:
            emit_final(1)

    @pl.when(phase == 1)
    def _forward_and_outputs():
        subs = range(grp)
        heads = range(ML_HEADS)
        chains = [(sub, h, d) for sub in subs for h in heads for d in range(2)]
        lane_of = lambda h, d: d * ML_HEADS + h
        rows = [slice(sub * CHUNK, (sub + 1) * CHUNK) for sub in subs]
        cidx = [grp * gi + sub for sub in subs]
        cum = [gate_store[c, 0] for c in cidx]
        r = [gate_store[c, 1] for c in cidx]
        pm = [gate_store[c, 2] for c in cidx]
        k_own = [[k_store[c, h] for h in heads] for c in cidx]
        v_th = [[v_store[c, h] for h in heads] for c in cidx]

        m_both = m_carry[0:1, :]
        m_row, big_m = [], []
        m_fwd = m_both
        for sub in subs:
            m_row.append(jnp.where(fwd_row, m_fwd, m_store[cidx[sub]][0:1, :]))
            big_m.append(jnp.maximum(m_row[sub], pm[sub]))
            m_fwd = cum[sub][CHUNK - 1:CHUNK, :] + big_m[sub][CHUNK - 1:CHUNK, :]
        m_fin = jnp.where(fwd_row, m_fwd, m_both)
        m_carry[...] = jnp.broadcast_to(m_fin, m_carry.shape)

        big_m_t = [t.T for t in big_m]
        w_inter_t = [jnp.exp(m_row[sub] - big_m[sub]).T for sub in subs]
        floor_t = [jnp.exp(-(cum[sub] + big_m[sub])).T for sub in subs]
        key = lax.broadcasted_iota(jnp.int32, (CHUNK, CHUNK), 0)
        qry = lax.broadcasted_iota(jnp.int32, (CHUNK, CHUNK), 1)
        masks = (key <= qry, key >= qry)
        q = [(qk_ref[rw, :ML_QK_WIDTH] * (ML_QK_DIM ** -0.5)).astype(bf16) for rw in rows]
        q_pair = {(sub, h): q[sub][:, (h // 2) * LANES:(h // 2 + 1) * LANES] for sub in subs for h in heads}
        s_raw_t = {(sub, h): _dot_nt(k_own[sub][h], q_pair[sub, h]) for sub in subs for h in heads}
        w_t = {(sub, h, d): jnp.exp(jnp.where(
            masks[d], r[sub][:, lane_of(h, d):lane_of(h, d) + 1] - big_m_t[sub][lane_of(h, d):lane_of(h, d) + 1, :],
            -jnp.inf)) for sub, h, d in chains}
        sw_t = {(sub, h, d): s_raw_t[sub, h] * w_t[sub, h, d] for sub, h, d in chains}
        num = {(sub, h, d): jnp.dot(v_th[sub][h], sw_t[sub, h, d].astype(bf16), preferred_element_type=f32)
               for sub, h, d in chains}
        den_intra = {ch: jnp.sum(sw_t[ch], axis=0, keepdims=True) for ch in chains}
        wi = {(sub, h, d): w_inter_t[sub][lane_of(h, d):lane_of(h, d) + 1, :] for sub, h, d in chains}
        gate = {(sub, h): jax.nn.sigmoid(og_ref[rows[sub], h * ML_V_DIM:(h + 1) * ML_V_DIM])
                for sub in subs for h in heads}

        for sub in subs:
            hd_pairs = [(h, d) for h in heads for d in range(2)]
            inter = {(h, d): _dot_nt(s_carry[0, h].astype(bf16) if d == 0 else s_store[cidx[sub], h],
                                     q_pair[sub, h]) for h, d in hd_pairs}
            den = {(h, d): den_intra[sub, h, d] + wi[sub, h, d] * inter[h, d][ML_V_DIM:ML_V_DIM + 1]
                   for h, d in hd_pairs}
            inv = {(h, d): 1.0 / jnp.maximum(jnp.abs(den[h, d]),
                                             floor_t[sub][lane_of(h, d):lane_of(h, d) + 1, :]) for h, d in hd_pairs}
            part = {(h, d): (num[sub, h, d] + wi[sub, h, d] * inter[h, d][:ML_V_DIM]) * inv[h, d]
                    for h, d in hd_pairs}
            out = [(part[h, 0] + part[h, 1]).T for h in heads]
            for h in heads:
                y_ref[rows[sub], h * ML_V_DIM:(h + 1) * ML_V_DIM] = (out[h] * gate[sub, h]).astype(y_ref.dtype)
            v_tr = [v.astype(f32) for v in v_th[sub]]
            state_update(cum[sub], r[sub], k_own[sub], v_tr, m_row[sub], big_m[sub], 0)

        if emit_state:
            @pl.when(gi == nb - 1)
            def _():
                emit_final(0)
                mfin_ref[...] = m_fin


def mlstm_mix(qk, v, og, gates, gate_b, state, row0, n_seq, seq_len, emit_state, out_buf=None,
              layer=0, state_bufs=None):
    nc = seq_len // CHUNK
    grp = math.gcd(nc, SCAN_G_MAX)
    blk_rows = grp * CHUNK
    nb = seq_len // blk_rows
    blk0 = row0 // blk_rows
    has_state = state is not None

    def block_map(s, p, c):
        return (blk0 + s * nb + jnp.where(p == 0, nb - 1 - c, c), 0)

    m_spec = pl.BlockSpec((None, 1, LANES), lambda s, p, c: (s, 0, 0))
    in_specs = [
        pl.BlockSpec((blk_rows, 2 * ML_QK_WIDTH), block_map),
        pl.BlockSpec((blk_rows, ML_V_WIDTH), block_map),
        pl.BlockSpec((blk_rows, ML_V_WIDTH), lambda s, p, c: (blk0 + s * nb + p * c, 0)),
        pl.BlockSpec((blk_rows, LANES), block_map),
        pl.BlockSpec((1, LANES), lambda s, p, c: (0, 0)),
    ]
    gb = jnp.concatenate([gate_b.reshape(1, 2 * ML_DIRS), jnp.zeros((1, LANES - 2 * ML_DIRS), f32)], axis=1)
    args = [qk, v, og, gates, gb]
    if has_state:
        c0, n0, m0 = state
        rows = jnp.concatenate([c0, jnp.broadcast_to(n0[..., None, :], n0.shape[:-1] + (ML_ST - ML_V_DIM, ML_QK_DIM))],
                               axis=-2)
        zeros = jnp.zeros_like(rows)
        odd = (jnp.arange(ML_HEADS) % 2 == 1)[:, None, None]
        s0 = jnp.where(odd, jnp.concatenate([zeros, rows], axis=-1), jnp.concatenate([rows, zeros], axis=-1))
        m0 = jnp.concatenate([m0.reshape(n_seq, 1, ML_DIRS), jnp.zeros((n_seq, 1, LANES - ML_DIRS), f32)], axis=-1)
        in_specs += [pl.BlockSpec((None, 2, ML_HEADS, ML_ST, LANES), lambda s, p, c: (s, 0, 0, 0, 0)), m_spec]
        args += [s0, m0]
    aliases = {}
    if out_buf is not None:
        aliases[len(args)] = 0
        in_specs.append(pl.BlockSpec(memory_space=pl.ANY))
        args.append(out_buf)
    out_shape = [jax.ShapeDtypeStruct((N_TOK, ML_V_WIDTH), bf16)]
    out_specs = [pl.BlockSpec((blk_rows, ML_V_WIDTH), lambda s, p, c: (blk0 + s * nb + p * c, 0))]
    if emit_state:
        if state_bufs is not None:
            for k_out, buf in enumerate(state_bufs):
                aliases[len(args)] = 1 + k_out
                in_specs.append(pl.BlockSpec(memory_space=pl.ANY))
                args.append(buf)
        out_shape += [jax.ShapeDtypeStruct((n_seq, N_ODD, 2, ML_HEADS, ML_V_DIM, ML_QK_DIM), f32),
                      jax.ShapeDtypeStruct((n_seq, N_ODD, 2, ML_HEADS, ML_QK_DIM), f32),
                      jax.ShapeDtypeStruct((n_seq, 1, LANES), f32)]
        out_specs += [pl.BlockSpec((None, None, 2, ML_HEADS, ML_V_DIM, ML_QK_DIM),
                                   lambda s, p, c: (s, layer, 0, 0, 0, 0)),
                      pl.BlockSpec((None, None, 2, ML_HEADS, ML_QK_DIM), lambda s, p, c: (s, layer, 0, 0, 0)),
                      m_spec]
    res = pl.pallas_call(
        functools.partial(_mlstm_t_kernel, nc, grp, has_state, len(aliases), emit_state),
        input_output_aliases=aliases,
        grid=(n_seq, 2, nb),
        in_specs=in_specs,
        out_specs=out_specs,
        out_shape=out_shape,
        scratch_shapes=[pltpu.VMEM((nc, ML_HEADS, ML_ST, LANES), bf16), pltpu.VMEM((nc, SUBLANES, LANES), f32),
                        pltpu.VMEM((nc, 3, CHUNK, LANES), f32), pltpu.VMEM((nc, ML_HEADS, CHUNK, LANES), bf16),
                        pltpu.VMEM((nc, ML_HEADS, ML_V_DIM, CHUNK), bf16),
                        pltpu.VMEM((2, ML_HEADS, ML_ST, LANES), f32), pltpu.VMEM((SUBLANES, LANES), f32)],
        compiler_params=pltpu.CompilerParams(
            dimension_semantics=("arbitrary", "arbitrary", "arbitrary"), vmem_limit_bytes=VMEM_LIMIT),
        name="mlstm_scan",
    )(*args)
    if not emit_state:
        return res[0]
    y, c_fin, n_fin, mfin = res
    return y, c_fin, n_fin, mfin[:, 0, :ML_DIRS].reshape(n_seq, 2, ML_HEADS)


def kernel(x_prompt, x_sample, c, cache_na_k, cache_na_v, state_ssd, state_mlstm_c, state_mlstm_n, state_mlstm_m,
           c_ctx, w_mod, b_mod, norm_mix, norm_ffn, w_in_even, w_out_even, na_rpb, ssd_conv_w, ssd_conv_b,
           ssd_dt_bias, ssd_a_log, ssd_d, ssd_norm, w_in_odd, w_out_odd, ml_conv_w, ml_conv_b, ml_gate_b,
           w_ff1, w_ff2, norm_f):
    xs = [x_prompt.reshape(N_PROMPT, D_MODEL), x_sample.reshape(DEC_BATCH * DEC_SEQ, D_MODEL)]
    cond = jnp.concatenate([c_ctx[None, :], c, jnp.zeros((SUBLANES - N_COND, D_MODEL), f32)], axis=0)
    mod = adaln_all(cond, w_mod, b_mod)[:, :N_COND].reshape(DEPTH, N_COND, 1, N_MOD * D_MODEL)

    even_main = 3 * NA_WIDTH + SSD_INNER + SSD_CONV_DIM
    odd_main = 2 * ML_QK_WIDTH + 2 * ML_V_WIDTH

    def tail_bf16(w, main):
        t = w[:, :, main:]
        return jnp.concatenate([t, jnp.zeros(t.shape[:2] + (LANES - t.shape[2],), f32)], axis=2).astype(bf16)

    wi_even, wt_even = cast_bf16(w_in_even, even_main), tail_bf16(w_in_even, even_main)
    wi_odd, wt_odd = cast_bf16(w_in_odd, odd_main), tail_bf16(w_in_odd, odd_main)
    wo_even, wo_odd = cast_bf16(w_out_even), cast_bf16(w_out_odd)
    w1_all, w2_all = cast_bf16(w_ff1), cast_bf16(w_ff2)

    out_m = []
    new_k = new_v = new_ssd = new_c = new_n = None
    for l in range(DEPTH):
        norm_last = norm_f if l == DEPTH - 1 else None
        if l % 2 == 0:
            e = l // 2
            o0 = 3 * NA_WIDTH
            segs = ((0, NA_WIDTH, NA_HEAD_DIM ** -0.5, False),
                    (NA_WIDTH, 2 * NA_WIDTH, 1.0, False),
                    (2 * NA_WIDTH, 3 * NA_WIDTH, 1.0, False),
                    (o0, o0 + SSD_INNER, 1.0, False),
                    (o0 + SSD_INNER, o0 + SSD_INNER + SSD_CONV_DIM, 1.0, True),
                    (None, None, 1.0, False))
            outs = ((0, bf16, False), (1, bf16, False), (2, bf16, False), (1, f32, True), (2, f32, True),
                    (3, f32, False), (4, f32, False), (5, f32, False))
            q, k, v, new_k, new_v, z, xbc, dt = in_proj(
                xs, mod[l], norm_mix[l], wi_even, wt_even, e, segs, outs, ssd_conv_w[e], ssd_conv_b[e],
                cache_bufs=None if new_k is None else (new_k, new_v))
            ssd_w = (ssd_dt_bias[e], ssd_a_log[e], ssd_d[e], ssd_norm[e])
            y_ssd, new_ssd = ssd_mix(xbc, dt, z, None, 0, BATCH, SEQ, *ssd_w, True, layer=e, state_buf=new_ssd)
            y_ssd = ssd_mix(xbc, dt, z, state_ssd[:, e], N_PROMPT, DEC_BATCH, DEC_SEQ, *ssd_w, False, out_buf=y_ssd)
            tokens = lambda t: jnp.swapaxes(t, 1, 2).reshape(DEC_BATCH, PAST_LEN, NA_WIDTH).astype(bf16)
            y_na = ctx_attention(q, k, v)
            y_na = na_latent(q, k, v, tokens(cache_na_k[:, e]), tokens(cache_na_v[:, e]),
                             na_bias_table(na_rpb[e]), y_na)
            res = out_mlp(xs, mod[l], norm_ffn[l], [y_na, y_ssd], wo_even, e, w1_all, w2_all, l, norm_last)
        else:
            o = l // 2
            a0 = 2 * ML_QK_WIDTH
            segs = ((0, a0, 1.0, True),
                    (a0, a0 + ML_V_WIDTH, 1.0, False),
                    (a0 + ML_V_WIDTH, a0 + 2 * ML_V_WIDTH, 1.0, False),
                    (None, None, 1.0, False))
            outs = ((0, f32, False), (1, bf16, False), (2, f32, False), (3, f32, False))
            qk, v, og, gates = in_proj(xs, mod[l], norm_mix[l], wi_odd, wt_odd, o, segs, outs,
                                       ml_conv_w[o], ml_conv_b[o], row_block=ODD_ROW_BLOCK)
            y_ml, new_c, new_n, m_fin = mlstm_mix(qk, v, og, gates, ml_gate_b[o], None, 0, BATCH, SEQ, True,
                                                  layer=o, state_bufs=None if new_c is None else (new_c, new_n))
            out_m.append(m_fin)
            state = (state_mlstm_c[:, o], state_mlstm_n[:, o], state_mlstm_m[:, o])
            y_ml = mlstm_mix(qk, v, og, gates, ml_gate_b[o], state, N_PROMPT, DEC_BATCH, DEC_SEQ, False, out_buf=y_ml)
            res = out_mlp(xs, mod[l], norm_ffn[l], [y_ml], wo_odd, o, w1_all, w2_all, l, norm_last)
        xs = list(res) if norm_last is not None else [res]

    y_prompt = xs[0].reshape(BATCH, SEQ, D_MODEL)
    y_sample = xs[1].reshape(DEC_BATCH, DEC_SEQ, D_MODEL)
    return (y_prompt, y_sample, new_k, new_v, new_ssd,
            new_c, new_n, jnp.stack(out_m, axis=1))
```

```python
import functools
import math

import jax
import jax.numpy as jnp
from jax import lax
from jax.experimental import pallas as pl
from jax.experimental.pallas import tpu as pltpu

D_MODEL = 1024
BATCH = 32
SEQ = 256
DEPTH = 4
DEC_BATCH = 2
DEC_SEQ = 4096
PAST_LEN = 256
GRID_W = 64
N_EVEN = (DEPTH + 1) // 2
N_ODD = DEPTH // 2
RMS_EPS = 1e-6
N_MOD = 6
D_FF = 4 * D_MODEL
CONV_K = 3
Q_BLOCK = 128
NA_HEADS = 8
NA_HEAD_DIM = 64
NA_WIDTH = NA_HEADS * NA_HEAD_DIM
NA_WIN_ROWS = 8
NA_WIN_COLS = 16
NA_RPB_ROWS = 2 * NA_WIN_ROWS - 1
NA_RPB_COLS = 2 * NA_WIN_COLS - 1
SSD_INNER = D_MODEL
SSD_HEAD_DIM = 64
SSD_HEADS = SSD_INNER // SSD_HEAD_DIM
SSD_GROUPS = 2
SSD_RPG = SSD_HEADS // SSD_GROUPS
SSD_STATE = 128
SSD_GN = SSD_GROUPS * SSD_STATE
SSD_CONV_DIM = SSD_INNER + 2 * SSD_GN
SSD_CHUNK = 128
ML_HEADS = 8
ML_QK_DIM = D_MODEL // 16
ML_V_DIM = D_MODEL // 8
ML_QK_WIDTH = ML_HEADS * ML_QK_DIM
ML_V_WIDTH = ML_HEADS * ML_V_DIM
ML_CHUNK = 64
EVEN_MIX = NA_WIDTH + SSD_INNER

N_PROMPT = BATCH * SEQ
N_TOK = N_PROMPT + DEC_BATCH * DEC_SEQ
N_COND = 1 + DEC_BATCH
LANES = 128
SUBLANES = 8
VMEM_LIMIT = 56 * 1024 * 1024
TM = 512
ODD_ROW_BLOCK = 128

f32 = jnp.float32
bf16 = jnp.bfloat16


def _cond_row(i, tm):
    start = i * tm
    return jnp.where(start < N_PROMPT, 0, (start - N_PROMPT) // DEC_SEQ + 1)


def _const_spec(shape):
    nd = len(shape)
    return pl.BlockSpec(shape, lambda i: (0,) * nd, pipeline_mode=pl.Buffered(1))


def _rms(x):
    return x * lax.rsqrt(jnp.mean(x * x, axis=-1, keepdims=True) + RMS_EPS)


def _modulated(x, g, mod, k):
    shift = mod[:, k * D_MODEL:(k + 1) * D_MODEL]
    scale = mod[:, (k + 1) * D_MODEL:(k + 2) * D_MODEL]
    return (_rms(x) * g) * (1.0 + scale) + shift


def _mod_kernel(c_ref, w_ref, b_ref, o_ref):
    c = c_ref[...]
    a = (c * jax.nn.sigmoid(c)).astype(bf16)
    o_ref[...] = jnp.dot(a, w_ref[...].astype(bf16), preferred_element_type=f32) + b_ref[...]


def adaln_all(cond, w_mod, b_mod):
    tn = 1536
    nj = N_MOD * D_MODEL // tn
    return pl.pallas_call(
        _mod_kernel,
        grid=(DEPTH, nj),
        in_specs=[
            pl.BlockSpec((SUBLANES, D_MODEL), lambda l, j: (0, 0)),
            pl.BlockSpec((None, D_MODEL, tn), lambda l, j: (l, 0, j)),
            pl.BlockSpec((None, 1, tn), lambda l, j: (l, 0, j)),
        ],
        out_specs=pl.BlockSpec((None, SUBLANES, tn), lambda l, j: (l, 0, j)),
        out_shape=jax.ShapeDtypeStruct((DEPTH, SUBLANES, N_MOD * D_MODEL), f32),
        compiler_params=pltpu.CompilerParams(
            dimension_semantics=("arbitrary", "arbitrary"), vmem_limit_bytes=VMEM_LIMIT),
        name="adaln",
    )(cond, w_mod, b_mod.reshape(DEPTH, 1, N_MOD * D_MODEL))


def _cast_kernel(x_ref, o_ref):
    o_ref[...] = x_ref[...].astype(o_ref.dtype)


def cast_bf16(w, cols=None):
    n_l, k, n = w.shape
    cols = n if cols is None else cols
    bk = 512
    spec = pl.BlockSpec((None, bk, cols), lambda l, i: (l, i, 0))
    return pl.pallas_call(
        _cast_kernel,
        grid=(n_l, k // bk),
        in_specs=[spec],
        out_specs=spec,
        out_shape=jax.ShapeDtypeStruct((n_l, k, cols), bf16),
        compiler_params=pltpu.CompilerParams(
            dimension_semantics=("arbitrary", "arbitrary"), vmem_limit_bytes=VMEM_LIMIT),
        name="cast_bf16",
    )(w)


def _layer_spec(w, l):
    nd = w.ndim - 1
    return pl.BlockSpec((None,) + w.shape[1:], lambda i: (l,) + (0,) * nd, pipeline_mode=pl.Buffered(1))


def _tile_x(x_refs, tm):
    if len(x_refs) == 1:
        return x_refs[0][...]
    return jnp.where(pl.program_id(0) < N_PROMPT // tm, x_refs[0][...], x_refs[1][...])


def _conv_silu_tile(y, prev_row, next_row, w, b):
    i = pl.program_id(0)
    rows = y.shape[0]
    is_prompt = i < N_PROMPT // rows
    seq_last = jnp.where(is_prompt, SEQ - 1, DEC_SEQ - 1)

    def act(up, mid, dn):
        c = up * w[0:1, :] + mid * w[1:2, :] + dn * w[2:3, :] + b
        return c * jax.nn.sigmoid(c)

    tile = act(pltpu.roll(y, 1, axis=0), y, pltpu.roll(y, rows - 1, axis=0))
    starts = set(range(0, rows, SEQ))
    fixes = []
    for t in sorted(starts | {(st - 1) % rows for st in starts}):
        g = i * rows + t
        pos = jnp.where(is_prompt, g & (SEQ - 1), (g - N_PROMPT) & (DEC_SEQ - 1))
        up = prev_row if t == 0 else y[t - 1:t]
        dn = next_row if t == rows - 1 else y[t + 1:t + 2]
        fixes.append((t, act(jnp.where(pos == 0, 0.0, up), y[t:t + 1], jnp.where(pos == seq_last, 0.0, dn))))
    return tile, fixes


def _in_proj_kernel(n_x, n_buf, row_block, segs, outs, *refs):
    x_refs = refs[:n_x]
    prev_ref, next_ref, mod_ref, g_ref, w_ref, wt_ref, cw_ref, cb_ref = refs[n_x:n_x + 8]
    o_refs = refs[n_x + 8 + n_buf:]
    mod, g = mod_ref[...], g_ref[...]
    x = _tile_x(x_refs, TM)
    n_blocks = TM // row_block
    halo = _modulated(jnp.concatenate([prev_ref[...], next_ref[...]], axis=0), g, mod, 0).astype(bf16)
    parts = [[] for _ in segs]
    for r in range(n_blocks):
        hb = _modulated(x[r * row_block:(r + 1) * row_block], g, mod, 0).astype(bf16)
        for i, (a, b, scale, conv) in enumerate(segs):
            w = wt_ref[...] if a is None else w_ref[:, a:b]
            lhs = jnp.concatenate([hb, halo], axis=0) if conv and r == n_blocks - 1 else hb
            parts[i].append(jnp.dot(lhs, w, preferred_element_type=f32))
    ys = []
    for (a, b, scale, conv), p in zip(segs, parts):
        y = jnp.concatenate(p, axis=0)
        if conv:
            ys.append(_conv_silu_tile(y[:TM], y[TM + SUBLANES - 1:TM + SUBLANES],
                                      y[TM + SUBLANES:TM + SUBLANES + 1], cw_ref[...], cb_ref[...]))
        else:
            ys.append(y if scale == 1.0 else y * scale)
    for (si, _, head_major), o_ref in zip(outs, o_refs):
        if head_major:
            @pl.when(pl.program_id(0) < N_PROMPT // TM)
            def _(o_ref=o_ref, si=si):
                for b in range(TM // SEQ):
                    for hd in range(NA_HEADS):
                        o_ref[b, hd] = ys[si][b * SEQ:(b + 1) * SEQ, hd * NA_HEAD_DIM:(hd + 1) * NA_HEAD_DIM]
        elif segs[si][3]:
            tile, fixes = ys[si]
            o_ref[...] = tile.astype(o_ref.dtype)
            for t, row in fixes:
                o_ref[t:t + 1, :] = row.astype(o_ref.dtype)
        else:
            o_ref[...] = ys[si].astype(o_ref.dtype)


def _x_specs(xs, tm):
    if len(xs) == 1:
        return [pl.BlockSpec((tm, D_MODEL), lambda i: (i, 0))]
    n_p = N_PROMPT // tm
    return [pl.BlockSpec((tm, D_MODEL), lambda i: (jnp.minimum(i, n_p - 1), 0)),
            pl.BlockSpec((tm, D_MODEL), lambda i: (jnp.maximum(i - n_p, 0), 0))]


def in_proj(xs, mod_l, g, w, w_tail, l, segs, outs, conv_w, conv_b, cache_bufs=None, row_block=TM):
    grid = (N_TOK // TM,)
    last_prompt = N_PROMPT // TM - 1
    halo_src = xs[-1]
    per = TM // SUBLANES
    blk0 = (halo_src.shape[0] - DEC_BATCH * DEC_SEQ) // SUBLANES
    n_blk = halo_src.shape[0] // SUBLANES
    tile0 = N_PROMPT // TM

    def prev_map(i):
        return (jnp.clip(blk0 + (i - tile0) * per - 1, 0, n_blk - 1), 0)

    def next_map(i):
        return (jnp.clip(blk0 + (i - tile0 + 1) * per, 0, n_blk - 1), 0)

    ch = conv_w.shape[1]
    out_shape, out_specs = [], []
    for si, dt, head_major in outs:
        width = LANES if segs[si][0] is None else segs[si][1] - segs[si][0]
        if head_major:
            out_shape.append(jax.ShapeDtypeStruct((BATCH, N_EVEN, NA_HEADS, SEQ, NA_HEAD_DIM), dt))
            out_specs.append(pl.BlockSpec((TM // SEQ, None, NA_HEADS, SEQ, NA_HEAD_DIM),
                                          lambda i: (jnp.minimum(i, last_prompt), l, 0, 0, 0)))
        else:
            out_shape.append(jax.ShapeDtypeStruct((N_TOK, width), dt))
            out_specs.append(pl.BlockSpec((TM, width), lambda i: (i, 0)))
    args = [*xs, halo_src, halo_src, mod_l, g.reshape(1, D_MODEL), w, w_tail, conv_w, conv_b.reshape(1, ch)]
    aliases = {}
    if cache_bufs is not None:
        head_major_outs = [k for k, o in enumerate(outs) if o[2]]
        for k_out, buf in zip(head_major_outs, cache_bufs):
            aliases[len(args)] = k_out
            args.append(buf)
    return pl.pallas_call(
        functools.partial(_in_proj_kernel, len(xs), len(aliases), row_block, segs, outs),
        input_output_aliases=aliases,
        grid=grid,
        in_specs=_x_specs(xs, TM) + [
            pl.BlockSpec((SUBLANES, D_MODEL), prev_map),
            pl.BlockSpec((SUBLANES, D_MODEL), next_map),
            pl.BlockSpec((None, 1, N_MOD * D_MODEL), lambda i: (_cond_row(i, TM), 0, 0)),
            _const_spec((1, D_MODEL)),
            _layer_spec(w, l),
            _layer_spec(w_tail, l),
            _const_spec((CONV_K, ch)),
            _const_spec((1, ch)),
        ] + [pl.BlockSpec(memory_space=pl.ANY)] * len(aliases),
        out_specs=out_specs,
        out_shape=out_shape,
        compiler_params=pltpu.CompilerParams(
            dimension_semantics=("arbitrary",), vmem_limit_bytes=VMEM_LIMIT),
        name="in_proj",
    )(*args)


def _out_mlp_kernel(n_x, n_mix, final, *refs):
    x_refs = refs[:n_x]
    mod_ref, g_ref = refs[n_x:n_x + 2]
    refs = refs[n_x + 2:]
    mix_refs = refs[:n_mix]
    wo_ref, w1_ref, w2_ref = refs[n_mix:n_mix + 3]
    rest = refs[n_mix + 3:]
    if final:
        gf_ref, op_ref, os_ref = rest
    else:
        (o_ref,) = rest
    mod = mod_ref[...]
    m = None
    k0 = 0
    for r in mix_refs:
        kw = r.shape[-1]
        part = jnp.dot(r[...].astype(bf16), wo_ref[k0:k0 + kw, :], preferred_element_type=f32)
        m = part if m is None else m + part
        k0 += kw
    x1 = _tile_x(x_refs, OUT_TM) + mod[:, 2 * D_MODEL:3 * D_MODEL] * m
    h2 = _modulated(x1, g_ref[...], mod, 3).astype(bf16)
    u = jnp.dot(h2, w1_ref[...], preferred_element_type=f32)
    a = jnp.square(jnp.maximum(u, 0.0)).astype(bf16)
    x2 = x1 + mod[:, 5 * D_MODEL:6 * D_MODEL] * jnp.dot(a, w2_ref[...], preferred_element_type=f32)
    if not final:
        o_ref[...] = x2
        return
    y = _rms(x2) * gf_ref[...]
    is_prompt = pl.program_id(0) < N_PROMPT // OUT_TM

    @pl.when(is_prompt)
    def _():
        op_ref[...] = y

    @pl.when(jnp.logical_not(is_prompt))
    def _():
        os_ref[...] = y


OUT_TM = 512


def out_mlp(xs, mod_l, g_ffn, mixes, w_out, l_out, w1, w2, l, norm_f=None):
    final = norm_f is not None
    tm = OUT_TM
    grid = (N_TOK // tm,)
    in_specs = _x_specs(xs, tm) + [
        pl.BlockSpec((None, 1, N_MOD * D_MODEL), lambda i: (_cond_row(i, tm), 0, 0)),
        _const_spec((1, D_MODEL)),
    ]
    in_specs += [pl.BlockSpec((tm, m.shape[-1]), lambda i: (i, 0)) for m in mixes]
    in_specs += [_layer_spec(w_out, l_out), _layer_spec(w1, l), _layer_spec(w2, l)]
    args = [*xs, mod_l, g_ffn.reshape(1, D_MODEL), *mixes, w_out, w1, w2]
    out_specs = pl.BlockSpec((tm, D_MODEL), lambda i: (i, 0))
    out_shape = jax.ShapeDtypeStruct((N_TOK, D_MODEL), f32)
    if final:
        in_specs.append(_const_spec((1, D_MODEL)))
        args.append(norm_f.reshape(1, D_MODEL))
        n_p = N_PROMPT // tm
        out_specs = [pl.BlockSpec((tm, D_MODEL), lambda i: (jnp.minimum(i, n_p - 1), 0)),
                     pl.BlockSpec((tm, D_MODEL), lambda i: (jnp.maximum(i - n_p, 0), 0))]
        out_shape = [jax.ShapeDtypeStruct((N_PROMPT, D_MODEL), f32),
                     jax.ShapeDtypeStruct((N_TOK - N_PROMPT, D_MODEL), f32)]
    return pl.pallas_call(
        functools.partial(_out_mlp_kernel, len(xs), len(mixes), final),
        grid=grid,
        in_specs=in_specs,
        out_specs=out_specs,
        out_shape=out_shape,
        compiler_params=pltpu.CompilerParams(
            dimension_semantics=("arbitrary",), vmem_limit_bytes=VMEM_LIMIT),
        name="out_mlp",
    )(*args)


NA_PAIRS = NA_HEADS // 2
NA_ROWS = DEC_SEQ // GRID_W
NA_WIN = NA_WIN_ROWS * GRID_W


def _dot_nt(a, b):
    return lax.dot_general(a, b, (((1,), (1,)), ((), ())), preferred_element_type=f32)


def _pair_stack(x):
    lane = lax.broadcasted_iota(jnp.int32, x.shape, 1)
    zero = jnp.zeros_like(x)
    return jnp.concatenate([jnp.where(lane < NA_HEAD_DIM, x, zero), jnp.where(lane >= NA_HEAD_DIM, x, zero)], axis=0)


def _pair_unstack(o):
    n = o.shape[0] // 2
    lane = lax.broadcasted_iota(jnp.int32, (n, LANES), 1)
    return jnp.where(lane < NA_HEAD_DIM, o[:n], o[n:])


def _softmax_pv(scores, values):
    def lane_tiles(blocks):
        return [b[:, k * LANES:(k + 1) * LANES] for b in blocks for k in range(b.shape[1] // LANES)]

    ms = [jnp.max(functools.reduce(jnp.maximum, lane_tiles(s)), axis=1, keepdims=True) for s in scores]
    ps = [[jnp.exp(b - m) for b in s] for s, m in zip(scores, ms)]
    invs = [1.0 / jnp.sum(functools.reduce(jnp.add, lane_tiles(p)), axis=1, keepdims=True) for p in ps]
    pn = [[(b * inv).astype(bf16) for b in p] for p, inv in zip(ps, invs)]
    return [functools.reduce(jnp.add, [jnp.dot(b, v, preferred_element_type=f32) for b, v in zip(p, vs)])
            for p, vs in zip(pn, values)]


def _ctx_attn_kernel(q_ref, k_ref, v_ref, o_ref):
    pairs = [slice(p * LANES, (p + 1) * LANES) for p in range(NA_PAIRS)]
    scores = [[_dot_nt(_pair_stack(q_ref[:, lanes]), k_ref[:, lanes])] for lanes in pairs]
    outs = _softmax_pv(scores, [[v_ref[:, lanes]] for lanes in pairs])
    for lanes, o in zip(pairs, outs):
        o_ref[:, lanes] = _pair_unstack(o).astype(o_ref.dtype)


def ctx_attention(q, k, v):
    spec = pl.BlockSpec((SEQ, NA_WIDTH), lambda b: (b, 0))
    return pl.pallas_call(
        _ctx_attn_kernel,
        grid=(BATCH,),
        in_specs=[spec, spec, spec],
        out_specs=spec,
        out_shape=jax.ShapeDtypeStruct((N_TOK, NA_WIDTH), bf16),
        compiler_params=pltpu.CompilerParams(dimension_semantics=("arbitrary",), vmem_limit_bytes=VMEM_LIMIT),
        name="ctx_attn",
    )(q, k, v)


def _na_bias_kernel(rpb_ref, o_ref):
    pair = pl.program_id(0)
    shape = (GRID_W, LANES)
    qc = lax.broadcasted_iota(jnp.int32, shape, 0)
    lane = lax.broadcasted_iota(jnp.int32, shape, 1)
    kc = lane & (GRID_W - 1)
    low = lane < GRID_W
    col_start = jnp.clip(qc - NA_WIN_COLS // 2, 0, GRID_W - NA_WIN_COLS)
    valid = (kc >= col_start) & (kc < col_start + NA_WIN_COLS)
    rel_c = jnp.clip(kc - qc + NA_WIN_COLS - 1, 0, NA_RPB_COLS - 1)
    for e in range(2):
        base = (2 * pair + e) * NA_RPB_ROWS
        pieces = []
        for rr in range(NA_RPB_ROWS - 1):
            val = jnp.zeros(shape, f32)
            for t in range(NA_RPB_COLS):
                s_lo = rpb_ref[(base + rr) * NA_RPB_COLS + t]
                s_hi = rpb_ref[(base + rr + 1) * NA_RPB_COLS + t]
                val = jnp.where(rel_c == t, jnp.where(low, s_lo, s_hi), val)
            pieces.append(jnp.where(valid, val, -jnp.inf))
        for d in range(NA_WIN_ROWS):
            for i in range(0, NA_WIN_ROWS, 2):
                o_ref[d, e * GRID_W:(e + 1) * GRID_W, i * GRID_W:(i + 2) * GRID_W] = pieces[d + i]


def na_bias_table(rpb):
    return pl.pallas_call(
        _na_bias_kernel,
        grid=(NA_PAIRS,),
        in_specs=[pl.BlockSpec(memory_space=pltpu.SMEM)],
        out_specs=pl.BlockSpec((None, NA_WIN_ROWS, 2 * GRID_W, NA_WIN), lambda p: (p, 0, 0, 0)),
        out_shape=jax.ShapeDtypeStruct((NA_PAIRS, NA_WIN_ROWS, 2 * GRID_W, NA_WIN), f32),
        compiler_params=pltpu.CompilerParams(dimension_semantics=("arbitrary",), vmem_limit_bytes=VMEM_LIMIT),
        name="na_bias",
    )(rpb.reshape(-1))


NA_STEP_ROWS = 8


def _na_first_key_row(r):
    return jnp.clip(r - NA_WIN_ROWS // 2, 0, NA_ROWS - NA_WIN_ROWS)


def _na_kernel(q_ref, k_ref, v_ref, kc_ref, vc_ref, bias_ref, buf_ref, o_ref):
    del buf_ref
    where, scores, values = [], [], []
    for j in range(NA_STEP_ROWS):
        r = pl.program_id(1) * NA_STEP_ROWS + j
        first = _na_first_key_row(r)
        start = pl.multiple_of(first * GRID_W, GRID_W)
        shift = first - r + NA_WIN_ROWS - 1
        rows = slice(j * GRID_W, (j + 1) * GRID_W)
        for p in range(NA_PAIRS):
            lanes = slice(p * LANES, (p + 1) * LANES)
            qq = _pair_stack(q_ref[rows, lanes])
            where.append((rows, lanes))
            scores.append([_dot_nt(qq, k_ref[pl.ds(start, NA_WIN), lanes]) + bias_ref[p, shift],
                           _dot_nt(qq, kc_ref[:, lanes])])
            values.append([v_ref[pl.ds(start, NA_WIN), lanes], vc_ref[:, lanes]])
    for (rows, lanes), o in zip(where, _softmax_pv(scores, values)):
        o_ref[rows, lanes] = _pair_unstack(o).astype(o_ref.dtype)


def na_latent(q, k, v, k_ctx, v_ctx, bias, buf):
    rows = NA_STEP_ROWS * GRID_W
    steps = NA_ROWS // NA_STEP_ROWS
    row0 = N_PROMPT // rows
    seq0 = N_PROMPT // DEC_SEQ
    kv_spec = pl.BlockSpec((DEC_SEQ, NA_WIDTH), lambda b, r: (seq0 + b, 0))
    ctx_spec = pl.BlockSpec((None, PAST_LEN, NA_WIDTH), lambda b, r: (b, 0, 0))
    return pl.pallas_call(
        _na_kernel,
        grid=(DEC_BATCH, steps),
        in_specs=[
            pl.BlockSpec((rows, NA_WIDTH), lambda b, r: (row0 + b * steps + r, 0)),
            kv_spec, kv_spec, ctx_spec, ctx_spec,
            pl.BlockSpec(bias.shape, lambda b, r: (0, 0, 0, 0), pipeline_mode=pl.Buffered(1)),
            pl.BlockSpec(memory_space=pl.ANY),
        ],
        out_specs=pl.BlockSpec((rows, NA_WIDTH), lambda b, r: (row0 + b * steps + r, 0)),
        out_shape=jax.ShapeDtypeStruct((N_TOK, NA_WIDTH), bf16),
        input_output_aliases={6: 0},
        compiler_params=pltpu.CompilerParams(
            dimension_semantics=("arbitrary", "arbitrary"), vmem_limit_bytes=VMEM_LIMIT),
        name="na_latent",
    )(q, k, v, k_ctx, v_ctx, bias, buf)


CHUNK = 128
SCAN_G_MAX = 4


def _cumsum_rows(a, reverse=False):
    row = lax.broadcasted_iota(jnp.int32, a.shape, 0)
    s = 1
    while s < CHUNK:
        if reverse:
            a = a + jnp.where(row < CHUNK - s, pltpu.roll(a, CHUNK - s, axis=0), 0.0)
        else:
            a = a + jnp.where(row >= s, pltpu.roll(a, s, axis=0), 0.0)
        s *= 2
    return a


def _ssd_kernel(nc, grp, has_h0, n_buf, emit_state, *refs):
    xbc_ref, dt_ref, z_ref = refs[:3]
    refs = refs[3:]
    if has_h0:
        h0_ref, refs = refs[0], refs[1:]
    dtb_ref, alog_ref, dskip_ref, g_ref = refs[:4]
    refs = refs[4 + n_buf:]
    y_ref, refs = refs[0], refs[1:]
    if emit_state:
        hfin_ref, refs = refs[0], refs[1:]
    hb_store, xt_store, gate_store, carry = refs

    phase = pl.program_id(1)
    gi = pl.program_id(2)
    nb = nc // grp
    gw = SSD_RPG * SSD_HEAD_DIM

    def load_h0(d):
        if has_h0:
            return h0_ref[d].reshape(SSD_INNER, SSD_STATE)
        return jnp.zeros((SSD_INNER, SSD_STATE), f32)

    def head_rows(v):
        return jnp.concatenate(
            [jnp.broadcast_to(v[h:h + 1, :], (SSD_HEAD_DIM, v.shape[1])) for h in range(SSD_HEADS)], axis=0)

    def state_update(x_t, b_bf, cum_t, dt_t, edge):
        at_edge = jnp.broadcast_to(cum_t[:, edge:edge + 1], cum_t.shape)
        w_end = jnp.exp(at_edge - cum_t) * dt_t
        xw = (x_t * head_rows(w_end)).astype(bf16)
        upd = jnp.concatenate(
            [jnp.dot(xw[g * gw:(g + 1) * gw], b_bf[:, g * SSD_STATE:(g + 1) * SSD_STATE], preferred_element_type=f32)
             for g in range(SSD_GROUPS)], axis=0)
        carry[...] = carry[...] * head_rows(jnp.exp(at_edge)) + upd

    @pl.when(jnp.logical_and(phase == 0, gi == 0))
    def _():
        carry[...] = load_h0(1)

    def backward_chunk(sub):
        rows = slice(sub * CHUNK, (sub + 1) * CHUNK)
        j = grp * (nb - 1 - gi) + sub
        x = xbc_ref[rows, :SSD_INNER]
        b_bf = xbc_ref[rows, SSD_INNER:SSD_INNER + SSD_GN].astype(bf16)
        dt = jax.nn.softplus(dt_ref[rows, :] + dtb_ref[...])
        a = dt * (-jnp.exp(alog_ref[...]))
        x_t = jnp.concatenate([x[:, k * LANES:(k + 1) * LANES].T for k in range(SSD_INNER // LANES)], axis=0)
        rcum = _cumsum_rows(a, reverse=True)
        xt_store[j] = x_t
        gate_store[j, 0] = dt
        gate_store[j, 1] = _cumsum_rows(a)
        gate_store[j, 2] = rcum
        hb_store[j] = carry[...].astype(bf16)
        state_update(x_t, b_bf, rcum.T[SSD_HEADS:2 * SSD_HEADS], dt.T[SSD_HEADS:2 * SSD_HEADS], 0)

    @pl.when(phase == 0)
    def _backward_states():
        for sub in reversed(range(grp)):
            backward_chunk(sub)

        if emit_state:
            @pl.when(gi == nb - 1)
            def _():
                hfin_ref[1] = carry[...].reshape(SSD_HEADS, SSD_HEAD_DIM, SSD_STATE)

    @pl.when(jnp.logical_and(phase == 1, gi == 0))
    def _():
        carry[...] = load_h0(0)

    @pl.when(phase == 1)
    def _forward_and_outputs():
        subs = range(grp)
        heads = range(SSD_HEADS)
        sub_heads = [(sub, h) for sub in subs for h in heads]
        rows = [slice(sub * CHUNK, (sub + 1) * CHUNK) for sub in subs]
        cidx = [grp * gi + sub for sub in subs]
        row = lax.broadcasted_iota(jnp.int32, (CHUNK, CHUNK), 0)
        col = lax.broadcasted_iota(jnp.int32, (CHUNK, CHUNK), 1)
        causal = col <= row
        anti = col >= row
        lane = lax.broadcasted_iota(jnp.int32, (CHUNK, LANES), 1)
        x = [xbc_ref[rw, :SSD_INNER] for rw in rows]
        x_bf = [xs.astype(bf16) for xs in x]
        b_bf = [xbc_ref[rw, SSD_INNER:SSD_INNER + SSD_GN].astype(bf16) for rw in rows]
        c_mat = [xbc_ref[rw, SSD_INNER + SSD_GN:].astype(bf16) for rw in rows]
        dt = [gate_store[c, 0] for c in cidx]
        cum = [gate_store[c, 1] for c in cidx]
        rcum = [gate_store[c, 2] for c in cidx]
        cum_t = [t.T for t in cum]
        rcum_t = [t.T for t in rcum]
        dt_t = [t.T for t in dt]
        cb = {(sub, g): _dot_nt(c_mat[sub][:, g * SSD_STATE:(g + 1) * SSD_STATE],
                                b_bf[sub][:, g * SSD_STATE:(g + 1) * SSD_STATE])
              for sub in subs for g in range(SSD_GROUPS)}
        log_dt = [jnp.log(t) for t in dt_t]
        key_f = [cum_t[sub] - log_dt[sub] for sub in subs]
        key_b = [rcum_t[sub] - log_dt[sub] for sub in subs]
        seg_f = {(sub, h): jnp.where(causal, cum[sub][:, h:h + 1] - key_f[sub][h:h + 1, :], -jnp.inf)
                 for sub, h in sub_heads}
        seg_b = {(sub, h): jnp.where(anti, rcum[sub][:, SSD_HEADS + h:SSD_HEADS + h + 1]
                                     - key_b[sub][SSD_HEADS + h:SSD_HEADS + h + 1, :], -jnp.inf)
                 for sub, h in sub_heads}
        e_f = {sh: jnp.exp(seg_f[sh]) for sh in sub_heads}
        e_b = {sh: jnp.exp(seg_b[sh]) for sh in sub_heads}
        ws = {(sub, h): (cb[sub, h // SSD_RPG] * (e_f[sub, h] + e_b[sub, h])).astype(bf16) for sub, h in sub_heads}
        rhs = {}
        for sub in subs:
            for p in range(SSD_HEADS // 2):
                xp = x_bf[sub][:, p * LANES:(p + 1) * LANES]
                zero = jnp.zeros_like(xp)
                rhs[sub, p] = jnp.concatenate([jnp.where(lane < SSD_HEAD_DIM, xp, zero),
                                               jnp.where(lane >= SSD_HEAD_DIM, xp, zero)], axis=0)
        y_intra = [jnp.concatenate(
            [jnp.dot(jnp.concatenate([ws[sub, 2 * p], ws[sub, 2 * p + 1]], axis=1), rhs[sub, p],
                     preferred_element_type=f32) for p in range(SSD_HEADS // 2)], axis=1) for sub in subs]

        for sub in subs:
            hf = carry[...].astype(bf16)
            hb = hb_store[cidx[sub]]

            def inter_t(h_all):
                return jnp.concatenate(
                    [_dot_nt(h_all[g * gw:(g + 1) * gw], c_mat[sub][:, g * SSD_STATE:(g + 1) * SSD_STATE])
                     for g in range(SSD_GROUPS)], axis=0)

            cum_f, cum_b = cum_t[sub][:SSD_HEADS], rcum_t[sub][SSD_HEADS:2 * SSD_HEADS]
            y_t = inter_t(hf) * head_rows(jnp.exp(cum_f)) + inter_t(hb) * head_rows(jnp.exp(cum_b))
            y_inter = jnp.concatenate([y_t[k * LANES:(k + 1) * LANES].T for k in range(SSD_INNER // LANES)], axis=1)
            state_update(xt_store[cidx[sub]], b_bf[sub], cum_f, dt_t[sub][:SSD_HEADS], CHUNK - 1)
            y = y_intra[sub] + y_inter + dskip_ref[...] * x[sub]

            zv = z_ref[rows[sub], :]
            yz = y * (zv * jax.nn.sigmoid(zv))
            y_ref[rows[sub], :] = (_rms(yz) * g_ref[...]).astype(y_ref.dtype)

        if emit_state:
            @pl.when(gi == nb - 1)
            def _():
                hfin_ref[0] = carry[...].reshape(SSD_HEADS, SSD_HEAD_DIM, SSD_STATE)


def ssd_mix(xbc, dt, z, h0, row0, n_seq, seq_len, dt_bias, a_log, d_skip, norm_g, emit_state, out_buf=None,
            layer=0, state_buf=None):
    nc = seq_len // CHUNK
    grp = math.gcd(nc, SCAN_G_MAX)
    blk_rows = grp * CHUNK
    nb = seq_len // blk_rows
    blk0 = row0 // blk_rows
    has_h0 = h0 is not None

    def block_map(s, p, c):
        return (blk0 + s * nb + jnp.where(p == 0, nb - 1 - c, c), 0)

    state_spec = pl.BlockSpec((None, 2, SSD_HEADS, SSD_HEAD_DIM, SSD_STATE), lambda s, p, c: (s, 0, 0, 0, 0))
    vec = lambda n: pl.BlockSpec((1, n), lambda s, p, c: (0, 0))
    in_specs = [
        pl.BlockSpec((blk_rows, SSD_CONV_DIM), block_map),
        pl.BlockSpec((blk_rows, LANES), block_map),
        pl.BlockSpec((blk_rows, SSD_INNER), lambda s, p, c: (blk0 + s * nb + p * c, 0)),
    ]
    args = [xbc, dt, z]
    if has_h0:
        in_specs.append(state_spec)
        args.append(h0)
    in_specs += [vec(LANES), vec(LANES), vec(SSD_INNER), vec(SSD_INNER)]
    pad = lambda t: jnp.concatenate([t.reshape(1, -1), jnp.zeros((1, LANES - t.size), f32)], axis=1)
    args += [pad(dt_bias), pad(a_log), jnp.repeat(d_skip, SSD_HEAD_DIM).reshape(1, SSD_INNER),
             norm_g.reshape(1, SSD_INNER)]
    aliases = {}
    if out_buf is not None:
        aliases = {len(args): 0}
        in_specs.append(pl.BlockSpec(memory_space=pl.ANY))
        args.append(out_buf)
    out_shape = [jax.ShapeDtypeStruct((N_TOK, SSD_INNER), bf16)]
    out_specs = [pl.BlockSpec((blk_rows, SSD_INNER), lambda s, p, c: (blk0 + s * nb + p * c, 0))]
    if emit_state:
        if state_buf is not None:
            aliases[len(args)] = 1
            in_specs.append(pl.BlockSpec(memory_space=pl.ANY))
            args.append(state_buf)
        out_shape.append(jax.ShapeDtypeStruct((n_seq, N_EVEN, 2, SSD_HEADS, SSD_HEAD_DIM, SSD_STATE), f32))
        out_specs.append(pl.BlockSpec((None, None, 2, SSD_HEADS, SSD_HEAD_DIM, SSD_STATE),
                                      lambda s, p, c: (s, layer, 0, 0, 0, 0)))
    res = pl.pallas_call(
        functools.partial(_ssd_kernel, nc, grp, has_h0, len(aliases), emit_state),
        input_output_aliases=aliases,
        grid=(n_seq, 2, nb),
        in_specs=in_specs,
        out_specs=out_specs,
        out_shape=out_shape,
        scratch_shapes=[pltpu.VMEM((nc, SSD_INNER, SSD_STATE), bf16), pltpu.VMEM((nc, SSD_INNER, CHUNK), f32),
                        pltpu.VMEM((nc, 3, CHUNK, LANES), f32), pltpu.VMEM((SSD_INNER, SSD_STATE), f32)],
        compiler_params=pltpu.CompilerParams(
            dimension_semantics=("arbitrary", "arbitrary", "arbitrary"), vmem_limit_bytes=VMEM_LIMIT),
        name="ssd_scan",
    )(*args)
    return res if emit_state else res[0]


ML_DIRS = 2 * ML_HEADS


def _cummax_rows(a, reverse=False):
    row = lax.broadcasted_iota(jnp.int32, a.shape, 0)
    s = 1
    while s < CHUNK:
        if reverse:
            a = jnp.maximum(a, jnp.where(row < CHUNK - s, pltpu.roll(a, CHUNK - s, axis=0), -jnp.inf))
        else:
            a = jnp.maximum(a, jnp.where(row >= s, pltpu.roll(a, s, axis=0), -jnp.inf))
        s *= 2
    return a


ML_ST = ML_V_DIM + 16


def _mlstm_t_kernel(nc, grp, has_state, n_buf, emit_state, *refs):
    qk_ref, v_ref, og_ref, gates_ref, gb_ref = refs[:5]
    refs = refs[5:]
    if has_state:
        s0_ref, m0_ref = refs[:2]
        refs = refs[2:]
    refs = refs[n_buf:]
    y_ref, refs = refs[0], refs[1:]
    if emit_state:
        cfin_ref, nfin_ref, mfin_ref = refs[:3]
        refs = refs[3:]
    s_store, m_store, gate_store, k_store, v_store, s_carry, m_carry = refs

    phase = pl.program_id(1)
    gi = pl.program_id(2)
    nb = nc // grp
    fwd_row = lax.broadcasted_iota(jnp.int32, (1, LANES), 1) < ML_HEADS

    def state_update(cum, r, k_own, v_tr, m_row, big_m, d):
        edge = CHUNK - 1 if d == 0 else 0
        m_edge = big_m[edge:edge + 1, :]
        wc_row = jnp.exp(m_row - m_edge)
        wk_t = jnp.exp(r.T - big_m.T[:, edge:edge + 1])
        lhs = []
        for h in range(ML_HEADS):
            wk = wk_t[d * ML_HEADS + h:d * ML_HEADS + h + 1, :]
            lhs.append(jnp.concatenate([v_tr[h] * wk, jnp.broadcast_to(wk, (ML_ST - ML_V_DIM, CHUNK))],
                                       axis=0).astype(bf16))
        upd = [jnp.dot(lhs[h], k_own[h], preferred_element_type=f32) for h in range(ML_HEADS)]
        for h in range(ML_HEADS):
            cl = d * ML_HEADS + h
            wc = jnp.broadcast_to(wc_row[:, cl:cl + 1], (ML_ST, LANES))
            s_carry[d, h] = wc * s_carry[d, h] + upd[h]
        return cum[edge:edge + 1, :] + m_edge

    def init_state(d):
        if has_state:
            s_carry[d] = s0_ref[d]
        else:
            s_carry[d] = jnp.zeros((ML_HEADS, ML_ST, LANES), f32)

    def emit_final(d):
        for h in range(ML_HEADS):
            tile = s_carry[d, h]
            if h % 2 == 1:
                tile = pltpu.roll(tile, ML_QK_DIM, axis=1)
            cfin_ref[d, h] = tile[:ML_V_DIM, :ML_QK_DIM]
            nfin_ref[d, h:h + 1, :] = tile[ML_V_DIM:ML_V_DIM + 1, :ML_QK_DIM]

    @pl.when(jnp.logical_and(phase == 0, gi == 0))
    def _():
        init_state(1)
        m_carry[...] = jnp.broadcast_to(m0_ref[...], m_carry.shape) if has_state else jnp.zeros(m_carry.shape, f32)

    def backward_chunk(sub):
        rows = slice(sub * CHUNK, (sub + 1) * CHUNK)
        lane = lax.broadcasted_iota(jnp.int32, (CHUNK, LANES), 1)
        fwd_lane = lane < ML_HEADS
        g = gates_ref[rows, :] + gb_ref[...]
        lf = pltpu.roll(jax.nn.log_sigmoid(g), LANES - ML_DIRS, axis=1)
        cum = jnp.where(fwd_lane, _cumsum_rows(lf), _cumsum_rows(lf, reverse=True))
        r = g - cum
        pm = jnp.where(fwd_lane, _cummax_rows(r), _cummax_rows(r, reverse=True))
        k = qk_ref[rows, ML_QK_WIDTH:]
        k_own = [jnp.where((lane < ML_QK_DIM) if h % 2 == 0 else (lane >= ML_QK_DIM),
                           k[:, (h // 2) * LANES:(h // 2 + 1) * LANES], 0.0).astype(bf16) for h in range(ML_HEADS)]
        v_tr = [v_ref[rows, h * ML_V_DIM:(h + 1) * ML_V_DIM].astype(f32).T for h in range(ML_HEADS)]

        j = grp * (nb - 1 - gi) + sub
        gate_store[j, 0] = cum
        gate_store[j, 1] = r
        gate_store[j, 2] = pm
        for h in range(ML_HEADS):
            k_store[j, h] = k_own[h]
            v_store[j, h] = v_tr[h].astype(bf16)
        m_row = m_carry[0:1, :]
        s_store[j] = s_carry[1].astype(bf16)
        m_store[j] = m_carry[...]
        big_m = jnp.maximum(m_row, pm)
        m_new = state_update(cum, r, k_own, v_tr, m_row, big_m, 1)
        m_carry[...] = jnp.broadcast_to(jnp.where(fwd_row, m_row, m_new), m_carry.shape)

    @pl.when(phase == 0)
    def _backward_states():
        for sub in reversed(range(grp)):
            backward_chunk(sub)

    @pl.when(jnp.logical_and(phase == 1, gi == 0))
    def _():
        init_state(0)
        if emit_state:
            emit_final(1)

    @pl.when(phase == 1)
    def _forward_and_outputs():
        subs = range(grp)
        heads = range(ML_HEADS)
        chains = [(sub, h, d) for sub in subs for h in heads for d in range(2)]
        lane_of = lambda h, d: d * ML_HEADS + h
        rows = [slice(sub * CHUNK, (sub + 1) * CHUNK) for sub in subs]
        cidx = [grp * gi + sub for sub in subs]
        cum = [gate_store[c, 0] for c in cidx]
        r = [gate_store[c, 1] for c in cidx]
        pm = [gate_store[c, 2] for c in cidx]
        k_own = [[k_store[c, h] for h in heads] for c in cidx]
        v_th = [[v_store[c, h] for h in heads] for c in cidx]

        m_both = m_carry[0:1, :]
        m_row, big_m = [], []
        m_fwd = m_both
        for sub in subs:
            m_row.append(jnp.where(fwd_row, m_fwd, m_store[cidx[sub]][0:1, :]))
            big_m.append(jnp.maximum(m_row[sub], pm[sub]))
            m_fwd = cum[sub][CHUNK - 1:CHUNK, :] + big_m[sub][CHUNK - 1:CHUNK, :]
        m_fin = jnp.where(fwd_row, m_fwd, m_both)
        m_carry[...] = jnp.broadcast_to(m_fin, m_carry.shape)

        big_m_t = [t.T for t in big_m]
        w_inter_t = [jnp.exp(m_row[sub] - big_m[sub]).T for sub in subs]
        floor_t = [jnp.exp(-(cum[sub] + big_m[sub])).T for sub in subs]
        key = lax.broadcasted_iota(jnp.int32, (CHUNK, CHUNK), 0)
        qry = lax.broadcasted_iota(jnp.int32, (CHUNK, CHUNK), 1)
        masks = (key <= qry, key >= qry)
        q = [(qk_ref[rw, :ML_QK_WIDTH] * (ML_QK_DIM ** -0.5)).astype(bf16) for rw in rows]
        q_pair = {(sub, h): q[sub][:, (h // 2) * LANES:(h // 2 + 1) * LANES] for sub in subs for h in heads}
        s_raw_t = {(sub, h): _dot_nt(k_own[sub][h], q_pair[sub, h]) for sub in subs for h in heads}
        w_t = {(sub, h, d): jnp.exp(jnp.where(
            masks[d], r[sub][:, lane_of(h, d):lane_of(h, d) + 1] - big_m_t[sub][lane_of(h, d):lane_of(h, d) + 1, :],
            -jnp.inf)) for sub, h, d in chains}
        sw_t = {(sub, h, d): s_raw_t[sub, h] * w_t[sub, h, d] for sub, h, d in chains}
        num = {(sub, h, d): jnp.dot(v_th[sub][h], sw_t[sub, h, d].astype(bf16), preferred_element_type=f32)
               for sub, h, d in chains}
        den_intra = {ch: jnp.sum(sw_t[ch], axis=0, keepdims=True) for ch in chains}
        wi = {(sub, h, d): w_inter_t[sub][lane_of(h, d):lane_of(h, d) + 1, :] for sub, h, d in chains}
        gate = {(sub, h): jax.nn.sigmoid(og_ref[rows[sub], h * ML_V_DIM:(h + 1) * ML_V_DIM])
                for sub in subs for h in heads}

        for sub in subs:
            hd_pairs = [(h, d) for h in heads for d in range(2)]
            inter = {(h, d): _dot_nt(s_carry[0, h].astype(bf16) if d == 0 else s_store[cidx[sub], h],
                                     q_pair[sub, h]) for h, d in hd_pairs}
            den = {(h, d): den_intra[sub, h, d] + wi[sub, h, d] * inter[h, d][ML_V_DIM:ML_V_DIM + 1]
                   for h, d in hd_pairs}
            inv = {(h, d): 1.0 / jnp.maximum(jnp.abs(den[h, d]),
                                             floor_t[sub][lane_of(h, d):lane_of(h, d) + 1, :]) for h, d in hd_pairs}
            part = {(h, d): (num[sub, h, d] + wi[sub, h, d] * inter[h, d][:ML_V_DIM]) * inv[h, d]
                    for h, d in hd_pairs}
            out = [(part[h, 0] + part[h, 1]).T for h in heads]
            for h in heads:
                y_ref[rows[sub], h * ML_V_DIM:(h + 1) * ML_V_DIM] = (out[h] * gate[sub, h]).astype(y_ref.dtype)
            v_tr = [v.astype(f32) for v in v_th[sub]]
            state_update(cum[sub], r[sub], k_own[sub], v_tr, m_row[sub], big_m[sub], 0)

        if emit_state:
            @pl.when(gi == nb - 1)
            def _():
                emit_final(0)
                mfin_ref[...] = m_fin


def mlstm_mix(qk, v, og, gates, gate_b, state, row0, n_seq, seq_len, emit_state, out_buf=None,
              layer=0, state_bufs=None):
    nc = seq_len // CHUNK
    grp = math.gcd(nc, SCAN_G_MAX)
    blk_rows = grp * CHUNK
    nb = seq_len // blk_rows
    blk0 = row0 // blk_rows
    has_state = state is not None

    def block_map(s, p, c):
        return (blk0 + s * nb + jnp.where(p == 0, nb - 1 - c, c), 0)

    m_spec = pl.BlockSpec((None, 1, LANES), lambda s, p, c: (s, 0, 0))
    in_specs = [
        pl.BlockSpec((blk_rows, 2 * ML_QK_WIDTH), block_map),
        pl.BlockSpec((blk_rows, ML_V_WIDTH), block_map),
        pl.BlockSpec((blk_rows, ML_V_WIDTH), lambda s, p, c: (blk0 + s * nb + p * c, 0)),
        pl.BlockSpec((blk_rows, LANES), block_map),
        pl.BlockSpec((1, LANES), lambda s, p, c: (0, 0)),
    ]
    gb = jnp.concatenate([gate_b.reshape(1, 2 * ML_DIRS), jnp.zeros((1, LANES - 2 * ML_DIRS), f32)], axis=1)
    args = [qk, v, og, gates, gb]
    if has_state:
        c0, n0, m0 = state
        rows = jnp.concatenate([c0, jnp.broadcast_to(n0[..., None, :], n0.shape[:-1] + (ML_ST - ML_V_DIM, ML_QK_DIM))],
                               axis=-2)
        zeros = jnp.zeros_like(rows)
        odd = (jnp.arange(ML_HEADS) % 2 == 1)[:, None, None]
        s0 = jnp.where(odd, jnp.concatenate([zeros, rows], axis=-1), jnp.concatenate([rows, zeros], axis=-1))
        m0 = jnp.concatenate([m0.reshape(n_seq, 1, ML_DIRS), jnp.zeros((n_seq, 1, LANES - ML_DIRS), f32)], axis=-1)
        in_specs += [pl.BlockSpec((None, 2, ML_HEADS, ML_ST, LANES), lambda s, p, c: (s, 0, 0, 0, 0)), m_spec]
        args += [s0, m0]
    aliases = {}
    if out_buf is not None:
        aliases[len(args)] = 0
        in_specs.append(pl.BlockSpec(memory_space=pl.ANY))
        args.append(out_buf)
    out_shape = [jax.ShapeDtypeStruct((N_TOK, ML_V_WIDTH), bf16)]
    out_specs = [pl.BlockSpec((blk_rows, ML_V_WIDTH), lambda s, p, c: (blk0 + s * nb + p * c, 0))]
    if emit_state:
        if state_bufs is not None:
            for k_out, buf in enumerate(state_bufs):
                aliases[len(args)] = 1 + k_out
                in_specs.append(pl.BlockSpec(memory_space=pl.ANY))
                args.append(buf)
        out_shape += [jax.ShapeDtypeStruct((n_seq, N_ODD, 2, ML_HEADS, ML_V_DIM, ML_QK_DIM), f32),
                      jax.ShapeDtypeStruct((n_seq, N_ODD, 2, ML_HEADS, ML_QK_DIM), f32),
                      jax.ShapeDtypeStruct((n_seq, 1, LANES), f32)]
        out_specs += [pl.BlockSpec((None, None, 2, ML_HEADS, ML_V_DIM, ML_QK_DIM),
                                   lambda s, p, c: (s, layer, 0, 0, 0, 0)),
                      pl.BlockSpec((None, None, 2, ML_HEADS, ML_QK_DIM), lambda s, p, c: (s, layer, 0, 0, 0)),
                      m_spec]
    res = pl.pallas_call(
        functools.partial(_mlstm_t_kernel, nc, grp, has_state, len(aliases), emit_state),
        input_output_aliases=aliases,
        grid=(n_seq, 2, nb),
        in_specs=in_specs,
        out_specs=out_specs,
        out_shape=out_shape,
        scratch_shapes=[pltpu.VMEM((nc, ML_HEADS, ML_ST, LANES), bf16), pltpu.VMEM((nc, SUBLANES, LANES), f32),
                        pltpu.VMEM((nc, 3, CHUNK, LANES), f32), pltpu.VMEM((nc, ML_HEADS, CHUNK, LANES), bf16),
                        pltpu.VMEM((nc, ML_HEADS, ML_V_DIM, CHUNK), bf16),
                        pltpu.VMEM((2, ML_HEADS, ML_ST, LANES), f32), pltpu.VMEM((SUBLANES, LANES), f32)],
        compiler_params=pltpu.CompilerParams(
            dimension_semantics=("arbitrary", "arbitrary", "arbitrary"), vmem_limit_bytes=VMEM_LIMIT),
        name="mlstm_scan",
    )(*args)
    if not emit_state:
        return res[0]
    y, c_fin, n_fin, mfin = res
    return y, c_fin, n_fin, mfin[:, 0, :ML_DIRS].reshape(n_seq, 2, ML_HEADS)


def kernel(x_prompt, x_sample, c, cache_na_k, cache_na_v, state_ssd, state_mlstm_c, state_mlstm_n, state_mlstm_m,
           c_ctx, w_mod, b_mod, norm_mix, norm_ffn, w_in_even, w_out_even, na_rpb, ssd_conv_w, ssd_conv_b,
           ssd_dt_bias, ssd_a_log, ssd_d, ssd_norm, w_in_odd, w_out_odd, ml_conv_w, ml_conv_b, ml_gate_b,
           w_ff1, w_ff2, norm_f):
    xs = [x_prompt.reshape(N_PROMPT, D_MODEL), x_sample.reshape(DEC_BATCH * DEC_SEQ, D_MODEL)]
    cond = jnp.concatenate([c_ctx[None, :], c, jnp.zeros((SUBLANES - N_COND, D_MODEL), f32)], axis=0)
    mod = adaln_all(cond, w_mod, b_mod)[:, :N_COND].reshape(DEPTH, N_COND, 1, N_MOD * D_MODEL)

    even_main = 3 * NA_WIDTH + SSD_INNER + SSD_CONV_DIM
    odd_main = 2 * ML_QK_WIDTH + 2 * ML_V_WIDTH

    def tail_bf16(w, main):
        t = w[:, :, main:]
        return jnp.concatenate([t, jnp.zeros(t.shape[:2] + (LANES - t.shape[2],), f32)], axis=2).astype(bf16)

    wi_even, wt_even = cast_bf16(w_in_even, even_main), tail_bf16(w_in_even, even_main)
    wi_odd, wt_odd = cast_bf16(w_in_odd, odd_main), tail_bf16(w_in_odd, odd_main)
    wo_even, wo_odd = cast_bf16(w_out_even), cast_bf16(w_out_odd)
    w1_all, w2_all = cast_bf16(w_ff1), cast_bf16(w_ff2)

    out_m = []
    new_k = new_v = new_ssd = new_c = new_n = None
    for l in range(DEPTH):
        norm_last = norm_f if l == DEPTH - 1 else None
        if l % 2 == 0:
            e = l // 2
            o0 = 3 * NA_WIDTH
            segs = ((0, NA_WIDTH, NA_HEAD_DIM ** -0.5, False),
                    (NA_WIDTH, 2 * NA_WIDTH, 1.0, False),
                    (2 * NA_WIDTH, 3 * NA_WIDTH, 1.0, False),
                    (o0, o0 + SSD_INNER, 1.0, False),
                    (o0 + SSD_INNER, o0 + SSD_INNER + SSD_CONV_DIM, 1.0, True),
                    (None, None, 1.0, False))
            outs = ((0, bf16, False), (1, bf16, False), (2, bf16, False), (1, f32, True), (2, f32, True),
                    (3, f32, False), (4, f32, False), (5, f32, False))
            q, k, v, new_k, new_v, z, xbc, dt = in_proj(
                xs, mod[l], norm_mix[l], wi_even, wt_even, e, segs, outs, ssd_conv_w[e], ssd_conv_b[e],
                cache_bufs=None if new_k is None else (new_k, new_v))
            ssd_w = (ssd_dt_bias[e], ssd_a_log[e], ssd_d[e], ssd_norm[e])
            y_ssd, new_ssd = ssd_mix(xbc, dt, z, None, 0, BATCH, SEQ, *ssd_w, True, layer=e, state_buf=new_ssd)
            y_ssd = ssd_mix(xbc, dt, z, state_ssd[:, e], N_PROMPT, DEC_BATCH, DEC_SEQ, *ssd_w, False, out_buf=y_ssd)
            tokens = lambda t: jnp.swapaxes(t, 1, 2).reshape(DEC_BATCH, PAST_LEN, NA_WIDTH).astype(bf16)
            y_na = ctx_attention(q, k, v)
            y_na = na_latent(q, k, v, tokens(cache_na_k[:, e]), tokens(cache_na_v[:, e]),
                             na_bias_table(na_rpb[e]), y_na)
            res = out_mlp(xs, mod[l], norm_ffn[l], [y_na, y_ssd], wo_even, e, w1_all, w2_all, l, norm_last)
        else:
            o = l // 2
            a0 = 2 * ML_QK_WIDTH
            segs = ((0, a0, 1.0, True),
                    (a0, a0 + ML_V_WIDTH, 1.0, False),
                    (a0 + ML_V_WIDTH, a0 + 2 * ML_V_WIDTH, 1.0, False),
                    (None, None, 1.0, False))
            outs = ((0, f32, False), (1, bf16, False), (2, f32, False), (3, f32, False))
            qk, v, og, gates = in_proj(xs, mod[l], norm_mix[l], wi_odd, wt_odd, o, segs, outs,
                                       ml_conv_w[o], ml_conv_b[o], row_block=ODD_ROW_BLOCK)
            y_ml, new_c, new_n, m_fin = mlstm_mix(qk, v, og, gates, ml_gate_b[o], None, 0, BATCH, SEQ, True,
                                                  layer=o, state_bufs=None if new_c is None else (new_c, new_n))
            out_m.append(m_fin)
            state = (state_mlstm_c[:, o], state_mlstm_n[:, o], state_mlstm_m[:, o])
            y_ml = mlstm_mix(qk, v, og, gates, ml_gate_b[o], state, N_PROMPT, DEC_BATCH, DEC_SEQ, False, out_buf=y_ml)
            res = out_mlp(xs, mod[l], norm_ffn[l], [y_ml], wo_odd, o, w1_all, w2_all, l, norm_last)
        xs = list(res) if norm_last is not None else [res]

    y_prompt = xs[0].reshape(BATCH, SEQ, D_MODEL)
    y_sample = xs[1].reshape(DEC_BATCH, DEC_SEQ, D_MODEL)
    return (y_prompt, y_sample, new_k, new_v, new_ssd,
            new_c, new_n, jnp.stack(out_m, axis=1))
```

```python
import functools
import math

import jax
import jax.numpy as jnp
from jax import lax
from jax.experimental import pallas as pl
from jax.experimental.pallas import tpu as pltpu

D_MODEL = 1024
BATCH = 32
SEQ = 256
DEPTH = 4
DEC_BATCH = 2
DEC_SEQ = 4096
PAST_LEN = 256
GRID_W = 64
N_EVEN = (DEPTH + 1) // 2
N_ODD = DEPTH // 2
RMS_EPS = 1e-6
N_MOD = 6
D_FF = 4 * D_MODEL
CONV_K = 3
Q_BLOCK = 128
NA_HEADS = 8
NA_HEAD_DIM = 64
NA_WIDTH = NA_HEADS * NA_HEAD_DIM
NA_WIN_ROWS = 8
NA_WIN_COLS = 16
NA_RPB_ROWS = 2 * NA_WIN_ROWS - 1
NA_RPB_COLS = 2 * NA_WIN_COLS - 1
SSD_INNER = D_MODEL
SSD_HEAD_DIM = 64
SSD_HEADS = SSD_INNER // SSD_HEAD_DIM
SSD_GROUPS = 2
SSD_RPG = SSD_HEADS // SSD_GROUPS
SSD_STATE = 128
SSD_GN = SSD_GROUPS * SSD_STATE
SSD_CONV_DIM = SSD_INNER + 2 * SSD_GN
SSD_CHUNK = 128
ML_HEADS = 8
ML_QK_DIM = D_MODEL // 16
ML_V_DIM = D_MODEL // 8
ML_QK_WIDTH = ML_HEADS * ML_QK_DIM
ML_V_WIDTH = ML_HEADS * ML_V_DIM
ML_CHUNK = 64
EVEN_MIX = NA_WIDTH + SSD_INNER

N_PROMPT = BATCH * SEQ
N_TOK = N_PROMPT + DEC_BATCH * DEC_SEQ
N_COND = 1 + DEC_BATCH
LANES = 128
SUBLANES = 8
VMEM_LIMIT = 56 * 1024 * 1024
TM = 512
ODD_ROW_BLOCK = 128

f32 = jnp.float32
bf16 = jnp.bfloat16


def _cond_row(i, tm):
    start = i * tm
    return jnp.where(start < N_PROMPT, 0, (start - N_PROMPT) // DEC_SEQ + 1)


def _const_spec(shape):
    nd = len(shape)
    return pl.BlockSpec(shape, lambda i: (0,) * nd, pipeline_mode=pl.Buffered(1))


def _rms(x):
    return x * lax.rsqrt(jnp.mean(x * x, axis=-1, keepdims=True) + RMS_EPS)


def _modulated(x, g, mod, k):
    shift = mod[:, k * D_MODEL:(k + 1) * D_MODEL]
    scale = mod[:, (k + 1) * D_MODEL:(k + 2) * D_MODEL]
    return (_rms(x) * g) * (1.0 + scale) + shift


def _mod_kernel(c_ref, w_ref, b_ref, o_ref):
    c = c_ref[...]
    a = (c * jax.nn.sigmoid(c)).astype(bf16)
    o_ref[...] = jnp.dot(a, w_ref[...].astype(bf16), preferred_element_type=f32) + b_ref[...]


def adaln_all(cond, w_mod, b_mod):
    tn = 1536
    nj = N_MOD * D_MODEL // tn
    return pl.pallas_call(
        _mod_kernel,
        grid=(DEPTH, nj),
        in_specs=[
            pl.BlockSpec((SUBLANES, D_MODEL), lambda l, j: (0, 0)),
            pl.BlockSpec((None, D_MODEL, tn), lambda l, j: (l, 0, j)),
            pl.BlockSpec((None, 1, tn), lambda l, j: (l, 0, j)),
        ],
        out_specs=pl.BlockSpec((None, SUBLANES, tn), lambda l, j: (l, 0, j)),
        out_shape=jax.ShapeDtypeStruct((DEPTH, SUBLANES, N_MOD * D_MODEL), f32),
        compiler_params=pltpu.CompilerParams(
            dimension_semantics=("arbitrary", "arbitrary"), vmem_limit_bytes=VMEM_LIMIT),
        name="adaln",
    )(cond, w_mod, b_mod.reshape(DEPTH, 1, N_MOD * D_MODEL))


def _cast_kernel(x_ref, o_ref):
    o_ref[...] = x_ref[...].astype(o_ref.dtype)


def cast_bf16(w, cols=None):
    n_l, k, n = w.shape
    cols = n if cols is None else cols
    bk = 512
    spec = pl.BlockSpec((None, bk, cols), lambda l, i: (l, i, 0))
    return pl.pallas_call(
        _cast_kernel,
        grid=(n_l, k // bk),
        in_specs=[spec],
        out_specs=spec,
        out_shape=jax.ShapeDtypeStruct((n_l, k, cols), bf16),
        compiler_params=pltpu.CompilerParams(
            dimension_semantics=("arbitrary", "arbitrary"), vmem_limit_bytes=VMEM_LIMIT),
        name="cast_bf16",
    )(w)


def _layer_spec(w, l):
    nd = w.ndim - 1
    return pl.BlockSpec((None,) + w.shape[1:], lambda i: (l,) + (0,) * nd, pipeline_mode=pl.Buffered(1))


def _tile_x(x_refs, tm):
    if len(x_refs) == 1:
        return x_refs[0][...]
    return jnp.where(pl.program_id(0) < N_PROMPT // tm, x_refs[0][...], x_refs[1][...])


def _conv_silu_tile(y, prev_row, next_row, w, b):
    i = pl.program_id(0)
    rows = y.shape[0]
    is_prompt = i < N_PROMPT // rows
    seq_last = jnp.where(is_prompt, SEQ - 1, DEC_SEQ - 1)

    def act(up, mid, dn):
        c = up * w[0:1, :] + mid * w[1:2, :] + dn * w[2:3, :] + b
        return c * jax.nn.sigmoid(c)

    tile = act(pltpu.roll(y, 1, axis=0), y, pltpu.roll(y, rows - 1, axis=0))
    starts = set(range(0, rows, SEQ))
    fixes = []
    for t in sorted(starts | {(st - 1) % rows for st in starts}):
        g = i * rows + t
        pos = jnp.where(is_prompt, g & (SEQ - 1), (g - N_PROMPT) & (DEC_SEQ - 1))
        up = prev_row if t == 0 else y[t - 1:t]
        dn = next_row if t == rows - 1 else y[t + 1:t + 2]
        fixes.append((t, act(jnp.where(pos == 0, 0.0, up), y[t:t + 1], jnp.where(pos == seq_last, 0.0, dn))))
    return tile, fixes


def _in_proj_kernel(n_x, n_buf, row_block, segs, outs, *refs):
    x_refs = refs[:n_x]
    prev_ref, next_ref, mod_ref, g_ref, w_ref, wt_ref, cw_ref, cb_ref = refs[n_x:n_x + 8]
    o_refs = refs[n_x + 8 + n_buf:]
    mod, g = mod_ref[...], g_ref[...]
    x = _tile_x(x_refs, TM)
    n_blocks = TM // row_block
    halo = _modulated(jnp.concatenate([prev_ref[...], next_ref[...]], axis=0), g, mod, 0).astype(bf16)
    parts = [[] for _ in segs]
    for r in range(n_blocks):
        hb = _modulated(x[r * row_block:(r + 1) * row_block], g, mod, 0).astype(bf16)
        for i, (a, b, scale, conv) in enumerate(segs):
            w = wt_ref[...] if a is None else w_ref[:, a:b]
            lhs = jnp.concatenate([hb, halo], axis=0) if conv and r == n_blocks - 1 else hb
            parts[i].append(jnp.dot(lhs, w, preferred_element_type=f32))
    ys = []
    for (a, b, scale, conv), p in zip(segs, parts):
        y = jnp.concatenate(p, axis=0)
        if conv:
            ys.append(_conv_silu_tile(y[:TM], y[TM + SUBLANES - 1:TM + SUBLANES],
                                      y[TM + SUBLANES:TM + SUBLANES + 1], cw_ref[...], cb_ref[...]))
        else:
            ys.append(y if scale == 1.0 else y * scale)
    for (si, _, head_major), o_ref in zip(outs, o_refs):
        if head_major:
            @pl.when(pl.program_id(0) < N_PROMPT // TM)
            def _(o_ref=o_ref, si=si):
                for b in range(TM // SEQ):
                    for hd in range(NA_HEADS):
                        o_ref[b, hd] = ys[si][b * SEQ:(b + 1) * SEQ, hd * NA_HEAD_DIM:(hd + 1) * NA_HEAD_DIM]
        elif segs[si][3]:
            tile, fixes = ys[si]
            o_ref[...] = tile.astype(o_ref.dtype)
            for t, row in fixes:
                o_ref[t:t + 1, :] = row.astype(o_ref.dtype)
        else:
            o_ref[...] = ys[si].astype(o_ref.dtype)


def _x_specs(xs, tm):
    if len(xs) == 1:
        return [pl.BlockSpec((tm, D_MODEL), lambda i: (i, 0))]
    n_p = N_PROMPT // tm
    return [pl.BlockSpec((tm, D_MODEL), lambda i: (jnp.minimum(i, n_p - 1), 0)),
            pl.BlockSpec((tm, D_MODEL), lambda i: (jnp.maximum(i - n_p, 0), 0))]


def in_proj(xs, mod_l, g, w, w_tail, l, segs, outs, conv_w, conv_b, cache_bufs=None, row_block=TM):
    grid = (N_TOK // TM,)
    last_prompt = N_PROMPT // TM - 1
    halo_src = xs[-1]
    per = TM // SUBLANES
    blk0 = (halo_src.shape[0] - DEC_BATCH * DEC_SEQ) // SUBLANES
    n_blk = halo_src.shape[0] // SUBLANES
    tile0 = N_PROMPT // TM

    def prev_map(i):
        return (jnp.clip(blk0 + (i - tile0) * per - 1, 0, n_blk - 1), 0)

    def next_map(i):
        return (jnp.clip(blk0 + (i - tile0 + 1) * per, 0, n_blk - 1), 0)

    ch = conv_w.shape[1]
    out_shape, out_specs = [], []
    for si, dt, head_major in outs:
        width = LANES if segs[si][0] is None else segs[si][1] - segs[si][0]
        if head_major:
            out_shape.append(jax.ShapeDtypeStruct((BATCH, N_EVEN, NA_HEADS, SEQ, NA_HEAD_DIM), dt))
            out_specs.append(pl.BlockSpec((TM // SEQ, None, NA_HEADS, SEQ, NA_HEAD_DIM),
                                          lambda i: (jnp.minimum(i, last_prompt), l, 0, 0, 0)))
        else:
            out_shape.append(jax.ShapeDtypeStruct((N_TOK, width), dt))
            out_specs.append(pl.BlockSpec((TM, width), lambda i: (i, 0)))
    args = [*xs, halo_src, halo_src, mod_l, g.reshape(1, D_MODEL), w, w_tail, conv_w, conv_b.reshape(1, ch)]
    aliases = {}
    if cache_bufs is not None:
        head_major_outs = [k for k, o in enumerate(outs) if o[2]]
        for k_out, buf in zip(head_major_outs, cache_bufs):
            aliases[len(args)] = k_out
            args.append(buf)
    return pl.pallas_call(
        functools.partial(_in_proj_kernel, len(xs), len(aliases), row_block, segs, outs),
        input_output_aliases=aliases,
        grid=grid,
        in_specs=_x_specs(xs, TM) + [
            pl.BlockSpec((SUBLANES, D_MODEL), prev_map),
            pl.BlockSpec((SUBLANES, D_MODEL), next_map),
            pl.BlockSpec((None, 1, N_MOD * D_MODEL), lambda i: (_cond_row(i, TM), 0, 0)),
            _const_spec((1, D_MODEL)),
            _layer_spec(w, l),
            _layer_spec(w_tail, l),
            _const_spec((CONV_K, ch)),
            _const_spec((1, ch)),
        ] + [pl.BlockSpec(memory_space=pl.ANY)] * len(aliases),
        out_specs=out_specs,
        out_shape=out_shape,
        compiler_params=pltpu.CompilerParams(
            dimension_semantics=("arbitrary",), vmem_limit_bytes=VMEM_LIMIT),
        name="in_proj",
    )(*args)


def _out_mlp_kernel(n_x, n_mix, final, *refs):
    x_refs = refs[:n_x]
    mod_ref, g_ref = refs[n_x:n_x + 2]
    refs = refs[n_x + 2:]
    mix_refs = refs[:n_mix]
    wo_ref, w1_ref, w2_ref = refs[n_mix:n_mix + 3]
    rest = refs[n_mix + 3:]
    if final:
        gf_ref, op_ref, os_ref = rest
    else:
        (o_ref,) = rest
    mod = mod_ref[...]
    m = None
    k0 = 0
    for r in mix_refs:
        kw = r.shape[-1]
        part = jnp.dot(r[...].astype(bf16), wo_ref[k0:k0 + kw, :], preferred_element_type=f32)
        m = part if m is None else m + part
        k0 += kw
    x1 = _tile_x(x_refs, OUT_TM) + mod[:, 2 * D_MODEL:3 * D_MODEL] * m
    h2 = _modulated(x1, g_ref[...], mod, 3).astype(bf16)
    u = jnp.dot(h2, w1_ref[...], preferred_element_type=f32)
    a = jnp.square(jnp.maximum(u, 0.0)).astype(bf16)
    x2 = x1 + mod[:, 5 * D_MODEL:6 * D_MODEL] * jnp.dot(a, w2_ref[...], preferred_element_type=f32)
    if not final:
        o_ref[...] = x2
        return
    y = _rms(x2) * gf_ref[...]
    is_prompt = pl.program_id(0) < N_PROMPT // OUT_TM

    @pl.when(is_prompt)
    def _():
        op_ref[...] = y

    @pl.when(jnp.logical_not(is_prompt))
    def _():
        os_ref[...] = y


OUT_TM = 512


def out_mlp(xs, mod_l, g_ffn, mixes, w_out, l_out, w1, w2, l, norm_f=None):
    final = norm_f is not None
    tm = OUT_TM
    grid = (N_TOK // tm,)
    in_specs = _x_specs(xs, tm) + [
        pl.BlockSpec((None, 1, N_MOD * D_MODEL), lambda i: (_cond_row(i, tm), 0, 0)),
        _const_spec((1, D_MODEL)),
    ]
    in_specs += [pl.BlockSpec((tm, m.shape[-1]), lambda i: (i, 0)) for m in mixes]
    in_specs += [_layer_spec(w_out, l_out), _layer_spec(w1, l), _layer_spec(w2, l)]
    args = [*xs, mod_l, g_ffn.reshape(1, D_MODEL), *mixes, w_out, w1, w2]
    out_specs = pl.BlockSpec((tm, D_MODEL), lambda i: (i, 0))
    out_shape = jax.ShapeDtypeStruct((N_TOK, D_MODEL), f32)
    if final:
        in_specs.append(_const_spec((1, D_MODEL)))
        args.append(norm_f.reshape(1, D_MODEL))
        n_p = N_PROMPT // tm
        out_specs = [pl.BlockSpec((tm, D_MODEL), lambda i: (jnp.minimum(i, n_p - 1), 0)),
                     pl.BlockSpec((tm, D_MODEL), lambda i: (jnp.maximum(i - n_p, 0), 0))]
        out_shape = [jax.ShapeDtypeStruct((N_PROMPT, D_MODEL), f32),
                     jax.ShapeDtypeStruct((N_TOK - N_PROMPT, D_MODEL), f32)]
    return pl.pallas_call(
        functools.partial(_out_mlp_kernel, len(xs), len(mixes), final),
        grid=grid,
        in_specs=in_specs,
        out_specs=out_specs,
        out_shape=out_shape,
        compiler_params=pltpu.CompilerParams(
            dimension_semantics=("arbitrary",), vmem_limit_bytes=VMEM_LIMIT),
        name="out_mlp",
    )(*args)


NA_PAIRS = NA_HEADS // 2
NA_ROWS = DEC_SEQ // GRID_W
NA_WIN = NA_WIN_ROWS * GRID_W


def _dot_nt(a, b):
    return lax.dot_general(a, b, (((1,), (1,)), ((), ())), preferred_element_type=f32)


def _pair_stack(x):
    lane = lax.broadcasted_iota(jnp.int32, x.shape, 1)
    zero = jnp.zeros_like(x)
    return jnp.concatenate([jnp.where(lane < NA_HEAD_DIM, x, zero), jnp.where(lane >= NA_HEAD_DIM, x, zero)], axis=0)


def _pair_unstack(o):
    n = o.shape[0] // 2
    lane = lax.broadcasted_iota(jnp.int32, (n, LANES), 1)
    return jnp.where(lane < NA_HEAD_DIM, o[:n], o[n:])


def _softmax_pv(scores, values):
    def lane_tiles(blocks):
        return [b[:, k * LANES:(k + 1) * LANES] for b in blocks for k in range(b.shape[1] // LANES)]

    ms = [jnp.max(functools.reduce(jnp.maximum, lane_tiles(s)), axis=1, keepdims=True) for s in scores]
    ps = [[jnp.exp(b - m) for b in s] for s, m in zip(scores, ms)]
    invs = [1.0 / jnp.sum(functools.reduce(jnp.add, lane_tiles(p)), axis=1, keepdims=True) for p in ps]
    pn = [[(b * inv).astype(bf16) for b in p] for p, inv in zip(ps, invs)]
    return [functools.reduce(jnp.add, [jnp.dot(b, v, preferred_element_type=f32) for b, v in zip(p, vs)])
            for p, vs in zip(pn, values)]


def _ctx_attn_kernel(q_ref, k_ref, v_ref, o_ref):
    pairs = [slice(p * LANES, (p + 1) * LANES) for p in range(NA_PAIRS)]
    scores = [[_dot_nt(_pair_stack(q_ref[:, lanes]), k_ref[:, lanes])] for lanes in pairs]
    outs = _softmax_pv(scores, [[v_ref[:, lanes]] for lanes in pairs])
    for lanes, o in zip(pairs, outs):
        o_ref[:, lanes] = _pair_unstack(o).astype(o_ref.dtype)


def ctx_attention(q, k, v):
    spec = pl.BlockSpec((SEQ, NA_WIDTH), lambda b: (b, 0))
    return pl.pallas_call(
        _ctx_attn_kernel,
        grid=(BATCH,),
        in_specs=[spec, spec, spec],
        out_specs=spec,
        out_shape=jax.ShapeDtypeStruct((N_TOK, NA_WIDTH), bf16),
        compiler_params=pltpu.CompilerParams(dimension_semantics=("arbitrary",), vmem_limit_bytes=VMEM_LIMIT),
        name="ctx_attn",
    )(q, k, v)


def _na_bias_kernel(rpb_ref, o_ref):
    pair = pl.program_id(0)
    shape = (GRID_W, LANES)
    qc = lax.broadcasted_iota(jnp.int32, shape, 0)
    lane = lax.broadcasted_iota(jnp.int32, shape, 1)
    kc = lane & (GRID_W - 1)
    low = lane < GRID_W
    col_start = jnp.clip(qc - NA_WIN_COLS // 2, 0, GRID_W - NA_WIN_COLS)
    valid = (kc >= col_start) & (kc < col_start + NA_WIN_COLS)
    rel_c = jnp.clip(kc - qc + NA_WIN_COLS - 1, 0, NA_RPB_COLS - 1)
    for e in range(2):
        base = (2 * pair + e) * NA_RPB_ROWS
        pieces = []
        for rr in range(NA_RPB_ROWS - 1):
            val = jnp.zeros(shape, f32)
            for t in range(NA_RPB_COLS):
                s_lo = rpb_ref[(base + rr) * NA_RPB_COLS + t]
                s_hi = rpb_ref[(base + rr + 1) * NA_RPB_COLS + t]
                val = jnp.where(rel_c == t, jnp.where(low, s_lo, s_hi), val)
            pieces.append(jnp.where(valid, val, -jnp.inf))
        for d in range(NA_WIN_ROWS):
            for i in range(0, NA_WIN_ROWS, 2):
                o_ref[d, e * GRID_W:(e + 1) * GRID_W, i * GRID_W:(i + 2) * GRID_W] = pieces[d + i]


def na_bias_table(rpb):
    return pl.pallas_call(
        _na_bias_kernel,
        grid=(NA_PAIRS,),
        in_specs=[pl.BlockSpec(memory_space=pltpu.SMEM)],
        out_specs=pl.BlockSpec((None, NA_WIN_ROWS, 2 * GRID_W, NA_WIN), lambda p: (p, 0, 0, 0)),
        out_shape=jax.ShapeDtypeStruct((NA_PAIRS, NA_WIN_ROWS, 2 * GRID_W, NA_WIN), f32),
        compiler_params=pltpu.CompilerParams(dimension_semantics=("arbitrary",), vmem_limit_bytes=VMEM_LIMIT),
        name="na_bias",
    )(rpb.reshape(-1))


NA_STEP_ROWS = 8


def _na_first_key_row(r):
    return jnp.clip(r - NA_WIN_ROWS // 2, 0, NA_ROWS - NA_WIN_ROWS)


def _na_kernel(q_ref, k_ref, v_ref, kc_ref, vc_ref, bias_ref, buf_ref, o_ref):
    del buf_ref
    where, scores, values = [], [], []
    for j in range(NA_STEP_ROWS):
        r = pl.program_id(1) * NA_STEP_ROWS + j
        first = _na_first_key_row(r)
        start = pl.multiple_of(first * GRID_W, GRID_W)
        shift = first - r + NA_WIN_ROWS - 1
        rows = slice(j * GRID_W, (j + 1) * GRID_W)
        for p in range(NA_PAIRS):
            lanes = slice(p * LANES, (p + 1) * LANES)
            qq = _pair_stack(q_ref[rows, lanes])
            where.append((rows, lanes))
            scores.append([_dot_nt(qq, k_ref[pl.ds(start, NA_WIN), lanes]) + bias_ref[p, shift],
                           _dot_nt(qq, kc_ref[:, lanes])])
            values.append([v_ref[pl.ds(start, NA_WIN), lanes], vc_ref[:, lanes]])
    for (rows, lanes), o in zip(where, _softmax_pv(scores, values)):
        o_ref[rows, lanes] = _pair_unstack(o).astype(o_ref.dtype)


def na_latent(q, k, v, k_ctx, v_ctx, bias, buf):
    rows = NA_STEP_ROWS * GRID_W
    steps = NA_ROWS // NA_STEP_ROWS
    row0 = N_PROMPT // rows
    seq0 = N_PROMPT // DEC_SEQ
    kv_spec = pl.BlockSpec((DEC_SEQ, NA_WIDTH), lambda b, r: (seq0 + b, 0))
    ctx_spec = pl.BlockSpec((None, PAST_LEN, NA_WIDTH), lambda b, r: (b, 0, 0))
    return pl.pallas_call(
        _na_kernel,
        grid=(DEC_BATCH, steps),
        in_specs=[
            pl.BlockSpec((rows, NA_WIDTH), lambda b, r: (row0 + b * steps + r, 0)),
            kv_spec, kv_spec, ctx_spec, ctx_spec,
            pl.BlockSpec(bias.shape, lambda b, r: (0, 0, 0, 0), pipeline_mode=pl.Buffered(1)),
            pl.BlockSpec(memory_space=pl.ANY),
        ],
        out_specs=pl.BlockSpec((rows, NA_WIDTH), lambda b, r: (row0 + b * steps + r, 0)),
        out_shape=jax.ShapeDtypeStruct((N_TOK, NA_WIDTH), bf16),
        input_output_aliases={6: 0},
        compiler_params=pltpu.CompilerParams(
            dimension_semantics=("arbitrary", "arbitrary"), vmem_limit_bytes=VMEM_LIMIT),
        name="na_latent",
    )(q, k, v, k_ctx, v_ctx, bias, buf)


CHUNK = 128
SCAN_G_MAX = 4


def _when(cond):
    return (lambda f: f()) if cond is True else pl.when(cond)


def _scan_phases(single, phase, gi, nb):
    if single:
        return True, True, True, True, True
    return (phase == 0, phase == 1, jnp.logical_and(phase == 0, gi == 0), jnp.logical_and(phase == 1, gi == 0),
            gi == nb - 1)


def _cumsum_rows(a, reverse=False):
    row = lax.broadcasted_iota(jnp.int32, a.shape, 0)
    s = 1
    while s < CHUNK:
        if reverse:
            a = a + jnp.where(row < CHUNK - s, pltpu.roll(a, CHUNK - s, axis=0), 0.0)
        else:
            a = a + jnp.where(row >= s, pltpu.roll(a, s, axis=0), 0.0)
        s *= 2
    return a


def _ssd_kernel(nc, grp, has_h0, n_buf, emit_state, *refs):
    xbc_ref, dt_ref, z_ref = refs[:3]
    refs = refs[3:]
    if has_h0:
        h0_ref, refs = refs[0], refs[1:]
    dtb_ref, alog_ref, dskip_ref, g_ref = refs[:4]
    refs = refs[4 + n_buf:]
    y_ref, refs = refs[0], refs[1:]
    if emit_state:
        hfin_ref, refs = refs[0], refs[1:]
    hb_store, xt_store, gate_store, carry = refs

    gi = pl.program_id(2)
    nb = nc // grp
    in_bwd, in_fwd, first_bwd, first_fwd, last = _scan_phases(nb == 1, pl.program_id(1), gi, nb)
    gw = SSD_RPG * SSD_HEAD_DIM

    def load_h0(d):
        if has_h0:
            return h0_ref[d].reshape(SSD_INNER, SSD_STATE)
        return jnp.zeros((SSD_INNER, SSD_STATE), f32)

    def head_rows(v):
        return jnp.concatenate(
            [jnp.broadcast_to(v[h:h + 1, :], (SSD_HEAD_DIM, v.shape[1])) for h in range(SSD_HEADS)], axis=0)

    def state_update(x_t, b_bf, cum_t, dt_t, edge):
        at_edge = jnp.broadcast_to(cum_t[:, edge:edge + 1], cum_t.shape)
        w_end = jnp.exp(at_edge - cum_t) * dt_t
        xw = (x_t * head_rows(w_end)).astype(bf16)
        upd = jnp.concatenate(
            [jnp.dot(xw[g * gw:(g + 1) * gw], b_bf[:, g * SSD_STATE:(g + 1) * SSD_STATE], preferred_element_type=f32)
             for g in range(SSD_GROUPS)], axis=0)
        carry[...] = carry[...] * head_rows(jnp.exp(at_edge)) + upd

    @_when(first_bwd)
    def _():
        carry[...] = load_h0(1)

    def backward_chunk(sub):
        rows = slice(sub * CHUNK, (sub + 1) * CHUNK)
        j = grp * (nb - 1 - gi) + sub
        x = xbc_ref[rows, :SSD_INNER]
        b_bf = xbc_ref[rows, SSD_INNER:SSD_INNER + SSD_GN].astype(bf16)
        dt = jax.nn.softplus(dt_ref[rows, :] + dtb_ref[...])
        a = dt * (-jnp.exp(alog_ref[...]))
        x_t = jnp.concatenate([x[:, k * LANES:(k + 1) * LANES].T for k in range(SSD_INNER // LANES)], axis=0)
        rcum = _cumsum_rows(a, reverse=True)
        xt_store[j] = x_t
        gate_store[j, 0] = dt
        gate_store[j, 1] = _cumsum_rows(a)
        gate_store[j, 2] = rcum
        hb_store[j] = carry[...].astype(bf16)
        state_update(x_t, b_bf, rcum.T[SSD_HEADS:2 * SSD_HEADS], dt.T[SSD_HEADS:2 * SSD_HEADS], 0)

    @_when(in_bwd)
    def _backward_states():
        for sub in reversed(range(grp)):
            backward_chunk(sub)

        if emit_state:
            @_when(last)
            def _():
                hfin_ref[1] = carry[...].reshape(SSD_HEADS, SSD_HEAD_DIM, SSD_STATE)

    @_when(first_fwd)
    def _():
        carry[...] = load_h0(0)

    @_when(in_fwd)
    def _forward_and_outputs():
        subs = range(grp)
        heads = range(SSD_HEADS)
        sub_heads = [(sub, h) for sub in subs for h in heads]
        rows = [slice(sub * CHUNK, (sub + 1) * CHUNK) for sub in subs]
        cidx = [grp * gi + sub for sub in subs]
        row = lax.broadcasted_iota(jnp.int32, (CHUNK, CHUNK), 0)
        col = lax.broadcasted_iota(jnp.int32, (CHUNK, CHUNK), 1)
        causal = col <= row
        anti = col >= row
        lane = lax.broadcasted_iota(jnp.int32, (CHUNK, LANES), 1)
        x = [xbc_ref[rw, :SSD_INNER] for rw in rows]
        x_bf = [xs.astype(bf16) for xs in x]
        b_bf = [xbc_ref[rw, SSD_INNER:SSD_INNER + SSD_GN].astype(bf16) for rw in rows]
        c_mat = [xbc_ref[rw, SSD_INNER + SSD_GN:].astype(bf16) for rw in rows]
        dt = [gate_store[c, 0] for c in cidx]
        cum = [gate_store[c, 1] for c in cidx]
        rcum = [gate_store[c, 2] for c in cidx]
        cum_t = [t.T for t in cum]
        rcum_t = [t.T for t in rcum]
        dt_t = [t.T for t in dt]
        cb = {(sub, g): _dot_nt(c_mat[sub][:, g * SSD_STATE:(g + 1) * SSD_STATE],
                                b_bf[sub][:, g * SSD_STATE:(g + 1) * SSD_STATE])
              for sub in subs for g in range(SSD_GROUPS)}
        log_dt = [jnp.log(t) for t in dt_t]
        key_f = [cum_t[sub] - log_dt[sub] for sub in subs]
        key_b = [rcum_t[sub] - log_dt[sub] for sub in subs]
        seg_f = {(sub, h): jnp.where(causal, cum[sub][:, h:h + 1] - key_f[sub][h:h + 1, :], -jnp.inf)
                 for sub, h in sub_heads}
        seg_b = {(sub, h): jnp.where(anti, rcum[sub][:, SSD_HEADS + h:SSD_HEADS + h + 1]
                                     - key_b[sub][SSD_HEADS + h:SSD_HEADS + h + 1, :], -jnp.inf)
                 for sub, h in sub_heads}
        e_f = {sh: jnp.exp(seg_f[sh]) for sh in sub_heads}
        e_b = {sh: jnp.exp(seg_b[sh]) for sh in sub_heads}
        ws = {(sub, h): (cb[sub, h // SSD_RPG] * (e_f[sub, h] + e_b[sub, h])).astype(bf16) for sub, h in sub_heads}
        rhs = {}
        for sub in subs:
            for p in range(SSD_HEADS // 2):
                xp = x_bf[sub][:, p * LANES:(p + 1) * LANES]
                zero = jnp.zeros_like(xp)
                rhs[sub, p] = jnp.concatenate([jnp.where(lane < SSD_HEAD_DIM, xp, zero),
                                               jnp.where(lane >= SSD_HEAD_DIM, xp, zero)], axis=0)
        y_intra = [jnp.concatenate(
            [jnp.dot(jnp.concatenate([ws[sub, 2 * p], ws[sub, 2 * p + 1]], axis=1), rhs[sub, p],
                     preferred_element_type=f32) for p in range(SSD_HEADS // 2)], axis=1) for sub in subs]

        for sub in subs:
            hf = carry[...].astype(bf16)
            hb = hb_store[cidx[sub]]

            def inter_t(h_all):
                return jnp.concatenate(
                    [_dot_nt(h_all[g * gw:(g + 1) * gw], c_mat[sub][:, g * SSD_STATE:(g + 1) * SSD_STATE])
                     for g in range(SSD_GROUPS)], axis=0)

            cum_f, cum_b = cum_t[sub][:SSD_HEADS], rcum_t[sub][SSD_HEADS:2 * SSD_HEADS]
            y_t = inter_t(hf) * head_rows(jnp.exp(cum_f)) + inter_t(hb) * head_rows(jnp.exp(cum_b))
            y_inter = jnp.concatenate([y_t[k * LANES:(k + 1) * LANES].T for k in range(SSD_INNER // LANES)], axis=1)
            state_update(xt_store[cidx[sub]], b_bf[sub], cum_f, dt_t[sub][:SSD_HEADS], CHUNK - 1)
            y = y_intra[sub] + y_inter + dskip_ref[...] * x[sub]

            zv = z_ref[rows[sub], :]
            yz = y * (zv * jax.nn.sigmoid(zv))
            y_ref[rows[sub], :] = (_rms(yz) * g_ref[...]).astype(y_ref.dtype)

        if emit_state:
            @_when(last)
            def _():
                hfin_ref[0] = carry[...].reshape(SSD_HEADS, SSD_HEAD_DIM, SSD_STATE)


def ssd_mix(xbc, dt, z, h0, row0, n_seq, seq_len, dt_bias, a_log, d_skip, norm_g, emit_state, out_buf=None,
            layer=0, state_buf=None):
    nc = seq_len // CHUNK
    grp = math.gcd(nc, SCAN_G_MAX)
    blk_rows = grp * CHUNK
    nb = seq_len // blk_rows
    blk0 = row0 // blk_rows
    has_h0 = h0 is not None

    def block_map(s, p, c):
        return (blk0 + s * nb + jnp.where(p == 0, nb - 1 - c, c), 0)

    state_spec = pl.BlockSpec((None, 2, SSD_HEADS, SSD_HEAD_DIM, SSD_STATE), lambda s, p, c: (s, 0, 0, 0, 0))
    vec = lambda n: pl.BlockSpec((1, n), lambda s, p, c: (0, 0))
    in_specs = [
        pl.BlockSpec((blk_rows, SSD_CONV_DIM), block_map),
        pl.BlockSpec((blk_rows, LANES), block_map),
        pl.BlockSpec((blk_rows, SSD_INNER), lambda s, p, c: (blk0 + s * nb + p * c, 0)),
    ]
    args = [xbc, dt, z]
    if has_h0:
        in_specs.append(state_spec)
        args.append(h0)
    in_specs += [vec(LANES), vec(LANES), vec(SSD_INNER), vec(SSD_INNER)]
    pad = lambda t: jnp.concatenate([t.reshape(1, -1), jnp.zeros((1, LANES - t.size), f32)], axis=1)
    args += [pad(dt_bias), pad(a_log), jnp.repeat(d_skip, SSD_HEAD_DIM).reshape(1, SSD_INNER),
             norm_g.reshape(1, SSD_INNER)]
    aliases = {}
    if out_buf is not None:
        aliases = {len(args): 0}
        in_specs.append(pl.BlockSpec(memory_space=pl.ANY))
        args.append(out_buf)
    out_shape = [jax.ShapeDtypeStruct((N_TOK, SSD_INNER), bf16)]
    out_specs = [pl.BlockSpec((blk_rows, SSD_INNER), lambda s, p, c: (blk0 + s * nb + p * c, 0))]
    if emit_state:
        if state_buf is not None:
            aliases[len(args)] = 1
            in_specs.append(pl.BlockSpec(memory_space=pl.ANY))
            args.append(state_buf)
        out_shape.append(jax.ShapeDtypeStruct((n_seq, N_EVEN, 2, SSD_HEADS, SSD_HEAD_DIM, SSD_STATE), f32))
        out_specs.append(pl.BlockSpec((None, None, 2, SSD_HEADS, SSD_HEAD_DIM, SSD_STATE),
                                      lambda s, p, c: (s, layer, 0, 0, 0, 0)))
    res = pl.pallas_call(
        functools.partial(_ssd_kernel, nc, grp, has_h0, len(aliases), emit_state),
        input_output_aliases=aliases,
        grid=(n_seq, 1 if nb == 1 else 2, nb),
        in_specs=in_specs,
        out_specs=out_specs,
        out_shape=out_shape,
        scratch_shapes=[pltpu.VMEM((nc, SSD_INNER, SSD_STATE), bf16), pltpu.VMEM((nc, SSD_INNER, CHUNK), f32),
                        pltpu.VMEM((nc, 3, CHUNK, LANES), f32), pltpu.VMEM((SSD_INNER, SSD_STATE), f32)],
        compiler_params=pltpu.CompilerParams(
            dimension_semantics=("arbitrary", "arbitrary", "arbitrary"), vmem_limit_bytes=VMEM_LIMIT),
        name="ssd_scan",
    )(*args)
    return res if emit_state else res[0]


ML_DIRS = 2 * ML_HEADS


def _cummax_rows(a, reverse=False):
    row = lax.broadcasted_iota(jnp.int32, a.shape, 0)
    s = 1
    while s < CHUNK:
        if reverse:
            a = jnp.maximum(a, jnp.where(row < CHUNK - s, pltpu.roll(a, CHUNK - s, axis=0), -jnp.inf))
        else:
            a = jnp.maximum(a, jnp.where(row >= s, pltpu.roll(a, s, axis=0), -jnp.inf))
        s *= 2
    return a


ML_ST = ML_V_DIM + 16


def _mlstm_t_kernel(nc, grp, has_state, n_buf, emit_state, *refs):
    qk_ref, v_ref, og_ref, gates_ref, gb_ref = refs[:5]
    refs = refs[5:]
    if has_state:
        s0_ref, m0_ref = refs[:2]
        refs = refs[2:]
    refs = refs[n_buf:]
    y_ref, refs = refs[0], refs[1:]
    if emit_state:
        cfin_ref, nfin_ref, mfin_ref = refs[:3]
        refs = refs[3:]
    s_store, m_store, gate_store, k_store, v_store, s_carry, m_carry = refs

    gi = pl.program_id(2)
    nb = nc // grp
    in_bwd, in_fwd, first_bwd, first_fwd, last = _scan_phases(nb == 1, pl.program_id(1), gi, nb)
    fwd_row = lax.broadcasted_iota(jnp.int32, (1, LANES), 1) < ML_HEADS

    def state_update(cum, r, k_own, v_tr, m_row, big_m, d):
        edge = CHUNK - 1 if d == 0 else 0
        m_edge = big_m[edge:edge + 1, :]
        wc_row = jnp.exp(m_row - m_edge)
        wk_t = jnp.exp(r.T - big_m.T[:, edge:edge + 1])
        lhs = []
        for h in range(ML_HEADS):
            wk = wk_t[d * ML_HEADS + h:d * ML_HEADS + h + 1, :]
            lhs.append(jnp.concatenate([v_tr[h] * wk, jnp.broadcast_to(wk, (ML_ST - ML_V_DIM, CHUNK))],
                                       axis=0).astype(bf16))
        upd = [jnp.dot(lhs[h], k_own[h], preferred_element_type=f32) for h in range(ML_HEADS)]
        for h in range(ML_HEADS):
            cl = d * ML_HEADS + h
            wc = jnp.broadcast_to(wc_row[:, cl:cl + 1], (ML_ST, LANES))
            s_carry[d, h] = wc * s_carry[d, h] + upd[h]
        return cum[edge:edge + 1, :] + m_edge

    def init_state(d):
        if has_state:
            s_carry[d] = s0_ref[d]
        else:
            s_carry[d] = jnp.zeros((ML_HEADS, ML_ST, LANES), f32)

    def emit_final(d):
        for h in range(ML_HEADS):
            tile = s_carry[d, h]
            if h % 2 == 1:
                tile = pltpu.roll(tile, ML_QK_DIM, axis=1)
            cfin_ref[d, h] = tile[:ML_V_DIM, :ML_QK_DIM]
            nfin_ref[d, h:h + 1, :] = tile[ML_V_DIM:ML_V_DIM + 1, :ML_QK_DIM]

    @_when(first_bwd)
    def _():
        init_state(1)
        m_carry[...] = jnp.broadcast_to(m0_ref[...], m_carry.shape) if has_state else jnp.zeros(m_carry.shape, f32)

    def backward_chunk(sub):
        rows = slice(sub * CHUNK, (sub + 1) * CHUNK)
        lane = lax.broadcasted_iota(jnp.int32, (CHUNK, LANES), 1)
        fwd_lane = lane < ML_HEADS
        g = gates_ref[rows, :] + gb_ref[...]
        lf = pltpu.roll(jax.nn.log_sigmoid(g), LANES - ML_DIRS, axis=1)
        cum = jnp.where(fwd_lane, _cumsum_rows(lf), _cumsum_rows(lf, reverse=True))
        r = g - cum
        pm = jnp.where(fwd_lane, _cummax_rows(r), _cummax_rows(r, reverse=True))
        k = qk_ref[rows, ML_QK_WIDTH:]
        k_own = [jnp.where((lane < ML_QK_DIM) if h % 2 == 0 else (lane >= ML_QK_DIM),
                           k[:, (h // 2) * LANES:(h // 2 + 1) * LANES], 0.0).astype(bf16) for h in range(ML_HEADS)]
        v_tr = [v_ref[rows, h * ML_V_DIM:(h + 1) * ML_V_DIM].astype(f32).T for h in range(ML_HEADS)]

        j = grp * (nb - 1 - gi) + sub
        gate_store[j, 0] = cum
        gate_store[j, 1] = r
        gate_store[j, 2] = pm
        for h in range(ML_HEADS):
            k_store[j, h] = k_own[h]
            v_store[j, h] = v_tr[h].astype(bf16)
        m_row = m_carry[0:1, :]
        s_store[j] = s_carry[1].astype(bf16)
        m_store[j] = m_carry[...]
        big_m = jnp.maximum(m_row, pm)
        m_new = state_update(cum, r, k_own, v_tr, m_row, big_m, 1)
        m_carry[...] = jnp.broadcast_to(jnp.where(fwd_row, m_row, m_new), m_carry.shape)

    @_when(in_bwd)
    def _backward_states():
        for sub in reversed(range(grp)):
            backward_chunk(sub)

    @_when(first_fwd)
    def _():
        init_state(0)
        if emit_state:
            emit_final(1)

    @_when(in_fwd)
    def _forward_and_outputs():
        subs = range(grp)
        heads = range(ML_HEADS)
        chains = [(sub, h, d) for sub in subs for h in heads for d in range(2)]
        lane_of = lambda h, d: d * ML_HEADS + h
        rows = [slice(sub * CHUNK, (sub + 1) * CHUNK) for sub in subs]
        cidx = [grp * gi + sub for sub in subs]
        cum = [gate_store[c, 0] for c in cidx]
        r = [gate_store[c, 1] for c in cidx]
        pm = [gate_store[c, 2] for c in cidx]
        k_own = [[k_store[c, h] for h in heads] for c in cidx]
        v_th = [[v_store[c, h] for h in heads] for c in cidx]

        m_both = m_carry[0:1, :]
        m_row, big_m = [], []
        m_fwd = m_both
        for sub in subs:
            m_row.append(jnp.where(fwd_row, m_fwd, m_store[cidx[sub]][0:1, :]))
            big_m.append(jnp.maximum(m_row[sub], pm[sub]))
            m_fwd = cum[sub][CHUNK - 1:CHUNK, :] + big_m[sub][CHUNK - 1:CHUNK, :]
        m_fin = jnp.where(fwd_row, m_fwd, m_both)
        m_carry[...] = jnp.broadcast_to(m_fin, m_carry.shape)

        big_m_t = [t.T for t in big_m]
        w_inter_t = [jnp.exp(m_row[sub] - big_m[sub]).T for sub in subs]
        floor_t = [jnp.exp(-(cum[sub] + big_m[sub])).T for sub in subs]
        key = lax.broadcasted_iota(jnp.int32, (CHUNK, CHUNK), 0)
        qry = lax.broadcasted_iota(jnp.int32, (CHUNK, CHUNK), 1)
        masks = (key <= qry, key >= qry)
        q = [(qk_ref[rw, :ML_QK_WIDTH] * (ML_QK_DIM ** -0.5)).astype(bf16) for rw in rows]
        q_pair = {(sub, h): q[sub][:, (h // 2) * LANES:(h // 2 + 1) * LANES] for sub in subs for h in heads}
        s_raw_t = {(sub, h): _dot_nt(k_own[sub][h], q_pair[sub, h]) for sub in subs for h in heads}
        w_t = {(sub, h, d): jnp.exp(jnp.where(
            masks[d], r[sub][:, lane_of(h, d):lane_of(h, d) + 1] - big_m_t[sub][lane_of(h, d):lane_of(h, d) + 1, :],
            -jnp.inf)) for sub, h, d in chains}
        sw_t = {(sub, h, d): s_raw_t[sub, h] * w_t[sub, h, d] for sub, h, d in chains}
        num = {(sub, h, d): jnp.dot(v_th[sub][h], sw_t[sub, h, d].astype(bf16), preferred_element_type=f32)
               for sub, h, d in chains}
        den_intra = {ch: jnp.sum(sw_t[ch], axis=0, keepdims=True) for ch in chains}
        wi = {(sub, h, d): w_inter_t[sub][lane_of(h, d):lane_of(h, d) + 1, :] for sub, h, d in chains}
        gate = {(sub, h): jax.nn.sigmoid(og_ref[rows[sub], h * ML_V_DIM:(h + 1) * ML_V_DIM])
                for sub in subs for h in heads}

        for sub in subs:
            hd_pairs = [(h, d) for h in heads for d in range(2)]
            inter = {(h, d): _dot_nt(s_carry[0, h].astype(bf16) if d == 0 else s_store[cidx[sub], h],
                                     q_pair[sub, h]) for h, d in hd_pairs}
            den = {(h, d): den_intra[sub, h, d] + wi[sub, h, d] * inter[h, d][ML_V_DIM:ML_V_DIM + 1]
                   for h, d in hd_pairs}
            inv = {(h, d): 1.0 / jnp.maximum(jnp.abs(den[h, d]),
                                             floor_t[sub][lane_of(h, d):lane_of(h, d) + 1, :]) for h, d in hd_pairs}
            part = {(h, d): (num[sub, h, d] + wi[sub, h, d] * inter[h, d][:ML_V_DIM]) * inv[h, d]
                    for h, d in hd_pairs}
            out = [(part[h, 0] + part[h, 1]).T for h in heads]
            for h in heads:
                y_ref[rows[sub], h * ML_V_DIM:(h + 1) * ML_V_DIM] = (out[h] * gate[sub, h]).astype(y_ref.dtype)
            v_tr = [v.astype(f32) for v in v_th[sub]]
            state_update(cum[sub], r[sub], k_own[sub], v_tr, m_row[sub], big_m[sub], 0)

        if emit_state:
            @_when(last)
            def _():
                emit_final(0)
                mfin_ref[...] = m_fin


def mlstm_mix(qk, v, og, gates, gate_b, state, row0, n_seq, seq_len, emit_state, out_buf=None,
              layer=0, state_bufs=None):
    nc = seq_len // CHUNK
    grp = math.gcd(nc, SCAN_G_MAX)
    blk_rows = grp * CHUNK
    nb = seq_len // blk_rows
    blk0 = row0 // blk_rows
    has_state = state is not None

    def block_map(s, p, c):
        return (blk0 + s * nb + jnp.where(p == 0, nb - 1 - c, c), 0)

    m_spec = pl.BlockSpec((None, 1, LANES), lambda s, p, c: (s, 0, 0))
    in_specs = [
        pl.BlockSpec((blk_rows, 2 * ML_QK_WIDTH), block_map),
        pl.BlockSpec((blk_rows, ML_V_WIDTH), block_map),
        pl.BlockSpec((blk_rows, ML_V_WIDTH), lambda s, p, c: (blk0 + s * nb + p * c, 0)),
        pl.BlockSpec((blk_rows, LANES), block_map),
        pl.BlockSpec((1, LANES), lambda s, p, c: (0, 0)),
    ]
    gb = jnp.concatenate([gate_b.reshape(1, 2 * ML_DIRS), jnp.zeros((1, LANES - 2 * ML_DIRS), f32)], axis=1)
    args = [qk, v, og, gates, gb]
    if has_state:
        c0, n0, m0 = state
        rows = jnp.concatenate([c0, jnp.broadcast_to(n0[..., None, :], n0.shape[:-1] + (ML_ST - ML_V_DIM, ML_QK_DIM))],
                               axis=-2)
        zeros = jnp.zeros_like(rows)
        odd = (jnp.arange(ML_HEADS) % 2 == 1)[:, None, None]
        s0 = jnp.where(odd, jnp.concatenate([zeros, rows], axis=-1), jnp.concatenate([rows, zeros], axis=-1))
        m0 = jnp.concatenate([m0.reshape(n_seq, 1, ML_DIRS), jnp.zeros((n_seq, 1, LANES - ML_DIRS), f32)], axis=-1)
        in_specs += [pl.BlockSpec((None, 2, ML_HEADS, ML_ST, LANES), lambda s, p, c: (s, 0, 0, 0, 0)), m_spec]
        args += [s0, m0]
    aliases = {}
    if out_buf is not None:
        aliases[len(args)] = 0
        in_specs.append(pl.BlockSpec(memory_space=pl.ANY))
        args.append(out_buf)
    out_shape = [jax.ShapeDtypeStruct((N_TOK, ML_V_WIDTH), bf16)]
    out_specs = [pl.BlockSpec((blk_rows, ML_V_WIDTH), lambda s, p, c: (blk0 + s * nb + p * c, 0))]
    if emit_state:
        if state_bufs is not None:
            for k_out, buf in enumerate(state_bufs):
                aliases[len(args)] = 1 + k_out
                in_specs.append(pl.BlockSpec(memory_space=pl.ANY))
                args.append(buf)
        out_shape += [jax.ShapeDtypeStruct((n_seq, N_ODD, 2, ML_HEADS, ML_V_DIM, ML_QK_DIM), f32),
                      jax.ShapeDtypeStruct((n_seq, N_ODD, 2, ML_HEADS, ML_QK_DIM), f32),
                      jax.ShapeDtypeStruct((n_seq, 1, LANES), f32)]
        out_specs += [pl.BlockSpec((None, None, 2, ML_HEADS, ML_V_DIM, ML_QK_DIM),
                                   lambda s, p, c: (s, layer, 0, 0, 0, 0)),
                      pl.BlockSpec((None, None, 2, ML_HEADS, ML_QK_DIM), lambda s, p, c: (s, layer, 0, 0, 0)),
                      m_spec]
    res = pl.pallas_call(
        functools.partial(_mlstm_t_kernel, nc, grp, has_state, len(aliases), emit_state),
        input_output_aliases=aliases,
        grid=(n_seq, 1 if nb == 1 else 2, nb),
        in_specs=in_specs,
        out_specs=out_specs,
        out_shape=out_shape,
        scratch_shapes=[pltpu.VMEM((nc, ML_HEADS, ML_ST, LANES), bf16), pltpu.VMEM((nc, SUBLANES, LANES), f32),
                        pltpu.VMEM((nc, 3, CHUNK, LANES), f32), pltpu.VMEM((nc, ML_HEADS, CHUNK, LANES), bf16),
                        pltpu.VMEM((nc, ML_HEADS, ML_V_DIM, CHUNK), bf16),
                        pltpu.VMEM((2, ML_HEADS, ML_ST, LANES), f32), pltpu.VMEM((SUBLANES, LANES), f32)],
        compiler_params=pltpu.CompilerParams(
            dimension_semantics=("arbitrary", "arbitrary", "arbitrary"), vmem_limit_bytes=VMEM_LIMIT),
        name="mlstm_scan",
    )(*args)
    if not emit_state:
        return res[0]
    y, c_fin, n_fin, mfin = res
    return y, c_fin, n_fin, mfin[:, 0, :ML_DIRS].reshape(n_seq, 2, ML_HEADS)


def kernel(x_prompt, x_sample, c, cache_na_k, cache_na_v, state_ssd, state_mlstm_c, state_mlstm_n, state_mlstm_m,
           c_ctx, w_mod, b_mod, norm_mix, norm_ffn, w_in_even, w_out_even, na_rpb, ssd_conv_w, ssd_conv_b,
           ssd_dt_bias, ssd_a_log, ssd_d, ssd_norm, w_in_odd, w_out_odd, ml_conv_w, ml_conv_b, ml_gate_b,
           w_ff1, w_ff2, norm_f):
    xs = [x_prompt.reshape(N_PROMPT, D_MODEL), x_sample.reshape(DEC_BATCH * DEC_SEQ, D_MODEL)]
    cond = jnp.concatenate([c_ctx[None, :], c, jnp.zeros((SUBLANES - N_COND, D_MODEL), f32)], axis=0)
    mod = adaln_all(cond, w_mod, b_mod)[:, :N_COND].reshape(DEPTH, N_COND, 1, N_MOD * D_MODEL)

    even_main = 3 * NA_WIDTH + SSD_INNER + SSD_CONV_DIM
    odd_main = 2 * ML_QK_WIDTH + 2 * ML_V_WIDTH

    def tail_bf16(w, main):
        t = w[:, :, main:]
        return jnp.concatenate([t, jnp.zeros(t.shape[:2] + (LANES - t.shape[2],), f32)], axis=2).astype(bf16)

    wi_even, wt_even = cast_bf16(w_in_even, even_main), tail_bf16(w_in_even, even_main)
    wi_odd, wt_odd = cast_bf16(w_in_odd, odd_main), tail_bf16(w_in_odd, odd_main)
    wo_even, wo_odd = cast_bf16(w_out_even), cast_bf16(w_out_odd)
    w1_all, w2_all = cast_bf16(w_ff1), cast_bf16(w_ff2)

    out_m = []
    new_k = new_v = new_ssd = new_c = new_n = None
    for l in range(DEPTH):
        norm_last = norm_f if l == DEPTH - 1 else None
        if l % 2 == 0:
            e = l // 2
            o0 = 3 * NA_WIDTH
            segs = ((0, NA_WIDTH, NA_HEAD_DIM ** -0.5, False),
                    (NA_WIDTH, 2 * NA_WIDTH, 1.0, False),
                    (2 * NA_WIDTH, 3 * NA_WIDTH, 1.0, False),
                    (o0, o0 + SSD_INNER, 1.0, False),
                    (o0 + SSD_INNER, o0 + SSD_INNER + SSD_CONV_DIM, 1.0, True),
                    (None, None, 1.0, False))
            outs = ((0, bf16, False), (1, bf16, False), (2, bf16, False), (1, f32, True), (2, f32, True),
                    (3, f32, False), (4, f32, False), (5, f32, False))
            q, k, v, new_k, new_v, z, xbc, dt = in_proj(
                xs, mod[l], norm_mix[l], wi_even, wt_even, e, segs, outs, ssd_conv_w[e], ssd_conv_b[e],
                cache_bufs=None if new_k is None else (new_k, new_v))
            ssd_w = (ssd_dt_bias[e], ssd_a_log[e], ssd_d[e], ssd_norm[e])
            y_ssd, new_ssd = ssd_mix(xbc, dt, z, None, 0, BATCH, SEQ, *ssd_w, True, layer=e, state_buf=new_ssd)
            y_ssd = ssd_mix(xbc, dt, z, state_ssd[:, e], N_PROMPT, DEC_BATCH, DEC_SEQ, *ssd_w, False, out_buf=y_ssd)
            tokens = lambda t: jnp.swapaxes(t, 1, 2).reshape(DEC_BATCH, PAST_LEN, NA_WIDTH).astype(bf16)
            y_na = ctx_attention(q, k, v)
            y_na = na_latent(q, k, v, tokens(cache_na_k[:, e]), tokens(cache_na_v[:, e]),
                             na_bias_table(na_rpb[e]), y_na)
            res = out_mlp(xs, mod[l], norm_ffn[l], [y_na, y_ssd], wo_even, e, w1_all, w2_all, l, norm_last)
        else:
            o = l // 2
            a0 = 2 * ML_QK_WIDTH
            segs = ((0, a0, 1.0, True),
                    (a0, a0 + ML_V_WIDTH, 1.0, False),
                    (a0 + ML_V_WIDTH, a0 + 2 * ML_V_WIDTH, 1.0, False),
                    (None, None, 1.0, False))
            outs = ((0, f32, False), (1, bf16, False), (2, f32, False), (3, f32, False))
            qk, v, og, gates = in_proj(xs, mod[l], norm_mix[l], wi_odd, wt_odd, o, segs, outs,
                                       ml_conv_w[o], ml_conv_b[o], row_block=ODD_ROW_BLOCK)
            y_ml, new_c, new_n, m_fin = mlstm_mix(qk, v, og, gates, ml_gate_b[o], None, 0, BATCH, SEQ, True,
                                                  layer=o, state_bufs=None if new_c is None else (new_c, new_n))
            out_m.append(m_fin)
            state = (state_mlstm_c[:, o], state_mlstm_n[:, o], state_mlstm_m[:, o])
            y_ml = mlstm_mix(qk, v, og, gates, ml_gate_b[o], state, N_PROMPT, DEC_BATCH, DEC_SEQ, False, out_buf=y_ml)
            res = out_mlp(xs, mod[l], norm_ffn[l], [y_ml], wo_odd, o, w1_all, w2_all, l, norm_last)
        xs = list(res) if norm_last is not None else [res]

    y_prompt = xs[0].reshape(BATCH, SEQ, D_MODEL)
    y_sample = xs[1].reshape(DEC_BATCH, DEC_SEQ, D_MODEL)
    return (y_prompt, y_sample, new_k, new_v, new_ssd,
            new_c, new_n, jnp.stack(out_m, axis=1))
```

```python
import functools
import math

import jax
import jax.numpy as jnp
from jax import lax
from jax.experimental import pallas as pl
from jax.experimental.pallas import tpu as pltpu

D_MODEL = 1024
BATCH = 32
SEQ = 256
DEPTH = 4
DEC_BATCH = 2
DEC_SEQ = 4096
PAST_LEN = 256
GRID_W = 64
N_EVEN = (DEPTH + 1) // 2
N_ODD = DEPTH // 2
RMS_EPS = 1e-6
N_MOD = 6
D_FF = 4 * D_MODEL
CONV_K = 3
Q_BLOCK = 128
NA_HEADS = 8
NA_HEAD_DIM = 64
NA_WIDTH = NA_HEADS * NA_HEAD_DIM
NA_WIN_ROWS = 8
NA_WIN_COLS = 16
NA_RPB_ROWS = 2 * NA_WIN_ROWS - 1
NA_RPB_COLS = 2 * NA_WIN_COLS - 1
SSD_INNER = D_MODEL
SSD_HEAD_DIM = 64
SSD_HEADS = SSD_INNER // SSD_HEAD_DIM
SSD_GROUPS = 2
SSD_RPG = SSD_HEADS // SSD_GROUPS
SSD_STATE = 128
SSD_GN = SSD_GROUPS * SSD_STATE
SSD_CONV_DIM = SSD_INNER + 2 * SSD_GN
SSD_CHUNK = 128
ML_HEADS = 8
ML_QK_DIM = D_MODEL // 16
ML_V_DIM = D_MODEL // 8
ML_QK_WIDTH = ML_HEADS * ML_QK_DIM
ML_V_WIDTH = ML_HEADS * ML_V_DIM
ML_CHUNK = 64
EVEN_MIX = NA_WIDTH + SSD_INNER

N_PROMPT = BATCH * SEQ
N_TOK = N_PROMPT + DEC_BATCH * DEC_SEQ
N_COND = 1 + DEC_BATCH
LANES = 128
SUBLANES = 8
VMEM_LIMIT = 56 * 1024 * 1024
TM = 512
ODD_ROW_BLOCK = 128

f32 = jnp.float32
bf16 = jnp.bfloat16


def _cond_row(i, tm):
    start = i * tm
    return jnp.where(start < N_PROMPT, 0, (start - N_PROMPT) // DEC_SEQ + 1)


def _const_spec(shape):
    nd = len(shape)
    return pl.BlockSpec(shape, lambda i: (0,) * nd, pipeline_mode=pl.Buffered(1))


def _rms(x):
    return x * lax.rsqrt(jnp.mean(x * x, axis=-1, keepdims=True) + RMS_EPS)


def _modulated(x, g, mod, k):
    shift = mod[:, k * D_MODEL:(k + 1) * D_MODEL]
    scale = mod[:, (k + 1) * D_MODEL:(k + 2) * D_MODEL]
    return (_rms(x) * g) * (1.0 + scale) + shift


def _mod_kernel(c_ref, w_ref, b_ref, o_ref):
    c = c_ref[...]
    a = (c * jax.nn.sigmoid(c)).astype(bf16)
    o_ref[...] = jnp.dot(a, w_ref[...].astype(bf16), preferred_element_type=f32) + b_ref[...]


def adaln_all(cond, w_mod, b_mod):
    tn = 1536
    nj = N_MOD * D_MODEL // tn
    return pl.pallas_call(
        _mod_kernel,
        grid=(DEPTH, nj),
        in_specs=[
            pl.BlockSpec((SUBLANES, D_MODEL), lambda l, j: (0, 0)),
            pl.BlockSpec((None, D_MODEL, tn), lambda l, j: (l, 0, j)),
            pl.BlockSpec((None, 1, tn), lambda l, j: (l, 0, j)),
        ],
        out_specs=pl.BlockSpec((None, SUBLANES, tn), lambda l, j: (l, 0, j)),
        out_shape=jax.ShapeDtypeStruct((DEPTH, SUBLANES, N_MOD * D_MODEL), f32),
        compiler_params=pltpu.CompilerParams(
            dimension_semantics=("arbitrary", "arbitrary"), vmem_limit_bytes=VMEM_LIMIT),
        name="adaln",
    )(cond, w_mod, b_mod.reshape(DEPTH, 1, N_MOD * D_MODEL))


def _cast_kernel(x_ref, o_ref):
    o_ref[...] = x_ref[...].astype(o_ref.dtype)


def cast_bf16(w, cols=None):
    n_l, k, n = w.shape
    cols = n if cols is None else cols
    bk = 512
    spec = pl.BlockSpec((None, bk, cols), lambda l, i: (l, i, 0))
    return pl.pallas_call(
        _cast_kernel,
        grid=(n_l, k // bk),
        in_specs=[spec],
        out_specs=spec,
        out_shape=jax.ShapeDtypeStruct((n_l, k, cols), bf16),
        compiler_params=pltpu.CompilerParams(
            dimension_semantics=("arbitrary", "arbitrary"), vmem_limit_bytes=VMEM_LIMIT),
        name="cast_bf16",
    )(w)


def _layer_spec(w, l):
    nd = w.ndim - 1
    return pl.BlockSpec((None,) + w.shape[1:], lambda i: (l,) + (0,) * nd, pipeline_mode=pl.Buffered(1))


def _tile_x(x_refs, tm):
    if len(x_refs) == 1:
        return x_refs[0][...]
    return jnp.where(pl.program_id(0) < N_PROMPT // tm, x_refs[0][...], x_refs[1][...])


def _conv_silu_tile(y, prev_row, next_row, w, b):
    i = pl.program_id(0)
    rows = y.shape[0]
    is_prompt = i < N_PROMPT // rows
    seq_last = jnp.where(is_prompt, SEQ - 1, DEC_SEQ - 1)

    def act(up, mid, dn):
        c = up * w[0:1, :] + mid * w[1:2, :] + dn * w[2:3, :] + b
        return c * jax.nn.sigmoid(c)

    tile = act(pltpu.roll(y, 1, axis=0), y, pltpu.roll(y, rows - 1, axis=0))
    starts = set(range(0, rows, SEQ))
    fixes = []
    for t in sorted(starts | {(st - 1) % rows for st in starts}):
        g = i * rows + t
        pos = jnp.where(is_prompt, g & (SEQ - 1), (g - N_PROMPT) & (DEC_SEQ - 1))
        up = prev_row if t == 0 else y[t - 1:t]
        dn = next_row if t == rows - 1 else y[t + 1:t + 2]
        fixes.append((t, act(jnp.where(pos == 0, 0.0, up), y[t:t + 1], jnp.where(pos == seq_last, 0.0, dn))))
    return tile, fixes


def _in_proj_kernel(n_x, n_buf, row_block, segs, outs, *refs):
    x_refs = refs[:n_x]
    prev_ref, next_ref, mod_ref, g_ref, w_ref, wt_ref, cw_ref, cb_ref = refs[n_x:n_x + 8]
    o_refs = refs[n_x + 8 + n_buf:]
    mod, g = mod_ref[...], g_ref[...]
    x = _tile_x(x_refs, TM)
    n_blocks = TM // row_block
    halo = _modulated(jnp.concatenate([prev_ref[...], next_ref[...]], axis=0), g, mod, 0).astype(bf16)
    parts = [[] for _ in segs]
    for r in range(n_blocks):
        hb = _modulated(x[r * row_block:(r + 1) * row_block], g, mod, 0).astype(bf16)
        for i, (a, b, scale, conv) in enumerate(segs):
            w = wt_ref[...] if a is None else w_ref[:, a:b]
            lhs = jnp.concatenate([hb, halo], axis=0) if conv and r == n_blocks - 1 else hb
            parts[i].append(jnp.dot(lhs, w, preferred_element_type=f32))
    ys = []
    for (a, b, scale, conv), p in zip(segs, parts):
        y = jnp.concatenate(p, axis=0)
        if conv:
            ys.append(_conv_silu_tile(y[:TM], y[TM + SUBLANES - 1:TM + SUBLANES],
                                      y[TM + SUBLANES:TM + SUBLANES + 1], cw_ref[...], cb_ref[...]))
        else:
            ys.append(y if scale == 1.0 else y * scale)
    for (si, _, head_major), o_ref in zip(outs, o_refs):
        if head_major:
            @pl.when(pl.program_id(0) < N_PROMPT // TM)
            def _(o_ref=o_ref, si=si):
                for b in range(TM // SEQ):
                    for hd in range(NA_HEADS):
                        o_ref[b, hd] = ys[si][b * SEQ:(b + 1) * SEQ, hd * NA_HEAD_DIM:(hd + 1) * NA_HEAD_DIM]
        elif segs[si][3]:
            tile, fixes = ys[si]
            o_ref[...] = tile.astype(o_ref.dtype)
            for t, row in fixes:
                o_ref[t:t + 1, :] = row.astype(o_ref.dtype)
        else:
            o_ref[...] = ys[si].astype(o_ref.dtype)


def _x_specs(xs, tm):
    if len(xs) == 1:
        return [pl.BlockSpec((tm, D_MODEL), lambda i: (i, 0))]
    n_p = N_PROMPT // tm
    return [pl.BlockSpec((tm, D_MODEL), lambda i: (jnp.minimum(i, n_p - 1), 0)),
            pl.BlockSpec((tm, D_MODEL), lambda i: (jnp.maximum(i - n_p, 0), 0))]


def in_proj(xs, mod_l, g, w, w_tail, l, segs, outs, conv_w, conv_b, cache_bufs=None, row_block=TM):
    grid = (N_TOK // TM,)
    last_prompt = N_PROMPT // TM - 1
    halo_src = xs[-1]
    per = TM // SUBLANES
    blk0 = (halo_src.shape[0] - DEC_BATCH * DEC_SEQ) // SUBLANES
    n_blk = halo_src.shape[0] // SUBLANES
    tile0 = N_PROMPT // TM

    def prev_map(i):
        return (jnp.clip(blk0 + (i - tile0) * per - 1, 0, n_blk - 1), 0)

    def next_map(i):
        return (jnp.clip(blk0 + (i - tile0 + 1) * per, 0, n_blk - 1), 0)

    ch = conv_w.shape[1]
    out_shape, out_specs = [], []
    for si, dt, head_major in outs:
        width = LANES if segs[si][0] is None else segs[si][1] - segs[si][0]
        if head_major:
            out_shape.append(jax.ShapeDtypeStruct((BATCH, N_EVEN, NA_HEADS, SEQ, NA_HEAD_DIM), dt))
            out_specs.append(pl.BlockSpec((TM // SEQ, None, NA_HEADS, SEQ, NA_HEAD_DIM),
                                          lambda i: (jnp.minimum(i, last_prompt), l, 0, 0, 0)))
        else:
            out_shape.append(jax.ShapeDtypeStruct((N_TOK, width), dt))
            out_specs.append(pl.BlockSpec((TM, width), lambda i: (i, 0)))
    args = [*xs, halo_src, halo_src, mod_l, g.reshape(1, D_MODEL), w, w_tail, conv_w, conv_b.reshape(1, ch)]
    aliases = {}
    if cache_bufs is not None:
        head_major_outs = [k for k, o in enumerate(outs) if o[2]]
        for k_out, buf in zip(head_major_outs, cache_bufs):
            aliases[len(args)] = k_out
            args.append(buf)
    return pl.pallas_call(
        functools.partial(_in_proj_kernel, len(xs), len(aliases), row_block, segs, outs),
        input_output_aliases=aliases,
        grid=grid,
        in_specs=_x_specs(xs, TM) + [
            pl.BlockSpec((SUBLANES, D_MODEL), prev_map),
            pl.BlockSpec((SUBLANES, D_MODEL), next_map),
            pl.BlockSpec((None, 1, N_MOD * D_MODEL), lambda i: (_cond_row(i, TM), 0, 0)),
            _const_spec((1, D_MODEL)),
            _layer_spec(w, l),
            _layer_spec(w_tail, l),
            _const_spec((CONV_K, ch)),
            _const_spec((1, ch)),
        ] + [pl.BlockSpec(memory_space=pl.ANY)] * len(aliases),
        out_specs=out_specs,
        out_shape=out_shape,
        compiler_params=pltpu.CompilerParams(
            dimension_semantics=("arbitrary",), vmem_limit_bytes=VMEM_LIMIT),
        name="in_proj",
    )(*args)


def _out_mlp_kernel(n_x, n_mix, final, *refs):
    x_refs = refs[:n_x]
    mod_ref, g_ref = refs[n_x:n_x + 2]
    refs = refs[n_x + 2:]
    mix_refs = refs[:n_mix]
    wo_ref, w1_ref, w2_ref = refs[n_mix:n_mix + 3]
    rest = refs[n_mix + 3:]
    if final:
        gf_ref, op_ref, os_ref = rest
    else:
        (o_ref,) = rest
    mod = mod_ref[...]
    m = None
    k0 = 0
    for r in mix_refs:
        kw = r.shape[-1]
        part = jnp.dot(r[...].astype(bf16), wo_ref[k0:k0 + kw, :], preferred_element_type=f32)
        m = part if m is None else m + part
        k0 += kw
    x1 = _tile_x(x_refs, OUT_TM) + mod[:, 2 * D_MODEL:3 * D_MODEL] * m
    h2 = _modulated(x1, g_ref[...], mod, 3).astype(bf16)
    u = jnp.dot(h2, w1_ref[...], preferred_element_type=f32)
    a = jnp.square(jnp.maximum(u, 0.0)).astype(bf16)
    x2 = x1 + mod[:, 5 * D_MODEL:6 * D_MODEL] * jnp.dot(a, w2_ref[...], preferred_element_type=f32)
    if not final:
        o_ref[...] = x2
        return
    y = _rms(x2) * gf_ref[...]
    is_prompt = pl.program_id(0) < N_PROMPT // OUT_TM

    @pl.when(is_prompt)
    def _():
        op_ref[...] = y

    @pl.when(jnp.logical_not(is_prompt))
    def _():
        os_ref[...] = y


OUT_TM = 512


def out_mlp(xs, mod_l, g_ffn, mixes, w_out, l_out, w1, w2, l, norm_f=None):
    final = norm_f is not None
    tm = OUT_TM
    grid = (N_TOK // tm,)
    in_specs = _x_specs(xs, tm) + [
        pl.BlockSpec((None, 1, N_MOD * D_MODEL), lambda i: (_cond_row(i, tm), 0, 0)),
        _const_spec((1, D_MODEL)),
    ]
    in_specs += [pl.BlockSpec((tm, m.shape[-1]), lambda i: (i, 0)) for m in mixes]
    in_specs += [_layer_spec(w_out, l_out), _layer_spec(w1, l), _layer_spec(w2, l)]
    args = [*xs, mod_l, g_ffn.reshape(1, D_MODEL), *mixes, w_out, w1, w2]
    out_specs = pl.BlockSpec((tm, D_MODEL), lambda i: (i, 0))
    out_shape = jax.ShapeDtypeStruct((N_TOK, D_MODEL), f32)
    if final:
        in_specs.append(_const_spec((1, D_MODEL)))
        args.append(norm_f.reshape(1, D_MODEL))
        n_p = N_PROMPT // tm
        out_specs = [pl.BlockSpec((tm, D_MODEL), lambda i: (jnp.minimum(i, n_p - 1), 0)),
                     pl.BlockSpec((tm, D_MODEL), lambda i: (jnp.maximum(i - n_p, 0), 0))]
        out_shape = [jax.ShapeDtypeStruct((N_PROMPT, D_MODEL), f32),
                     jax.ShapeDtypeStruct((N_TOK - N_PROMPT, D_MODEL), f32)]
    return pl.pallas_call(
        functools.partial(_out_mlp_kernel, len(xs), len(mixes), final),
        grid=grid,
        in_specs=in_specs,
        out_specs=out_specs,
        out_shape=out_shape,
        compiler_params=pltpu.CompilerParams(
            dimension_semantics=("arbitrary",), vmem_limit_bytes=VMEM_LIMIT),
        name="out_mlp",
    )(*args)


NA_PAIRS = NA_HEADS // 2
NA_ROWS = DEC_SEQ // GRID_W
NA_WIN = NA_WIN_ROWS * GRID_W


def _dot_nt(a, b):
    return lax.dot_general(a, b, (((1,), (1,)), ((), ())), preferred_element_type=f32)


def _pair_stack(x):
    lane = lax.broadcasted_iota(jnp.int32, x.shape, 1)
    zero = jnp.zeros_like(x)
    return jnp.concatenate([jnp.where(lane < NA_HEAD_DIM, x, zero), jnp.where(lane >= NA_HEAD_DIM, x, zero)], axis=0)


def _pair_unstack(o):
    n = o.shape[0] // 2
    lane = lax.broadcasted_iota(jnp.int32, (n, LANES), 1)
    return jnp.where(lane < NA_HEAD_DIM, o[:n], o[n:])


def _softmax_pv(scores, values):
    def lane_tiles(blocks):
        return [b[:, k * LANES:(k + 1) * LANES] for b in blocks for k in range(b.shape[1] // LANES)]

    ms = [jnp.max(functools.reduce(jnp.maximum, lane_tiles(s)), axis=1, keepdims=True) for s in scores]
    ps = [[jnp.exp(b - m) for b in s] for s, m in zip(scores, ms)]
    invs = [1.0 / jnp.sum(functools.reduce(jnp.add, lane_tiles(p)), axis=1, keepdims=True) for p in ps]
    pn = [[(b * inv).astype(bf16) for b in p] for p, inv in zip(ps, invs)]
    return [functools.reduce(jnp.add, [jnp.dot(b, v, preferred_element_type=f32) for b, v in zip(p, vs)])
            for p, vs in zip(pn, values)]


CTX_STEP_SEQS = 4


def _ctx_attn_kernel(q_ref, k_ref, v_ref, o_ref):
    where = [(slice(b * SEQ, (b + 1) * SEQ), slice(p * LANES, (p + 1) * LANES))
             for b in range(CTX_STEP_SEQS) for p in range(NA_PAIRS)]
    scores = [[_dot_nt(_pair_stack(q_ref[rows, lanes]), k_ref[rows, lanes])] for rows, lanes in where]
    outs = _softmax_pv(scores, [[v_ref[rows, lanes]] for rows, lanes in where])
    for (rows, lanes), o in zip(where, outs):
        o_ref[rows, lanes] = _pair_unstack(o).astype(o_ref.dtype)


def ctx_attention(q, k, v):
    spec = pl.BlockSpec((CTX_STEP_SEQS * SEQ, NA_WIDTH), lambda b: (b, 0))
    return pl.pallas_call(
        _ctx_attn_kernel,
        grid=(BATCH // CTX_STEP_SEQS,),
        in_specs=[spec, spec, spec],
        out_specs=spec,
        out_shape=jax.ShapeDtypeStruct((N_TOK, NA_WIDTH), bf16),
        compiler_params=pltpu.CompilerParams(dimension_semantics=("arbitrary",), vmem_limit_bytes=VMEM_LIMIT),
        name="ctx_attn",
    )(q, k, v)


def _na_bias_kernel(rpb_ref, o_ref):
    pair = pl.program_id(0)
    shape = (GRID_W, LANES)
    qc = lax.broadcasted_iota(jnp.int32, shape, 0)
    lane = lax.broadcasted_iota(jnp.int32, shape, 1)
    kc = lane & (GRID_W - 1)
    low = lane < GRID_W
    col_start = jnp.clip(qc - NA_WIN_COLS // 2, 0, GRID_W - NA_WIN_COLS)
    valid = (kc >= col_start) & (kc < col_start + NA_WIN_COLS)
    rel_c = jnp.clip(kc - qc + NA_WIN_COLS - 1, 0, NA_RPB_COLS - 1)
    for e in range(2):
        base = (2 * pair + e) * NA_RPB_ROWS
        pieces = []
        for rr in range(NA_RPB_ROWS - 1):
            val = jnp.zeros(shape, f32)
            for t in range(NA_RPB_COLS):
                s_lo = rpb_ref[(base + rr) * NA_RPB_COLS + t]
                s_hi = rpb_ref[(base + rr + 1) * NA_RPB_COLS + t]
                val = jnp.where(rel_c == t, jnp.where(low, s_lo, s_hi), val)
            pieces.append(jnp.where(valid, val, -jnp.inf))
        for d in range(NA_WIN_ROWS):
            for i in range(0, NA_WIN_ROWS, 2):
                o_ref[d, e * GRID_W:(e + 1) * GRID_W, i * GRID_W:(i + 2) * GRID_W] = pieces[d + i]


def na_bias_table(rpb):
    return pl.pallas_call(
        _na_bias_kernel,
        grid=(NA_PAIRS,),
        in_specs=[pl.BlockSpec(memory_space=pltpu.SMEM)],
        out_specs=pl.BlockSpec((None, NA_WIN_ROWS, 2 * GRID_W, NA_WIN), lambda p: (p, 0, 0, 0)),
        out_shape=jax.ShapeDtypeStruct((NA_PAIRS, NA_WIN_ROWS, 2 * GRID_W, NA_WIN), f32),
        compiler_params=pltpu.CompilerParams(dimension_semantics=("arbitrary",), vmem_limit_bytes=VMEM_LIMIT),
        name="na_bias",
    )(rpb.reshape(-1))


NA_STEP_ROWS = 8


def _na_first_key_row(r):
    return jnp.clip(r - NA_WIN_ROWS // 2, 0, NA_ROWS - NA_WIN_ROWS)


def _na_kernel(q_ref, k_ref, v_ref, kc_ref, vc_ref, bias_ref, buf_ref, o_ref):
    del buf_ref
    where, scores, values = [], [], []
    for j in range(NA_STEP_ROWS):
        r = pl.program_id(1) * NA_STEP_ROWS + j
        first = _na_first_key_row(r)
        start = pl.multiple_of(first * GRID_W, GRID_W)
        shift = first - r + NA_WIN_ROWS - 1
        rows = slice(j * GRID_W, (j + 1) * GRID_W)
        for p in range(NA_PAIRS):
            lanes = slice(p * LANES, (p + 1) * LANES)
            qq = _pair_stack(q_ref[rows, lanes])
            where.append((rows, lanes))
            scores.append([_dot_nt(qq, k_ref[pl.ds(start, NA_WIN), lanes]) + bias_ref[p, shift],
                           _dot_nt(qq, kc_ref[:, lanes])])
            values.append([v_ref[pl.ds(start, NA_WIN), lanes], vc_ref[:, lanes]])
    for (rows, lanes), o in zip(where, _softmax_pv(scores, values)):
        o_ref[rows, lanes] = _pair_unstack(o).astype(o_ref.dtype)


def na_latent(q, k, v, k_ctx, v_ctx, bias, buf):
    rows = NA_STEP_ROWS * GRID_W
    steps = NA_ROWS // NA_STEP_ROWS
    row0 = N_PROMPT // rows
    seq0 = N_PROMPT // DEC_SEQ
    kv_spec = pl.BlockSpec((DEC_SEQ, NA_WIDTH), lambda b, r: (seq0 + b, 0))
    ctx_spec = pl.BlockSpec((None, PAST_LEN, NA_WIDTH), lambda b, r: (b, 0, 0))
    return pl.pallas_call(
        _na_kernel,
        grid=(DEC_BATCH, steps),
        in_specs=[
            pl.BlockSpec((rows, NA_WIDTH), lambda b, r: (row0 + b * steps + r, 0)),
            kv_spec, kv_spec, ctx_spec, ctx_spec,
            pl.BlockSpec(bias.shape, lambda b, r: (0, 0, 0, 0), pipeline_mode=pl.Buffered(1)),
            pl.BlockSpec(memory_space=pl.ANY),
        ],
        out_specs=pl.BlockSpec((rows, NA_WIDTH), lambda b, r: (row0 + b * steps + r, 0)),
        out_shape=jax.ShapeDtypeStruct((N_TOK, NA_WIDTH), bf16),
        input_output_aliases={6: 0},
        compiler_params=pltpu.CompilerParams(
            dimension_semantics=("arbitrary", "arbitrary"), vmem_limit_bytes=VMEM_LIMIT),
        name="na_latent",
    )(q, k, v, k_ctx, v_ctx, bias, buf)


CHUNK = 128
SCAN_G_MAX = 4


def _when(cond):
    return (lambda f: f()) if cond is True else pl.when(cond)


def _scan_phases(single, phase, gi, nb):
    if single:
        return True, True, True, True, True
    return (phase == 0, phase == 1, jnp.logical_and(phase == 0, gi == 0), jnp.logical_and(phase == 1, gi == 0),
            gi == nb - 1)


def _cumsum_rows(a, reverse=False):
    row = lax.broadcasted_iota(jnp.int32, a.shape, 0)
    s = 1
    while s < CHUNK:
        if reverse:
            a = a + jnp.where(row < CHUNK - s, pltpu.roll(a, CHUNK - s, axis=0), 0.0)
        else:
            a = a + jnp.where(row >= s, pltpu.roll(a, s, axis=0), 0.0)
        s *= 2
    return a


def _ssd_kernel(nc, grp, has_h0, n_buf, emit_state, *refs):
    xbc_ref, dt_ref, z_ref = refs[:3]
    refs = refs[3:]
    if has_h0:
        h0_ref, refs = refs[0], refs[1:]
    dtb_ref, alog_ref, dskip_ref, g_ref = refs[:4]
    refs = refs[4 + n_buf:]
    y_ref, refs = refs[0], refs[1:]
    if emit_state:
        hfin_ref, refs = refs[0], refs[1:]
    hb_store, xt_store, gate_store, carry = refs

    gi = pl.program_id(2)
    nb = nc // grp
    in_bwd, in_fwd, first_bwd, first_fwd, last = _scan_phases(nb == 1, pl.program_id(1), gi, nb)
    gw = SSD_RPG * SSD_HEAD_DIM

    def load_h0(d):
        if has_h0:
            return h0_ref[d].reshape(SSD_INNER, SSD_STATE)
        return jnp.zeros((SSD_INNER, SSD_STATE), f32)

    def head_rows(v):
        return jnp.concatenate(
            [jnp.broadcast_to(v[h:h + 1, :], (SSD_HEAD_DIM, v.shape[1])) for h in range(SSD_HEADS)], axis=0)

    def state_update(x_t, b_bf, cum_t, dt_t, edge):
        at_edge = jnp.broadcast_to(cum_t[:, edge:edge + 1], cum_t.shape)
        w_end = jnp.exp(at_edge - cum_t) * dt_t
        xw = (x_t * head_rows(w_end)).astype(bf16)
        upd = jnp.concatenate(
            [jnp.dot(xw[g * gw:(g + 1) * gw], b_bf[:, g * SSD_STATE:(g + 1) * SSD_STATE], preferred_element_type=f32)
             for g in range(SSD_GROUPS)], axis=0)
        carry[...] = carry[...] * head_rows(jnp.exp(at_edge)) + upd

    @_when(first_bwd)
    def _():
        carry[...] = load_h0(1)

    def backward_chunk(sub):
        rows = slice(sub * CHUNK, (sub + 1) * CHUNK)
        j = grp * (nb - 1 - gi) + sub
        x = xbc_ref[rows, :SSD_INNER]
        b_bf = xbc_ref[rows, SSD_INNER:SSD_INNER + SSD_GN].astype(bf16)
        dt = jax.nn.softplus(dt_ref[rows, :] + dtb_ref[...])
        a = dt * (-jnp.exp(alog_ref[...]))
        x_t = jnp.concatenate([x[:, k * LANES:(k + 1) * LANES].T for k in range(SSD_INNER // LANES)], axis=0)
        rcum = _cumsum_rows(a, reverse=True)
        xt_store[j] = x_t
        gate_store[j, 0] = dt
        gate_store[j, 1] = _cumsum_rows(a)
        gate_store[j, 2] = rcum
        hb_store[j] = carry[...].astype(bf16)
        state_update(x_t, b_bf, rcum.T[SSD_HEADS:2 * SSD_HEADS], dt.T[SSD_HEADS:2 * SSD_HEADS], 0)

    @_when(in_bwd)
    def _backward_states():
        for sub in reversed(range(grp)):
            backward_chunk(sub)

        if emit_state:
            @_when(last)
            def _():
                hfin_ref[1] = carry[...].reshape(SSD_HEADS, SSD_HEAD_DIM, SSD_STATE)

    @_when(first_fwd)
    def _():
        carry[...] = load_h0(0)

    @_when(in_fwd)
    def _forward_and_outputs():
        subs = range(grp)
        heads = range(SSD_HEADS)
        sub_heads = [(sub, h) for sub in subs for h in heads]
        rows = [slice(sub * CHUNK, (sub + 1) * CHUNK) for sub in subs]
        cidx = [grp * gi + sub for sub in subs]
        row = lax.broadcasted_iota(jnp.int32, (CHUNK, CHUNK), 0)
        col = lax.broadcasted_iota(jnp.int32, (CHUNK, CHUNK), 1)
        causal = col <= row
        anti = col >= row
        lane = lax.broadcasted_iota(jnp.int32, (CHUNK, LANES), 1)
        x = [xbc_ref[rw, :SSD_INNER] for rw in rows]
        x_bf = [xs.astype(bf16) for xs in x]
        b_bf = [xbc_ref[rw, SSD_INNER:SSD_INNER + SSD_GN].astype(bf16) for rw in rows]
        c_mat = [xbc_ref[rw, SSD_INNER + SSD_GN:].astype(bf16) for rw in rows]
        dt = [gate_store[c, 0] for c in cidx]
        cum = [gate_store[c, 1] for c in cidx]
        rcum = [gate_store[c, 2] for c in cidx]
        cum_t = [t.T for t in cum]
        rcum_t = [t.T for t in rcum]
        dt_t = [t.T for t in dt]
        cb = {(sub, g): _dot_nt(c_mat[sub][:, g * SSD_STATE:(g + 1) * SSD_STATE],
                                b_bf[sub][:, g * SSD_STATE:(g + 1) * SSD_STATE])
              for sub in subs for g in range(SSD_GROUPS)}
        log_dt = [jnp.log(t) for t in dt_t]
        key_f = [cum_t[sub] - log_dt[sub] for sub in subs]
        key_b = [rcum_t[sub] - log_dt[sub] for sub in subs]
        seg_f = {(sub, h): jnp.where(causal, cum[sub][:, h:h + 1] - key_f[sub][h:h + 1, :], -jnp.inf)
                 for sub, h in sub_heads}
        seg_b = {(sub, h): jnp.where(anti, rcum[sub][:, SSD_HEADS + h:SSD_HEADS + h + 1]
                                     - key_b[sub][SSD_HEADS + h:SSD_HEADS + h + 1, :], -jnp.inf)
                 for sub, h in sub_heads}
        e_f = {sh: jnp.exp(seg_f[sh]) for sh in sub_heads}
        e_b = {sh: jnp.exp(seg_b[sh]) for sh in sub_heads}
        ws = {(sub, h): (cb[sub, h // SSD_RPG] * (e_f[sub, h] + e_b[sub, h])).astype(bf16) for sub, h in sub_heads}
        rhs = {}
        for sub in subs:
            for p in range(SSD_HEADS // 2):
                xp = x_bf[sub][:, p * LANES:(p + 1) * LANES]
                zero = jnp.zeros_like(xp)
                rhs[sub, p] = jnp.concatenate([jnp.where(lane < SSD_HEAD_DIM, xp, zero),
                                               jnp.where(lane >= SSD_HEAD_DIM, xp, zero)], axis=0)
        y_intra = [jnp.concatenate(
            [jnp.dot(jnp.concatenate([ws[sub, 2 * p], ws[sub, 2 * p + 1]], axis=1), rhs[sub, p],
                     preferred_element_type=f32) for p in range(SSD_HEADS // 2)], axis=1) for sub in subs]

        for sub in subs:
            hf = carry[...].astype(bf16)
            hb = hb_store[cidx[sub]]

            def inter_t(h_all):
                return jnp.concatenate(
                    [_dot_nt(h_all[g * gw:(g + 1) * gw], c_mat[sub][:, g * SSD_STATE:(g + 1) * SSD_STATE])
                     for g in range(SSD_GROUPS)], axis=0)

            cum_f, cum_b = cum_t[sub][:SSD_HEADS], rcum_t[sub][SSD_HEADS:2 * SSD_HEADS]
            y_t = inter_t(hf) * head_rows(jnp.exp(cum_f)) + inter_t(hb) * head_rows(jnp.exp(cum_b))
            y_inter = jnp.concatenate([y_t[k * LANES:(k + 1) * LANES].T for k in range(SSD_INNER // LANES)], axis=1)
            state_update(xt_store[cidx[sub]], b_bf[sub], cum_f, dt_t[sub][:SSD_HEADS], CHUNK - 1)
            y = y_intra[sub] + y_inter + dskip_ref[...] * x[sub]

            zv = z_ref[rows[sub], :]
            yz = y * (zv * jax.nn.sigmoid(zv))
            y_ref[rows[sub], :] = (_rms(yz) * g_ref[...]).astype(y_ref.dtype)

        if emit_state:
            @_when(last)
            def _():
                hfin_ref[0] = carry[...].reshape(SSD_HEADS, SSD_HEAD_DIM, SSD_STATE)


def ssd_mix(xbc, dt, z, h0, row0, n_seq, seq_len, dt_bias, a_log, d_skip, norm_g, emit_state, out_buf=None,
            layer=0, state_buf=None):
    nc = seq_len // CHUNK
    grp = math.gcd(nc, SCAN_G_MAX)
    blk_rows = grp * CHUNK
    nb = seq_len // blk_rows
    blk0 = row0 // blk_rows
    has_h0 = h0 is not None

    def block_map(s, p, c):
        return (blk0 + s * nb + jnp.where(p == 0, nb - 1 - c, c), 0)

    state_spec = pl.BlockSpec((None, 2, SSD_HEADS, SSD_HEAD_DIM, SSD_STATE), lambda s, p, c: (s, 0, 0, 0, 0))
    vec = lambda n: pl.BlockSpec((1, n), lambda s, p, c: (0, 0))
    in_specs = [
        pl.BlockSpec((blk_rows, SSD_CONV_DIM), block_map),
        pl.BlockSpec((blk_rows, LANES), block_map),
        pl.BlockSpec((blk_rows, SSD_INNER), lambda s, p, c: (blk0 + s * nb + p * c, 0)),
    ]
    args = [xbc, dt, z]
    if has_h0:
        in_specs.append(state_spec)
        args.append(h0)
    in_specs += [vec(LANES), vec(LANES), vec(SSD_INNER), vec(SSD_INNER)]
    pad = lambda t: jnp.concatenate([t.reshape(1, -1), jnp.zeros((1, LANES - t.size), f32)], axis=1)
    args += [pad(dt_bias), pad(a_log), jnp.repeat(d_skip, SSD_HEAD_DIM).reshape(1, SSD_INNER),
             norm_g.reshape(1, SSD_INNER)]
    aliases = {}
    if out_buf is not None:
        aliases = {len(args): 0}
        in_specs.append(pl.BlockSpec(memory_space=pl.ANY))
        args.append(out_buf)
    out_shape = [jax.ShapeDtypeStruct((N_TOK, SSD_INNER), bf16)]
    out_specs = [pl.BlockSpec((blk_rows, SSD_INNER), lambda s, p, c: (blk0 + s * nb + p * c, 0))]
    if emit_state:
        if state_buf is not None:
            aliases[len(args)] = 1
            in_specs.append(pl.BlockSpec(memory_space=pl.ANY))
            args.append(state_buf)
        out_shape.append(jax.ShapeDtypeStruct((n_seq, N_EVEN, 2, SSD_HEADS, SSD_HEAD_DIM, SSD_STATE), f32))
        out_specs.append(pl.BlockSpec((None, None, 2, SSD_HEADS, SSD_HEAD_DIM, SSD_STATE),
                                      lambda s, p, c: (s, layer, 0, 0, 0, 0)))
    res = pl.pallas_call(
        functools.partial(_ssd_kernel, nc, grp, has_h0, len(aliases), emit_state),
        input_output_aliases=aliases,
        grid=(n_seq, 1 if nb == 1 else 2, nb),
        in_specs=in_specs,
        out_specs=out_specs,
        out_shape=out_shape,
        scratch_shapes=[pltpu.VMEM((nc, SSD_INNER, SSD_STATE), bf16), pltpu.VMEM((nc, SSD_INNER, CHUNK), f32),
                        pltpu.VMEM((nc, 3, CHUNK, LANES), f32), pltpu.VMEM((SSD_INNER, SSD_STATE), f32)],
        compiler_params=pltpu.CompilerParams(
            dimension_semantics=("arbitrary", "arbitrary", "arbitrary"), vmem_limit_bytes=VMEM_LIMIT),
        name="ssd_scan",
    )(*args)
    return res if emit_state else res[0]


ML_DIRS = 2 * ML_HEADS


def _cummax_rows(a, reverse=False):
    row = lax.broadcasted_iota(jnp.int32, a.shape, 0)
    s = 1
    while s < CHUNK:
        if reverse:
            a = jnp.maximum(a, jnp.where(row < CHUNK - s, pltpu.roll(a, CHUNK - s, axis=0), -jnp.inf))
        else:
            a = jnp.maximum(a, jnp.where(row >= s, pltpu.roll(a, s, axis=0), -jnp.inf))
        s *= 2
    return a


ML_ST = ML_V_DIM + 16


def _mlstm_t_kernel(nc, grp, has_state, n_buf, emit_state, *refs):
    qk_ref, v_ref, og_ref, gates_ref, gb_ref = refs[:5]
    refs = refs[5:]
    if has_state:
        s0_ref, m0_ref = refs[:2]
        refs = refs[2:]
    refs = refs[n_buf:]
    y_ref, refs = refs[0], refs[1:]
    if emit_state:
        cfin_ref, nfin_ref, mfin_ref = refs[:3]
        refs = refs[3:]
    s_store, m_store, gate_store, k_store, v_store, s_carry, m_carry = refs

    gi = pl.program_id(2)
    nb = nc // grp
    in_bwd, in_fwd, first_bwd, first_fwd, last = _scan_phases(nb == 1, pl.program_id(1), gi, nb)
    fwd_row = lax.broadcasted_iota(jnp.int32, (1, LANES), 1) < ML_HEADS

    def state_update(cum, r, k_own, v_tr, m_row, big_m, d):
        edge = CHUNK - 1 if d == 0 else 0
        m_edge = big_m[edge:edge + 1, :]
        wc_row = jnp.exp(m_row - m_edge)
        wk_t = jnp.exp(r.T - big_m.T[:, edge:edge + 1])
        lhs = []
        for h in range(ML_HEADS):
            wk = wk_t[d * ML_HEADS + h:d * ML_HEADS + h + 1, :]
            lhs.append(jnp.concatenate([v_tr[h] * wk, jnp.broadcast_to(wk, (ML_ST - ML_V_DIM, CHUNK))],
                                       axis=0).astype(bf16))
        upd = [jnp.dot(lhs[h], k_own[h], preferred_element_type=f32) for h in range(ML_HEADS)]
        for h in range(ML_HEADS):
            cl = d * ML_HEADS + h
            wc = jnp.broadcast_to(wc_row[:, cl:cl + 1], (ML_ST, LANES))
            s_carry[d, h] = wc * s_carry[d, h] + upd[h]
        return cum[edge:edge + 1, :] + m_edge

    def init_state(d):
        if has_state:
            s_carry[d] = s0_ref[d]
        else:
            s_carry[d] = jnp.zeros((ML_HEADS, ML_ST, LANES), f32)

    def emit_final(d):
        for h in range(ML_HEADS):
            tile = s_carry[d, h]
            if h % 2 == 1:
                tile = pltpu.roll(tile, ML_QK_DIM, axis=1)
            cfin_ref[d, h] = tile[:ML_V_DIM, :ML_QK_DIM]
            nfin_ref[d, h:h + 1, :] = tile[ML_V_DIM:ML_V_DIM + 1, :ML_QK_DIM]

    @_when(first_bwd)
    def _():
        init_state(1)
        m_carry[...] = jnp.broadcast_to(m0_ref[...], m_carry.shape) if has_state else jnp.zeros(m_carry.shape, f32)

    def backward_chunk(sub):
        rows = slice(sub * CHUNK, (sub + 1) * CHUNK)
        lane = lax.broadcasted_iota(jnp.int32, (CHUNK, LANES), 1)
        fwd_lane = lane < ML_HEADS
        g = gates_ref[rows, :] + gb_ref[...]
        lf = pltpu.roll(jax.nn.log_sigmoid(g), LANES - ML_DIRS, axis=1)
        cum = jnp.where(fwd_lane, _cumsum_rows(lf), _cumsum_rows(lf, reverse=True))
        r = g - cum
        pm = jnp.where(fwd_lane, _cummax_rows(r), _cummax_rows(r, reverse=True))
        k = qk_ref[rows, ML_QK_WIDTH:]
        k_own = [jnp.where((lane < ML_QK_DIM) if h % 2 == 0 else (lane >= ML_QK_DIM),
                           k[:, (h // 2) * LANES:(h // 2 + 1) * LANES], 0.0).astype(bf16) for h in range(ML_HEADS)]
        v_tr = [v_ref[rows, h * ML_V_DIM:(h + 1) * ML_V_DIM].astype(f32).T for h in range(ML_HEADS)]

        j = grp * (nb - 1 - gi) + sub
        gate_store[j, 0] = cum
        gate_store[j, 1] = r
        gate_store[j, 2] = pm
        for h in range(ML_HEADS):
            k_store[j, h] = k_own[h]
            v_store[j, h] = v_tr[h].astype(bf16)
        m_row = m_carry[0:1, :]
        s_store[j] = s_carry[1].astype(bf16)
        m_store[j] = m_carry[...]
        big_m = jnp.maximum(m_row, pm)
        m_new = state_update(cum, r, k_own, v_tr, m_row, big_m, 1)
        m_carry[...] = jnp.broadcast_to(jnp.where(fwd_row, m_row, m_new), m_carry.shape)

    @_when(in_bwd)
    def _backward_states():
        for sub in reversed(range(grp)):
            backward_chunk(sub)

    @_when(first_fwd)
    def _():
        init_state(0)
        if emit_state:
            emit_final(1)

    @_when(in_fwd)
    def _forward_and_outputs():
        subs = range(grp)
        heads = range(ML_HEADS)
        chains = [(sub, h, d) for sub in subs for h in heads for d in range(2)]
        lane_of = lambda h, d: d * ML_HEADS + h
        rows = [slice(sub * CHUNK, (sub + 1) * CHUNK) for sub in subs]
        cidx = [grp * gi + sub for sub in subs]
        cum = [gate_store[c, 0] for c in cidx]
        r = [gate_store[c, 1] for c in cidx]
        pm = [gate_store[c, 2] for c in cidx]
        k_own = [[k_store[c, h] for h in heads] for c in cidx]
        v_th = [[v_store[c, h] for h in heads] for c in cidx]

        m_both = m_carry[0:1, :]
        m_row, big_m = [], []
        m_fwd = m_both
        for sub in subs:
            m_row.append(jnp.where(fwd_row, m_fwd, m_store[cidx[sub]][0:1, :]))
            big_m.append(jnp.maximum(m_row[sub], pm[sub]))
            m_fwd = cum[sub][CHUNK - 1:CHUNK, :] + big_m[sub][CHUNK - 1:CHUNK, :]
        m_fin = jnp.where(fwd_row, m_fwd, m_both)
        m_carry[...] = jnp.broadcast_to(m_fin, m_carry.shape)

        big_m_t = [t.T for t in big_m]
        w_inter_t = [jnp.exp(m_row[sub] - big_m[sub]).T for sub in subs]
        floor_t = [jnp.exp(-(cum[sub] + big_m[sub])).T for sub in subs]
        key = lax.broadcasted_iota(jnp.int32, (CHUNK, CHUNK), 0)
        qry = lax.broadcasted_iota(jnp.int32, (CHUNK, CHUNK), 1)
        masks = (key <= qry, key >= qry)
        q = [(qk_ref[rw, :ML_QK_WIDTH] * (ML_QK_DIM ** -0.5)).astype(bf16) for rw in rows]
        q_pair = {(sub, h): q[sub][:, (h // 2) * LANES:(h // 2 + 1) * LANES] for sub in subs for h in heads}
        s_raw_t = {(sub, h): _dot_nt(k_own[sub][h], q_pair[sub, h]) for sub in subs for h in heads}
        w_t = {(sub, h, d): jnp.exp(jnp.where(
            masks[d], r[sub][:, lane_of(h, d):lane_of(h, d) + 1] - big_m_t[sub][lane_of(h, d):lane_of(h, d) + 1, :],
            -jnp.inf)) for sub, h, d in chains}
        sw_t = {(sub, h, d): s_raw_t[sub, h] * w_t[sub, h, d] for sub, h, d in chains}
        num = {(sub, h, d): jnp.dot(v_th[sub][h], sw_t[sub, h, d].astype(bf16), preferred_element_type=f32)
               for sub, h, d in chains}
        den_intra = {ch: jnp.sum(sw_t[ch], axis=0, keepdims=True) for ch in chains}
        wi = {(sub, h, d): w_inter_t[sub][lane_of(h, d):lane_of(h, d) + 1, :] for sub, h, d in chains}
        gate = {(sub, h): jax.nn.sigmoid(og_ref[rows[sub], h * ML_V_DIM:(h + 1) * ML_V_DIM])
                for sub in subs for h in heads}

        for sub in subs:
            hd_pairs = [(h, d) for h in heads for d in range(2)]
            inter = {(h, d): _dot_nt(s_carry[0, h].astype(bf16) if d == 0 else s_store[cidx[sub], h],
                                     q_pair[sub, h]) for h, d in hd_pairs}
            den = {(h, d): den_intra[sub, h, d] + wi[sub, h, d] * inter[h, d][ML_V_DIM:ML_V_DIM + 1]
                   for h, d in hd_pairs}
            inv = {(h, d): 1.0 / jnp.maximum(jnp.abs(den[h, d]),
                                             floor_t[sub][lane_of(h, d):lane_of(h, d) + 1, :]) for h, d in hd_pairs}
            part = {(h, d): (num[sub, h, d] + wi[sub, h, d] * inter[h, d][:ML_V_DIM]) * inv[h, d]
                    for h, d in hd_pairs}
            out = [(part[h, 0] + part[h, 1]).T for h in heads]
            for h in heads:
                y_ref[rows[sub], h * ML_V_DIM:(h + 1) * ML_V_DIM] = (out[h] * gate[sub, h]).astype(y_ref.dtype)
            v_tr = [v.astype(f32) for v in v_th[sub]]
            state_update(cum[sub], r[sub], k_own[sub], v_tr, m_row[sub], big_m[sub], 0)

        if emit_state:
            @_when(last)
            def _():
                emit_final(0)
                mfin_ref[...] = m_fin


def mlstm_mix(qk, v, og, gates, gate_b, state, row0, n_seq, seq_len, emit_state, out_buf=None,
              layer=0, state_bufs=None):
    nc = seq_len // CHUNK
    grp = math.gcd(nc, SCAN_G_MAX)
    blk_rows = grp * CHUNK
    nb = seq_len // blk_rows
    blk0 = row0 // blk_rows
    has_state = state is not None

    def block_map(s, p, c):
        return (blk0 + s * nb + jnp.where(p == 0, nb - 1 - c, c), 0)

    m_spec = pl.BlockSpec((None, 1, LANES), lambda s, p, c: (s, 0, 0))
    in_specs = [
        pl.BlockSpec((blk_rows, 2 * ML_QK_WIDTH), block_map),
        pl.BlockSpec((blk_rows, ML_V_WIDTH), block_map),
        pl.BlockSpec((blk_rows, ML_V_WIDTH), lambda s, p, c: (blk0 + s * nb + p * c, 0)),
        pl.BlockSpec((blk_rows, LANES), block_map),
        pl.BlockSpec((1, LANES), lambda s, p, c: (0, 0)),
    ]
    gb = jnp.concatenate([gate_b.reshape(1, 2 * ML_DIRS), jnp.zeros((1, LANES - 2 * ML_DIRS), f32)], axis=1)
    args = [qk, v, og, gates, gb]
    if has_state:
        c0, n0, m0 = state
        rows = jnp.concatenate([c0, jnp.broadcast_to(n0[..., None, :], n0.shape[:-1] + (ML_ST - ML_V_DIM, ML_QK_DIM))],
                               axis=-2)
        zeros = jnp.zeros_like(rows)
        odd = (jnp.arange(ML_HEADS) % 2 == 1)[:, None, None]
        s0 = jnp.where(odd, jnp.concatenate([zeros, rows], axis=-1), jnp.concatenate([rows, zeros], axis=-1))
        m0 = jnp.concatenate([m0.reshape(n_seq, 1, ML_DIRS), jnp.zeros((n_seq, 1, LANES - ML_DIRS), f32)], axis=-1)
        in_specs += [pl.BlockSpec((None, 2, ML_HEADS, ML_ST, LANES), lambda s, p, c: (s, 0, 0, 0, 0)), m_spec]
        args += [s0, m0]
    aliases = {}
    if out_buf is not None:
        aliases[len(args)] = 0
        in_specs.append(pl.BlockSpec(memory_space=pl.ANY))
        args.append(out_buf)
    out_shape = [jax.ShapeDtypeStruct((N_TOK, ML_V_WIDTH), bf16)]
    out_specs = [pl.BlockSpec((blk_rows, ML_V_WIDTH), lambda s, p, c: (blk0 + s * nb + p * c, 0))]
    if emit_state:
        if state_bufs is not None:
            for k_out, buf in enumerate(state_bufs):
                aliases[len(args)] = 1 + k_out
                in_specs.append(pl.BlockSpec(memory_space=pl.ANY))
                args.append(buf)
        out_shape += [jax.ShapeDtypeStruct((n_seq, N_ODD, 2, ML_HEADS, ML_V_DIM, ML_QK_DIM), f32),
                      jax.ShapeDtypeStruct((n_seq, N_ODD, 2, ML_HEADS, ML_QK_DIM), f32),
                      jax.ShapeDtypeStruct((n_seq, 1, LANES), f32)]
        out_specs += [pl.BlockSpec((None, None, 2, ML_HEADS, ML_V_DIM, ML_QK_DIM),
                                   lambda s, p, c: (s, layer, 0, 0, 0, 0)),
                      pl.BlockSpec((None, None, 2, ML_HEADS, ML_QK_DIM), lambda s, p, c: (s, layer, 0, 0, 0)),
                      m_spec]
    res = pl.pallas_call(
        functools.partial(_mlstm_t_kernel, nc, grp, has_state, len(aliases), emit_state),
        input_output_aliases=aliases,
        grid=(n_seq, 1 if nb == 1 else 2, nb),
        in_specs=in_specs,
        out_specs=out_specs,
        out_shape=out_shape,
        scratch_shapes=[pltpu.VMEM((nc, ML_HEADS, ML_ST, LANES), bf16), pltpu.VMEM((nc, SUBLANES, LANES), f32),
                        pltpu.VMEM((nc, 3, CHUNK, LANES), f32), pltpu.VMEM((nc, ML_HEADS, CHUNK, LANES), bf16),
                        pltpu.VMEM((nc, ML_HEADS, ML_V_DIM, CHUNK), bf16),
                        pltpu.VMEM((2, ML_HEADS, ML_ST, LANES), f32), pltpu.VMEM((SUBLANES, LANES), f32)],
        compiler_params=pltpu.CompilerParams(
            dimension_semantics=("arbitrary", "arbitrary", "arbitrary"), vmem_limit_bytes=VMEM_LIMIT),
        name="mlstm_scan",
    )(*args)
    if not emit_state:
        return res[0]
    y, c_fin, n_fin, mfin = res
    return y, c_fin, n_fin, mfin[:, 0, :ML_DIRS].reshape(n_seq, 2, ML_HEADS)


def kernel(x_prompt, x_sample, c, cache_na_k, cache_na_v, state_ssd, state_mlstm_c, state_mlstm_n, state_mlstm_m,
           c_ctx, w_mod, b_mod, norm_mix, norm_ffn, w_in_even, w_out_even, na_rpb, ssd_conv_w, ssd_conv_b,
           ssd_dt_bias, ssd_a_log, ssd_d, ssd_norm, w_in_odd, w_out_odd, ml_conv_w, ml_conv_b, ml_gate_b,
           w_ff1, w_ff2, norm_f):
    xs = [x_prompt.reshape(N_PROMPT, D_MODEL), x_sample.reshape(DEC_BATCH * DEC_SEQ, D_MODEL)]
    cond = jnp.concatenate([c_ctx[None, :], c, jnp.zeros((SUBLANES - N_COND, D_MODEL), f32)], axis=0)
    mod = adaln_all(cond, w_mod, b_mod)[:, :N_COND].reshape(DEPTH, N_COND, 1, N_MOD * D_MODEL)

    even_main = 3 * NA_WIDTH + SSD_INNER + SSD_CONV_DIM
    odd_main = 2 * ML_QK_WIDTH + 2 * ML_V_WIDTH

    def tail_bf16(w, main):
        t = w[:, :, main:]
        return jnp.concatenate([t, jnp.zeros(t.shape[:2] + (LANES - t.shape[2],), f32)], axis=2).astype(bf16)

    wi_even, wt_even = cast_bf16(w_in_even, even_main), tail_bf16(w_in_even, even_main)
    wi_odd, wt_odd = cast_bf16(w_in_odd, odd_main), tail_bf16(w_in_odd, odd_main)
    wo_even, wo_odd = cast_bf16(w_out_even), cast_bf16(w_out_odd)
    w1_all, w2_all = cast_bf16(w_ff1), cast_bf16(w_ff2)

    out_m = []
    new_k = new_v = new_ssd = new_c = new_n = None
    for l in range(DEPTH):
        norm_last = norm_f if l == DEPTH - 1 else None
        if l % 2 == 0:
            e = l // 2
            o0 = 3 * NA_WIDTH
            segs = ((0, NA_WIDTH, NA_HEAD_DIM ** -0.5, False),
                    (NA_WIDTH, 2 * NA_WIDTH, 1.0, False),
                    (2 * NA_WIDTH, 3 * NA_WIDTH, 1.0, False),
                    (o0, o0 + SSD_INNER, 1.0, False),
                    (o0 + SSD_INNER, o0 + SSD_INNER + SSD_CONV_DIM, 1.0, True),
                    (None, None, 1.0, False))
            outs = ((0, bf16, False), (1, bf16, False), (2, bf16, False), (1, f32, True), (2, f32, True),
                    (3, f32, False), (4, f32, False), (5, f32, False))
            q, k, v, new_k, new_v, z, xbc, dt = in_proj(
                xs, mod[l], norm_mix[l], wi_even, wt_even, e, segs, outs, ssd_conv_w[e], ssd_conv_b[e],
                cache_bufs=None if new_k is None else (new_k, new_v))
            ssd_w = (ssd_dt_bias[e], ssd_a_log[e], ssd_d[e], ssd_norm[e])
            y_ssd, new_ssd = ssd_mix(xbc, dt, z, None, 0, BATCH, SEQ, *ssd_w, True, layer=e, state_buf=new_ssd)
            y_ssd = ssd_mix(xbc, dt, z, state_ssd[:, e], N_PROMPT, DEC_BATCH, DEC_SEQ, *ssd_w, False, out_buf=y_ssd)
            tokens = lambda t: jnp.swapaxes(t, 1, 2).reshape(DEC_BATCH, PAST_LEN, NA_WIDTH).astype(bf16)
            y_na = ctx_attention(q, k, v)
            y_na = na_latent(q, k, v, tokens(cache_na_k[:, e]), tokens(cache_na_v[:, e]),
                             na_bias_table(na_rpb[e]), y_na)
            res = out_mlp(xs, mod[l], norm_ffn[l], [y_na, y_ssd], wo_even, e, w1_all, w2_all, l, norm_last)
        else:
            o = l // 2
            a0 = 2 * ML_QK_WIDTH
            segs = ((0, a0, 1.0, True),
                    (a0, a0 + ML_V_WIDTH, 1.0, False),
                    (a0 + ML_V_WIDTH, a0 + 2 * ML_V_WIDTH, 1.0, False),
                    (None, None, 1.0, False))
            outs = ((0, f32, False), (1, bf16, False), (2, f32, False), (3, f32, False))
            qk, v, og, gates = in_proj(xs, mod[l], norm_mix[l], wi_odd, wt_odd, o, segs, outs,
                                       ml_conv_w[o], ml_conv_b[o], row_block=ODD_ROW_BLOCK)
            y_ml, new_c, new_n, m_fin = mlstm_mix(qk, v, og, gates, ml_gate_b[o], None, 0, BATCH, SEQ, True,
                                                  layer=o, state_bufs=None if new_c is None else (new_c, new_n))
            out_m.append(m_fin)
            state = (state_mlstm_c[:, o], state_mlstm_n[:, o], state_mlstm_m[:, o])
            y_ml = mlstm_mix(qk, v, og, gates, ml_gate_b[o], state, N_PROMPT, DEC_BATCH, DEC_SEQ, False, out_buf=y_ml)
            res = out_mlp(xs, mod[l], norm_ffn[l], [y_ml], wo_odd, o, w1_all, w2_all, l, norm_last)
        xs = list(res) if norm_last is not None else [res]

    y_prompt = xs[0].reshape(BATCH, SEQ, D_MODEL)
    y_sample = xs[1].reshape(DEC_BATCH, DEC_SEQ, D_MODEL)
    return (y_prompt, y_sample, new_k, new_v, new_ssd,
            new_c, new_n, jnp.stack(out_m, axis=1))
```

```python
import functools
import math

import jax
import jax.numpy as jnp
from jax import lax
from jax.experimental import pallas as pl
from jax.experimental.pallas import tpu as pltpu

D_MODEL = 1024
BATCH = 32
SEQ = 256
DEPTH = 4
DEC_BATCH = 2
DEC_SEQ = 4096
PAST_LEN = 256
GRID_W = 64
N_EVEN = (DEPTH + 1) // 2
N_ODD = DEPTH // 2
RMS_EPS = 1e-6
N_MOD = 6
D_FF = 4 * D_MODEL
CONV_K = 3
Q_BLOCK = 128
NA_HEADS = 8
NA_HEAD_DIM = 64
NA_WIDTH = NA_HEADS * NA_HEAD_DIM
NA_WIN_ROWS = 8
NA_WIN_COLS = 16
NA_RPB_ROWS = 2 * NA_WIN_ROWS - 1
NA_RPB_COLS = 2 * NA_WIN_COLS - 1
SSD_INNER = D_MODEL
SSD_HEAD_DIM = 64
SSD_HEADS = SSD_INNER // SSD_HEAD_DIM
SSD_GROUPS = 2
SSD_RPG = SSD_HEADS // SSD_GROUPS
SSD_STATE = 128
SSD_GN = SSD_GROUPS * SSD_STATE
SSD_CONV_DIM = SSD_INNER + 2 * SSD_GN
SSD_CHUNK = 128
ML_HEADS = 8
ML_QK_DIM = D_MODEL // 16
ML_V_DIM = D_MODEL // 8
ML_QK_WIDTH = ML_HEADS * ML_QK_DIM
ML_V_WIDTH = ML_HEADS * ML_V_DIM
ML_CHUNK = 64
EVEN_MIX = NA_WIDTH + SSD_INNER

N_PROMPT = BATCH * SEQ
N_TOK = N_PROMPT + DEC_BATCH * DEC_SEQ
N_COND = 1 + DEC_BATCH
LANES = 128
SUBLANES = 8
VMEM_LIMIT = 56 * 1024 * 1024
TM = 512
ODD_ROW_BLOCK = 128

f32 = jnp.float32
bf16 = jnp.bfloat16


def _cond_row(i, tm):
    start = i * tm
    return jnp.where(start < N_PROMPT, 0, (start - N_PROMPT) // DEC_SEQ + 1)


def _const_spec(shape):
    nd = len(shape)
    return pl.BlockSpec(shape, lambda i: (0,) * nd, pipeline_mode=pl.Buffered(1))


def _rms(x):
    return x * lax.rsqrt(jnp.mean(x * x, axis=-1, keepdims=True) + RMS_EPS)


def _modulated(x, g, mod, k):
    shift = mod[:, k * D_MODEL:(k + 1) * D_MODEL]
    scale = mod[:, (k + 1) * D_MODEL:(k + 2) * D_MODEL]
    return (_rms(x) * g) * (1.0 + scale) + shift


def _mod_kernel(c_ref, w_ref, b_ref, o_ref):
    c = c_ref[...]
    a = (c * jax.nn.sigmoid(c)).astype(bf16)
    o_ref[...] = jnp.dot(a, w_ref[...].astype(bf16), preferred_element_type=f32) + b_ref[...]


def adaln_all(cond, w_mod, b_mod):
    tn = 1536
    nj = N_MOD * D_MODEL // tn
    return pl.pallas_call(
        _mod_kernel,
        grid=(DEPTH, nj),
        in_specs=[
            pl.BlockSpec((SUBLANES, D_MODEL), lambda l, j: (0, 0)),
            pl.BlockSpec((None, D_MODEL, tn), lambda l, j: (l, 0, j)),
            pl.BlockSpec((None, 1, tn), lambda l, j: (l, 0, j)),
        ],
        out_specs=pl.BlockSpec((None, SUBLANES, tn), lambda l, j: (l, 0, j)),
        out_shape=jax.ShapeDtypeStruct((DEPTH, SUBLANES, N_MOD * D_MODEL), f32),
        compiler_params=pltpu.CompilerParams(
            dimension_semantics=("arbitrary", "arbitrary"), vmem_limit_bytes=VMEM_LIMIT),
        name="adaln",
    )(cond, w_mod, b_mod.reshape(DEPTH, 1, N_MOD * D_MODEL))


def _cast_kernel(x_ref, o_ref):
    o_ref[...] = x_ref[...].astype(o_ref.dtype)


def cast_bf16(w, cols=None):
    n_l, k, n = w.shape
    cols = n if cols is None else cols
    bk = 512
    spec = pl.BlockSpec((None, bk, cols), lambda l, i: (l, i, 0))
    return pl.pallas_call(
        _cast_kernel,
        grid=(n_l, k // bk),
        in_specs=[spec],
        out_specs=spec,
        out_shape=jax.ShapeDtypeStruct((n_l, k, cols), bf16),
        compiler_params=pltpu.CompilerParams(
            dimension_semantics=("arbitrary", "arbitrary"), vmem_limit_bytes=VMEM_LIMIT),
        name="cast_bf16",
    )(w)


def _layer_spec(w, l):
    nd = w.ndim - 1
    return pl.BlockSpec((None,) + w.shape[1:], lambda i: (l,) + (0,) * nd, pipeline_mode=pl.Buffered(1))


def _tile_x(x_refs, tm):
    if len(x_refs) == 1:
        return x_refs[0][...]
    return jnp.where(pl.program_id(0) < N_PROMPT // tm, x_refs[0][...], x_refs[1][...])


def _conv_silu_tile(y, prev_row, next_row, w, b):
    i = pl.program_id(0)
    rows = y.shape[0]
    is_prompt = i < N_PROMPT // rows
    seq_last = jnp.where(is_prompt, SEQ - 1, DEC_SEQ - 1)

    def act(up, mid, dn):
        c = up * w[0:1, :] + mid * w[1:2, :] + dn * w[2:3, :] + b
        return c * jax.nn.sigmoid(c)

    tile = act(pltpu.roll(y, 1, axis=0), y, pltpu.roll(y, rows - 1, axis=0))
    starts = set(range(0, rows, SEQ))
    fixes = []
    for t in sorted(starts | {(st - 1) % rows for st in starts}):
        g = i * rows + t
        pos = jnp.where(is_prompt, g & (SEQ - 1), (g - N_PROMPT) & (DEC_SEQ - 1))
        up = prev_row if t == 0 else y[t - 1:t]
        dn = next_row if t == rows - 1 else y[t + 1:t + 2]
        fixes.append((t, act(jnp.where(pos == 0, 0.0, up), y[t:t + 1], jnp.where(pos == seq_last, 0.0, dn))))
    return tile, fixes


def _in_proj_kernel(n_x, n_buf, row_block, segs, outs, *refs):
    x_refs = refs[:n_x]
    prev_ref, next_ref, mod_ref, g_ref, w_ref, wt_ref, cw_ref, cb_ref = refs[n_x:n_x + 8]
    o_refs = refs[n_x + 8 + n_buf:]
    mod, g = mod_ref[...], g_ref[...]
    x = _tile_x(x_refs, TM)
    n_blocks = TM // row_block
    halo = _modulated(jnp.concatenate([prev_ref[...], next_ref[...]], axis=0), g, mod, 0).astype(bf16)
    parts = [[] for _ in segs]
    for r in range(n_blocks):
        hb = _modulated(x[r * row_block:(r + 1) * row_block], g, mod, 0).astype(bf16)
        for i, (a, b, scale, conv) in enumerate(segs):
            w = wt_ref[...] if a is None else w_ref[:, a:b]
            lhs = jnp.concatenate([hb, halo], axis=0) if conv and r == n_blocks - 1 else hb
            parts[i].append(jnp.dot(lhs, w, preferred_element_type=f32))
    ys = []
    for (a, b, scale, conv), p in zip(segs, parts):
        y = jnp.concatenate(p, axis=0)
        if conv:
            ys.append(_conv_silu_tile(y[:TM], y[TM + SUBLANES - 1:TM + SUBLANES],
                                      y[TM + SUBLANES:TM + SUBLANES + 1], cw_ref[...], cb_ref[...]))
        else:
            ys.append(y if scale == 1.0 else y * scale)
    for (si, _, head_major), o_ref in zip(outs, o_refs):
        if head_major:
            @pl.when(pl.program_id(0) < N_PROMPT // TM)
            def _(o_ref=o_ref, si=si):
                for b in range(TM // SEQ):
                    for hd in range(NA_HEADS):
                        o_ref[b, hd] = ys[si][b * SEQ:(b + 1) * SEQ, hd * NA_HEAD_DIM:(hd + 1) * NA_HEAD_DIM]
        elif segs[si][3]:
            tile, fixes = ys[si]
            o_ref[...] = tile.astype(o_ref.dtype)
            for t, row in fixes:
                o_ref[t:t + 1, :] = row.astype(o_ref.dtype)
        else:
            o_ref[...] = ys[si].astype(o_ref.dtype)


def _x_specs(xs, tm):
    if len(xs) == 1:
        return [pl.BlockSpec((tm, D_MODEL), lambda i: (i, 0))]
    n_p = N_PROMPT // tm
    return [pl.BlockSpec((tm, D_MODEL), lambda i: (jnp.minimum(i, n_p - 1), 0)),
            pl.BlockSpec((tm, D_MODEL), lambda i: (jnp.maximum(i - n_p, 0), 0))]


def in_proj(xs, mod_l, g, w, w_tail, l, segs, outs, conv_w, conv_b, cache_bufs=None, row_block=TM):
    grid = (N_TOK // TM,)
    last_prompt = N_PROMPT // TM - 1
    halo_src = xs[-1]
    per = TM // SUBLANES
    blk0 = (halo_src.shape[0] - DEC_BATCH * DEC_SEQ) // SUBLANES
    n_blk = halo_src.shape[0] // SUBLANES
    tile0 = N_PROMPT // TM

    def prev_map(i):
        return (jnp.clip(blk0 + (i - tile0) * per - 1, 0, n_blk - 1), 0)

    def next_map(i):
        return (jnp.clip(blk0 + (i - tile0 + 1) * per, 0, n_blk - 1), 0)

    ch = conv_w.shape[1]
    out_shape, out_specs = [], []
    for si, dt, head_major in outs:
        width = LANES if segs[si][0] is None else segs[si][1] - segs[si][0]
        if head_major:
            out_shape.append(jax.ShapeDtypeStruct((BATCH, N_EVEN, NA_HEADS, SEQ, NA_HEAD_DIM), dt))
            out_specs.append(pl.BlockSpec((TM // SEQ, None, NA_HEADS, SEQ, NA_HEAD_DIM),
                                          lambda i: (jnp.minimum(i, last_prompt), l, 0, 0, 0)))
        else:
            out_shape.append(jax.ShapeDtypeStruct((N_TOK, width), dt))
            out_specs.append(pl.BlockSpec((TM, width), lambda i: (i, 0)))
    args = [*xs, halo_src, halo_src, mod_l, g.reshape(1, D_MODEL), w, w_tail, conv_w, conv_b.reshape(1, ch)]
    aliases = {}
    if cache_bufs is not None:
        head_major_outs = [k for k, o in enumerate(outs) if o[2]]
        for k_out, buf in zip(head_major_outs, cache_bufs):
            aliases[len(args)] = k_out
            args.append(buf)
    return pl.pallas_call(
        functools.partial(_in_proj_kernel, len(xs), len(aliases), row_block, segs, outs),
        input_output_aliases=aliases,
        grid=grid,
        in_specs=_x_specs(xs, TM) + [
            pl.BlockSpec((SUBLANES, D_MODEL), prev_map),
            pl.BlockSpec((SUBLANES, D_MODEL), next_map),
            pl.BlockSpec((None, 1, N_MOD * D_MODEL), lambda i: (_cond_row(i, TM), 0, 0)),
            _const_spec((1, D_MODEL)),
            _layer_spec(w, l),
            _layer_spec(w_tail, l),
            _const_spec((CONV_K, ch)),
            _const_spec((1, ch)),
        ] + [pl.BlockSpec(memory_space=pl.ANY)] * len(aliases),
        out_specs=out_specs,
        out_shape=out_shape,
        compiler_params=pltpu.CompilerParams(
            dimension_semantics=("arbitrary",), vmem_limit_bytes=VMEM_LIMIT),
        name="in_proj",
    )(*args)


def _out_mlp_kernel(n_x, n_mix, final, *refs):
    x_refs = refs[:n_x]
    mod_ref, g_ref = refs[n_x:n_x + 2]
    refs = refs[n_x + 2:]
    mix_refs = refs[:n_mix]
    wo_ref, w1_ref, w2_ref = refs[n_mix:n_mix + 3]
    rest = refs[n_mix + 3:]
    if final:
        gf_ref, op_ref, os_ref = rest
    else:
        (o_ref,) = rest
    mod = mod_ref[...]
    m = None
    k0 = 0
    for r in mix_refs:
        kw = r.shape[-1]
        part = jnp.dot(r[...].astype(bf16), wo_ref[k0:k0 + kw, :], preferred_element_type=f32)
        m = part if m is None else m + part
        k0 += kw
    x1 = _tile_x(x_refs, OUT_TM) + mod[:, 2 * D_MODEL:3 * D_MODEL] * m
    h2 = _modulated(x1, g_ref[...], mod, 3).astype(bf16)
    f = None
    for j in range(D_FF // FF_BLOCK):
        cols = slice(j * FF_BLOCK, (j + 1) * FF_BLOCK)
        u = jnp.dot(h2, w1_ref[:, cols], preferred_element_type=f32)
        a = jnp.square(jnp.maximum(u, 0.0)).astype(bf16)
        part = jnp.dot(a, w2_ref[cols, :], preferred_element_type=f32)
        f = part if f is None else f + part
    x2 = x1 + mod[:, 5 * D_MODEL:6 * D_MODEL] * f
    if not final:
        o_ref[...] = x2
        return
    y = _rms(x2) * gf_ref[...]
    is_prompt = pl.program_id(0) < N_PROMPT // OUT_TM

    @pl.when(is_prompt)
    def _():
        op_ref[...] = y

    @pl.when(jnp.logical_not(is_prompt))
    def _():
        os_ref[...] = y


OUT_TM = 512
FF_BLOCK = 2048


def out_mlp(xs, mod_l, g_ffn, mixes, w_out, l_out, w1, w2, l, norm_f=None):
    final = norm_f is not None
    tm = OUT_TM
    grid = (N_TOK // tm,)
    in_specs = _x_specs(xs, tm) + [
        pl.BlockSpec((None, 1, N_MOD * D_MODEL), lambda i: (_cond_row(i, tm), 0, 0)),
        _const_spec((1, D_MODEL)),
    ]
    in_specs += [pl.BlockSpec((tm, m.shape[-1]), lambda i: (i, 0)) for m in mixes]
    in_specs += [_layer_spec(w_out, l_out), _layer_spec(w1, l), _layer_spec(w2, l)]
    args = [*xs, mod_l, g_ffn.reshape(1, D_MODEL), *mixes, w_out, w1, w2]
    out_specs = pl.BlockSpec((tm, D_MODEL), lambda i: (i, 0))
    out_shape = jax.ShapeDtypeStruct((N_TOK, D_MODEL), f32)
    if final:
        in_specs.append(_const_spec((1, D_MODEL)))
        args.append(norm_f.reshape(1, D_MODEL))
        n_p = N_PROMPT // tm
        out_specs = [pl.BlockSpec((tm, D_MODEL), lambda i: (jnp.minimum(i, n_p - 1), 0)),
                     pl.BlockSpec((tm, D_MODEL), lambda i: (jnp.maximum(i - n_p, 0), 0))]
        out_shape = [jax.ShapeDtypeStruct((N_PROMPT, D_MODEL), f32),
                     jax.ShapeDtypeStruct((N_TOK - N_PROMPT, D_MODEL), f32)]
    return pl.pallas_call(
        functools.partial(_out_mlp_kernel, len(xs), len(mixes), final),
        grid=grid,
        in_specs=in_specs,
        out_specs=out_specs,
        out_shape=out_shape,
        compiler_params=pltpu.CompilerParams(
            dimension_semantics=("arbitrary",), vmem_limit_bytes=VMEM_LIMIT),
        name="out_mlp",
    )(*args)


NA_PAIRS = NA_HEADS // 2
NA_ROWS = DEC_SEQ // GRID_W
NA_WIN = NA_WIN_ROWS * GRID_W


def _dot_nt(a, b):
    return lax.dot_general(a, b, (((1,), (1,)), ((), ())), preferred_element_type=f32)


def _pair_stack(x):
    lane = lax.broadcasted_iota(jnp.int32, x.shape, 1)
    zero = jnp.zeros_like(x)
    return jnp.concatenate([jnp.where(lane < NA_HEAD_DIM, x, zero), jnp.where(lane >= NA_HEAD_DIM, x, zero)], axis=0)


def _pair_unstack(o):
    n = o.shape[0] // 2
    lane = lax.broadcasted_iota(jnp.int32, (n, LANES), 1)
    return jnp.where(lane < NA_HEAD_DIM, o[:n], o[n:])


def _softmax_pv(scores, values):
    def lane_tiles(blocks):
        return [b[:, k * LANES:(k + 1) * LANES] for b in blocks for k in range(b.shape[1] // LANES)]

    ms = [jnp.max(functools.reduce(jnp.maximum, lane_tiles(s)), axis=1, keepdims=True) for s in scores]
    ps = [[jnp.exp(b - m) for b in s] for s, m in zip(scores, ms)]
    invs = [1.0 / jnp.sum(functools.reduce(jnp.add, lane_tiles(p)), axis=1, keepdims=True) for p in ps]
    pn = [[(b * inv).astype(bf16) for b in p] for p, inv in zip(ps, invs)]
    return [functools.reduce(jnp.add, [jnp.dot(b, v, preferred_element_type=f32) for b, v in zip(p, vs)])
            for p, vs in zip(pn, values)]


CTX_STEP_SEQS = 4


def _ctx_attn_kernel(q_ref, k_ref, v_ref, o_ref):
    where = [(slice(b * SEQ, (b + 1) * SEQ), slice(p * LANES, (p + 1) * LANES))
             for b in range(CTX_STEP_SEQS) for p in range(NA_PAIRS)]
    scores = [[_dot_nt(_pair_stack(q_ref[rows, lanes]), k_ref[rows, lanes])] for rows, lanes in where]
    outs = _softmax_pv(scores, [[v_ref[rows, lanes]] for rows, lanes in where])
    for (rows, lanes), o in zip(where, outs):
        o_ref[rows, lanes] = _pair_unstack(o).astype(o_ref.dtype)


def ctx_attention(q, k, v):
    spec = pl.BlockSpec((CTX_STEP_SEQS * SEQ, NA_WIDTH), lambda b: (b, 0))
    return pl.pallas_call(
        _ctx_attn_kernel,
        grid=(BATCH // CTX_STEP_SEQS,),
        in_specs=[spec, spec, spec],
        out_specs=spec,
        out_shape=jax.ShapeDtypeStruct((N_TOK, NA_WIDTH), bf16),
        compiler_params=pltpu.CompilerParams(dimension_semantics=("arbitrary",), vmem_limit_bytes=VMEM_LIMIT),
        name="ctx_attn",
    )(q, k, v)


def _na_bias_kernel(rpb_ref, o_ref):
    pair = pl.program_id(0)
    shape = (GRID_W, LANES)
    qc = lax.broadcasted_iota(jnp.int32, shape, 0)
    lane = lax.broadcasted_iota(jnp.int32, shape, 1)
    kc = lane & (GRID_W - 1)
    low = lane < GRID_W
    col_start = jnp.clip(qc - NA_WIN_COLS // 2, 0, GRID_W - NA_WIN_COLS)
    valid = (kc >= col_start) & (kc < col_start + NA_WIN_COLS)
    rel_c = jnp.clip(kc - qc + NA_WIN_COLS - 1, 0, NA_RPB_COLS - 1)
    for e in range(2):
        base = (2 * pair + e) * NA_RPB_ROWS
        pieces = []
        for rr in range(NA_RPB_ROWS - 1):
            val = jnp.zeros(shape, f32)
            for t in range(NA_RPB_COLS):
                s_lo = rpb_ref[(base + rr) * NA_RPB_COLS + t]
                s_hi = rpb_ref[(base + rr + 1) * NA_RPB_COLS + t]
                val = jnp.where(rel_c == t, jnp.where(low, s_lo, s_hi), val)
            pieces.append(jnp.where(valid, val, -jnp.inf))
        for d in range(NA_WIN_ROWS):
            for i in range(0, NA_WIN_ROWS, 2):
                o_ref[d, e * GRID_W:(e + 1) * GRID_W, i * GRID_W:(i + 2) * GRID_W] = pieces[d + i]


def na_bias_table(rpb):
    return pl.pallas_call(
        _na_bias_kernel,
        grid=(NA_PAIRS,),
        in_specs=[pl.BlockSpec(memory_space=pltpu.SMEM)],
        out_specs=pl.BlockSpec((None, NA_WIN_ROWS, 2 * GRID_W, NA_WIN), lambda p: (p, 0, 0, 0)),
        out_shape=jax.ShapeDtypeStruct((NA_PAIRS, NA_WIN_ROWS, 2 * GRID_W, NA_WIN), f32),
        compiler_params=pltpu.CompilerParams(dimension_semantics=("arbitrary",), vmem_limit_bytes=VMEM_LIMIT),
        name="na_bias",
    )(rpb.reshape(-1))


NA_STEP_ROWS = 8


def _na_first_key_row(r):
    return jnp.clip(r - NA_WIN_ROWS // 2, 0, NA_ROWS - NA_WIN_ROWS)


def _na_kernel(q_ref, k_ref, v_ref, kc_ref, vc_ref, bias_ref, buf_ref, o_ref):
    del buf_ref
    where, scores, values = [], [], []
    for j in range(NA_STEP_ROWS):
        r = pl.program_id(1) * NA_STEP_ROWS + j
        first = _na_first_key_row(r)
        start = pl.multiple_of(first * GRID_W, GRID_W)
        shift = first - r + NA_WIN_ROWS - 1
        rows = slice(j * GRID_W, (j + 1) * GRID_W)
        for p in range(NA_PAIRS):
            lanes = slice(p * LANES, (p + 1) * LANES)
            qq = _pair_stack(q_ref[rows, lanes])
            where.append((rows, lanes))
            scores.append([_dot_nt(qq, k_ref[pl.ds(start, NA_WIN), lanes]) + bias_ref[p, shift],
                           _dot_nt(qq, kc_ref[:, lanes])])
            values.append([v_ref[pl.ds(start, NA_WIN), lanes], vc_ref[:, lanes]])
    for (rows, lanes), o in zip(where, _softmax_pv(scores, values)):
        o_ref[rows, lanes] = _pair_unstack(o).astype(o_ref.dtype)


def na_latent(q, k, v, k_ctx, v_ctx, bias, buf):
    rows = NA_STEP_ROWS * GRID_W
    steps = NA_ROWS // NA_STEP_ROWS
    row0 = N_PROMPT // rows
    seq0 = N_PROMPT // DEC_SEQ
    kv_spec = pl.BlockSpec((DEC_SEQ, NA_WIDTH), lambda b, r: (seq0 + b, 0))
    ctx_spec = pl.BlockSpec((None, PAST_LEN, NA_WIDTH), lambda b, r: (b, 0, 0))
    return pl.pallas_call(
        _na_kernel,
        grid=(DEC_BATCH, steps),
        in_specs=[
            pl.BlockSpec((rows, NA_WIDTH), lambda b, r: (row0 + b * steps + r, 0)),
            kv_spec, kv_spec, ctx_spec, ctx_spec,
            pl.BlockSpec(bias.shape, lambda b, r: (0, 0, 0, 0), pipeline_mode=pl.Buffered(1)),
            pl.BlockSpec(memory_space=pl.ANY),
        ],
        out_specs=pl.BlockSpec((rows, NA_WIDTH), lambda b, r: (row0 + b * steps + r, 0)),
        out_shape=jax.ShapeDtypeStruct((N_TOK, NA_WIDTH), bf16),
        input_output_aliases={6: 0},
        compiler_params=pltpu.CompilerParams(
            dimension_semantics=("arbitrary", "arbitrary"), vmem_limit_bytes=VMEM_LIMIT),
        name="na_latent",
    )(q, k, v, k_ctx, v_ctx, bias, buf)


CHUNK = 128
SCAN_G_MAX = 4


def _when(cond):
    return (lambda f: f()) if cond is True else pl.when(cond)


def _scan_phases(single, phase, gi, nb):
    if single:
        return True, True, True, True, True
    return (phase == 0, phase == 1, jnp.logical_and(phase == 0, gi == 0), jnp.logical_and(phase == 1, gi == 0),
            gi == nb - 1)


def _cumsum_rows(a, reverse=False):
    row = lax.broadcasted_iota(jnp.int32, a.shape, 0)
    s = 1
    while s < CHUNK:
        if reverse:
            a = a + jnp.where(row < CHUNK - s, pltpu.roll(a, CHUNK - s, axis=0), 0.0)
        else:
            a = a + jnp.where(row >= s, pltpu.roll(a, s, axis=0), 0.0)
        s *= 2
    return a


def _ssd_kernel(nc, grp, has_h0, n_buf, emit_state, *refs):
    xbc_ref, dt_ref, z_ref = refs[:3]
    refs = refs[3:]
    if has_h0:
        h0_ref, refs = refs[0], refs[1:]
    dtb_ref, alog_ref, dskip_ref, g_ref = refs[:4]
    refs = refs[4 + n_buf:]
    y_ref, refs = refs[0], refs[1:]
    if emit_state:
        hfin_ref, refs = refs[0], refs[1:]
    hb_store, xt_store, gate_store, carry = refs

    gi = pl.program_id(2)
    nb = nc // grp
    in_bwd, in_fwd, first_bwd, first_fwd, last = _scan_phases(nb == 1, pl.program_id(1), gi, nb)
    gw = SSD_RPG * SSD_HEAD_DIM

    def load_h0(d):
        if has_h0:
            return h0_ref[d].reshape(SSD_INNER, SSD_STATE)
        return jnp.zeros((SSD_INNER, SSD_STATE), f32)

    def head_rows(v):
        return jnp.concatenate(
            [jnp.broadcast_to(v[h:h + 1, :], (SSD_HEAD_DIM, v.shape[1])) for h in range(SSD_HEADS)], axis=0)

    def state_update(x_t, b_bf, cum_t, dt_t, edge):
        at_edge = jnp.broadcast_to(cum_t[:, edge:edge + 1], cum_t.shape)
        w_end = jnp.exp(at_edge - cum_t) * dt_t
        xw = (x_t * head_rows(w_end)).astype(bf16)
        upd = jnp.concatenate(
            [jnp.dot(xw[g * gw:(g + 1) * gw], b_bf[:, g * SSD_STATE:(g + 1) * SSD_STATE], preferred_element_type=f32)
             for g in range(SSD_GROUPS)], axis=0)
        carry[...] = carry[...] * head_rows(jnp.exp(at_edge)) + upd

    @_when(first_bwd)
    def _():
        carry[...] = load_h0(1)

    def backward_chunk(sub):
        rows = slice(sub * CHUNK, (sub + 1) * CHUNK)
        j = grp * (nb - 1 - gi) + sub
        x = xbc_ref[rows, :SSD_INNER]
        b_bf = xbc_ref[rows, SSD_INNER:SSD_INNER + SSD_GN].astype(bf16)
        dt = jax.nn.softplus(dt_ref[rows, :] + dtb_ref[...])
        a = dt * (-jnp.exp(alog_ref[...]))
        x_t = jnp.concatenate([x[:, k * LANES:(k + 1) * LANES].T for k in range(SSD_INNER // LANES)], axis=0)
        rcum = _cumsum_rows(a, reverse=True)
        xt_store[j] = x_t
        gate_store[j, 0] = dt
        gate_store[j, 1] = _cumsum_rows(a)
        gate_store[j, 2] = rcum
        hb_store[j] = carry[...].astype(bf16)
        state_update(x_t, b_bf, rcum.T[SSD_HEADS:2 * SSD_HEADS], dt.T[SSD_HEADS:2 * SSD_HEADS], 0)

    @_when(in_bwd)
    def _backward_states():
        for sub in reversed(range(grp)):
            backward_chunk(sub)

        if emit_state:
            @_when(last)
            def _():
                hfin_ref[1] = carry[...].reshape(SSD_HEADS, SSD_HEAD_DIM, SSD_STATE)

    @_when(first_fwd)
    def _():
        carry[...] = load_h0(0)

    @_when(in_fwd)
    def _forward_and_outputs():
        subs = range(grp)
        heads = range(SSD_HEADS)
        sub_heads = [(sub, h) for sub in subs for h in heads]
        rows = [slice(sub * CHUNK, (sub + 1) * CHUNK) for sub in subs]
        cidx = [grp * gi + sub for sub in subs]
        row = lax.broadcasted_iota(jnp.int32, (CHUNK, CHUNK), 0)
        col = lax.broadcasted_iota(jnp.int32, (CHUNK, CHUNK), 1)
        causal = col <= row
        anti = col >= row
        lane = lax.broadcasted_iota(jnp.int32, (CHUNK, LANES), 1)
        x = [xbc_ref[rw, :SSD_INNER] for rw in rows]
        x_bf = [xs.astype(bf16) for xs in x]
        b_bf = [xbc_ref[rw, SSD_INNER:SSD_INNER + SSD_GN].astype(bf16) for rw in rows]
        c_mat = [xbc_ref[rw, SSD_INNER + SSD_GN:].astype(bf16) for rw in rows]
        dt = [gate_store[c, 0] for c in cidx]
        cum = [gate_store[c, 1] for c in cidx]
        rcum = [gate_store[c, 2] for c in cidx]
        cum_t = [t.T for t in cum]
        rcum_t = [t.T for t in rcum]
        dt_t = [t.T for t in dt]
        cb = {(sub, g): _dot_nt(c_mat[sub][:, g * SSD_STATE:(g + 1) * SSD_STATE],
                                b_bf[sub][:, g * SSD_STATE:(g + 1) * SSD_STATE])
              for sub in subs for g in range(SSD_GROUPS)}
        log_dt = [jnp.log(t) for t in dt_t]
        key_f = [cum_t[sub] - log_dt[sub] for sub in subs]
        key_b = [rcum_t[sub] - log_dt[sub] for sub in subs]
        seg_f = {(sub, h): jnp.where(causal, cum[sub][:, h:h + 1] - key_f[sub][h:h + 1, :], -jnp.inf)
                 for sub, h in sub_heads}
        seg_b = {(sub, h): jnp.where(anti, rcum[sub][:, SSD_HEADS + h:SSD_HEADS + h + 1]
                                     - key_b[sub][SSD_HEADS + h:SSD_HEADS + h + 1, :], -jnp.inf)
                 for sub, h in sub_heads}
        e_f = {sh: jnp.exp(seg_f[sh]) for sh in sub_heads}
        e_b = {sh: jnp.exp(seg_b[sh]) for sh in sub_heads}
        ws = {(sub, h): (cb[sub, h // SSD_RPG] * (e_f[sub, h] + e_b[sub, h])).astype(bf16) for sub, h in sub_heads}
        rhs = {}
        for sub in subs:
            for p in range(SSD_HEADS // 2):
                xp = x_bf[sub][:, p * LANES:(p + 1) * LANES]
                zero = jnp.zeros_like(xp)
                rhs[sub, p] = jnp.concatenate([jnp.where(lane < SSD_HEAD_DIM, xp, zero),
                                               jnp.where(lane >= SSD_HEAD_DIM, xp, zero)], axis=0)
        y_intra = [jnp.concatenate(
            [jnp.dot(jnp.concatenate([ws[sub, 2 * p], ws[sub, 2 * p + 1]], axis=1), rhs[sub, p],
                     preferred_element_type=f32) for p in range(SSD_HEADS // 2)], axis=1) for sub in subs]

        for sub in subs:
            hf = carry[...].astype(bf16)
            hb = hb_store[cidx[sub]]

            def inter_t(h_all):
                return jnp.concatenate(
                    [_dot_nt(h_all[g * gw:(g + 1) * gw], c_mat[sub][:, g * SSD_STATE:(g + 1) * SSD_STATE])
                     for g in range(SSD_GROUPS)], axis=0)

            cum_f, cum_b = cum_t[sub][:SSD_HEADS], rcum_t[sub][SSD_HEADS:2 * SSD_HEADS]
            y_t = inter_t(hf) * head_rows(jnp.exp(cum_f)) + inter_t(hb) * head_rows(jnp.exp(cum_b))
            y_inter = jnp.concatenate([y_t[k * LANES:(k + 1) * LANES].T for k in range(SSD_INNER // LANES)], axis=1)
            state_update(xt_store[cidx[sub]], b_bf[sub], cum_f, dt_t[sub][:SSD_HEADS], CHUNK - 1)
            y = y_intra[sub] + y_inter + dskip_ref[...] * x[sub]

            zv = z_ref[rows[sub], :]
            yz = y * (zv * jax.nn.sigmoid(zv))
            y_ref[rows[sub], :] = (_rms(yz) * g_ref[...]).astype(y_ref.dtype)

        if emit_state:
            @_when(last)
            def _():
                hfin_ref[0] = carry[...].reshape(SSD_HEADS, SSD_HEAD_DIM, SSD_STATE)


def ssd_mix(xbc, dt, z, h0, row0, n_seq, seq_len, dt_bias, a_log, d_skip, norm_g, emit_state, out_buf=None,
            layer=0, state_buf=None):
    nc = seq_len // CHUNK
    grp = math.gcd(nc, SCAN_G_MAX)
    blk_rows = grp * CHUNK
    nb = seq_len // blk_rows
    blk0 = row0 // blk_rows
    has_h0 = h0 is not None

    def block_map(s, p, c):
        return (blk0 + s * nb + jnp.where(p == 0, nb - 1 - c, c), 0)

    state_spec = pl.BlockSpec((None, 2, SSD_HEADS, SSD_HEAD_DIM, SSD_STATE), lambda s, p, c: (s, 0, 0, 0, 0))
    vec = lambda n: pl.BlockSpec((1, n), lambda s, p, c: (0, 0))
    in_specs = [
        pl.BlockSpec((blk_rows, SSD_CONV_DIM), block_map),
        pl.BlockSpec((blk_rows, LANES), block_map),
        pl.BlockSpec((blk_rows, SSD_INNER), lambda s, p, c: (blk0 + s * nb + p * c, 0)),
    ]
    args = [xbc, dt, z]
    if has_h0:
        in_specs.append(state_spec)
        args.append(h0)
    in_specs += [vec(LANES), vec(LANES), vec(SSD_INNER), vec(SSD_INNER)]
    pad = lambda t: jnp.concatenate([t.reshape(1, -1), jnp.zeros((1, LANES - t.size), f32)], axis=1)
    args += [pad(dt_bias), pad(a_log), jnp.repeat(d_skip, SSD_HEAD_DIM).reshape(1, SSD_INNER),
             norm_g.reshape(1, SSD_INNER)]
    aliases = {}
    if out_buf is not None:
        aliases = {len(args): 0}
        in_specs.append(pl.BlockSpec(memory_space=pl.ANY))
        args.append(out_buf)
    out_shape = [jax.ShapeDtypeStruct((N_TOK, SSD_INNER), bf16)]
    out_specs = [pl.BlockSpec((blk_rows, SSD_INNER), lambda s, p, c: (blk0 + s * nb + p * c, 0))]
    if emit_state:
        if state_buf is not None:
            aliases[len(args)] = 1
            in_specs.append(pl.BlockSpec(memory_space=pl.ANY))
            args.append(state_buf)
        out_shape.append(jax.ShapeDtypeStruct((n_seq, N_EVEN, 2, SSD_HEADS, SSD_HEAD_DIM, SSD_STATE), f32))
        out_specs.append(pl.BlockSpec((None, None, 2, SSD_HEADS, SSD_HEAD_DIM, SSD_STATE),
                                      lambda s, p, c: (s, layer, 0, 0, 0, 0)))
    res = pl.pallas_call(
        functools.partial(_ssd_kernel, nc, grp, has_h0, len(aliases), emit_state),
        input_output_aliases=aliases,
        grid=(n_seq, 1 if nb == 1 else 2, nb),
        in_specs=in_specs,
        out_specs=out_specs,
        out_shape=out_shape,
        scratch_shapes=[pltpu.VMEM((nc, SSD_INNER, SSD_STATE), bf16), pltpu.VMEM((nc, SSD_INNER, CHUNK), f32),
                        pltpu.VMEM((nc, 3, CHUNK, LANES), f32), pltpu.VMEM((SSD_INNER, SSD_STATE), f32)],
        compiler_params=pltpu.CompilerParams(
            dimension_semantics=("arbitrary", "arbitrary", "arbitrary"), vmem_limit_bytes=VMEM_LIMIT),
        name="ssd_scan",
    )(*args)
    return res if emit_state else res[0]


ML_DIRS = 2 * ML_HEADS


def _cummax_rows(a, reverse=False):
    row = lax.broadcasted_iota(jnp.int32, a.shape, 0)
    s = 1
    while s < CHUNK:
        if reverse:
            a = jnp.maximum(a, jnp.where(row < CHUNK - s, pltpu.roll(a, CHUNK - s, axis=0), -jnp.inf))
        else:
            a = jnp.maximum(a, jnp.where(row >= s, pltpu.roll(a, s, axis=0), -jnp.inf))
        s *= 2
    return a


ML_ST = ML_V_DIM + 16


def _mlstm_t_kernel(nc, grp, has_state, n_buf, emit_state, *refs):
    qk_ref, v_ref, og_ref, gates_ref, gb_ref = refs[:5]
    refs = refs[5:]
    if has_state:
        s0_ref, m0_ref = refs[:2]
        refs = refs[2:]
    refs = refs[n_buf:]
    y_ref, refs = refs[0], refs[1:]
    if emit_state:
        cfin_ref, nfin_ref, mfin_ref = refs[:3]
        refs = refs[3:]
    s_store, m_store, gate_store, k_store, v_store, s_carry, m_carry = refs

    gi = pl.program_id(2)
    nb = nc // grp
    in_bwd, in_fwd, first_bwd, first_fwd, last = _scan_phases(nb == 1, pl.program_id(1), gi, nb)
    fwd_row = lax.broadcasted_iota(jnp.int32, (1, LANES), 1) < ML_HEADS

    def state_update(cum, r, k_own, v_tr, m_row, big_m, d):
        edge = CHUNK - 1 if d == 0 else 0
        m_edge = big_m[edge:edge + 1, :]
        wc_row = jnp.exp(m_row - m_edge)
        wk_t = jnp.exp(r.T - big_m.T[:, edge:edge + 1])
        lhs = []
        for h in range(ML_HEADS):
            wk = wk_t[d * ML_HEADS + h:d * ML_HEADS + h + 1, :]
            lhs.append(jnp.concatenate([v_tr[h] * wk, jnp.broadcast_to(wk, (ML_ST - ML_V_DIM, CHUNK))],
                                       axis=0).astype(bf16))
        upd = [jnp.dot(lhs[h], k_own[h], preferred_element_type=f32) for h in range(ML_HEADS)]
        for h in range(ML_HEADS):
            cl = d * ML_HEADS + h
            wc = jnp.broadcast_to(wc_row[:, cl:cl + 1], (ML_ST, LANES))
            s_carry[d, h] = wc * s_carry[d, h] + upd[h]
        return cum[edge:edge + 1, :] + m_edge

    def init_state(d):
        if has_state:
            s_carry[d] = s0_ref[d]
        else:
            s_carry[d] = jnp.zeros((ML_HEADS, ML_ST, LANES), f32)

    def emit_final(d):
        for h in range(ML_HEADS):
            tile = s_carry[d, h]
            if h % 2 == 1:
                tile = pltpu.roll(tile, ML_QK_DIM, axis=1)
            cfin_ref[d, h] = tile[:ML_V_DIM, :ML_QK_DIM]
            nfin_ref[d, h:h + 1, :] = tile[ML_V_DIM:ML_V_DIM + 1, :ML_QK_DIM]

    @_when(first_bwd)
    def _():
        init_state(1)
        m_carry[...] = jnp.broadcast_to(m0_ref[...], m_carry.shape) if has_state else jnp.zeros(m_carry.shape, f32)

    def backward_chunk(sub):
        rows = slice(sub * CHUNK, (sub + 1) * CHUNK)
        lane = lax.broadcasted_iota(jnp.int32, (CHUNK, LANES), 1)
        fwd_lane = lane < ML_HEADS
        g = gates_ref[rows, :] + gb_ref[...]
        lf = pltpu.roll(jax.nn.log_sigmoid(g), LANES - ML_DIRS, axis=1)
        cum = jnp.where(fwd_lane, _cumsum_rows(lf), _cumsum_rows(lf, reverse=True))
        r = g - cum
        pm = jnp.where(fwd_lane, _cummax_rows(r), _cummax_rows(r, reverse=True))
        k = qk_ref[rows, ML_QK_WIDTH:]
        k_own = [jnp.where((lane < ML_QK_DIM) if h % 2 == 0 else (lane >= ML_QK_DIM),
                           k[:, (h // 2) * LANES:(h // 2 + 1) * LANES], 0.0).astype(bf16) for h in range(ML_HEADS)]
        v_tr = [v_ref[rows, h * ML_V_DIM:(h + 1) * ML_V_DIM].astype(f32).T for h in range(ML_HEADS)]

        j = grp * (nb - 1 - gi) + sub
        gate_store[j, 0] = cum
        gate_store[j, 1] = r
        gate_store[j, 2] = pm
        for h in range(ML_HEADS):
            k_store[j, h] = k_own[h]
            v_store[j, h] = v_tr[h].astype(bf16)
        m_row = m_carry[0:1, :]
        s_store[j] = s_carry[1].astype(bf16)
        m_store[j] = m_carry[...]
        big_m = jnp.maximum(m_row, pm)
        m_new = state_update(cum, r, k_own, v_tr, m_row, big_m, 1)
        m_carry[...] = jnp.broadcast_to(jnp.where(fwd_row, m_row, m_new), m_carry.shape)

    @_when(in_bwd)
    def _backward_states():
        for sub in reversed(range(grp)):
            backward_chunk(sub)

    @_when(first_fwd)
    def _():
        init_state(0)
        if emit_state:
            emit_final(1)

    @_when(in_fwd)
    def _forward_and_outputs():
        subs = range(grp)
        heads = range(ML_HEADS)
        chains = [(sub, h, d) for sub in subs for h in heads for d in range(2)]
        lane_of = lambda h, d: d * ML_HEADS + h
        rows = [slice(sub * CHUNK, (sub + 1) * CHUNK) for sub in subs]
        cidx = [grp * gi + sub for sub in subs]
        cum = [gate_store[c, 0] for c in cidx]
        r = [gate_store[c, 1] for c in cidx]
        pm = [gate_store[c, 2] for c in cidx]
        k_own = [[k_store[c, h] for h in heads] for c in cidx]
        v_th = [[v_store[c, h] for h in heads] for c in cidx]

        m_both = m_carry[0:1, :]
        m_row, big_m = [], []
        m_fwd = m_both
        for sub in subs:
            m_row.append(jnp.where(fwd_row, m_fwd, m_store[cidx[sub]][0:1, :]))
            big_m.append(jnp.maximum(m_row[sub], pm[sub]))
            m_fwd = cum[sub][CHUNK - 1:CHUNK, :] + big_m[sub][CHUNK - 1:CHUNK, :]
        m_fin = jnp.where(fwd_row, m_fwd, m_both)
        m_carry[...] = jnp.broadcast_to(m_fin, m_carry.shape)

        big_m_t = [t.T for t in big_m]
        w_inter_t = [jnp.exp(m_row[sub] - big_m[sub]).T for sub in subs]
        floor_t = [jnp.exp(-(cum[sub] + big_m[sub])).T for sub in subs]
        key = lax.broadcasted_iota(jnp.int32, (CHUNK, CHUNK), 0)
        qry = lax.broadcasted_iota(jnp.int32, (CHUNK, CHUNK), 1)
        masks = (key <= qry, key >= qry)
        q = [(qk_ref[rw, :ML_QK_WIDTH] * (ML_QK_DIM ** -0.5)).astype(bf16) for rw in rows]
        q_pair = {(sub, h): q[sub][:, (h // 2) * LANES:(h // 2 + 1) * LANES] for sub in subs for h in heads}
        s_raw_t = {(sub, h): _dot_nt(k_own[sub][h], q_pair[sub, h]) for sub in subs for h in heads}
        w_t = {(sub, h, d): jnp.exp(jnp.where(
            masks[d], r[sub][:, lane_of(h, d):lane_of(h, d) + 1] - big_m_t[sub][lane_of(h, d):lane_of(h, d) + 1, :],
            -jnp.inf)) for sub, h, d in chains}
        sw_t = {(sub, h, d): s_raw_t[sub, h] * w_t[sub, h, d] for sub, h, d in chains}
        num = {(sub, h, d): jnp.dot(v_th[sub][h], sw_t[sub, h, d].astype(bf16), preferred_element_type=f32)
               for sub, h, d in chains}
        den_intra = {ch: jnp.sum(sw_t[ch], axis=0, keepdims=True) for ch in chains}
        wi = {(sub, h, d): w_inter_t[sub][lane_of(h, d):lane_of(h, d) + 1, :] for sub, h, d in chains}
        gate = {(sub, h): jax.nn.sigmoid(og_ref[rows[sub], h * ML_V_DIM:(h + 1) * ML_V_DIM])
                for sub in subs for h in heads}

        for sub in subs:
            hd_pairs = [(h, d) for h in heads for d in range(2)]
            inter = {(h, d): _dot_nt(s_carry[0, h].astype(bf16) if d == 0 else s_store[cidx[sub], h],
                                     q_pair[sub, h]) for h, d in hd_pairs}
            den = {(h, d): den_intra[sub, h, d] + wi[sub, h, d] * inter[h, d][ML_V_DIM:ML_V_DIM + 1]
                   for h, d in hd_pairs}
            inv = {(h, d): 1.0 / jnp.maximum(jnp.abs(den[h, d]),
                                             floor_t[sub][lane_of(h, d):lane_of(h, d) + 1, :]) for h, d in hd_pairs}
            part = {(h, d): (num[sub, h, d] + wi[sub, h, d] * inter[h, d][:ML_V_DIM]) * inv[h, d]
                    for h, d in hd_pairs}
            out = [(part[h, 0] + part[h, 1]).T for h in heads]
            for h in heads:
                y_ref[rows[sub], h * ML_V_DIM:(h + 1) * ML_V_DIM] = (out[h] * gate[sub, h]).astype(y_ref.dtype)
            v_tr = [v.astype(f32) for v in v_th[sub]]
            state_update(cum[sub], r[sub], k_own[sub], v_tr, m_row[sub], big_m[sub], 0)

        if emit_state:
            @_when(last)
            def _():
                emit_final(0)
                mfin_ref[...] = m_fin


def mlstm_mix(qk, v, og, gates, gate_b, state, row0, n_seq, seq_len, emit_state, out_buf=None,
              layer=0, state_bufs=None):
    nc = seq_len // CHUNK
    grp = math.gcd(nc, SCAN_G_MAX)
    blk_rows = grp * CHUNK
    nb = seq_len // blk_rows
    blk0 = row0 // blk_rows
    has_state = state is not None

    def block_map(s, p, c):
        return (blk0 + s * nb + jnp.where(p == 0, nb - 1 - c, c), 0)

    m_spec = pl.BlockSpec((None, 1, LANES), lambda s, p, c: (s, 0, 0))
    in_specs = [
        pl.BlockSpec((blk_rows, 2 * ML_QK_WIDTH), block_map),
        pl.BlockSpec((blk_rows, ML_V_WIDTH), block_map),
        pl.BlockSpec((blk_rows, ML_V_WIDTH), lambda s, p, c: (blk0 + s * nb + p * c, 0)),
        pl.BlockSpec((blk_rows, LANES), block_map),
        pl.BlockSpec((1, LANES), lambda s, p, c: (0, 0)),
    ]
    gb = jnp.concatenate([gate_b.reshape(1, 2 * ML_DIRS), jnp.zeros((1, LANES - 2 * ML_DIRS), f32)], axis=1)
    args = [qk, v, og, gates, gb]
    if has_state:
        c0, n0, m0 = state
        rows = jnp.concatenate([c0, jnp.broadcast_to(n0[..., None, :], n0.shape[:-1] + (ML_ST - ML_V_DIM, ML_QK_DIM))],
                               axis=-2)
        zeros = jnp.zeros_like(rows)
        odd = (jnp.arange(ML_HEADS) % 2 == 1)[:, None, None]
        s0 = jnp.where(odd, jnp.concatenate([zeros, rows], axis=-1), jnp.concatenate([rows, zeros], axis=-1))
        m0 = jnp.concatenate([m0.reshape(n_seq, 1, ML_DIRS), jnp.zeros((n_seq, 1, LANES - ML_DIRS), f32)], axis=-1)
        in_specs += [pl.BlockSpec((None, 2, ML_HEADS, ML_ST, LANES), lambda s, p, c: (s, 0, 0, 0, 0)), m_spec]
        args += [s0, m0]
    aliases = {}
    if out_buf is not None:
        aliases[len(args)] = 0
        in_specs.append(pl.BlockSpec(memory_space=pl.ANY))
        args.append(out_buf)
    out_shape = [jax.ShapeDtypeStruct((N_TOK, ML_V_WIDTH), bf16)]
    out_specs = [pl.BlockSpec((blk_rows, ML_V_WIDTH), lambda s, p, c: (blk0 + s * nb + p * c, 0))]
    if emit_state:
        if state_bufs is not None:
            for k_out, buf in enumerate(state_bufs):
                aliases[len(args)] = 1 + k_out
                in_specs.append(pl.BlockSpec(memory_space=pl.ANY))
                args.append(buf)
        out_shape += [jax.ShapeDtypeStruct((n_seq, N_ODD, 2, ML_HEADS, ML_V_DIM, ML_QK_DIM), f32),
                      jax.ShapeDtypeStruct((n_seq, N_ODD, 2, ML_HEADS, ML_QK_DIM), f32),
                      jax.ShapeDtypeStruct((n_seq, 1, LANES), f32)]
        out_specs += [pl.BlockSpec((None, None, 2, ML_HEADS, ML_V_DIM, ML_QK_DIM),
                                   lambda s, p, c: (s, layer, 0, 0, 0, 0)),
                      pl.BlockSpec((None, None, 2, ML_HEADS, ML_QK_DIM), lambda s, p, c: (s, layer, 0, 0, 0)),
                      m_spec]
    res = pl.pallas_call(
        functools.partial(_mlstm_t_kernel, nc, grp, has_state, len(aliases), emit_state),
        input_output_aliases=aliases,
        grid=(n_seq, 1 if nb == 1 else 2, nb),
        in_specs=in_specs,
        out_specs=out_specs,
        out_shape=out_shape,
        scratch_shapes=[pltpu.VMEM((nc, ML_HEADS, ML_ST, LANES), bf16), pltpu.VMEM((nc, SUBLANES, LANES), f32),
                        pltpu.VMEM((nc, 3, CHUNK, LANES), f32), pltpu.VMEM((nc, ML_HEADS, CHUNK, LANES), bf16),
                        pltpu.VMEM((nc, ML_HEADS, ML_V_DIM, CHUNK), bf16),
                        pltpu.VMEM((2, ML_HEADS, ML_ST, LANES), f32), pltpu.VMEM((SUBLANES, LANES), f32)],
        compiler_params=pltpu.CompilerParams(
            dimension_semantics=("arbitrary", "arbitrary", "arbitrary"), vmem_limit_bytes=VMEM_LIMIT),
        name="mlstm_scan",
    )(*args)
    if not emit_state:
        return res[0]
    y, c_fin, n_fin, mfin = res
    return y, c_fin, n_fin, mfin[:, 0, :ML_DIRS].reshape(n_seq, 2, ML_HEADS)


def kernel(x_prompt, x_sample, c, cache_na_k, cache_na_v, state_ssd, state_mlstm_c, state_mlstm_n, state_mlstm_m,
           c_ctx, w_mod, b_mod, norm_mix, norm_ffn, w_in_even, w_out_even, na_rpb, ssd_conv_w, ssd_conv_b,
           ssd_dt_bias, ssd_a_log, ssd_d, ssd_norm, w_in_odd, w_out_odd, ml_conv_w, ml_conv_b, ml_gate_b,
           w_ff1, w_ff2, norm_f):
    xs = [x_prompt.reshape(N_PROMPT, D_MODEL), x_sample.reshape(DEC_BATCH * DEC_SEQ, D_MODEL)]
    cond = jnp.concatenate([c_ctx[None, :], c, jnp.zeros((SUBLANES - N_COND, D_MODEL), f32)], axis=0)
    mod = adaln_all(cond, w_mod, b_mod)[:, :N_COND].reshape(DEPTH, N_COND, 1, N_MOD * D_MODEL)

    even_main = 3 * NA_WIDTH + SSD_INNER + SSD_CONV_DIM
    odd_main = 2 * ML_QK_WIDTH + 2 * ML_V_WIDTH

    def tail_bf16(w, main):
        t = w[:, :, main:]
        return jnp.concatenate([t, jnp.zeros(t.shape[:2] + (LANES - t.shape[2],), f32)], axis=2).astype(bf16)

    wi_even, wt_even = cast_bf16(w_in_even, even_main), tail_bf16(w_in_even, even_main)
    wi_odd, wt_odd = cast_bf16(w_in_odd, odd_main), tail_bf16(w_in_odd, odd_main)
    wo_even, wo_odd = cast_bf16(w_out_even), cast_bf16(w_out_odd)
    w1_all, w2_all = cast_bf16(w_ff1), cast_bf16(w_ff2)

    out_m = []
    new_k = new_v = new_ssd = new_c = new_n = None
    for l in range(DEPTH):
        norm_last = norm_f if l == DEPTH - 1 else None
        if l % 2 == 0:
            e = l // 2
            o0 = 3 * NA_WIDTH
            segs = ((0, NA_WIDTH, NA_HEAD_DIM ** -0.5, False),
                    (NA_WIDTH, 2 * NA_WIDTH, 1.0, False),
                    (2 * NA_WIDTH, 3 * NA_WIDTH, 1.0, False),
                    (o0, o0 + SSD_INNER, 1.0, False),
                    (o0 + SSD_INNER, o0 + SSD_INNER + SSD_CONV_DIM, 1.0, True),
                    (None, None, 1.0, False))
            outs = ((0, bf16, False), (1, bf16, False), (2, bf16, False), (1, f32, True), (2, f32, True),
                    (3, f32, False), (4, f32, False), (5, f32, False))
            q, k, v, new_k, new_v, z, xbc, dt = in_proj(
                xs, mod[l], norm_mix[l], wi_even, wt_even, e, segs, outs, ssd_conv_w[e], ssd_conv_b[e],
                cache_bufs=None if new_k is None else (new_k, new_v))
            ssd_w = (ssd_dt_bias[e], ssd_a_log[e], ssd_d[e], ssd_norm[e])
            y_ssd, new_ssd = ssd_mix(xbc, dt, z, None, 0, BATCH, SEQ, *ssd_w, True, layer=e, state_buf=new_ssd)
            y_ssd = ssd_mix(xbc, dt, z, state_ssd[:, e], N_PROMPT, DEC_BATCH, DEC_SEQ, *ssd_w, False, out_buf=y_ssd)
            tokens = lambda t: jnp.swapaxes(t, 1, 2).reshape(DEC_BATCH, PAST_LEN, NA_WIDTH).astype(bf16)
            y_na = ctx_attention(q, k, v)
            y_na = na_latent(q, k, v, tokens(cache_na_k[:, e]), tokens(cache_na_v[:, e]),
                             na_bias_table(na_rpb[e]), y_na)
            res = out_mlp(xs, mod[l], norm_ffn[l], [y_na, y_ssd], wo_even, e, w1_all, w2_all, l, norm_last)
        else:
            o = l // 2
            a0 = 2 * ML_QK_WIDTH
            segs = ((0, a0, 1.0, True),
                    (a0, a0 + ML_V_WIDTH, 1.0, False),
                    (a0 + ML_V_WIDTH, a0 + 2 * ML_V_WIDTH, 1.0, False),
                    (None, None, 1.0, False))
            outs = ((0, f32, False), (1, bf16, False), (2, f32, False), (3, f32, False))
            qk, v, og, gates = in_proj(xs, mod[l], norm_mix[l], wi_odd, wt_odd, o, segs, outs,
                                       ml_conv_w[o], ml_conv_b[o], row_block=ODD_ROW_BLOCK)
            y_ml, new_c, new_n, m_fin = mlstm_mix(qk, v, og, gates, ml_gate_b[o], None, 0, BATCH, SEQ, True,
                                                  layer=o, state_bufs=None if new_c is None else (new_c, new_n))
            out_m.append(m_fin)
            state = (state_mlstm_c[:, o], state_mlstm_n[:, o], state_mlstm_m[:, o])
            y_ml = mlstm_mix(qk, v, og, gates, ml_gate_b[o], state, N_PROMPT, DEC_BATCH, DEC_SEQ, False, out_buf=y_ml)
            res = out_mlp(xs, mod[l], norm_ffn[l], [y_ml], wo_odd, o, w1_all, w2_all, l, norm_last)
        xs = list(res) if norm_last is not None else [res]

    y_prompt = xs[0].reshape(BATCH, SEQ, D_MODEL)
    y_sample = xs[1].reshape(DEC_BATCH, DEC_SEQ, D_MODEL)
    return (y_prompt, y_sample, new_k, new_v, new_ssd,
            new_c, new_n, jnp.stack(out_m, axis=1))
```
